```python
import math
import jax, jax.numpy as jnp
from jax import lax
import numpy as np


D_MODEL = 2048
BATCH = 16
SEQ = 256
DEPTH = 1
DEC_BATCH = 2
DEC_SEQ = 2048
PAST_LEN = 256

GRID_W = 64
A_HEADS = 8
A_DH = 64
A_DV = 128
B_HEADS = 8
B_NOPE = 128
B_ROPE = 64
B_DV = 128
Q_RANK = 512
KV_RANK = 256
N_EXPERTS = 16
EXPERT_FF = 1536
CAPACITY_FACTOR = 2
ROPE_BASE = 10000.0
Q_BLOCK = 128
EPS = 1e-6

A_QK = A_HEADS * 2 * A_DH
A_V = A_HEADS * A_DV
MIX_WIDTH = A_HEADS * A_DV + B_HEADS * B_DV
IN_COLS = 2 * A_QK + A_V + Q_RANK + KV_RANK + B_ROPE
IN_SPLITS = (A_QK, 2 * A_QK, 2 * A_QK + A_V, 2 * A_QK + A_V + Q_RANK, 2 * A_QK + A_V + Q_RANK + KV_RANK)

kernel_name = "hybrid_diffattn_mla_expert_choice_diffusion_step"


def _rmsnorm(x, g):
    xf = x.astype(jnp.float32)
    y = xf * lax.rsqrt(jnp.mean(xf * xf, axis=-1, keepdims=True) + EPS)
    return (y * g.astype(jnp.float32)).astype(x.dtype)


def _modulate(x, gain, shift, scale):
    return _rmsnorm(x, gain) * (1 + scale) + shift


def _rope_1d(x, pos):
    d = x.shape[-1]
    nf = d // 2
    inv = ROPE_BASE ** (-jnp.arange(nf, dtype=jnp.float32) * 2.0 / d)
    ang = pos.astype(jnp.float32)[:, None] * inv[None, :]
    cos = jnp.cos(ang)[None, :, None, :].astype(x.dtype)
    sin = jnp.sin(ang)[None, :, None, :].astype(x.dtype)
    x1, x2 = x[..., :nf], x[..., nf:]
    return jnp.concatenate([x1 * cos - x2 * sin, x2 * cos + x1 * sin], axis=-1)


def _rope_2d(x):
    n_tok = x.shape[1]
    rows = n_tok // GRID_W
    t = jnp.arange(rows * GRID_W)
    row = t // GRID_W
    col = t % GRID_W
    half = x.shape[-1] // 2
    return jnp.concatenate([_rope_1d(x[..., :half], row), _rope_1d(x[..., half:], col)], axis=-1)


def _to_blocks(q):
    b, n = q.shape[:2]
    return jnp.moveaxis(q.reshape((b, n // Q_BLOCK, Q_BLOCK) + q.shape[2:]), 1, 0)


def _from_blocks(o):
    o = jnp.moveaxis(o, 0, 1)
    return o.reshape((o.shape[0], o.shape[1] * o.shape[2]) + o.shape[3:])


def _diff_attention(q, k, v, lam):
    scale = A_DH ** -0.5

    def block(qb):
        s = jnp.einsum('bqhcd,bkhcd->bhcqk', qb, k).astype(jnp.float32) * scale
        p = jax.nn.softmax(s, axis=-1)
        p = p[:, :, 0] - lam * p[:, :, 1]
        return jnp.einsum('bhqk,bkhd->bqhd', p.astype(v.dtype), v)

    return _from_blocks(lax.map(block, _to_blocks(q)))


def _mla_attention(qn, qr, kn, kr, v):
    scale = (B_NOPE + B_ROPE) ** -0.5

    def block(qs):
        qnb, qrb = qs
        s = (jnp.einsum('bqhd,bkhd->bhqk', qnb, kn) + jnp.einsum('bqhr,bkr->bhqk', qrb, kr)).astype(jnp.float32) * scale
        p = jax.nn.softmax(s, axis=-1)
        return jnp.einsum('bhqk,bkhd->bqhd', p.astype(v.dtype), v)

    return _from_blocks(lax.map(block, (_to_blocks(qn), _to_blocks(qr))))


def _expand_kv(ckv, w_ukv, kn_gain):
    b, n = ckv.shape[:2]
    kv = (ckv @ w_ukv).reshape(b, n, B_HEADS, B_NOPE + B_DV)
    return _rmsnorm(kv[..., :B_NOPE], kn_gain), kv[..., B_NOPE:]


def _expert_choice_ffn(h, w_router, w1, w3, w2):
    b, n, d = h.shape
    n_tok = b * n
    cap = CAPACITY_FACTOR * n_tok // N_EXPERTS
    tok = h.reshape(n_tok, d)
    aff = jax.nn.softmax((tok @ w_router).astype(jnp.float32), axis=-1)
    gate, idx = lax.top_k(aff.T, cap)
    xe = jnp.take(tok, idx, axis=0)
    hid = jax.nn.silu(jnp.einsum('ecd,edf->ecf', xe, w1)) * jnp.einsum('ecd,edf->ecf', xe, w3)
    ye = jnp.einsum('ecf,efd->ecd', hid, w2) * gate[..., None].astype(h.dtype)
    out = jnp.zeros_like(tok).at[idx.reshape(-1)].add(ye.reshape(-1, d))
    return out.reshape(b, n, d)


def _layer(x, mod, lp, lam_init, ctx):
    b, n, _ = x.shape
    sh1, sc1, g1, sh2, sc2, g2 = jnp.split(mod, 6, axis=-1)
    h = _modulate(x, lp['norm1_gain'], sh1, sc1)
    aq, ak, av, q_lat, ckv, kr = jnp.split(h @ lp['w_in'], IN_SPLITS, axis=-1)
    aq = _rmsnorm(aq.reshape(b, n, A_HEADS, 2, A_DH), lp['a_q_gain'])
    ak = _rmsnorm(ak.reshape(b, n, A_HEADS, 2, A_DH), lp['a_k_gain'])
    av = av.reshape(b, n, A_HEADS, A_DV)
    qb = (_rmsnorm(q_lat, lp['q_a_gain']) @ lp['w_uq']).reshape(b, n, B_HEADS, B_NOPE + B_ROPE)
    qn = _rmsnorm(qb[..., :B_NOPE], lp['b_qn_gain'])
    qr = _rmsnorm(qb[..., B_NOPE:], lp['b_qr_gain'])
    ckv = _rmsnorm(ckv, lp['kv_a_gain'])
    kr = _rmsnorm(kr, lp['b_kr_gain'])
    lam = (jnp.exp(jnp.sum(lp['a_lambda_q1'].astype(jnp.float32) * lp['a_lambda_k1'].astype(jnp.float32)))
           - jnp.exp(jnp.sum(lp['a_lambda_q2'].astype(jnp.float32) * lp['a_lambda_k2'].astype(jnp.float32)))
           + lam_init)
    if ctx is None:
        ctx_state = (ak, av, ckv, kr)
        keys_a, vals_a = ak, av
        kn_all, bv_all = _expand_kv(ckv, lp['w_ukv'], lp['b_kn_gain'])
        kr_all = kr
    else:
        cache_k, cache_v, cache_ckv, cache_kr = ctx
        aq = _rope_2d(aq.reshape(b, n, A_HEADS * 2, A_DH)).reshape(b, n, A_HEADS, 2, A_DH)
        ak = _rope_2d(ak.reshape(b, n, A_HEADS * 2, A_DH)).reshape(b, n, A_HEADS, 2, A_DH)
        keys_a = jnp.concatenate([ak, cache_k], axis=1)
        vals_a = jnp.concatenate([av, cache_v], axis=1)
        qr = _rope_2d(qr)
        kr = _rope_2d(kr[:, :, None, :])[:, :, 0]
        kn_all, bv_all = _expand_kv(jnp.concatenate([ckv, cache_ckv], axis=1), lp['w_ukv'], lp['b_kn_gain'])
        kr_all = jnp.concatenate([kr, cache_kr], axis=1)
        ctx_state = None
    oa = _diff_attention(aq, keys_a, vals_a, lam)
    oa = _rmsnorm(oa, lp['a_sub_gain']) * (1.0 - lam_init)
    ob = _mla_attention(qn, qr, kn_all, kr_all, bv_all)
    o = jnp.concatenate([oa.reshape(b, n, A_HEADS * A_DV), ob.reshape(b, n, B_HEADS * B_DV)], axis=-1) @ lp['w_o']
    x = x + g1 * o
    h2 = _modulate(x, lp['norm2_gain'], sh2, sc2)
    x = x + g2 * _expert_choice_ffn(h2, lp['w_router'], lp['w_exp1'], lp['w_exp3'], lp['w_exp2'])
    return x, ctx_state


def setup_inputs(seed: int = 0) -> dict:
    key = jax.random.key(seed)
    ks = jax.random.split(key, 40)

    def nrm(k, shape, scale=1.0):
        return jax.random.normal(k, shape, jnp.float32) * scale

    def gain(k, n):
        return 1.0 + 0.02 * jax.random.normal(k, (DEPTH, n), jnp.float32)

    return {
        'x_prompt': nrm(ks[0], (BATCH, SEQ, D_MODEL)),
        'x_sample': nrm(ks[1], (DEC_BATCH, DEC_SEQ, D_MODEL)),
        'cache_diff_k': nrm(ks[2], (DEC_BATCH, DEPTH, PAST_LEN, A_HEADS, 2, A_DH)),
        'cache_diff_v': nrm(ks[3], (DEC_BATCH, DEPTH, PAST_LEN, A_HEADS, A_DV)),
        'cache_mla_ckv': nrm(ks[4], (DEC_BATCH, DEPTH, PAST_LEN, KV_RANK)),
        'cache_mla_krope': nrm(ks[5], (DEC_BATCH, DEPTH, PAST_LEN, B_ROPE)),
        'c': nrm(ks[6], (DEC_BATCH, D_MODEL)),
        'c_ctx': nrm(ks[7], (D_MODEL,)),
        'w_ada': nrm(ks[8], (DEPTH, D_MODEL, 6 * D_MODEL), 0.5 * D_MODEL ** -0.5),
        'b_ada': nrm(ks[9], (DEPTH, 6 * D_MODEL), 0.02),
        'norm1_gain': gain(ks[10], D_MODEL),
        'norm2_gain': gain(ks[11], D_MODEL),
        'w_in': nrm(ks[12], (DEPTH, D_MODEL, IN_COLS), D_MODEL ** -0.5),
        'a_q_gain': gain(ks[13], A_DH),
        'a_k_gain': gain(ks[14], A_DH),
        'a_lambda_q1': nrm(ks[15], (DEPTH, A_DH), 0.1),
        'a_lambda_k1': nrm(ks[16], (DEPTH, A_DH), 0.1),
        'a_lambda_q2': nrm(ks[17], (DEPTH, A_DH), 0.1),
        'a_lambda_k2': nrm(ks[18], (DEPTH, A_DH), 0.1),
        'a_sub_gain': gain(ks[19], A_DV),
        'q_a_gain': gain(ks[20], Q_RANK),
        'w_uq': nrm(ks[21], (DEPTH, Q_RANK, B_HEADS * (B_NOPE + B_ROPE)), Q_RANK ** -0.5),
        'b_qn_gain': gain(ks[22], B_NOPE),
        'b_qr_gain': gain(ks[23], B_ROPE),
        'kv_a_gain': gain(ks[24], KV_RANK),
        'w_ukv': nrm(ks[25], (DEPTH, KV_RANK, B_HEADS * (B_NOPE + B_DV)), KV_RANK ** -0.5),
        'b_kn_gain': gain(ks[26], B_NOPE),
        'b_kr_gain': gain(ks[27], B_ROPE),
        'w_o': nrm(ks[28], (DEPTH, MIX_WIDTH, D_MODEL), MIX_WIDTH ** -0.5),
        'w_router': nrm(ks[29], (DEPTH, D_MODEL, N_EXPERTS), D_MODEL ** -0.5),
        'w_exp1': nrm(ks[30], (DEPTH, N_EXPERTS, D_MODEL, EXPERT_FF), D_MODEL ** -0.5),
        'w_exp3': nrm(ks[31], (DEPTH, N_EXPERTS, D_MODEL, EXPERT_FF), D_MODEL ** -0.5),
        'w_exp2': nrm(ks[32], (DEPTH, N_EXPERTS, EXPERT_FF, D_MODEL), EXPERT_FF ** -0.5),
    }


def reference(x_prompt, x_sample, cache_diff_k, cache_diff_v, cache_mla_ckv, cache_mla_krope, c, c_ctx,
              w_ada, b_ada, norm1_gain, norm2_gain, w_in, a_q_gain, a_k_gain, a_lambda_q1, a_lambda_k1,
              a_lambda_q2, a_lambda_k2, a_sub_gain, q_a_gain, w_uq, b_qn_gain, b_qr_gain, kv_a_gain, w_ukv,
              b_kn_gain, b_kr_gain, w_o, w_router, w_exp1, w_exp3, w_exp2):
    y_p = x_prompt
    y_s = x_sample
    new_k, new_v, new_ckv, new_kr = [], [], [], []
    for l in range(DEPTH):
        lp = {
            'norm1_gain': norm1_gain[l], 'norm2_gain': norm2_gain[l], 'w_in': w_in[l],
            'a_q_gain': a_q_gain[l], 'a_k_gain': a_k_gain[l],
            'a_lambda_q1': a_lambda_q1[l], 'a_lambda_k1': a_lambda_k1[l],
            'a_lambda_q2': a_lambda_q2[l], 'a_lambda_k2': a_lambda_k2[l], 'a_sub_gain': a_sub_gain[l],
            'q_a_gain': q_a_gain[l], 'w_uq': w_uq[l], 'b_qn_gain': b_qn_gain[l], 'b_qr_gain': b_qr_gain[l],
            'kv_a_gain': kv_a_gain[l], 'w_ukv': w_ukv[l], 'b_kn_gain': b_kn_gain[l], 'b_kr_gain': b_kr_gain[l],
            'w_o': w_o[l], 'w_router': w_router[l], 'w_exp1': w_exp1[l], 'w_exp3': w_exp3[l], 'w_exp2': w_exp2[l],
        }
        lam_init = 0.8 - 0.6 * math.exp(-0.3 * l)
        mod_ctx = (jax.nn.silu(c_ctx) @ w_ada[l] + b_ada[l])[None, None, :]
        mod_lat = (jax.nn.silu(c) @ w_ada[l] + b_ada[l])[:, None, :]
        y_p, (k_l, v_l, ckv_l, kr_l) = _layer(y_p, mod_ctx, lp, lam_init, None)
        new_k.append(k_l)
        new_v.append(v_l)
        new_ckv.append(ckv_l)
        new_kr.append(kr_l)
        y_s, _ = _layer(y_s, mod_lat, lp, lam_init,
                        (cache_diff_k[:, l], cache_diff_v[:, l], cache_mla_ckv[:, l], cache_mla_krope[:, l]))
    new_diff_k = jnp.stack(new_k, axis=1)
    new_diff_v = jnp.stack(new_v, axis=1)
    new_mla_ckv = jnp.stack(new_ckv, axis=1)
    new_mla_krope = jnp.stack(new_kr, axis=1)
    return (y_p, y_s, new_diff_k, new_diff_v, new_mla_ckv, new_mla_krope)
```

```python
import functools
import math

import numpy as np
import jax
import jax.numpy as jnp
from jax import lax
from jax.experimental import pallas as pl
from jax.experimental.pallas import tpu as pltpu

F32 = jnp.float32
BF16 = jnp.bfloat16

D_MODEL = 2048
GRID_W = 64
A_HEADS = 8
A_DH = 64
A_DV = 128
B_HEADS = 8
B_NOPE = 128
B_ROPE = 64
B_DV = 128
Q_RANK = 512
KV_RANK = 256
N_EXPERTS = 16
EXPERT_FF = 1536
CAPACITY_FACTOR = 2
ROPE_BASE = 10000.0
EPS = 1e-6
LAM_INIT = 0.8 - 0.6 * math.exp(-0.3 * 0)

A_QK = A_HEADS * 2 * A_DH
A_V = A_HEADS * A_DV
B_QK = B_NOPE + B_ROPE
IN_COLS = 2 * A_QK + A_V + Q_RANK + KV_RANK + B_ROPE
LANES = 128
IN_COLS_PAD = IN_COLS + LANES - B_ROPE
MXU_DIM = 256
ROW_CHUNKS = D_MODEL // LANES
VMEM_LIMIT = 56 * 1024 * 1024


def _cparams(sem):
    return pltpu.CompilerParams(dimension_semantics=sem, vmem_limit_bytes=VMEM_LIMIT)


def _resident(shape):
    nd = len(shape)
    return pl.BlockSpec(shape, lambda *_: (0,) * nd, pipeline_mode=pl.Buffered(1))


def _rms(x, gain):
    return x * lax.rsqrt(jnp.mean(x * x, axis=-1, keepdims=True) + EPS) * gain


def _seg_rms(y, segmat, gain):
    y2 = (y * y).astype(BF16)
    outs = []
    for c in range(y.shape[1] // MXU_DIM):
        sl = slice(c * MXU_DIM, (c + 1) * MXU_DIM)
        ms = jnp.dot(y2[:, sl], segmat, preferred_element_type=F32)
        outs.append(y[:, sl] * lax.rsqrt(ms + EPS) * gain)
    return jnp.concatenate(outs, axis=1)


def _rope(y, cos, sin):
    outs = []
    for c in range(y.shape[1] // LANES):
        yc = y[:, c * LANES:(c + 1) * LANES]
        lane = lax.broadcasted_iota(jnp.int32, yc.shape, 1)
        partner = jnp.where((lane & 16) == 0, pltpu.roll(yc, LANES - 16, 1), pltpu.roll(yc, 16, 1))
        outs.append(yc * cos + partner * sin)
    return outs[0] if len(outs) == 1 else jnp.concatenate(outs, axis=1)


def _rope_tables(n_tok):
    t = np.arange(n_tok)
    row, col = t // GRID_W, t % GRID_W
    nf = A_DH // 4
    inv = (ROPE_BASE ** (-np.arange(nf, dtype=np.float32) * 2.0 / (A_DH // 2))).astype(np.float32)
    lane = np.arange(A_DH)
    pos = np.where(lane[None, :] < A_DH // 2, row[:, None], col[:, None]).astype(np.float32)
    ang = pos * inv[lane % nf][None, :]
    sign = np.where((lane % (2 * nf)) < nf, -1.0, 1.0)[None, :]
    cos = np.tile(np.cos(ang), (1, LANES // A_DH)).astype(np.float32)
    sin = np.tile(np.sin(ang) * sign, (1, LANES // A_DH)).astype(np.float32)
    return jnp.asarray(cos), jnp.asarray(sin)


def _seg_matrix(width):
    i = np.arange(MXU_DIM)
    return jnp.asarray(((i[:, None] // width) == (i[None, :] // width)).astype(np.float32) / width, dtype=BF16)


def _ada_kernel(c_ref, w_ref, b_ref, o_ref):
    c = c_ref[...]
    s = c * jax.nn.sigmoid(c)
    o_ref[...] = jnp.dot(s.astype(BF16), w_ref[...].astype(BF16), preferred_element_type=F32) + b_ref[...]


def _ada(cvec, w_ada, b_ada):
    tn = 1024
    n = w_ada.shape[1]
    return pl.pallas_call(
        _ada_kernel,
        grid=(n // tn,),
        in_specs=[pl.BlockSpec((8, D_MODEL), lambda j: (0, 0)),
                  pl.BlockSpec((D_MODEL, tn), lambda j: (0, j)),
                  pl.BlockSpec((1, tn), lambda j: (0, j))],
        out_specs=pl.BlockSpec((8, tn), lambda j: (0, j)),
        out_shape=jax.ShapeDtypeStruct((8, n), F32),
        compiler_params=_cparams(("arbitrary",)),
        name="ada_mod",
    )(cvec, w_ada, b_ada)


def _expand_kv(ckv_n, kr, wukv_ref, kng, km_ref, bv_ref):
    kv = jnp.dot(ckv_n.astype(BF16), wukv_ref[...], preferred_element_type=F32)
    krb = kr[:, :B_ROPE].astype(BF16)
    for h in range(B_HEADS):
        base = h * (B_NOPE + B_DV)
        kn = _rms(kv[:, base:base + B_NOPE], kng)
        km_ref[h] = jnp.concatenate([kn.astype(BF16), krb], axis=1)
        bv_ref[h] = kv[:, base + B_NOPE:base + B_NOPE + B_DV].astype(BF16)


def _in_kernel(sample, *refs):
    (x_ref, sh_ref, sc_ref, n1g_ref, win_ref, aqg_ref, akg_ref, qag_ref, wuq_ref, qng_ref, qrg_ref,
     kvag_ref, wukv_ref, kng_ref, krg_ref, seg_ref) = refs[:16]
    refs = refs[16:]
    if sample:
        cos_ref, sin_ref = refs[:2]
        refs = refs[2:]
        cos, sin = cos_ref[...], sin_ref[...]
    aq_ref, ak_ref, av_ref, qm_ref, km_ref, bv_ref = refs[:6]
    refs = refs[6:]

    x = x_ref[...]
    h = _rms(x, n1g_ref[...]) * (1.0 + sc_ref[...]) + sh_ref[...]
    hb = h.astype(BF16)

    def proj(c0, c1):
        return jnp.dot(hb, win_ref[:, c0:c1], preferred_element_type=F32)

    seg = seg_ref[...]
    aq = _seg_rms(proj(0, A_QK), seg, aqg_ref[...])
    ak = _seg_rms(proj(A_QK, 2 * A_QK), seg, akg_ref[...])
    if sample:
        aq = _rope(aq, cos, sin)
        ak = _rope(ak, cos, sin)
    aq_ref[...] = (aq * (A_DH ** -0.5)).astype(BF16)
    ak_ref[...] = ak.astype(BF16)
    av = proj(2 * A_QK, 2 * A_QK + A_V)
    av_ref[...] = av.astype(BF16)

    c0 = 2 * A_QK + A_V
    q_lat = _rms(proj(c0, c0 + Q_RANK), qag_ref[...])
    qb = jnp.dot(q_lat.astype(BF16), wuq_ref[...], preferred_element_type=F32)
    qr = _seg_rms(qb[:, B_HEADS * B_NOPE:], seg, qrg_ref[...])
    if sample:
        qr = _rope(qr, cos, sin)
    scale = B_QK ** -0.5
    for hd in range(B_HEADS):
        qn = _rms(qb[:, hd * B_NOPE:(hd + 1) * B_NOPE], qng_ref[...])
        qm_ref[hd] = jnp.concatenate(
            [(qn * scale).astype(BF16), (qr[:, hd * B_ROPE:(hd + 1) * B_ROPE] * scale).astype(BF16)], axis=1)

    c0 += Q_RANK
    ckv = _rms(proj(c0, c0 + KV_RANK), kvag_ref[...])
    c0 += KV_RANK
    kr_raw = proj(c0, c0 + LANES)
    kr = kr_raw * lax.rsqrt(jnp.sum(kr_raw * kr_raw, axis=-1, keepdims=True) * (1.0 / B_ROPE) + EPS) * krg_ref[...]
    if sample:
        kr = _rope(kr, cos, sin)
    _expand_kv(ckv, kr, wukv_ref, kng_ref[...], km_ref, bv_ref)
    if not sample:
        ak32_ref, av32_ref, ckv32_ref, kr32_ref = refs
        ak32_ref[...] = ak
        av32_ref[...] = av
        ckv32_ref[...] = ckv
        kr32_ref[...] = kr[:, :B_ROPE]


def _in_proj(x, mod, sample, tiles_per_mod, w, tables):
    m = x.shape[0]
    tm = 256
    nt = m // tm
    row = lambda i: (i, 0)
    modspec = lambda k: pl.BlockSpec((None, 1, D_MODEL), lambda i: (i // tiles_per_mod, 0, k))
    in_specs = [pl.BlockSpec((tm, D_MODEL), row), modspec(0), modspec(1), _resident((1, D_MODEL)),
                _resident((D_MODEL, IN_COLS_PAD)), _resident((1, MXU_DIM)), _resident((1, MXU_DIM)),
                _resident((1, Q_RANK)), _resident((Q_RANK, B_HEADS * B_QK)), _resident((1, B_NOPE)),
                _resident((1, MXU_DIM)), _resident((1, KV_RANK)), _resident((KV_RANK, B_HEADS * (B_NOPE + B_DV))),
                _resident((1, B_NOPE)), _resident((1, LANES)), _resident((MXU_DIM, MXU_DIM))]
    args = [x, mod, mod, w["norm1_gain"], w["w_in"], w["a_q_gain"], w["a_k_gain"], w["q_a_gain"], w["w_uq"],
            w["b_qn_gain"], w["b_qr_gain"], w["kv_a_gain"], w["w_ukv"], w["b_kn_gain"], w["b_kr_gain"], w["seg64"]]
    if sample:
        per = tables[0].shape[0] // tm
        in_specs += [pl.BlockSpec((tm, LANES), lambda i: (i % per, 0))] * 2
        args += list(tables)
    out_shape = [jax.ShapeDtypeStruct((m, A_QK), BF16), jax.ShapeDtypeStruct((m, A_QK), BF16),
                 jax.ShapeDtypeStruct((m, A_V), BF16), jax.ShapeDtypeStruct((B_HEADS, m, B_QK), BF16),
                 jax.ShapeDtypeStruct((B_HEADS, m, B_QK), BF16), jax.ShapeDtypeStruct((B_HEADS, m, B_DV), BF16)]
    hspec = lambda d: pl.BlockSpec((B_HEADS, tm, d), lambda i: (0, i, 0))
    out_specs = [pl.BlockSpec((tm, A_QK), row), pl.BlockSpec((tm, A_QK), row), pl.BlockSpec((tm, A_V), row),
                 hspec(B_QK), hspec(B_QK), hspec(B_DV)]
    if not sample:
        out_shape += [jax.ShapeDtypeStruct((m, A_QK), F32), jax.ShapeDtypeStruct((m, A_V), F32),
                      jax.ShapeDtypeStruct((m, KV_RANK), F32), jax.ShapeDtypeStruct((m, B_ROPE), F32)]
        out_specs += [pl.BlockSpec((tm, A_QK), row), pl.BlockSpec((tm, A_V), row),
                      pl.BlockSpec((tm, KV_RANK), row), pl.BlockSpec((tm, B_ROPE), row)]
    return pl.pallas_call(
        functools.partial(_in_kernel, sample),
        grid=(nt,), in_specs=in_specs, out_specs=out_specs, out_shape=out_shape,
        compiler_params=_cparams(("arbitrary",)),
        name="in_proj_sample" if sample else "in_proj_prompt",
    )(*args)


def _cache_kv_kernel(ckv_ref, kr_ref, wukv_ref, kng_ref, km_ref, bv_ref):
    _expand_kv(ckv_ref[...], kr_ref[...], wukv_ref, kng_ref[...], km_ref, bv_ref)


def _cache_kv(ckv, kr, w):
    m = ckv.shape[0]
    return pl.pallas_call(
        _cache_kv_kernel,
        grid=(1,),
        in_specs=[_resident((m, KV_RANK)), _resident((m, B_ROPE)),
                  _resident((KV_RANK, B_HEADS * (B_NOPE + B_DV))), _resident((1, B_NOPE))],
        out_specs=[pl.BlockSpec((B_HEADS, m, B_QK), lambda i: (0, 0, 0)),
                   pl.BlockSpec((B_HEADS, m, B_DV), lambda i: (0, 0, 0))],
        out_shape=[jax.ShapeDtypeStruct((B_HEADS, m, B_QK), BF16), jax.ShapeDtypeStruct((B_HEADS, m, B_DV), BF16)],
        compiler_params=_cparams(("arbitrary",)),
        name="cache_kv",
    )(ckv, kr, w["w_ukv"], w["b_kn_gain"])


def _qk(q, k):
    return lax.dot_general(q, k, (((1,), (1,)), ((), ())), preferred_element_type=F32)


def _diff_attn_kernel(hb, q_ref, k_ref, v_ref, lq1, lk1, lq2, lk2, sub_ref, o_ref):
    lam = (jnp.exp(jnp.sum(lq1[...] * lk1[...], axis=-1, keepdims=True))
           - jnp.exp(jnp.sum(lq2[...] * lk2[...], axis=-1, keepdims=True)) + LAM_INIT)
    for h in range(hb):
        sl = slice(h * LANES, (h + 1) * LANES)
        q, k, v = q_ref[:, sl], k_ref[:, sl], v_ref[:, sl]
        lane = lax.broadcasted_iota(jnp.int32, q.shape, 1)
        zero = jnp.zeros_like(q)
        s1 = _qk(jnp.where(lane < A_DH, q, zero), k)
        s2 = _qk(jnp.where(lane >= A_DH, q, zero), k)
        e1 = jnp.exp(s1 - jnp.max(s1, axis=-1, keepdims=True))
        e2 = jnp.exp(s2 - jnp.max(s2, axis=-1, keepdims=True))
        r1 = 1.0 / jnp.sum(e1, axis=-1, keepdims=True)
        r2 = lam / jnp.sum(e2, axis=-1, keepdims=True)
        p = e1 * r1 - e2 * r2
        o = jnp.dot(p.astype(BF16), v, preferred_element_type=F32)
        o_ref[:, sl] = (_rms(o, sub_ref[...]) * (1.0 - LAM_INIT)).astype(BF16)


def _diff_attn(q, k, v, nb, lq, lk, tq, hb, w):
    nq = lq // tq
    vec = _resident((1, A_DH))
    return pl.pallas_call(
        functools.partial(_diff_attn_kernel, hb),
        grid=(nb, A_HEADS // hb, nq),
        in_specs=[pl.BlockSpec((tq, hb * LANES), lambda b, g, i: (b * nq + i, g)),
                  pl.BlockSpec((lk, hb * LANES), lambda b, g, i: (b, g)),
                  pl.BlockSpec((lk, hb * LANES), lambda b, g, i: (b, g)),
                  vec, vec, vec, vec, _resident((1, A_DV))],
        out_specs=pl.BlockSpec((tq, hb * LANES), lambda b, g, i: (b * nq + i, g)),
        out_shape=jax.ShapeDtypeStruct((nb * lq, A_V), BF16),
        compiler_params=_cparams(("arbitrary", "arbitrary", "arbitrary")),
        name=f"diff_attn_{lk}",
    )(q, k, v, w["a_lambda_q1"], w["a_lambda_k1"], w["a_lambda_q2"], w["a_lambda_k2"], w["a_sub_gain"])


def _mla_attn_kernel(hb, q_ref, k_ref, v_ref, o_ref):
    for h in range(hb):
        s = _qk(q_ref[h], k_ref[h])
        e = jnp.exp(s - jnp.max(s, axis=-1, keepdims=True))
        r = 1.0 / jnp.sum(e, axis=-1, keepdims=True)
        o = jnp.dot(e.astype(BF16), v_ref[h], preferred_element_type=F32) * r
        o_ref[:, h * LANES:(h + 1) * LANES] = o.astype(BF16)


def _mla_attn(q, k, v, nb, lq, lk, tq, hb):
    nq = lq // tq
    return pl.pallas_call(
        functools.partial(_mla_attn_kernel, hb),
        grid=(nb, B_HEADS // hb, nq),
        in_specs=[pl.BlockSpec((hb, tq, B_QK), lambda b, g, i: (g, b * nq + i, 0)),
                  pl.BlockSpec((hb, lk, B_QK), lambda b, g, i: (g, b, 0)),
                  pl.BlockSpec((hb, lk, B_DV), lambda b, g, i: (g, b, 0))],
        out_specs=pl.BlockSpec((tq, hb * LANES), lambda b, g, i: (b * nq + i, g)),
        out_shape=jax.ShapeDtypeStruct((nb * lq, B_HEADS * B_DV), BF16),
        compiler_params=_cparams(("arbitrary", "arbitrary", "arbitrary")),
        name=f"mla_attn_{lk}",
    )(q, k, v)


def _out_kernel(oa_ref, ob_ref, x_ref, g1_ref, sh2_ref, sc2_ref, n2g_ref, woa_ref, wob_ref, wr_ref,
                x1_ref, h3_ref, aff_ref):
    tm = x_ref.shape[0]
    o = (jnp.dot(oa_ref[...], woa_ref[...], preferred_element_type=F32)
         + jnp.dot(ob_ref[...], wob_ref[...], preferred_element_type=F32))
    x1 = x_ref[...] + g1_ref[...] * o
    x1_ref[...] = x1
    h2 = _rms(x1, n2g_ref[...]) * (1.0 + sc2_ref[...]) + sh2_ref[...]
    for c in range(ROW_CHUNKS):
        h3_ref[pl.ds(c, tm, stride=ROW_CHUNKS), :] = h2[:, c * LANES:(c + 1) * LANES]
    logits = jnp.dot(h2.astype(BF16), wr_ref[...], preferred_element_type=F32)
    lane = lax.broadcasted_iota(jnp.int32, logits.shape, 1)
    logits = jnp.where(lane < N_EXPERTS, logits, -jnp.inf)
    e = jnp.exp(logits - jnp.max(logits, axis=-1, keepdims=True))
    aff_ref[...] = e / jnp.sum(e, axis=-1, keepdims=True)


def _out_proj(oa, ob, x, mod, tiles_per_mod, w, tag):
    m = x.shape[0]
    tm = 256
    row = lambda i: (i, 0)
    modspec = lambda k: pl.BlockSpec((None, 1, D_MODEL), lambda i: (i // tiles_per_mod, 0, k))
    return pl.pallas_call(
        _out_kernel,
        grid=(m // tm,),
        in_specs=[pl.BlockSpec((tm, A_V), row), pl.BlockSpec((tm, A_V), row), pl.BlockSpec((tm, D_MODEL), row),
                  modspec(2), modspec(3), modspec(4), _resident((1, D_MODEL)),
                  _resident((A_V, D_MODEL)), _resident((A_V, D_MODEL)), _resident((D_MODEL, LANES))],
        out_specs=[pl.BlockSpec((tm, D_MODEL), row), pl.BlockSpec((tm * ROW_CHUNKS, LANES), row),
                   pl.BlockSpec((tm, LANES), row)],
        out_shape=[jax.ShapeDtypeStruct((m, D_MODEL), F32), jax.ShapeDtypeStruct((m * ROW_CHUNKS, LANES), F32),
                   jax.ShapeDtypeStruct((m, LANES), F32)],
        compiler_params=_cparams(("arbitrary",)),
        name="out_proj_" + tag,
    )(oa, ob, x, mod, mod, mod, w["norm2_gain"], w["w_oa"], w["w_ob"], w["w_router"])


ROUTE_BLK = 256
COMBINE_T = 128
RANK_BITS = 4


def _route_kernel(cap, aff_ref, idx_ref, rk_ref, gate_ref, st_ref, cnt_ref, kmax_ref, pos_scr, pk_scr):
    n = aff_ref.shape[0]
    nblk = n // ROUTE_BLK
    aff = aff_ref[...]
    bits = pltpu.bitcast(aff, jnp.int32)
    lane = lax.broadcasted_iota(jnp.int32, (n, LANES), 1)
    tok = lax.broadcasted_iota(jnp.int32, (n, LANES), 0)

    def bisect(_, carry):
        lo, hi = carry
        mid = lo + ((hi - lo) >> 1)
        ok = jnp.sum((bits >= mid).astype(jnp.int32), axis=0, keepdims=True) >= cap
        return jnp.where(ok, mid, lo), jnp.where(ok, hi, mid)

    lo0 = jnp.zeros((1, LANES), jnp.int32)
    hi0 = jnp.full((1, LANES), 0x7F800001, jnp.int32)
    thr, _ = lax.fori_loop(0, 31, bisect, (lo0, hi0))
    gt = bits > thr
    eq = bits == thr
    need = cap - jnp.sum(gt.astype(jnp.int32), axis=0, keepdims=True)

    r = lax.broadcasted_iota(jnp.int32, (ROUTE_BLK, ROUTE_BLK), 0)
    c = lax.broadcasted_iota(jnp.int32, (ROUTE_BLK, ROUTE_BLK), 1)
    tri = (c < r).astype(BF16)

    def prefix(mask):
        m = mask.astype(F32)
        carry = jnp.zeros((1, LANES), F32)
        outs = []
        for b in range(nblk):
            mb = m[b * ROUTE_BLK:(b + 1) * ROUTE_BLK]
            outs.append(jnp.dot(tri, mb.astype(BF16), preferred_element_type=F32) + carry)
            carry = carry + jnp.sum(mb, axis=0, keepdims=True)
        return jnp.concatenate(outs, axis=0).astype(jnp.int32)

    sel = (gt | (eq & (prefix(eq) < need))) & (lane < N_EXPERTS)
    pos = prefix(sel)
    self_ = sel.astype(BF16)
    er = lax.broadcasted_iota(jnp.int32, (LANES, LANES), 0)
    ec = lax.broadcasted_iota(jnp.int32, (LANES, LANES), 1)
    rank = jnp.dot(self_, (er < ec).astype(BF16), preferred_element_type=F32).astype(jnp.int32)
    cnt = jnp.dot(self_, jnp.ones((LANES, LANES), BF16), preferred_element_type=F32).astype(jnp.int32)
    cnt_ref[...] = cnt
    nt = n // COMBINE_T
    kmax_ref[...] = jnp.max(cnt.reshape(nt, COMBINE_T, LANES), axis=1)

    pos_scr[...] = pos
    st_ref[...] = jnp.zeros(st_ref.shape, jnp.int32)
    st_ref[0:nt, :] = pos_scr[pl.ds(0, nt, stride=COMBINE_T), :]
    st_ref[nt:nt + 1, :] = jnp.full((1, LANES), cap, jnp.int32)
    pos_scr[...] = jnp.where(sel, pos, -1)
    pk_scr[...] = (tok << RANK_BITS) + rank

    jio = lax.broadcasted_iota(jnp.int32, (ROUTE_BLK, cap), 1)
    for e in range(N_EXPERTS):
        def blk(b, carry):
            acc_p, acc_g = carry
            rows = pl.ds(pl.multiple_of(b * ROUTE_BLK, ROUTE_BLK), ROUTE_BLK)
            oh = pos_scr[rows, :][:, e:e + 1] == jio
            pk = jnp.where(oh, pk_scr[rows, :][:, e:e + 1], 0)
            g = jnp.where(oh, aff_ref[rows, :][:, e:e + 1], 0.0)
            return (acc_p + jnp.sum(pk.reshape(ROUTE_BLK // 8, 8, cap), axis=0),
                    acc_g + jnp.sum(g.reshape(ROUTE_BLK // 8, 8, cap), axis=0))

        acc_p, acc_g = lax.fori_loop(0, nblk, blk, (jnp.zeros((8, cap), jnp.int32), jnp.zeros((8, cap), F32)))
        packed = jnp.sum(acc_p, axis=0, keepdims=True)
        idx_ref[e:e + 1, :] = packed >> RANK_BITS
        rk_ref[e:e + 1, :] = packed & (N_EXPERTS - 1)
        gate_ref[e:e + 1, :] = jnp.sum(acc_g, axis=0, keepdims=True)


def _route(aff, tag):
    n = aff.shape[0]
    cap = CAPACITY_FACTOR * n // N_EXPERTS
    nt = n // COMBINE_T
    full = lambda s: pl.BlockSpec(s, lambda i: (0,) * len(s))
    return pl.pallas_call(
        functools.partial(_route_kernel, cap),
        grid=(1,),
        in_specs=[full((n, LANES))],
        out_specs=[full((N_EXPERTS, cap)), full((N_EXPERTS, cap)), full((N_EXPERTS, cap)),
                   full((nt + 8, LANES)), full((n, LANES)), full((nt, LANES))],
        out_shape=[jax.ShapeDtypeStruct((N_EXPERTS, cap), jnp.int32), jax.ShapeDtypeStruct((N_EXPERTS, cap), jnp.int32),
                   jax.ShapeDtypeStruct((N_EXPERTS, cap), F32), jax.ShapeDtypeStruct((nt + 8, LANES), jnp.int32),
                   jax.ShapeDtypeStruct((n, LANES), jnp.int32), jax.ShapeDtypeStruct((nt, LANES), jnp.int32)],
        scratch_shapes=[pltpu.VMEM((n, LANES), jnp.int32), pltpu.VMEM((n, LANES), jnp.int32)],
        compiler_params=_cparams(("arbitrary",)),
        name="route_" + tag,
    )(aff)


FF_CHUNK = 256


def _ffn_kernel(cap, idx_s, hp_hbm, hs_hbm, gate_ref, w1_ref, w3_ref, w2_ref, yp_hbm, ys_hbm,
                stage, xe, acc, sem):
    e = pl.program_id(0)
    f = pl.program_id(1)
    rows = cap * ROW_CHUNKS
    srcs = (hp_hbm, hs_hbm)
    dsts = (yp_hbm, ys_hbm)

    @pl.when(f == 0)
    def _gather():
        for p in range(2):
            def issue(j, carry):
                tok = idx_s[(p * N_EXPERTS + e) * cap + j]
                pltpu.make_async_copy(
                    srcs[p].at[pl.ds(pl.multiple_of(tok * ROW_CHUNKS, ROW_CHUNKS), ROW_CHUNKS)],
                    stage.at[pl.ds(pl.multiple_of((p * cap + j) * ROW_CHUNKS, ROW_CHUNKS), ROW_CHUNKS)],
                    sem.at[p]).start()
                return carry
            lax.fori_loop(0, cap, issue, 0)
        for p in range(2):
            pltpu.make_async_copy(srcs[p].at[pl.ds(0, rows)], stage.at[pl.ds(p * rows, rows)], sem.at[p]).wait()
        for c in range(ROW_CHUNKS):
            xe[:, c * LANES:(c + 1) * LANES] = stage[pl.ds(c, 2 * cap, stride=ROW_CHUNKS), :].astype(BF16)
        acc[...] = jnp.zeros(acc.shape, F32)

    x = xe[...]
    a = jnp.dot(x, w1_ref[...].astype(BF16), preferred_element_type=F32)
    b = jnp.dot(x, w3_ref[...].astype(BF16), preferred_element_type=F32)
    hid = (a * jax.nn.sigmoid(a)) * b
    acc[...] += jnp.dot(hid.astype(BF16), w2_ref[...].astype(BF16), preferred_element_type=F32)

    @pl.when(f == pl.num_programs(1) - 1)
    def _emit():
        for c in range(ROW_CHUNKS):
            sl = slice(c * LANES, (c + 1) * LANES)
            stage[pl.ds(c, 2 * cap, stride=ROW_CHUNKS), :] = acc[:, sl] * gate_ref[...]
        copies = [pltpu.make_async_copy(stage.at[pl.ds(p * rows, rows)],
                                        dsts[p].at[pl.ds(pl.multiple_of(e * rows, rows), rows)], sem.at[p])
                  for p in range(2)]
        for cp in copies:
            cp.start()
        for cp in copies:
            cp.wait()


def _ffn(idx_flat, h3p, h3s, gate, w1, w3, w2, cap):
    nf = EXPERT_FF // FF_CHUNK
    rows = cap * ROW_CHUNKS
    grid_spec = pltpu.PrefetchScalarGridSpec(
        num_scalar_prefetch=1,
        grid=(N_EXPERTS, nf),
        in_specs=[pl.BlockSpec(memory_space=pl.ANY), pl.BlockSpec(memory_space=pl.ANY),
                  pl.BlockSpec((None, 2 * cap, 1), lambda e, f, *_: (e, 0, 0)),
                  pl.BlockSpec((None, D_MODEL, FF_CHUNK), lambda e, f, *_: (e, 0, f)),
                  pl.BlockSpec((None, D_MODEL, FF_CHUNK), lambda e, f, *_: (e, 0, f)),
                  pl.BlockSpec((None, FF_CHUNK, D_MODEL), lambda e, f, *_: (e, f, 0))],
        out_specs=[pl.BlockSpec(memory_space=pl.ANY), pl.BlockSpec(memory_space=pl.ANY)],
        scratch_shapes=[pltpu.VMEM((2 * rows, LANES), F32), pltpu.VMEM((2 * cap, D_MODEL), BF16),
                        pltpu.VMEM((2 * cap, D_MODEL), F32), pltpu.SemaphoreType.DMA((2,))])
    out = jax.ShapeDtypeStruct((N_EXPERTS * rows, LANES), F32)
    return pl.pallas_call(
        functools.partial(_ffn_kernel, cap),
        grid_spec=grid_spec, out_shape=[out, out],
        compiler_params=_cparams(("arbitrary", "arbitrary")),
        name="expert_ffn",
    )(idx_flat, h3p, h3s, gate, w1, w3, w2)


def _combine_kernel(cap, idx_s, rk_s, st_s, km_s, ye_hbm, x1_ref, g2_ref, cnt_ref, o_ref, buf, sem):
    t = pl.program_id(0)
    tt = COMBINE_T
    t0 = t * tt

    @pl.when(t == 0)
    def _init():
        buf[...] = jnp.zeros(buf.shape, F32)

    total = 0
    for e in range(N_EXPERTS):
        lo = st_s[t * LANES + e]
        hi = st_s[(t + 1) * LANES + e]

        def issue(j, carry):
            tok = idx_s[e * cap + j]
            slot = rk_s[e * cap + j]
            dst = (slot * tt + tok - t0) * ROW_CHUNKS
            pltpu.make_async_copy(
                ye_hbm.at[pl.ds(pl.multiple_of((e * cap + j) * ROW_CHUNKS, ROW_CHUNKS), ROW_CHUNKS)],
                buf.at[pl.ds(pl.multiple_of(dst, ROW_CHUNKS), ROW_CHUNKS)], sem).start()
            return carry
        lax.fori_loop(lo, hi, issue, 0)
        total = total + (hi - lo)

    def wait_one(i, carry):
        pltpu.make_async_copy(ye_hbm.at[pl.ds(0, ROW_CHUNKS)], buf.at[pl.ds(0, ROW_CHUNKS)], sem).wait()
        return carry
    lax.fori_loop(0, total, wait_one, 0)

    kmax = km_s[t * LANES]
    cnt = cnt_ref[...]
    for c in range(ROW_CHUNKS):
        def add_slot(k, a):
            v = buf[pl.ds(k * (tt * ROW_CHUNKS) + c, tt, stride=ROW_CHUNKS), :]
            return a + jnp.where(cnt > k, v, 0.0)
        a = lax.fori_loop(0, kmax, add_slot, jnp.zeros((tt, LANES), F32))
        sl = slice(c * LANES, (c + 1) * LANES)
        o_ref[:, sl] = x1_ref[:, sl] + g2_ref[:, sl] * a


def _combine(idx, rk, st, km, ye, x1, mod, tiles_per_mod, cnt, cap, tag):
    n = x1.shape[0]
    tt = COMBINE_T
    grid_spec = pltpu.PrefetchScalarGridSpec(
        num_scalar_prefetch=4,
        grid=(n // tt,),
        in_specs=[pl.BlockSpec(memory_space=pl.ANY),
                  pl.BlockSpec((tt, D_MODEL), lambda i, *_: (i, 0)),
                  pl.BlockSpec((None, 1, D_MODEL), lambda i, *_: (i // tiles_per_mod, 0, 5)),
                  pl.BlockSpec((tt, LANES), lambda i, *_: (i, 0))],
        out_specs=pl.BlockSpec((tt, D_MODEL), lambda i, *_: (i, 0)),
        scratch_shapes=[pltpu.VMEM((N_EXPERTS * tt * ROW_CHUNKS, LANES), F32), pltpu.SemaphoreType.DMA(())])
    return pl.pallas_call(
        functools.partial(_combine_kernel, cap),
        grid_spec=grid_spec, out_shape=jax.ShapeDtypeStruct((n, D_MODEL), F32),
        compiler_params=_cparams(("arbitrary",)),
        name="combine_" + tag,
    )(idx.reshape(-1), rk.reshape(-1), st.reshape(-1), km.reshape(-1), ye, x1, mod, cnt)


def _prep_weights(norm1_gain, norm2_gain, w_in, a_q_gain, a_k_gain, a_lambda_q1, a_lambda_k1, a_lambda_q2,
                  a_lambda_k2, a_sub_gain, q_a_gain, w_uq, b_qn_gain, b_qr_gain, kv_a_gain, w_ukv, b_kn_gain,
                  b_kr_gain, w_o, w_router):
    tile = lambda g, width: jnp.tile(g.reshape(1, -1), (1, width // g.shape[-1]))
    wuq = w_uq[0].reshape(Q_RANK, B_HEADS, B_QK)
    wuq = jnp.concatenate([wuq[:, :, :B_NOPE].reshape(Q_RANK, -1), wuq[:, :, B_NOPE:].reshape(Q_RANK, -1)], axis=1)
    return {
        "norm1_gain": norm1_gain.reshape(1, -1), "norm2_gain": norm2_gain.reshape(1, -1),
        "w_in": jnp.pad(w_in[0], ((0, 0), (0, IN_COLS_PAD - IN_COLS))).astype(BF16),
        "a_q_gain": tile(a_q_gain, MXU_DIM), "a_k_gain": tile(a_k_gain, MXU_DIM),
        "a_lambda_q1": a_lambda_q1.reshape(1, -1), "a_lambda_k1": a_lambda_k1.reshape(1, -1),
        "a_lambda_q2": a_lambda_q2.reshape(1, -1), "a_lambda_k2": a_lambda_k2.reshape(1, -1),
        "a_sub_gain": a_sub_gain.reshape(1, -1), "q_a_gain": q_a_gain.reshape(1, -1),
        "w_uq": wuq.astype(BF16), "b_qn_gain": b_qn_gain.reshape(1, -1), "b_qr_gain": tile(b_qr_gain, MXU_DIM),
        "kv_a_gain": kv_a_gain.reshape(1, -1), "w_ukv": w_ukv[0].astype(BF16),
        "b_kn_gain": b_kn_gain.reshape(1, -1), "b_kr_gain": tile(b_kr_gain, LANES),
        "w_oa": w_o[0, :A_V].astype(BF16), "w_ob": w_o[0, A_V:].astype(BF16),
        "w_router": jnp.pad(w_router[0], ((0, 0), (0, LANES - N_EXPERTS))).astype(BF16),
        "seg64": _seg_matrix(A_DH),
    }


def kernel(x_prompt, x_sample, cache_diff_k, cache_diff_v, cache_mla_ckv, cache_mla_krope, c, c_ctx, w_ada, b_ada, norm1_gain, norm2_gain, w_in, a_q_gain, a_k_gain, a_lambda_q1, a_lambda_k1, a_lambda_q2, a_lambda_k2, a_sub_gain, q_a_gain, w_uq, b_qn_gain, b_qr_gain, kv_a_gain, w_ukv, b_kn_gain, b_kr_gain, w_o, w_router, w_exp1, w_exp3, w_exp2):
    nbp, lp, _ = x_prompt.shape
    nbs, ls, _ = x_sample.shape
    past = cache_diff_k.shape[2]
    w = _prep_weights(norm1_gain, norm2_gain, w_in, a_q_gain, a_k_gain, a_lambda_q1, a_lambda_k1, a_lambda_q2,
                      a_lambda_k2, a_sub_gain, q_a_gain, w_uq, b_qn_gain, b_qr_gain, kv_a_gain, w_ukv, b_kn_gain,
                      b_kr_gain, w_o, w_router)

    cvec = jnp.concatenate([c_ctx[None], c, jnp.zeros((8 - 1 - nbs, D_MODEL), F32)], axis=0)
    mod = _ada(cvec, w_ada[0], b_ada)
    mod_p = mod[0:1].reshape(1, 1, -1)
    mod_s = mod[1:1 + nbs].reshape(nbs, 1, -1)

    xp = x_prompt.reshape(nbp * lp, D_MODEL)
    xs = x_sample.reshape(nbs * ls, D_MODEL)
    tm = 256
    aq_p, ak_p, av_p, qm_p, km_p, bv_p, ak32, av32, ckv32, kr32 = _in_proj(xp, mod_p, False, nbp * lp // tm, w, None)
    aq_s, ak_s, av_s, qm_s, km_s, bv_s = _in_proj(xs, mod_s, True, ls // tm, w, _rope_tables(ls))
    km_c, bv_c = _cache_kv(cache_mla_ckv.reshape(nbs * past, KV_RANK), cache_mla_krope.reshape(nbs * past, B_ROPE), w)

    def with_cache(new, cache, lead):
        new = new.reshape(lead + (nbs, ls, new.shape[-1]))
        cache = cache.reshape(lead + (nbs, past, new.shape[-1]))
        return jnp.concatenate([new, cache], axis=len(lead) + 1).reshape(lead + (nbs * (ls + past), new.shape[-1]))

    ak_all = with_cache(ak_s, cache_diff_k.astype(BF16), ())
    av_all = with_cache(av_s, cache_diff_v.astype(BF16), ())
    km_all = with_cache(km_s, km_c, (B_HEADS,))
    bv_all = with_cache(bv_s, bv_c, (B_HEADS,))

    oa_p = _diff_attn(aq_p, ak_p, av_p, nbp, lp, lp, lp, A_HEADS, w)
    ob_p = _mla_attn(qm_p, km_p, bv_p, nbp, lp, lp, lp, B_HEADS)
    oa_s = _diff_attn(aq_s, ak_all, av_all, nbs, ls, ls + past, 256, 1, w)
    ob_s = _mla_attn(qm_s, km_all, bv_all, nbs, ls, ls + past, 256, 1)

    x1_p, h3_p, aff_p = _out_proj(oa_p, ob_p, xp, mod_p, nbp * lp // tm, w, "prompt")
    x1_s, h3_s, aff_s = _out_proj(oa_s, ob_s, xs, mod_s, ls // tm, w, "sample")

    idx_p, rk_p, gate_p, st_p, cnt_p, km_p_ = _route(aff_p, "prompt")
    idx_s, rk_s, gate_s, st_s, cnt_s, km_s_ = _route(aff_s, "sample")
    cap = idx_p.shape[1]
    idx_flat = jnp.stack([idx_p, idx_s]).reshape(-1)
    gate = jnp.concatenate([gate_p, gate_s], axis=1)[..., None]
    ye_p, ye_s = _ffn(idx_flat, h3_p, h3_s, gate, w_exp1[0], w_exp3[0], w_exp2[0], cap)

    y_p = _combine(idx_p, rk_p, st_p, km_p_, ye_p, x1_p, mod_p, nbp * lp // COMBINE_T, cnt_p, cap, "prompt")
    y_s = _combine(idx_s, rk_s, st_s, km_s_, ye_s, x1_s, mod_s, ls // COMBINE_T, cnt_s, cap, "sample")

    return (y_p.reshape(nbp, lp, D_MODEL), y_s.reshape(nbs, ls, D_MODEL),
            ak32.reshape(nbp, 1, lp, A_HEADS, 2, A_DH), av32.reshape(nbp, 1, lp, A_HEADS, A_DV),
            ckv32.reshape(nbp, 1, lp, KV_RANK), kr32.reshape(nbp, 1, lp, B_ROPE))
```

```python
import functools
import math

import numpy as np
import jax
import jax.numpy as jnp
from jax import lax
from jax.experimental import pallas as pl
from jax.experimental.pallas import tpu as pltpu

F32 = jnp.float32
BF16 = jnp.bfloat16

D_MODEL = 2048
GRID_W = 64
A_HEADS = 8
A_DH = 64
A_DV = 128
B_HEADS = 8
B_NOPE = 128
B_ROPE = 64
B_DV = 128
Q_RANK = 512
KV_RANK = 256
N_EXPERTS = 16
EXPERT_FF = 1536
CAPACITY_FACTOR = 2
ROPE_BASE = 10000.0
EPS = 1e-6
LAM_INIT = 0.8 - 0.6 * math.exp(-0.3 * 0)

A_QK = A_HEADS * 2 * A_DH
A_V = A_HEADS * A_DV
B_QK = B_NOPE + B_ROPE
IN_COLS = 2 * A_QK + A_V + Q_RANK + KV_RANK + B_ROPE
LANES = 128
IN_COLS_PAD = IN_COLS + LANES - B_ROPE
MXU_DIM = 256
VMEM_LIMIT = 56 * 1024 * 1024


def _cparams(sem):
    return pltpu.CompilerParams(dimension_semantics=sem, vmem_limit_bytes=VMEM_LIMIT)


def _resident(shape):
    nd = len(shape)
    return pl.BlockSpec(shape, lambda *_: (0,) * nd, pipeline_mode=pl.Buffered(1))


def _rms(x, gain):
    return x * lax.rsqrt(jnp.mean(x * x, axis=-1, keepdims=True) + EPS) * gain


def _seg_rms(y, segmat, gain):
    y2 = (y * y).astype(BF16)
    outs = []
    for c in range(y.shape[1] // MXU_DIM):
        sl = slice(c * MXU_DIM, (c + 1) * MXU_DIM)
        ms = jnp.dot(y2[:, sl], segmat, preferred_element_type=F32)
        outs.append(y[:, sl] * lax.rsqrt(ms + EPS) * gain)
    return jnp.concatenate(outs, axis=1)


def _rope(y, cos, sin):
    outs = []
    for c in range(y.shape[1] // LANES):
        yc = y[:, c * LANES:(c + 1) * LANES]
        lane = lax.broadcasted_iota(jnp.int32, yc.shape, 1)
        partner = jnp.where((lane & 16) == 0, pltpu.roll(yc, LANES - 16, 1), pltpu.roll(yc, 16, 1))
        outs.append(yc * cos + partner * sin)
    return outs[0] if len(outs) == 1 else jnp.concatenate(outs, axis=1)


def _rope_tables(n_tok):
    t = np.arange(n_tok)
    row, col = t // GRID_W, t % GRID_W
    nf = A_DH // 4
    inv = (ROPE_BASE ** (-np.arange(nf, dtype=np.float32) * 2.0 / (A_DH // 2))).astype(np.float32)
    lane = np.arange(A_DH)
    pos = np.where(lane[None, :] < A_DH // 2, row[:, None], col[:, None]).astype(np.float32)
    ang = pos * inv[lane % nf][None, :]
    sign = np.where((lane % (2 * nf)) < nf, -1.0, 1.0)[None, :]
    cos = np.tile(np.cos(ang), (1, LANES // A_DH)).astype(np.float32)
    sin = np.tile(np.sin(ang) * sign, (1, LANES // A_DH)).astype(np.float32)
    return jnp.asarray(cos), jnp.asarray(sin)


def _seg_matrix(width):
    i = np.arange(MXU_DIM)
    return jnp.asarray(((i[:, None] // width) == (i[None, :] // width)).astype(np.float32) / width, dtype=BF16)


def _ada_kernel(c_ref, w_ref, b_ref, o_ref):
    c = c_ref[...]
    s = c * jax.nn.sigmoid(c)
    o_ref[...] = jnp.dot(s.astype(BF16), w_ref[...].astype(BF16), preferred_element_type=F32) + b_ref[...]


def _ada(cvec, w_ada, b_ada):
    tn = 1024
    n = w_ada.shape[1]
    return pl.pallas_call(
        _ada_kernel,
        grid=(n // tn,),
        in_specs=[pl.BlockSpec((8, D_MODEL), lambda j: (0, 0)),
                  pl.BlockSpec((D_MODEL, tn), lambda j: (0, j)),
                  pl.BlockSpec((1, tn), lambda j: (0, j))],
        out_specs=pl.BlockSpec((8, tn), lambda j: (0, j)),
        out_shape=jax.ShapeDtypeStruct((8, n), F32),
        compiler_params=_cparams(("arbitrary",)),
        name="ada_mod",
    )(cvec, w_ada, b_ada)


def _expand_kv(ckv_n, kr, wukv_ref, kng, km_ref, bv_ref):
    kv = jnp.dot(ckv_n.astype(BF16), wukv_ref[...], preferred_element_type=F32)
    krb = kr[:, :B_ROPE].astype(BF16)
    for h in range(B_HEADS):
        base = h * (B_NOPE + B_DV)
        kn = _rms(kv[:, base:base + B_NOPE], kng)
        km_ref[h] = jnp.concatenate([kn.astype(BF16), krb], axis=1)
        bv_ref[h] = kv[:, base + B_NOPE:base + B_NOPE + B_DV].astype(BF16)


def _in_kernel(sample, *refs):
    (x_ref, sh_ref, sc_ref, n1g_ref, win_ref, aqg_ref, akg_ref, qag_ref, wuq_ref, qng_ref, qrg_ref,
     kvag_ref, wukv_ref, kng_ref, krg_ref, seg_ref) = refs[:16]
    refs = refs[16:]
    if sample:
        cos_ref, sin_ref = refs[:2]
        refs = refs[2:]
        cos, sin = cos_ref[...], sin_ref[...]
    aq_ref, ak_ref, av_ref, qm_ref, km_ref, bv_ref = refs[:6]
    refs = refs[6:]

    x = x_ref[...]
    h = _rms(x, n1g_ref[...]) * (1.0 + sc_ref[...]) + sh_ref[...]
    hb = h.astype(BF16)

    def proj(c0, c1):
        return jnp.dot(hb, win_ref[:, c0:c1], preferred_element_type=F32)

    seg = seg_ref[...]
    aq = _seg_rms(proj(0, A_QK), seg, aqg_ref[...])
    ak = _seg_rms(proj(A_QK, 2 * A_QK), seg, akg_ref[...])
    if sample:
        aq = _rope(aq, cos, sin)
        ak = _rope(ak, cos, sin)
    aq_ref[...] = (aq * (A_DH ** -0.5)).astype(BF16)
    ak_ref[...] = ak.astype(BF16)
    av = proj(2 * A_QK, 2 * A_QK + A_V)
    av_ref[...] = av.astype(BF16)

    c0 = 2 * A_QK + A_V
    q_lat = _rms(proj(c0, c0 + Q_RANK), qag_ref[...])
    qb = jnp.dot(q_lat.astype(BF16), wuq_ref[...], preferred_element_type=F32)
    qr = _seg_rms(qb[:, B_HEADS * B_NOPE:], seg, qrg_ref[...])
    if sample:
        qr = _rope(qr, cos, sin)
    scale = B_QK ** -0.5
    for hd in range(B_HEADS):
        qn = _rms(qb[:, hd * B_NOPE:(hd + 1) * B_NOPE], qng_ref[...])
        qm_ref[hd] = jnp.concatenate(
            [(qn * scale).astype(BF16), (qr[:, hd * B_ROPE:(hd + 1) * B_ROPE] * scale).astype(BF16)], axis=1)

    c0 += Q_RANK
    ckv = _rms(proj(c0, c0 + KV_RANK), kvag_ref[...])
    c0 += KV_RANK
    kr_raw = proj(c0, c0 + LANES)
    kr = kr_raw * lax.rsqrt(jnp.sum(kr_raw * kr_raw, axis=-1, keepdims=True) * (1.0 / B_ROPE) + EPS) * krg_ref[...]
    if sample:
        kr = _rope(kr, cos, sin)
    _expand_kv(ckv, kr, wukv_ref, kng_ref[...], km_ref, bv_ref)
    if not sample:
        ak32_ref, av32_ref, ckv32_ref, kr32_ref = refs
        ak32_ref[...] = ak
        av32_ref[...] = av
        ckv32_ref[...] = ckv
        kr32_ref[...] = kr[:, :B_ROPE]


def _in_proj(x, mod, sample, tiles_per_mod, w, tables):
    m = x.shape[0]
    tm = 256
    nt = m // tm
    row = lambda i: (i, 0)
    modspec = lambda k: pl.BlockSpec((None, 1, D_MODEL), lambda i: (i // tiles_per_mod, 0, k))
    in_specs = [pl.BlockSpec((tm, D_MODEL), row), modspec(0), modspec(1), _resident((1, D_MODEL)),
                _resident((D_MODEL, IN_COLS_PAD)), _resident((1, MXU_DIM)), _resident((1, MXU_DIM)),
                _resident((1, Q_RANK)), _resident((Q_RANK, B_HEADS * B_QK)), _resident((1, B_NOPE)),
                _resident((1, MXU_DIM)), _resident((1, KV_RANK)), _resident((KV_RANK, B_HEADS * (B_NOPE + B_DV))),
                _resident((1, B_NOPE)), _resident((1, LANES)), _resident((MXU_DIM, MXU_DIM))]
    args = [x, mod, mod, w["norm1_gain"], w["w_in"], w["a_q_gain"], w["a_k_gain"], w["q_a_gain"], w["w_uq"],
            w["b_qn_gain"], w["b_qr_gain"], w["kv_a_gain"], w["w_ukv"], w["b_kn_gain"], w["b_kr_gain"], w["seg64"]]
    if sample:
        per = tables[0].shape[0] // tm
        in_specs += [pl.BlockSpec((tm, LANES), lambda i: (i % per, 0))] * 2
        args += list(tables)
    out_shape = [jax.ShapeDtypeStruct((m, A_QK), BF16), jax.ShapeDtypeStruct((m, A_QK), BF16),
                 jax.ShapeDtypeStruct((m, A_V), BF16), jax.ShapeDtypeStruct((B_HEADS, m, B_QK), BF16),
                 jax.ShapeDtypeStruct((B_HEADS, m, B_QK), BF16), jax.ShapeDtypeStruct((B_HEADS, m, B_DV), BF16)]
    hspec = lambda d: pl.BlockSpec((B_HEADS, tm, d), lambda i: (0, i, 0))
    out_specs = [pl.BlockSpec((tm, A_QK), row), pl.BlockSpec((tm, A_QK), row), pl.BlockSpec((tm, A_V), row),
                 hspec(B_QK), hspec(B_QK), hspec(B_DV)]
    if not sample:
        out_shape += [jax.ShapeDtypeStruct((m, A_QK), F32), jax.ShapeDtypeStruct((m, A_V), F32),
                      jax.ShapeDtypeStruct((m, KV_RANK), F32), jax.ShapeDtypeStruct((m, B_ROPE), F32)]
        out_specs += [pl.BlockSpec((tm, A_QK), row), pl.BlockSpec((tm, A_V), row),
                      pl.BlockSpec((tm, KV_RANK), row), pl.BlockSpec((tm, B_ROPE), row)]
    return pl.pallas_call(
        functools.partial(_in_kernel, sample),
        grid=(nt,), in_specs=in_specs, out_specs=out_specs, out_shape=out_shape,
        compiler_params=_cparams(("arbitrary",)),
        name="in_proj_sample" if sample else "in_proj_prompt",
    )(*args)


def _cache_kv_kernel(ckv_ref, kr_ref, wukv_ref, kng_ref, km_ref, bv_ref):
    _expand_kv(ckv_ref[...], kr_ref[...], wukv_ref, kng_ref[...], km_ref, bv_ref)


def _cache_kv(ckv, kr, w):
    m = ckv.shape[0]
    return pl.pallas_call(
        _cache_kv_kernel,
        grid=(1,),
        in_specs=[_resident((m, KV_RANK)), _resident((m, B_ROPE)),
                  _resident((KV_RANK, B_HEADS * (B_NOPE + B_DV))), _resident((1, B_NOPE))],
        out_specs=[pl.BlockSpec((B_HEADS, m, B_QK), lambda i: (0, 0, 0)),
                   pl.BlockSpec((B_HEADS, m, B_DV), lambda i: (0, 0, 0))],
        out_shape=[jax.ShapeDtypeStruct((B_HEADS, m, B_QK), BF16), jax.ShapeDtypeStruct((B_HEADS, m, B_DV), BF16)],
        compiler_params=_cparams(("arbitrary",)),
        name="cache_kv",
    )(ckv, kr, w["w_ukv"], w["b_kn_gain"])


def _qk(q, k):
    return lax.dot_general(q, k, (((1,), (1,)), ((), ())), preferred_element_type=F32)


def _diff_attn_kernel(hb, q_ref, k_ref, v_ref, lq1, lk1, lq2, lk2, sub_ref, o_ref):
    lam = (jnp.exp(jnp.sum(lq1[...] * lk1[...], axis=-1, keepdims=True))
           - jnp.exp(jnp.sum(lq2[...] * lk2[...], axis=-1, keepdims=True)) + LAM_INIT)
    for h in range(hb):
        sl = slice(h * LANES, (h + 1) * LANES)
        q, k, v = q_ref[:, sl], k_ref[:, sl], v_ref[:, sl]
        lane = lax.broadcasted_iota(jnp.int32, q.shape, 1)
        zero = jnp.zeros_like(q)
        s1 = _qk(jnp.where(lane < A_DH, q, zero), k)
        s2 = _qk(jnp.where(lane >= A_DH, q, zero), k)
        e1 = jnp.exp(s1 - jnp.max(s1, axis=-1, keepdims=True))
        e2 = jnp.exp(s2 - jnp.max(s2, axis=-1, keepdims=True))
        r1 = 1.0 / jnp.sum(e1, axis=-1, keepdims=True)
        r2 = lam / jnp.sum(e2, axis=-1, keepdims=True)
        p = e1 * r1 - e2 * r2
        o = jnp.dot(p.astype(BF16), v, preferred_element_type=F32)
        o_ref[:, sl] = (_rms(o, sub_ref[...]) * (1.0 - LAM_INIT)).astype(BF16)


def _diff_attn(q, k, v, nb, lq, lk, tq, hb, w):
    nq = lq // tq
    vec = _resident((1, A_DH))
    return pl.pallas_call(
        functools.partial(_diff_attn_kernel, hb),
        grid=(nb, A_HEADS // hb, nq),
        in_specs=[pl.BlockSpec((tq, hb * LANES), lambda b, g, i: (b * nq + i, g)),
                  pl.BlockSpec((lk, hb * LANES), lambda b, g, i: (b, g)),
                  pl.BlockSpec((lk, hb * LANES), lambda b, g, i: (b, g)),
                  vec, vec, vec, vec, _resident((1, A_DV))],
        out_specs=pl.BlockSpec((tq, hb * LANES), lambda b, g, i: (b * nq + i, g)),
        out_shape=jax.ShapeDtypeStruct((nb * lq, A_V), BF16),
        compiler_params=_cparams(("arbitrary", "arbitrary", "arbitrary")),
        name=f"diff_attn_{lk}",
    )(q, k, v, w["a_lambda_q1"], w["a_lambda_k1"], w["a_lambda_q2"], w["a_lambda_k2"], w["a_sub_gain"])


def _mla_attn_kernel(hb, q_ref, k_ref, v_ref, o_ref):
    for h in range(hb):
        s = _qk(q_ref[h], k_ref[h])
        e = jnp.exp(s - jnp.max(s, axis=-1, keepdims=True))
        r = 1.0 / jnp.sum(e, axis=-1, keepdims=True)
        o = jnp.dot(e.astype(BF16), v_ref[h], preferred_element_type=F32) * r
        o_ref[:, h * LANES:(h + 1) * LANES] = o.astype(BF16)


def _mla_attn(q, k, v, nb, lq, lk, tq, hb):
    nq = lq // tq
    return pl.pallas_call(
        functools.partial(_mla_attn_kernel, hb),
        grid=(nb, B_HEADS // hb, nq),
        in_specs=[pl.BlockSpec((hb, tq, B_QK), lambda b, g, i: (g, b * nq + i, 0)),
                  pl.BlockSpec((hb, lk, B_QK), lambda b, g, i: (g, b, 0)),
                  pl.BlockSpec((hb, lk, B_DV), lambda b, g, i: (g, b, 0))],
        out_specs=pl.BlockSpec((tq, hb * LANES), lambda b, g, i: (b * nq + i, g)),
        out_shape=jax.ShapeDtypeStruct((nb * lq, B_HEADS * B_DV), BF16),
        compiler_params=_cparams(("arbitrary", "arbitrary", "arbitrary")),
        name=f"mla_attn_{lk}",
    )(q, k, v)


def _out_kernel(oa_ref, ob_ref, x_ref, g1_ref, sh2_ref, sc2_ref, n2g_ref, woa_ref, wob_ref, wr_ref,
                x1_ref, h2_ref, aff_ref):
    o = (jnp.dot(oa_ref[...], woa_ref[...], preferred_element_type=F32)
         + jnp.dot(ob_ref[...], wob_ref[...], preferred_element_type=F32))
    x1 = x_ref[...] + g1_ref[...] * o
    x1_ref[...] = x1
    h2 = _rms(x1, n2g_ref[...]) * (1.0 + sc2_ref[...]) + sh2_ref[...]
    h2_ref[...] = h2
    logits = jnp.dot(h2.astype(BF16), wr_ref[...], preferred_element_type=F32)
    lane = lax.broadcasted_iota(jnp.int32, logits.shape, 1)
    logits = jnp.where(lane < N_EXPERTS, logits, -jnp.inf)
    e = jnp.exp(logits - jnp.max(logits, axis=-1, keepdims=True))
    aff_ref[...] = e / jnp.sum(e, axis=-1, keepdims=True)


def _out_proj(oa, ob, x, mod, tiles_per_mod, w, tag):
    m = x.shape[0]
    tm = 256
    row = lambda i: (i, 0)
    modspec = lambda k: pl.BlockSpec((None, 1, D_MODEL), lambda i: (i // tiles_per_mod, 0, k))
    return pl.pallas_call(
        _out_kernel,
        grid=(m // tm,),
        in_specs=[pl.BlockSpec((tm, A_V), row), pl.BlockSpec((tm, A_V), row), pl.BlockSpec((tm, D_MODEL), row),
                  modspec(2), modspec(3), modspec(4), _resident((1, D_MODEL)),
                  _resident((A_V, D_MODEL)), _resident((A_V, D_MODEL)), _resident((D_MODEL, LANES))],
        out_specs=[pl.BlockSpec((tm, D_MODEL), row), pl.BlockSpec((tm, D_MODEL), row),
                   pl.BlockSpec((tm, LANES), row)],
        out_shape=[jax.ShapeDtypeStruct((m, D_MODEL), F32), jax.ShapeDtypeStruct((m, D_MODEL), F32),
                   jax.ShapeDtypeStruct((m, LANES), F32)],
        compiler_params=_cparams(("arbitrary",)),
        name="out_proj_" + tag,
    )(oa, ob, x, mod, mod, mod, w["norm2_gain"], w["w_oa"], w["w_ob"], w["w_router"])


ROUTE_BLK = 256
COMBINE_T = 128
RANK_BITS = 4


def _route_kernel(cap, aff_ref, idx_ref, rk_ref, gate_ref, st_ref, cnt_ref, kmax_ref, pos_scr, pk_scr):
    n = aff_ref.shape[0]
    nblk = n // ROUTE_BLK
    aff = aff_ref[...]
    bits = pltpu.bitcast(aff, jnp.int32)
    lane = lax.broadcasted_iota(jnp.int32, (n, LANES), 1)
    tok = lax.broadcasted_iota(jnp.int32, (n, LANES), 0)

    def bisect(_, carry):
        lo, hi = carry
        mid = lo + ((hi - lo) >> 1)
        ok = jnp.sum((bits >= mid).astype(jnp.int32), axis=0, keepdims=True) >= cap
        return jnp.where(ok, mid, lo), jnp.where(ok, hi, mid)

    lo0 = jnp.zeros((1, LANES), jnp.int32)
    hi0 = jnp.full((1, LANES), 0x7F800001, jnp.int32)
    thr, _ = lax.fori_loop(0, 31, bisect, (lo0, hi0))
    gt = bits > thr
    eq = bits == thr
    need = cap - jnp.sum(gt.astype(jnp.int32), axis=0, keepdims=True)

    r = lax.broadcasted_iota(jnp.int32, (ROUTE_BLK, ROUTE_BLK), 0)
    c = lax.broadcasted_iota(jnp.int32, (ROUTE_BLK, ROUTE_BLK), 1)
    tri = (c < r).astype(BF16)

    def prefix(mask):
        m = mask.astype(F32)
        carry = jnp.zeros((1, LANES), F32)
        outs = []
        for b in range(nblk):
            mb = m[b * ROUTE_BLK:(b + 1) * ROUTE_BLK]
            outs.append(jnp.dot(tri, mb.astype(BF16), preferred_element_type=F32) + carry)
            carry = carry + jnp.sum(mb, axis=0, keepdims=True)
        return jnp.concatenate(outs, axis=0).astype(jnp.int32)

    sel = (gt | (eq & (prefix(eq) < need))) & (lane < N_EXPERTS)
    pos = prefix(sel)
    self_ = sel.astype(BF16)
    er = lax.broadcasted_iota(jnp.int32, (LANES, LANES), 0)
    ec = lax.broadcasted_iota(jnp.int32, (LANES, LANES), 1)
    rank = jnp.dot(self_, (er < ec).astype(BF16), preferred_element_type=F32).astype(jnp.int32)
    cnt = jnp.dot(self_, jnp.ones((LANES, LANES), BF16), preferred_element_type=F32).astype(jnp.int32)
    cnt_ref[...] = cnt
    nt = n // COMBINE_T
    kmax_ref[...] = jnp.max(cnt.reshape(nt, COMBINE_T, LANES), axis=1)

    pos_scr[...] = pos
    st_ref[...] = jnp.zeros(st_ref.shape, jnp.int32)
    st_ref[0:nt, :] = pos_scr[pl.ds(0, nt, stride=COMBINE_T), :]
    st_ref[nt:nt + 1, :] = jnp.full((1, LANES), cap, jnp.int32)
    pos_scr[...] = jnp.where(sel, pos, -1)
    pk_scr[...] = (tok << RANK_BITS) + rank

    jio = lax.broadcasted_iota(jnp.int32, (ROUTE_BLK, cap), 1)
    for e in range(N_EXPERTS):
        def blk(b, carry):
            acc_p, acc_g = carry
            rows = pl.ds(pl.multiple_of(b * ROUTE_BLK, ROUTE_BLK), ROUTE_BLK)
            oh = pos_scr[rows, :][:, e:e + 1] == jio
            pk = jnp.where(oh, pk_scr[rows, :][:, e:e + 1], 0)
            g = jnp.where(oh, aff_ref[rows, :][:, e:e + 1], 0.0)
            return (acc_p + jnp.sum(pk.reshape(ROUTE_BLK // 8, 8, cap), axis=0),
                    acc_g + jnp.sum(g.reshape(ROUTE_BLK // 8, 8, cap), axis=0))

        acc_p, acc_g = lax.fori_loop(0, nblk, blk, (jnp.zeros((8, cap), jnp.int32), jnp.zeros((8, cap), F32)))
        packed = jnp.sum(acc_p, axis=0, keepdims=True)
        idx_ref[e:e + 1, :] = packed >> RANK_BITS
        rk_ref[e:e + 1, :] = packed & (N_EXPERTS - 1)
        gate_ref[e:e + 1, :] = jnp.sum(acc_g, axis=0, keepdims=True)


def _route(aff, tag):
    n = aff.shape[0]
    cap = CAPACITY_FACTOR * n // N_EXPERTS
    nt = n // COMBINE_T
    full = lambda s: pl.BlockSpec(s, lambda i: (0,) * len(s))
    return pl.pallas_call(
        functools.partial(_route_kernel, cap),
        grid=(1,),
        in_specs=[full((n, LANES))],
        out_specs=[full((N_EXPERTS, cap)), full((N_EXPERTS, cap)), full((N_EXPERTS, cap)),
                   full((nt + 8, LANES)), full((n, LANES)), full((nt, LANES))],
        out_shape=[jax.ShapeDtypeStruct((N_EXPERTS, cap), jnp.int32), jax.ShapeDtypeStruct((N_EXPERTS, cap), jnp.int32),
                   jax.ShapeDtypeStruct((N_EXPERTS, cap), F32), jax.ShapeDtypeStruct((nt + 8, LANES), jnp.int32),
                   jax.ShapeDtypeStruct((n, LANES), jnp.int32), jax.ShapeDtypeStruct((nt, LANES), jnp.int32)],
        scratch_shapes=[pltpu.VMEM((n, LANES), jnp.int32), pltpu.VMEM((n, LANES), jnp.int32)],
        compiler_params=_cparams(("arbitrary",)),
        name="route_" + tag,
    )(aff)


FF_CHUNK = 256


def _ffn_kernel(cap, nf, per, idx_s, hp_hbm, hs_hbm, gate_ref, w1_ref, w3_ref, w2_ref, yp_hbm, ys_hbm,
                xg, xe, acc, gsem, osem):
    e = pl.program_id(0)
    f = pl.program_id(1)
    ne = pl.num_programs(0)
    srcs = (hp_hbm, hs_hbm)
    dsts = (yp_hbm, ys_hbm)
    slot = lax.rem(e, 2)

    def issue_part(ex, part, sl, live):
        for p in range(2):
            for i in range(per):
                j = part * per + i
                jc = jnp.minimum(j, cap - 1)
                tok = idx_s[(p * N_EXPERTS + ex) * cap + jc]

                @pl.when(live & (j < cap))
                def _():
                    pltpu.make_async_copy(srcs[p].at[pl.ds(tok, 1), :], xg.at[sl, pl.ds(p * cap + jc, 1), :],
                                          gsem.at[sl]).start()

    def out_copy(p, ex):
        return pltpu.make_async_copy(acc.at[pl.ds(p * cap, cap), :],
                                     dsts[p].at[pl.ds(pl.multiple_of(ex * cap, cap), cap), :], osem)

    @pl.when((e == 0) & (f == 0))
    def _prologue():
        def part_body(part, carry):
            issue_part(0, part, 0, True)
            return carry
        lax.fori_loop(0, nf, part_body, 0)

    @pl.when(f == 0)
    def _start_expert():
        for p in range(2):
            pltpu.make_async_copy(srcs[p].at[pl.ds(0, cap), :], xg.at[slot, pl.ds(p * cap, cap), :],
                                  gsem.at[slot]).wait()
        xe[...] = xg[slot].astype(BF16)

    issue_part(jnp.minimum(e + 1, ne - 1), f, 1 - slot, e + 1 < ne)

    x = xe[...]
    a = jnp.dot(x, w1_ref[...].astype(BF16), preferred_element_type=F32)
    b = jnp.dot(x, w3_ref[...].astype(BF16), preferred_element_type=F32)
    hid = ((a * jax.nn.sigmoid(a)) * b).astype(BF16)

    def down():
        return jnp.dot(hid, w2_ref[...].astype(BF16), preferred_element_type=F32)

    @pl.when(f == 0)
    def _first():
        @pl.when(e > 0)
        def _():
            for p in range(2):
                out_copy(p, e - 1).wait()
        acc[...] = down()

    @pl.when((f > 0) & (f < nf - 1))
    def _middle():
        acc[...] += down()

    @pl.when(f == nf - 1)
    def _last():
        acc[...] = (acc[...] + down()) * gate_ref[...]
        for p in range(2):
            out_copy(p, e).start()

        @pl.when(e == ne - 1)
        def _():
            for p in range(2):
                out_copy(p, e).wait()


def _ffn(idx_flat, h2p, h2s, gate, w1, w3, w2, cap):
    nf = EXPERT_FF // FF_CHUNK
    per = -(-cap // nf)
    grid_spec = pltpu.PrefetchScalarGridSpec(
        num_scalar_prefetch=1,
        grid=(N_EXPERTS, nf),
        in_specs=[pl.BlockSpec(memory_space=pl.ANY), pl.BlockSpec(memory_space=pl.ANY),
                  pl.BlockSpec((None, 2 * cap, 1), lambda e, f, *_: (e, 0, 0)),
                  pl.BlockSpec((None, D_MODEL, FF_CHUNK), lambda e, f, *_: (e, 0, f)),
                  pl.BlockSpec((None, D_MODEL, FF_CHUNK), lambda e, f, *_: (e, 0, f)),
                  pl.BlockSpec((None, FF_CHUNK, D_MODEL), lambda e, f, *_: (e, f, 0))],
        out_specs=[pl.BlockSpec(memory_space=pl.ANY), pl.BlockSpec(memory_space=pl.ANY)],
        scratch_shapes=[pltpu.VMEM((2, 2 * cap, D_MODEL), F32), pltpu.VMEM((2 * cap, D_MODEL), BF16),
                        pltpu.VMEM((2 * cap, D_MODEL), F32), pltpu.SemaphoreType.DMA((2,)),
                        pltpu.SemaphoreType.DMA(())])
    out = jax.ShapeDtypeStruct((N_EXPERTS * cap, D_MODEL), F32)
    return pl.pallas_call(
        functools.partial(_ffn_kernel, cap, nf, per),
        grid_spec=grid_spec, out_shape=[out, out],
        compiler_params=_cparams(("arbitrary", "arbitrary")),
        name="expert_ffn",
    )(idx_flat, h2p, h2s, gate, w1, w3, w2)


COMBINE_CW = 256
WAIT_ROWS = 32


def _combine_kernel(cap, idx_s, rk_s, st_s, km_s, ye_hbm, x1_ref, g2_ref, cnt_ref, o_ref, buf, sem):
    t = pl.program_id(0)
    nt = pl.num_programs(0)
    tt = COMBINE_T
    slot = lax.rem(t, 2)

    def issue_tile(tile, sl):
        base = tile * tt
        for e in range(N_EXPERTS):
            lo = st_s[tile * LANES + e]
            hi = st_s[(tile + 1) * LANES + e]

            def issue(j, carry):
                tok = idx_s[e * cap + j]
                row = rk_s[e * cap + j] * tt + tok - base
                pltpu.make_async_copy(ye_hbm.at[pl.ds(e * cap + j, 1), :], buf.at[sl, pl.ds(row, 1), :],
                                      sem.at[sl]).start()
                return carry
            lax.fori_loop(lo, hi, issue, 0)

    def wait_tile(tile, sl):
        total = 0
        for e in range(N_EXPERTS):
            total = total + st_s[(tile + 1) * LANES + e] - st_s[tile * LANES + e]

        def wait_rows(n):
            def body(i, carry):
                pltpu.make_async_copy(ye_hbm.at[pl.ds(0, n), :], buf.at[sl, pl.ds(0, n), :], sem.at[sl]).wait()
                return carry
            return body
        lax.fori_loop(0, total // WAIT_ROWS, wait_rows(WAIT_ROWS), 0)
        lax.fori_loop(0, lax.rem(total, WAIT_ROWS), wait_rows(1), 0)

    @pl.when(t == 0)
    def _init():
        buf[...] = jnp.zeros(buf.shape, F32)
        issue_tile(0, 0)

    @pl.when(t + 1 < nt)
    def _prefetch():
        issue_tile(t + 1, 1 - slot)

    wait_tile(t, slot)

    kmax = km_s[t * LANES]
    cnt = jnp.tile(cnt_ref[...], (1, COMBINE_CW // LANES))
    for c in range(D_MODEL // COMBINE_CW):
        cs = slice(c * COMBINE_CW, (c + 1) * COMBINE_CW)

        def add_slot(k, a):
            v = buf[slot, pl.ds(pl.multiple_of(k * tt, tt), tt), cs]
            return a + jnp.where(cnt > k, v, 0.0)
        a = lax.fori_loop(0, kmax, add_slot, jnp.zeros((tt, COMBINE_CW), F32))
        o_ref[:, cs] = x1_ref[:, cs] + g2_ref[:, cs] * a


def _combine(idx, rk, st, km, ye, x1, mod, tiles_per_mod, cnt, cap, tag):
    n = x1.shape[0]
    tt = COMBINE_T
    grid_spec = pltpu.PrefetchScalarGridSpec(
        num_scalar_prefetch=4,
        grid=(n // tt,),
        in_specs=[pl.BlockSpec(memory_space=pl.ANY),
                  pl.BlockSpec((tt, D_MODEL), lambda i, *_: (i, 0)),
                  pl.BlockSpec((None, 1, D_MODEL), lambda i, *_: (i // tiles_per_mod, 0, 5)),
                  pl.BlockSpec((tt, LANES), lambda i, *_: (i, 0))],
        out_specs=pl.BlockSpec((tt, D_MODEL), lambda i, *_: (i, 0)),
        scratch_shapes=[pltpu.VMEM((2, N_EXPERTS * tt, D_MODEL), F32), pltpu.SemaphoreType.DMA((2,))])
    return pl.pallas_call(
        functools.partial(_combine_kernel, cap),
        grid_spec=grid_spec, out_shape=jax.ShapeDtypeStruct((n, D_MODEL), F32),
        compiler_params=_cparams(("arbitrary",)),
        name="combine_" + tag,
    )(idx.reshape(-1), rk.reshape(-1), st.reshape(-1), km.reshape(-1), ye, x1, mod, cnt)


def _prep_weights(norm1_gain, norm2_gain, w_in, a_q_gain, a_k_gain, a_lambda_q1, a_lambda_k1, a_lambda_q2,
                  a_lambda_k2, a_sub_gain, q_a_gain, w_uq, b_qn_gain, b_qr_gain, kv_a_gain, w_ukv, b_kn_gain,
                  b_kr_gain, w_o, w_router):
    tile = lambda g, width: jnp.tile(g.reshape(1, -1), (1, width // g.shape[-1]))
    wuq = w_uq[0].reshape(Q_RANK, B_HEADS, B_QK)
    wuq = jnp.concatenate([wuq[:, :, :B_NOPE].reshape(Q_RANK, -1), wuq[:, :, B_NOPE:].reshape(Q_RANK, -1)], axis=1)
    return {
        "norm1_gain": norm1_gain.reshape(1, -1), "norm2_gain": norm2_gain.reshape(1, -1),
        "w_in": jnp.pad(w_in[0], ((0, 0), (0, IN_COLS_PAD - IN_COLS))).astype(BF16),
        "a_q_gain": tile(a_q_gain, MXU_DIM), "a_k_gain": tile(a_k_gain, MXU_DIM),
        "a_lambda_q1": a_lambda_q1.reshape(1, -1), "a_lambda_k1": a_lambda_k1.reshape(1, -1),
        "a_lambda_q2": a_lambda_q2.reshape(1, -1), "a_lambda_k2": a_lambda_k2.reshape(1, -1),
        "a_sub_gain": a_sub_gain.reshape(1, -1), "q_a_gain": q_a_gain.reshape(1, -1),
        "w_uq": wuq.astype(BF16), "b_qn_gain": b_qn_gain.reshape(1, -1), "b_qr_gain": tile(b_qr_gain, MXU_DIM),
        "kv_a_gain": kv_a_gain.reshape(1, -1), "w_ukv": w_ukv[0].astype(BF16),
        "b_kn_gain": b_kn_gain.reshape(1, -1), "b_kr_gain": tile(b_kr_gain, LANES),
        "w_oa": w_o[0, :A_V].astype(BF16), "w_ob": w_o[0, A_V:].astype(BF16),
        "w_router": jnp.pad(w_router[0], ((0, 0), (0, LANES - N_EXPERTS))).astype(BF16),
        "seg64": _seg_matrix(A_DH),
    }


def kernel(x_prompt, x_sample, cache_diff_k, cache_diff_v, cache_mla_ckv, cache_mla_krope, c, c_ctx, w_ada, b_ada, norm1_gain, norm2_gain, w_in, a_q_gain, a_k_gain, a_lambda_q1, a_lambda_k1, a_lambda_q2, a_lambda_k2, a_sub_gain, q_a_gain, w_uq, b_qn_gain, b_qr_gain, kv_a_gain, w_ukv, b_kn_gain, b_kr_gain, w_o, w_router, w_exp1, w_exp3, w_exp2):
    nbp, lp, _ = x_prompt.shape
    nbs, ls, _ = x_sample.shape
    past = cache_diff_k.shape[2]
    w = _prep_weights(norm1_gain, norm2_gain, w_in, a_q_gain, a_k_gain, a_lambda_q1, a_lambda_k1, a_lambda_q2,
                      a_lambda_k2, a_sub_gain, q_a_gain, w_uq, b_qn_gain, b_qr_gain, kv_a_gain, w_ukv, b_kn_gain,
                      b_kr_gain, w_o, w_router)

    cvec = jnp.concatenate([c_ctx[None], c, jnp.zeros((8 - 1 - nbs, D_MODEL), F32)], axis=0)
    mod = _ada(cvec, w_ada[0], b_ada)
    mod_p = mod[0:1].reshape(1, 1, -1)
    mod_s = mod[1:1 + nbs].reshape(nbs, 1, -1)

    xp = x_prompt.reshape(nbp * lp, D_MODEL)
    xs = x_sample.reshape(nbs * ls, D_MODEL)
    tm = 256
    aq_p, ak_p, av_p, qm_p, km_p, bv_p, ak32, av32, ckv32, kr32 = _in_proj(xp, mod_p, False, nbp * lp // tm, w, None)
    aq_s, ak_s, av_s, qm_s, km_s, bv_s = _in_proj(xs, mod_s, True, ls // tm, w, _rope_tables(ls))
    km_c, bv_c = _cache_kv(cache_mla_ckv.reshape(nbs * past, KV_RANK), cache_mla_krope.reshape(nbs * past, B_ROPE), w)

    def with_cache(new, cache, lead):
        new = new.reshape(lead + (nbs, ls, new.shape[-1]))
        cache = cache.reshape(lead + (nbs, past, new.shape[-1]))
        return jnp.concatenate([new, cache], axis=len(lead) + 1).reshape(lead + (nbs * (ls + past), new.shape[-1]))

    ak_all = with_cache(ak_s, cache_diff_k.astype(BF16), ())
    av_all = with_cache(av_s, cache_diff_v.astype(BF16), ())
    km_all = with_cache(km_s, km_c, (B_HEADS,))
    bv_all = with_cache(bv_s, bv_c, (B_HEADS,))

    oa_p = _diff_attn(aq_p, ak_p, av_p, nbp, lp, lp, lp, A_HEADS, w)
    ob_p = _mla_attn(qm_p, km_p, bv_p, nbp, lp, lp, lp, B_HEADS)
    oa_s = _diff_attn(aq_s, ak_all, av_all, nbs, ls, ls + past, 256, 1, w)
    ob_s = _mla_attn(qm_s, km_all, bv_all, nbs, ls, ls + past, 256, 1)

    x1_p, h2_p, aff_p = _out_proj(oa_p, ob_p, xp, mod_p, nbp * lp // tm, w, "prompt")
    x1_s, h2_s, aff_s = _out_proj(oa_s, ob_s, xs, mod_s, ls // tm, w, "sample")

    idx_p, rk_p, gate_p, st_p, cnt_p, km_p_ = _route(aff_p, "prompt")
    idx_s, rk_s, gate_s, st_s, cnt_s, km_s_ = _route(aff_s, "sample")
    cap = idx_p.shape[1]
    idx_flat = jnp.stack([idx_p, idx_s]).reshape(-1)
    gate = jnp.concatenate([gate_p, gate_s], axis=1)[..., None]
    ye_p, ye_s = _ffn(idx_flat, h2_p, h2_s, gate, w_exp1[0], w_exp3[0], w_exp2[0], cap)

    y_p = _combine(idx_p, rk_p, st_p, km_p_, ye_p, x1_p, mod_p, nbp * lp // COMBINE_T, cnt_p, cap, "prompt")
    y_s = _combine(idx_s, rk_s, st_s, km_s_, ye_s, x1_s, mod_s, ls // COMBINE_T, cnt_s, cap, "sample")

    return (y_p.reshape(nbp, lp, D_MODEL), y_s.reshape(nbs, ls, D_MODEL),
            ak32.reshape(nbp, 1, lp, A_HEADS, 2, A_DH), av32.reshape(nbp, 1, lp, A_HEADS, A_DV),
            ckv32.reshape(nbp, 1, lp, KV_RANK), kr32.reshape(nbp, 1, lp, B_ROPE))
```

```python
import functools
import math

import numpy as np
import jax
import jax.numpy as jnp
from jax import lax
from jax.experimental import pallas as pl
from jax.experimental.pallas import tpu as pltpu

F32 = jnp.float32
BF16 = jnp.bfloat16

D_MODEL = 2048
GRID_W = 64
A_HEADS = 8
A_DH = 64
A_DV = 128
B_HEADS = 8
B_NOPE = 128
B_ROPE = 64
B_DV = 128
Q_RANK = 512
KV_RANK = 256
N_EXPERTS = 16
EXPERT_FF = 1536
CAPACITY_FACTOR = 2
ROPE_BASE = 10000.0
EPS = 1e-6
LAM_INIT = 0.8 - 0.6 * math.exp(-0.3 * 0)
LOG2E = math.log2(math.e)

A_QK = A_HEADS * 2 * A_DH
A_V = A_HEADS * A_DV
B_QK = B_NOPE + B_ROPE
IN_COLS = 2 * A_QK + A_V + Q_RANK + KV_RANK + B_ROPE
LANES = 128
MXU_DIM = 256
VMEM_LIMIT = 56 * 1024 * 1024


def _cparams(sem):
    return pltpu.CompilerParams(dimension_semantics=sem, vmem_limit_bytes=VMEM_LIMIT)


def _resident(shape):
    nd = len(shape)
    return pl.BlockSpec(shape, lambda *_: (0,) * nd, pipeline_mode=pl.Buffered(1))


def _rms(x, gain):
    return x * lax.rsqrt(jnp.mean(x * x, axis=-1, keepdims=True) + EPS) * gain


def _seg_rms(y, segmat, gain):
    y2 = (y * y).astype(BF16)
    outs = []
    for c in range(y.shape[1] // MXU_DIM):
        sl = slice(c * MXU_DIM, (c + 1) * MXU_DIM)
        ms = jnp.dot(y2[:, sl], segmat, preferred_element_type=F32)
        outs.append(y[:, sl] * lax.rsqrt(ms + EPS) * gain)
    return jnp.concatenate(outs, axis=1)


def _rope(y, cos, sin):
    outs = []
    for c in range(y.shape[1] // LANES):
        yc = y[:, c * LANES:(c + 1) * LANES]
        lane = lax.broadcasted_iota(jnp.int32, yc.shape, 1)
        partner = jnp.where((lane & 16) == 0, pltpu.roll(yc, LANES - 16, 1), pltpu.roll(yc, 16, 1))
        outs.append(yc * cos + partner * sin)
    return outs[0] if len(outs) == 1 else jnp.concatenate(outs, axis=1)


def _rope_tables(n_tok):
    t = np.arange(n_tok)
    row, col = t // GRID_W, t % GRID_W
    nf = A_DH // 4
    inv = ROPE_BASE ** (-np.arange(nf, dtype=np.float64) * 2.0 / (A_DH // 2))
    lane = np.arange(A_DH)
    pos = np.where(lane[None, :] < A_DH // 2, row[:, None], col[:, None]).astype(np.float64)
    ang = pos * inv[lane % nf][None, :]
    sign = np.where((lane % (2 * nf)) < nf, -1.0, 1.0)[None, :]
    cos = np.tile(np.cos(ang), (1, LANES // A_DH)).astype(np.float32)
    sin = np.tile(np.sin(ang) * sign, (1, LANES // A_DH)).astype(np.float32)
    return jnp.asarray(cos), jnp.asarray(sin)


def _seg_matrix(width):
    i = np.arange(MXU_DIM)
    return jnp.asarray(((i[:, None] // width) == (i[None, :] // width)).astype(np.float32) / width, dtype=BF16)


def _ada_kernel(c_ref, w_ref, b_ref, o_ref):
    c = c_ref[...]
    s = c * jax.nn.sigmoid(c)
    o_ref[...] = jnp.dot(s.astype(BF16), w_ref[...].astype(BF16), preferred_element_type=F32) + b_ref[...]


def _ada(cvec, w_ada, b_ada):
    tn = 1024
    n = w_ada.shape[1]
    return pl.pallas_call(
        _ada_kernel,
        grid=(n // tn,),
        in_specs=[pl.BlockSpec((8, D_MODEL), lambda j: (0, 0)),
                  pl.BlockSpec((D_MODEL, tn), lambda j: (0, j)),
                  pl.BlockSpec((1, tn), lambda j: (0, j))],
        out_specs=pl.BlockSpec((8, tn), lambda j: (0, j)),
        out_shape=jax.ShapeDtypeStruct((8, n), F32),
        compiler_params=_cparams(("arbitrary",)),
        name="ada_mod",
    )(cvec, w_ada, b_ada)


def _expand_kv(ckv_n, kr, wukv_ref, kng, km_ref, bv_ref):
    kv = jnp.dot(ckv_n.astype(BF16), wukv_ref[...], preferred_element_type=F32)
    krb = kr[:, :B_ROPE].astype(BF16)
    for h in range(B_HEADS):
        base = h * (B_NOPE + B_DV)
        kn = _rms(kv[:, base:base + B_NOPE], kng)
        km_ref[h] = jnp.concatenate([kn.astype(BF16), krb], axis=1)
        bv_ref[h] = kv[:, base + B_NOPE:base + B_NOPE + B_DV].astype(BF16)


def _in_kernel(sample, *refs):
    (x_ref, sh_ref, sc_ref, n1g_ref, win_ref, aqg_ref, akg_ref, qag_ref, wuq_ref, qng_ref, qrg_ref,
     kvag_ref, wukv_ref, kng_ref, krg_ref, seg_ref) = refs[:16]
    refs = refs[16:]
    if sample:
        cos_ref, sin_ref = refs[:2]
        refs = refs[2:]
        cos, sin = cos_ref[...], sin_ref[...]
    aq_ref, ak_ref, av_ref, qm_ref, km_ref, bv_ref = refs[:6]
    refs = refs[6:]

    x = x_ref[...]
    h = _rms(x, n1g_ref[...]) * (1.0 + sc_ref[...]) + sh_ref[...]
    hb = h.astype(BF16)

    def proj(c0, c1):
        return jnp.dot(hb, win_ref[:, c0:c1], preferred_element_type=F32)

    seg = seg_ref[...]
    aq = _seg_rms(proj(0, A_QK), seg, aqg_ref[...])
    ak = _seg_rms(proj(A_QK, 2 * A_QK), seg, akg_ref[...])
    if sample:
        aq = _rope(aq, cos, sin)
        ak = _rope(ak, cos, sin)
    aq_ref[...] = (aq * (A_DH ** -0.5 * LOG2E)).astype(BF16)
    ak_ref[...] = ak.astype(BF16)
    av = proj(2 * A_QK, 2 * A_QK + A_V)
    av_ref[...] = av.astype(BF16)

    c0 = 2 * A_QK + A_V
    q_lat = _rms(proj(c0, c0 + Q_RANK), qag_ref[...])
    qb = jnp.dot(q_lat.astype(BF16), wuq_ref[...], preferred_element_type=F32)
    qr = _seg_rms(qb[:, B_HEADS * B_NOPE:], seg, qrg_ref[...])
    if sample:
        qr = _rope(qr, cos, sin)
    scale = B_QK ** -0.5 * LOG2E
    for hd in range(B_HEADS):
        qn = _rms(qb[:, hd * B_NOPE:(hd + 1) * B_NOPE], qng_ref[...])
        qm_ref[hd] = jnp.concatenate(
            [(qn * scale).astype(BF16), (qr[:, hd * B_ROPE:(hd + 1) * B_ROPE] * scale).astype(BF16)], axis=1)

    c0 += Q_RANK
    ckv = _rms(proj(c0, c0 + KV_RANK), kvag_ref[...])
    c0 += KV_RANK
    kr64 = proj(c0, c0 + B_ROPE)
    kr_raw = jnp.concatenate([kr64, jnp.zeros_like(kr64)], axis=1)
    kr = kr_raw * lax.rsqrt(jnp.sum(kr_raw * kr_raw, axis=-1, keepdims=True) * (1.0 / B_ROPE) + EPS) * krg_ref[...]
    if sample:
        kr = _rope(kr, cos, sin)
    _expand_kv(ckv, kr, wukv_ref, kng_ref[...], km_ref, bv_ref)
    if not sample:
        ak32_ref, av32_ref, ckv32_ref, kr32_ref = refs
        ak32_ref[...] = ak
        av32_ref[...] = av
        ckv32_ref[...] = ckv
        kr32_ref[...] = kr[:, :B_ROPE]


def _in_proj(x, mod, sample, tiles_per_mod, w, tables):
    m = x.shape[0]
    tm = 256
    nt = m // tm
    row = lambda i: (i, 0)
    modspec = lambda k: pl.BlockSpec((None, 1, D_MODEL), lambda i: (i // tiles_per_mod, 0, k))
    in_specs = [pl.BlockSpec((tm, D_MODEL), row), modspec(0), modspec(1), _resident((1, D_MODEL)),
                _resident((D_MODEL, IN_COLS)), _resident((1, MXU_DIM)), _resident((1, MXU_DIM)),
                _resident((1, Q_RANK)), _resident((Q_RANK, B_HEADS * B_QK)), _resident((1, B_NOPE)),
                _resident((1, MXU_DIM)), _resident((1, KV_RANK)), _resident((KV_RANK, B_HEADS * (B_NOPE + B_DV))),
                _resident((1, B_NOPE)), _resident((1, LANES)), _resident((MXU_DIM, MXU_DIM))]
    args = [x, mod, mod, w["norm1_gain"], w["w_in"], w["a_q_gain"], w["a_k_gain"], w["q_a_gain"], w["w_uq"],
            w["b_qn_gain"], w["b_qr_gain"], w["kv_a_gain"], w["w_ukv"], w["b_kn_gain"], w["b_kr_gain"], w["seg64"]]
    if sample:
        per = tables[0].shape[0] // tm
        in_specs += [pl.BlockSpec((tm, LANES), lambda i: (i % per, 0))] * 2
        args += list(tables)
    out_shape = [jax.ShapeDtypeStruct((m, A_QK), BF16), jax.ShapeDtypeStruct((m, A_QK), BF16),
                 jax.ShapeDtypeStruct((m, A_V), BF16), jax.ShapeDtypeStruct((B_HEADS, m, B_QK), BF16),
                 jax.ShapeDtypeStruct((B_HEADS, m, B_QK), BF16), jax.ShapeDtypeStruct((B_HEADS, m, B_DV), BF16)]
    hspec = lambda d: pl.BlockSpec((B_HEADS, tm, d), lambda i: (0, i, 0))
    out_specs = [pl.BlockSpec((tm, A_QK), row), pl.BlockSpec((tm, A_QK), row), pl.BlockSpec((tm, A_V), row),
                 hspec(B_QK), hspec(B_QK), hspec(B_DV)]
    if not sample:
        out_shape += [jax.ShapeDtypeStruct((m, A_QK), F32), jax.ShapeDtypeStruct((m, A_V), F32),
                      jax.ShapeDtypeStruct((m, KV_RANK), F32), jax.ShapeDtypeStruct((m, B_ROPE), F32)]
        out_specs += [pl.BlockSpec((tm, A_QK), row), pl.BlockSpec((tm, A_V), row),
                      pl.BlockSpec((tm, KV_RANK), row), pl.BlockSpec((tm, B_ROPE), row)]
    return pl.pallas_call(
        functools.partial(_in_kernel, sample),
        grid=(nt,), in_specs=in_specs, out_specs=out_specs, out_shape=out_shape,
        compiler_params=_cparams(("arbitrary",)),
        name="in_proj_sample" if sample else "in_proj_prompt",
    )(*args)


def _cache_kv_kernel(ckv_ref, kr_ref, wukv_ref, kng_ref, km_ref, bv_ref):
    _expand_kv(ckv_ref[...], kr_ref[...], wukv_ref, kng_ref[...], km_ref, bv_ref)


def _cache_kv(ckv, kr, w):
    m = ckv.shape[0]
    return pl.pallas_call(
        _cache_kv_kernel,
        grid=(1,),
        in_specs=[_resident((m, KV_RANK)), _resident((m, B_ROPE)),
                  _resident((KV_RANK, B_HEADS * (B_NOPE + B_DV))), _resident((1, B_NOPE))],
        out_specs=[pl.BlockSpec((B_HEADS, m, B_QK), lambda i: (0, 0, 0)),
                   pl.BlockSpec((B_HEADS, m, B_DV), lambda i: (0, 0, 0))],
        out_shape=[jax.ShapeDtypeStruct((B_HEADS, m, B_QK), BF16), jax.ShapeDtypeStruct((B_HEADS, m, B_DV), BF16)],
        compiler_params=_cparams(("arbitrary",)),
        name="cache_kv",
    )(ckv, kr, w["w_ukv"], w["b_kn_gain"])


def _qk(q, k):
    return lax.dot_general(q, k, (((1,), (1,)), ((), ())), preferred_element_type=F32)


def _scores(q, k, kc):
    s = _qk(q, k)
    return s if kc is None else jnp.concatenate([s, _qk(q, kc)], axis=1)


def _weighted(p, v, vc):
    if vc is None:
        return jnp.dot(p, v, preferred_element_type=F32)
    lk = v.shape[0]
    return (jnp.dot(p[:, :lk], v, preferred_element_type=F32)
            + jnp.dot(p[:, lk:], vc, preferred_element_type=F32))


def _diff_attn_kernel(hb, cached, q_ref, k_ref, v_ref, *refs):
    if cached:
        kc_ref, vc_ref = refs[:2]
        refs = refs[2:]
    lq1, lk1, lq2, lk2, sub_ref, o_ref = refs
    lam = (jnp.exp(jnp.sum(lq1[...] * lk1[...], axis=-1, keepdims=True))
           - jnp.exp(jnp.sum(lq2[...] * lk2[...], axis=-1, keepdims=True)) + LAM_INIT)
    for h in range(hb):
        sl = slice(h * LANES, (h + 1) * LANES)
        q, k, v = q_ref[:, sl], k_ref[:, sl], v_ref[:, sl]
        kc = kc_ref[:, sl].astype(BF16) if cached else None
        vc = vc_ref[:, sl].astype(BF16) if cached else None
        lane = lax.broadcasted_iota(jnp.int32, q.shape, 1)
        zero = jnp.zeros_like(q)
        s1 = _scores(jnp.where(lane < A_DH, q, zero), k, kc)
        s2 = _scores(jnp.where(lane >= A_DH, q, zero), k, kc)
        e1 = jnp.exp2(s1 - jnp.max(s1, axis=-1, keepdims=True))
        e2 = jnp.exp2(s2 - jnp.max(s2, axis=-1, keepdims=True))
        l1 = jnp.sum(e1, axis=-1, keepdims=True)
        l2 = jnp.sum(e2, axis=-1, keepdims=True)
        p = e1 - e2 * (lam * l1 / l2)
        o = _weighted(p.astype(BF16), v, vc) * (1.0 / l1)
        o_ref[:, sl] = (_rms(o, sub_ref[...]) * (1.0 - LAM_INIT)).astype(BF16)


def _diff_attn(q, k, v, cache, nb, lq, tq, hb, w):
    nq = lq // tq
    vec = _resident((1, A_DH))
    kv = lambda rows: pl.BlockSpec((rows, hb * LANES), lambda b, g, i: (b, g))
    in_specs = [pl.BlockSpec((tq, hb * LANES), lambda b, g, i: (b * nq + i, g)), kv(lq), kv(lq)]
    args = [q, k, v]
    if cache is not None:
        past = cache[0].shape[0] // nb
        in_specs += [kv(past), kv(past)]
        args += list(cache)
    return pl.pallas_call(
        functools.partial(_diff_attn_kernel, hb, cache is not None),
        grid=(nb, A_HEADS // hb, nq),
        in_specs=in_specs + [vec, vec, vec, vec, _resident((1, A_DV))],
        out_specs=pl.BlockSpec((tq, hb * LANES), lambda b, g, i: (b * nq + i, g)),
        out_shape=jax.ShapeDtypeStruct((nb * lq, A_V), BF16),
        compiler_params=_cparams(("arbitrary", "arbitrary", "arbitrary")),
        name="diff_attn_cached" if cache is not None else "diff_attn",
    )(*args, w["a_lambda_q1"], w["a_lambda_k1"], w["a_lambda_q2"], w["a_lambda_k2"], w["a_sub_gain"])


def _mla_attn_kernel(hb, cached, q_ref, k_ref, v_ref, *refs):
    if cached:
        kc_ref, vc_ref, o_ref = refs
    else:
        (o_ref,) = refs
    for h in range(hb):
        s = _scores(q_ref[h], k_ref[h], kc_ref[h] if cached else None)
        e = jnp.exp2(s - jnp.max(s, axis=-1, keepdims=True))
        r = 1.0 / jnp.sum(e, axis=-1, keepdims=True)
        o = _weighted(e.astype(BF16), v_ref[h], vc_ref[h] if cached else None) * r
        o_ref[:, h * LANES:(h + 1) * LANES] = o.astype(BF16)


def _mla_attn(q, k, v, cache, nb, lq, tq, hb):
    nq = lq // tq
    kv = lambda rows, d: pl.BlockSpec((hb, rows, d), lambda b, g, i: (g, b, 0))
    in_specs = [pl.BlockSpec((hb, tq, B_QK), lambda b, g, i: (g, b * nq + i, 0)), kv(lq, B_QK), kv(lq, B_DV)]
    args = [q, k, v]
    if cache is not None:
        past = cache[0].shape[1] // nb
        in_specs += [kv(past, B_QK), kv(past, B_DV)]
        args += list(cache)
    return pl.pallas_call(
        functools.partial(_mla_attn_kernel, hb, cache is not None),
        grid=(nb, B_HEADS // hb, nq),
        in_specs=in_specs,
        out_specs=pl.BlockSpec((tq, hb * LANES), lambda b, g, i: (b * nq + i, g)),
        out_shape=jax.ShapeDtypeStruct((nb * lq, B_HEADS * B_DV), BF16),
        compiler_params=_cparams(("arbitrary", "arbitrary", "arbitrary")),
        name="mla_attn_cached" if cache is not None else "mla_attn",
    )(*args)


def _out_kernel(oa_ref, ob_ref, x_ref, g1_ref, sh2_ref, sc2_ref, n2g_ref, woa_ref, wob_ref, wr_ref,
                x1_ref, h2_ref, aff_ref):
    o = (jnp.dot(oa_ref[...], woa_ref[...], preferred_element_type=F32)
         + jnp.dot(ob_ref[...], wob_ref[...], preferred_element_type=F32))
    x1 = x_ref[...] + g1_ref[...] * o
    x1_ref[...] = x1
    h2 = _rms(x1, n2g_ref[...]) * (1.0 + sc2_ref[...]) + sh2_ref[...]
    h2_ref[...] = h2
    logits = jnp.dot(h2.astype(BF16), wr_ref[...], preferred_element_type=F32)
    lane = lax.broadcasted_iota(jnp.int32, logits.shape, 1)
    logits = jnp.where(lane < N_EXPERTS, logits, -jnp.inf)
    e = jnp.exp(logits - jnp.max(logits, axis=-1, keepdims=True))
    aff_ref[...] = e / jnp.sum(e, axis=-1, keepdims=True)


def _out_proj(oa, ob, x, mod, tiles_per_mod, w, tag):
    m = x.shape[0]
    tm = 256
    row = lambda i: (i, 0)
    modspec = lambda k: pl.BlockSpec((None, 1, D_MODEL), lambda i: (i // tiles_per_mod, 0, k))
    return pl.pallas_call(
        _out_kernel,
        grid=(m // tm,),
        in_specs=[pl.BlockSpec((tm, A_V), row), pl.BlockSpec((tm, A_V), row), pl.BlockSpec((tm, D_MODEL), row),
                  modspec(2), modspec(3), modspec(4), _resident((1, D_MODEL)),
                  _resident((A_V, D_MODEL)), _resident((A_V, D_MODEL)), _resident((D_MODEL, LANES))],
        out_specs=[pl.BlockSpec((tm, D_MODEL), row), pl.BlockSpec((tm, D_MODEL), row),
                   pl.BlockSpec((tm, LANES), row)],
        out_shape=[jax.ShapeDtypeStruct((m, D_MODEL), F32), jax.ShapeDtypeStruct((m, D_MODEL), F32),
                   jax.ShapeDtypeStruct((m, LANES), F32)],
        compiler_params=_cparams(("arbitrary",)),
        name="out_proj_" + tag,
    )(oa, ob, x, mod, mod, mod, w["norm2_gain"], w["w_oa"], w["w_ob"], w["w_router"])


ROUTE_BLK = 256
COMBINE_T = 128
RANK_BITS = 4


def _route_kernel(cap, aff_ref, st_ref, cnt_ref, kmax_ref, posm_ref, pk_ref):
    n = aff_ref.shape[0]
    nblk = n // ROUTE_BLK
    aff = aff_ref[...]
    lane = lax.broadcasted_iota(jnp.int32, (n, LANES), 1)
    tok = lax.broadcasted_iota(jnp.int32, (n, LANES), 0)

    def narrow(carry):
        lo, hi = carry
        mid = lo + (hi - lo) * 0.5
        mid = jnp.where(mid < hi, mid, lo)
        above = aff > mid
        few = jnp.sum(above.astype(jnp.int32), axis=0, keepdims=True) < cap
        up = jnp.min(jnp.where(above, aff, jnp.inf), axis=0, keepdims=True)
        dn = jnp.max(jnp.where(above, -jnp.inf, aff), axis=0, keepdims=True)
        return jnp.where(few, lo, up), jnp.where(few, dn, hi)

    bounds = (jnp.min(aff, axis=0, keepdims=True), jnp.max(aff, axis=0, keepdims=True))
    thr, _ = lax.while_loop(lambda c: jnp.max((c[0] < c[1]).astype(jnp.int32)) > 0, narrow, bounds)
    gt = aff > thr
    eq = aff == thr
    need = cap - jnp.sum(gt.astype(jnp.int32), axis=0, keepdims=True)

    r = lax.broadcasted_iota(jnp.int32, (ROUTE_BLK, ROUTE_BLK), 0)
    c = lax.broadcasted_iota(jnp.int32, (ROUTE_BLK, ROUTE_BLK), 1)
    tri = (c < r).astype(BF16)

    def prefix(mask):
        m = mask.astype(F32)
        carry = jnp.zeros((1, LANES), F32)
        outs = []
        for b in range(nblk):
            mb = m[b * ROUTE_BLK:(b + 1) * ROUTE_BLK]
            outs.append(jnp.dot(tri, mb.astype(BF16), preferred_element_type=F32) + carry)
            carry = carry + jnp.sum(mb, axis=0, keepdims=True)
        return jnp.concatenate(outs, axis=0).astype(jnp.int32)

    sel = (gt | (eq & (prefix(eq) < need))) & (lane < N_EXPERTS)
    pos = prefix(sel)
    self_ = sel.astype(BF16)
    er = lax.broadcasted_iota(jnp.int32, (LANES, LANES), 0)
    ec = lax.broadcasted_iota(jnp.int32, (LANES, LANES), 1)
    rank = jnp.dot(self_, (er < ec).astype(BF16), preferred_element_type=F32).astype(jnp.int32)
    cnt = jnp.dot(self_, jnp.ones((LANES, LANES), BF16), preferred_element_type=F32).astype(jnp.int32)
    cnt_ref[...] = cnt
    nt = n // COMBINE_T
    kmax_ref[...] = jnp.max(cnt.reshape(nt, COMBINE_T, LANES), axis=1)

    posm_ref[...] = pos
    st_ref[...] = jnp.zeros(st_ref.shape, jnp.int32)
    st_ref[0:nt, :] = posm_ref[pl.ds(0, nt, stride=COMBINE_T), :]
    st_ref[nt:nt + 1, :] = jnp.full((1, LANES), cap, jnp.int32)
    posm_ref[...] = jnp.where(sel, pos, -1)
    pk_ref[...] = (tok << RANK_BITS) + rank


def _compact_kernel(cap, st_s, posm_ref, pk_ref, aff_ref, idx_ref, rk_ref, gate_ref, accp, accg):
    tt = COMBINE_T
    nt = posm_ref.shape[0] // tt
    accp[...] = jnp.zeros(accp.shape, jnp.int32)
    accg[...] = jnp.zeros(accg.shape, F32)
    jio = lax.broadcasted_iota(jnp.int32, (tt, LANES), 1)

    def tile_body(t, carry):
        rows = pl.ds(pl.multiple_of(t * tt, tt), tt)
        posm, pk, af = posm_ref[rows, :], pk_ref[rows, :], aff_ref[rows, :]
        for e in range(N_EXPERTS):
            g0 = jnp.minimum(lax.shift_right_logical(st_s[t * LANES + e], 7), cap // LANES - 1)
            rel = posm[:, e:e + 1] - g0 * LANES
            for d in range(2):
                oh = rel == jio + d * LANES
                accp[e, g0 + d] += jnp.sum(jnp.where(oh, pk[:, e:e + 1], 0).reshape(tt // 8, 8, LANES), axis=0)
                accg[e, g0 + d] += jnp.sum(jnp.where(oh, af[:, e:e + 1], 0.0).reshape(tt // 8, 8, LANES), axis=0)
        return carry

    lax.fori_loop(0, nt, tile_body, 0)
    for e in range(N_EXPERTS):
        packed = jnp.concatenate([jnp.sum(accp[e, g], axis=0, keepdims=True) for g in range(cap // LANES)], axis=1)
        idx_ref[e:e + 1, :] = packed >> RANK_BITS
        rk_ref[e:e + 1, :] = packed & (N_EXPERTS - 1)
        gate_ref[e:e + 1, :] = jnp.concatenate(
            [jnp.sum(accg[e, g], axis=0, keepdims=True) for g in range(cap // LANES)], axis=1)


def _route(aff, tag):
    n = aff.shape[0]
    cap = CAPACITY_FACTOR * n // N_EXPERTS
    nt = n // COMBINE_T
    full = lambda s: pl.BlockSpec(s, lambda i, *_: (0,) * len(s))
    i32 = lambda s: jax.ShapeDtypeStruct(s, jnp.int32)
    st, cnt, kmax, posm, pk = pl.pallas_call(
        functools.partial(_route_kernel, cap),
        grid=(1,),
        in_specs=[full((n, LANES))],
        out_specs=[full((nt + 8, LANES)), full((n, LANES)), full((nt, LANES)), full((n, LANES)), full((n, LANES))],
        out_shape=[i32((nt + 8, LANES)), i32((n, LANES)), i32((nt, LANES)), i32((n, LANES)), i32((n, LANES))],
        compiler_params=_cparams(("arbitrary",)),
        name="route_" + tag,
    )(aff)
    groups = cap // LANES + 1
    idx, rk, gate = pl.pallas_call(
        functools.partial(_compact_kernel, cap),
        grid_spec=pltpu.PrefetchScalarGridSpec(
            num_scalar_prefetch=1, grid=(1,),
            in_specs=[full((n, LANES)), full((n, LANES)), full((n, LANES))],
            out_specs=[full((N_EXPERTS, cap)), full((N_EXPERTS, cap)), full((N_EXPERTS, cap))],
            scratch_shapes=[pltpu.VMEM((N_EXPERTS, groups, 8, LANES), jnp.int32),
                            pltpu.VMEM((N_EXPERTS, groups, 8, LANES), F32)]),
        out_shape=[i32((N_EXPERTS, cap)), i32((N_EXPERTS, cap)), jax.ShapeDtypeStruct((N_EXPERTS, cap), F32)],
        compiler_params=_cparams(("arbitrary",)),
        name="compact_" + tag,
    )(st.reshape(-1), posm, pk, aff)
    return idx, rk, gate, st, cnt, kmax


FF_CHUNK = 256


def _ffn_kernel(cap, nf, per, idx_s, hp_hbm, hs_hbm, gate_ref, w1_ref, w3_ref, w2_ref, yp_hbm, ys_hbm,
                xg, xe, acc, gsem, osem):
    e = pl.program_id(0)
    f = pl.program_id(1)
    ne = pl.num_programs(0)
    srcs = (hp_hbm, hs_hbm)
    dsts = (yp_hbm, ys_hbm)
    slot = lax.rem(e, 2)

    def issue_part(ex, part, sl, live):
        for p in range(2):
            for i in range(per):
                j = part * per + i
                jc = jnp.minimum(j, cap - 1)
                tok = idx_s[(p * N_EXPERTS + ex) * cap + jc]

                @pl.when(live & (j < cap))
                def _():
                    pltpu.make_async_copy(srcs[p].at[pl.ds(tok, 1), :], xg.at[sl, pl.ds(p * cap + jc, 1), :],
                                          gsem.at[sl]).start()

    def out_copy(p, ex):
        return pltpu.make_async_copy(acc.at[pl.ds(p * cap, cap), :],
                                     dsts[p].at[pl.ds(pl.multiple_of(ex * cap, cap), cap), :], osem)

    @pl.when((e == 0) & (f == 0))
    def _prologue():
        def part_body(part, carry):
            issue_part(0, part, 0, True)
            return carry
        lax.fori_loop(0, nf, part_body, 0)

    @pl.when(f == 0)
    def _start_expert():
        for p in range(2):
            pltpu.make_async_copy(srcs[p].at[pl.ds(0, cap), :], xg.at[slot, pl.ds(p * cap, cap), :],
                                  gsem.at[slot]).wait()
        xe[...] = xg[slot].astype(BF16)

    issue_part(jnp.minimum(e + 1, ne - 1), f, 1 - slot, e + 1 < ne)

    x = xe[...]
    a = jnp.dot(x, w1_ref[...].astype(BF16), preferred_element_type=F32)
    b = jnp.dot(x, w3_ref[...].astype(BF16), preferred_element_type=F32)
    hid = ((a * jax.nn.sigmoid(a)) * b).astype(BF16)

    def down():
        return jnp.dot(hid, w2_ref[...].astype(BF16), preferred_element_type=F32)

    @pl.when(f == 0)
    def _first():
        @pl.when(e > 0)
        def _():
            for p in range(2):
                out_copy(p, e - 1).wait()
        acc[...] = down()

    @pl.when((f > 0) & (f < nf - 1))
    def _middle():
        acc[...] += down()

    @pl.when(f == nf - 1)
    def _last():
        acc[...] = (acc[...] + down()) * gate_ref[...]
        for p in range(2):
            out_copy(p, e).start()

        @pl.when(e == ne - 1)
        def _():
            for p in range(2):
                out_copy(p, e).wait()


def _ffn(idx_flat, h2p, h2s, gate, w1, w3, w2, cap):
    nf = EXPERT_FF // FF_CHUNK
    per = -(-cap // nf)
    grid_spec = pltpu.PrefetchScalarGridSpec(
        num_scalar_prefetch=1,
        grid=(N_EXPERTS, nf),
        in_specs=[pl.BlockSpec(memory_space=pl.ANY), pl.BlockSpec(memory_space=pl.ANY),
                  pl.BlockSpec((None, 2 * cap, 1), lambda e, f, *_: (e, 0, 0)),
                  pl.BlockSpec((None, D_MODEL, FF_CHUNK), lambda e, f, *_: (e, 0, f)),
                  pl.BlockSpec((None, D_MODEL, FF_CHUNK), lambda e, f, *_: (e, 0, f)),
                  pl.BlockSpec((None, FF_CHUNK, D_MODEL), lambda e, f, *_: (e, f, 0))],
        out_specs=[pl.BlockSpec(memory_space=pl.ANY), pl.BlockSpec(memory_space=pl.ANY)],
        scratch_shapes=[pltpu.VMEM((2, 2 * cap, D_MODEL), F32), pltpu.VMEM((2 * cap, D_MODEL), BF16),
                        pltpu.VMEM((2 * cap, D_MODEL), F32), pltpu.SemaphoreType.DMA((2,)),
                        pltpu.SemaphoreType.DMA(())])
    out = jax.ShapeDtypeStruct((N_EXPERTS * cap, D_MODEL), F32)
    return pl.pallas_call(
        functools.partial(_ffn_kernel, cap, nf, per),
        grid_spec=grid_spec, out_shape=[out, out],
        compiler_params=_cparams(("arbitrary", "arbitrary")),
        name="expert_ffn",
    )(idx_flat, h2p, h2s, gate, w1, w3, w2)


COMBINE_CW = 256
WAIT_ROWS = 32


def _combine_kernel(cap, idx_s, rk_s, st_s, km_s, ye_hbm, x1_ref, g2_ref, cnt_ref, o_ref, buf, sem):
    t = pl.program_id(0)
    nt = pl.num_programs(0)
    tt = COMBINE_T
    slot = lax.rem(t, 2)

    def issue_tile(tile, sl):
        base = tile * tt
        for e in range(N_EXPERTS):
            lo = st_s[tile * LANES + e]
            hi = st_s[(tile + 1) * LANES + e]

            def issue(j, carry):
                tok = idx_s[e * cap + j]
                row = rk_s[e * cap + j] * tt + tok - base
                pltpu.make_async_copy(ye_hbm.at[pl.ds(e * cap + j, 1), :], buf.at[sl, pl.ds(row, 1), :],
                                      sem.at[sl]).start()
                return carry
            lax.fori_loop(lo, hi, issue, 0)

    def wait_tile(tile, sl):
        total = 0
        for e in range(N_EXPERTS):
            total = total + st_s[(tile + 1) * LANES + e] - st_s[tile * LANES + e]

        def wait_rows(n):
            def body(i, carry):
                pltpu.make_async_copy(ye_hbm.at[pl.ds(0, n), :], buf.at[sl, pl.ds(0, n), :], sem.at[sl]).wait()
                return carry
            return body
        lax.fori_loop(0, total // WAIT_ROWS, wait_rows(WAIT_ROWS), 0)
        lax.fori_loop(0, lax.rem(total, WAIT_ROWS), wait_rows(1), 0)

    @pl.when(t == 0)
    def _init():
        buf[...] = jnp.zeros(buf.shape, F32)
        issue_tile(0, 0)

    @pl.when(t + 1 < nt)
    def _prefetch():
        issue_tile(t + 1, 1 - slot)

    wait_tile(t, slot)

    kmax = km_s[t * LANES]
    cnt = jnp.tile(cnt_ref[...], (1, COMBINE_CW // LANES))
    for c in range(D_MODEL // COMBINE_CW):
        cs = slice(c * COMBINE_CW, (c + 1) * COMBINE_CW)

        def add_slot(k, a):
            v = buf[slot, pl.ds(pl.multiple_of(k * tt, tt), tt), cs]
            return a + jnp.where(cnt > k, v, 0.0)
        a = lax.fori_loop(0, kmax, add_slot, jnp.zeros((tt, COMBINE_CW), F32))
        o_ref[:, cs] = x1_ref[:, cs] + g2_ref[:, cs] * a


def _combine(idx, rk, st, km, ye, x1, mod, tiles_per_mod, cnt, cap, tag):
    n = x1.shape[0]
    tt = COMBINE_T
    grid_spec = pltpu.PrefetchScalarGridSpec(
        num_scalar_prefetch=4,
        grid=(n // tt,),
        in_specs=[pl.BlockSpec(memory_space=pl.ANY),
                  pl.BlockSpec((tt, D_MODEL), lambda i, *_: (i, 0)),
                  pl.BlockSpec((None, 1, D_MODEL), lambda i, *_: (i // tiles_per_mod, 0, 5)),
                  pl.BlockSpec((tt, LANES), lambda i, *_: (i, 0))],
        out_specs=pl.BlockSpec((tt, D_MODEL), lambda i, *_: (i, 0)),
        scratch_shapes=[pltpu.VMEM((2, N_EXPERTS * tt, D_MODEL), F32), pltpu.SemaphoreType.DMA((2,))])
    return pl.pallas_call(
        functools.partial(_combine_kernel, cap),
        grid_spec=grid_spec, out_shape=jax.ShapeDtypeStruct((n, D_MODEL), F32),
        compiler_params=_cparams(("arbitrary",)),
        name="combine_" + tag,
    )(idx.reshape(-1), rk.reshape(-1), st.reshape(-1), km.reshape(-1), ye, x1, mod, cnt)


def _prep_weights(norm1_gain, norm2_gain, w_in, a_q_gain, a_k_gain, a_lambda_q1, a_lambda_k1, a_lambda_q2,
                  a_lambda_k2, a_sub_gain, q_a_gain, w_uq, b_qn_gain, b_qr_gain, kv_a_gain, w_ukv, b_kn_gain,
                  b_kr_gain, w_o, w_router):
    tile = lambda g, width: jnp.tile(g.reshape(1, -1), (1, width // g.shape[-1]))
    wuq = w_uq[0].reshape(Q_RANK, B_HEADS, B_QK)
    wuq = jnp.concatenate([wuq[:, :, :B_NOPE].reshape(Q_RANK, -1), wuq[:, :, B_NOPE:].reshape(Q_RANK, -1)], axis=1)
    return {
        "norm1_gain": norm1_gain.reshape(1, -1), "norm2_gain": norm2_gain.reshape(1, -1),
        "w_in": w_in[0].astype(BF16),
        "a_q_gain": tile(a_q_gain, MXU_DIM), "a_k_gain": tile(a_k_gain, MXU_DIM),
        "a_lambda_q1": a_lambda_q1.reshape(1, -1), "a_lambda_k1": a_lambda_k1.reshape(1, -1),
        "a_lambda_q2": a_lambda_q2.reshape(1, -1), "a_lambda_k2": a_lambda_k2.reshape(1, -1),
        "a_sub_gain": a_sub_gain.reshape(1, -1), "q_a_gain": q_a_gain.reshape(1, -1),
        "w_uq": wuq.astype(BF16), "b_qn_gain": b_qn_gain.reshape(1, -1), "b_qr_gain": tile(b_qr_gain, MXU_DIM),
        "kv_a_gain": kv_a_gain.reshape(1, -1), "w_ukv": w_ukv[0].astype(BF16),
        "b_kn_gain": b_kn_gain.reshape(1, -1), "b_kr_gain": tile(b_kr_gain, LANES),
        "w_oa": w_o[0, :A_V].astype(BF16), "w_ob": w_o[0, A_V:].astype(BF16),
        "w_router": jnp.pad(w_router[0], ((0, 0), (0, LANES - N_EXPERTS))).astype(BF16),
        "seg64": _seg_matrix(A_DH),
    }


def kernel(x_prompt, x_sample, cache_diff_k, cache_diff_v, cache_mla_ckv, cache_mla_krope, c, c_ctx, w_ada, b_ada, norm1_gain, norm2_gain, w_in, a_q_gain, a_k_gain, a_lambda_q1, a_lambda_k1, a_lambda_q2, a_lambda_k2, a_sub_gain, q_a_gain, w_uq, b_qn_gain, b_qr_gain, kv_a_gain, w_ukv, b_kn_gain, b_kr_gain, w_o, w_router, w_exp1, w_exp3, w_exp2):
    nbp, lp, _ = x_prompt.shape
    nbs, ls, _ = x_sample.shape
    past = cache_diff_k.shape[2]
    w = _prep_weights(norm1_gain, norm2_gain, w_in, a_q_gain, a_k_gain, a_lambda_q1, a_lambda_k1, a_lambda_q2,
                      a_lambda_k2, a_sub_gain, q_a_gain, w_uq, b_qn_gain, b_qr_gain, kv_a_gain, w_ukv, b_kn_gain,
                      b_kr_gain, w_o, w_router)

    cvec = jnp.concatenate([c_ctx[None], c, jnp.zeros((8 - 1 - nbs, D_MODEL), F32)], axis=0)
    mod = _ada(cvec, w_ada[0], b_ada)
    mod_p = mod[0:1].reshape(1, 1, -1)
    mod_s = mod[1:1 + nbs].reshape(nbs, 1, -1)

    xp = x_prompt.reshape(nbp * lp, D_MODEL)
    xs = x_sample.reshape(nbs * ls, D_MODEL)
    tm = 256
    aq_p, ak_p, av_p, qm_p, km_p, bv_p, ak32, av32, ckv32, kr32 = _in_proj(xp, mod_p, False, nbp * lp // tm, w, None)
    aq_s, ak_s, av_s, qm_s, km_s, bv_s = _in_proj(xs, mod_s, True, ls // tm, w, _rope_tables(ls))
    km_c, bv_c = _cache_kv(cache_mla_ckv.reshape(nbs * past, KV_RANK), cache_mla_krope.reshape(nbs * past, B_ROPE), w)

    cache_a = (cache_diff_k.reshape(nbs * past, A_QK), cache_diff_v.reshape(nbs * past, A_V))

    oa_p = _diff_attn(aq_p, ak_p, av_p, None, nbp, lp, lp, A_HEADS, w)
    ob_p = _mla_attn(qm_p, km_p, bv_p, None, nbp, lp, lp, B_HEADS)
    oa_s = _diff_attn(aq_s, ak_s, av_s, cache_a, nbs, ls, 256, 2, w)
    ob_s = _mla_attn(qm_s, km_s, bv_s, (km_c, bv_c), nbs, ls, 256, 2)

    x1_p, h2_p, aff_p = _out_proj(oa_p, ob_p, xp, mod_p, nbp * lp // tm, w, "prompt")
    x1_s, h2_s, aff_s = _out_proj(oa_s, ob_s, xs, mod_s, ls // tm, w, "sample")

    idx_p, rk_p, gate_p, st_p, cnt_p, km_p_ = _route(aff_p, "prompt")
    idx_s, rk_s, gate_s, st_s, cnt_s, km_s_ = _route(aff_s, "sample")
    cap = idx_p.shape[1]
    idx_flat = jnp.stack([idx_p, idx_s]).reshape(-1)
    gate = jnp.concatenate([gate_p, gate_s], axis=1)[..., None]
    ye_p, ye_s = _ffn(idx_flat, h2_p, h2_s, gate, w_exp1[0], w_exp3[0], w_exp2[0], cap)

    y_p = _combine(idx_p, rk_p, st_p, km_p_, ye_p, x1_p, mod_p, nbp * lp // COMBINE_T, cnt_p, cap, "prompt")
    y_s = _combine(idx_s, rk_s, st_s, km_s_, ye_s, x1_s, mod_s, ls // COMBINE_T, cnt_s, cap, "sample")

    return (y_p.reshape(nbp, lp, D_MODEL), y_s.reshape(nbs, ls, D_MODEL),
            ak32.reshape(nbp, 1, lp, A_HEADS, 2, A_DH), av32.reshape(nbp, 1, lp, A_HEADS, A_DV),
            ckv32.reshape(nbp, 1, lp, KV_RANK), kr32.reshape(nbp, 1, lp, B_ROPE))
```

```python
import functools
import math

import numpy as np
import jax
import jax.numpy as jnp
from jax import lax
from jax.experimental import pallas as pl
from jax.experimental.pallas import tpu as pltpu

F32 = jnp.float32
BF16 = jnp.bfloat16

D_MODEL = 2048
GRID_W = 64
A_HEADS = 8
A_DH = 64
A_DV = 128
B_HEADS = 8
B_NOPE = 128
B_ROPE = 64
B_DV = 128
Q_RANK = 512
KV_RANK = 256
N_EXPERTS = 16
EXPERT_FF = 1536
CAPACITY_FACTOR = 2
ROPE_BASE = 10000.0
EPS = 1e-6
LAM_INIT = 0.8 - 0.6 * math.exp(-0.3 * 0)
LOG2E = math.log2(math.e)

A_QK = A_HEADS * 2 * A_DH
A_V = A_HEADS * A_DV
B_QK = B_NOPE + B_ROPE
IN_COLS = 2 * A_QK + A_V + Q_RANK + KV_RANK + B_ROPE
LANES = 128
MXU_DIM = 256
VMEM_LIMIT = 56 * 1024 * 1024


def _cparams(sem):
    return pltpu.CompilerParams(dimension_semantics=sem, vmem_limit_bytes=VMEM_LIMIT)


def _resident(shape):
    nd = len(shape)
    return pl.BlockSpec(shape, lambda *_: (0,) * nd, pipeline_mode=pl.Buffered(1))


def _rms(x, gain):
    return x * lax.rsqrt(jnp.mean(x * x, axis=-1, keepdims=True) + EPS) * gain


def _seg_rms(y, segmat, gain):
    y2 = (y * y).astype(BF16)
    outs = []
    for c in range(y.shape[1] // MXU_DIM):
        sl = slice(c * MXU_DIM, (c + 1) * MXU_DIM)
        ms = jnp.dot(y2[:, sl], segmat, preferred_element_type=F32)
        outs.append(y[:, sl] * lax.rsqrt(ms + EPS) * gain)
    return jnp.concatenate(outs, axis=1)


def _rope(y, cos, sin):
    outs = []
    for c in range(y.shape[1] // LANES):
        yc = y[:, c * LANES:(c + 1) * LANES]
        lane = lax.broadcasted_iota(jnp.int32, yc.shape, 1)
        partner = jnp.where((lane & 16) == 0, pltpu.roll(yc, LANES - 16, 1), pltpu.roll(yc, 16, 1))
        outs.append(yc * cos + partner * sin)
    return outs[0] if len(outs) == 1 else jnp.concatenate(outs, axis=1)


def _rope_tables(n_tok):
    t = np.arange(n_tok)
    row, col = t // GRID_W, t % GRID_W
    nf = A_DH // 4
    inv = ROPE_BASE ** (-np.arange(nf, dtype=np.float64) * 2.0 / (A_DH // 2))
    lane = np.arange(A_DH)
    pos = np.where(lane[None, :] < A_DH // 2, row[:, None], col[:, None]).astype(np.float64)
    ang = pos * inv[lane % nf][None, :]
    sign = np.where((lane % (2 * nf)) < nf, -1.0, 1.0)[None, :]
    cos = np.tile(np.cos(ang), (1, LANES // A_DH)).astype(np.float32)
    sin = np.tile(np.sin(ang) * sign, (1, LANES // A_DH)).astype(np.float32)
    return jnp.asarray(cos), jnp.asarray(sin)


def _seg_matrix(width):
    i = np.arange(MXU_DIM)
    return jnp.asarray(((i[:, None] // width) == (i[None, :] // width)).astype(np.float32) / width, dtype=BF16)


def _ada_kernel(c_ref, w_ref, b_ref, o_ref):
    c = c_ref[...]
    s = c * jax.nn.sigmoid(c)
    o_ref[...] = jnp.dot(s.astype(BF16), w_ref[...].astype(BF16), preferred_element_type=F32) + b_ref[...]


def _ada(cvec, w_ada, b_ada):
    tn = 1024
    n = w_ada.shape[1]
    return pl.pallas_call(
        _ada_kernel,
        grid=(n // tn,),
        in_specs=[pl.BlockSpec((8, D_MODEL), lambda j: (0, 0)),
                  pl.BlockSpec((D_MODEL, tn), lambda j: (0, j)),
                  pl.BlockSpec((1, tn), lambda j: (0, j))],
        out_specs=pl.BlockSpec((8, tn), lambda j: (0, j)),
        out_shape=jax.ShapeDtypeStruct((8, n), F32),
        compiler_params=_cparams(("arbitrary",)),
        name="ada_mod",
    )(cvec, w_ada, b_ada)


def _expand_kv(ckv_n, kr, wukv_ref, kng, km_ref, bv_ref):
    kv = jnp.dot(ckv_n.astype(BF16), wukv_ref[...], preferred_element_type=F32)
    krb = kr[:, :B_ROPE].astype(BF16)
    for h in range(B_HEADS):
        base = h * (B_NOPE + B_DV)
        kn = _rms(kv[:, base:base + B_NOPE], kng)
        km_ref[h] = jnp.concatenate([kn.astype(BF16), krb], axis=1)
        bv_ref[h] = kv[:, base + B_NOPE:base + B_NOPE + B_DV].astype(BF16)


def _in_kernel(sample, *refs):
    (x_ref, sh_ref, sc_ref, n1g_ref, win_ref, aqg_ref, akg_ref, qag_ref, wuq_ref, qng_ref, qrg_ref,
     kvag_ref, wukv_ref, kng_ref, krg_ref, seg_ref) = refs[:16]
    refs = refs[16:]
    if sample:
        cos_ref, sin_ref = refs[:2]
        refs = refs[2:]
        cos, sin = cos_ref[...], sin_ref[...]
    aq_ref, ak_ref, av_ref, qm_ref, km_ref, bv_ref = refs[:6]
    refs = refs[6:]

    x = x_ref[...]
    h = _rms(x, n1g_ref[...]) * (1.0 + sc_ref[...]) + sh_ref[...]
    hb = h.astype(BF16)

    def proj(c0, c1):
        return jnp.dot(hb, win_ref[:, c0:c1], preferred_element_type=F32)

    seg = seg_ref[...]
    aq = _seg_rms(proj(0, A_QK), seg, aqg_ref[...])
    ak = _seg_rms(proj(A_QK, 2 * A_QK), seg, akg_ref[...])
    if sample:
        aq = _rope(aq, cos, sin)
        ak = _rope(ak, cos, sin)
    aq_ref[...] = (aq * (A_DH ** -0.5 * LOG2E)).astype(BF16)
    ak_ref[...] = ak.astype(BF16)
    av = proj(2 * A_QK, 2 * A_QK + A_V)
    av_ref[...] = av.astype(BF16)

    c0 = 2 * A_QK + A_V
    q_lat = _rms(proj(c0, c0 + Q_RANK), qag_ref[...])
    qb = jnp.dot(q_lat.astype(BF16), wuq_ref[...], preferred_element_type=F32)
    qr = _seg_rms(qb[:, B_HEADS * B_NOPE:], seg, qrg_ref[...])
    if sample:
        qr = _rope(qr, cos, sin)
    scale = B_QK ** -0.5 * LOG2E
    for hd in range(B_HEADS):
        qn = _rms(qb[:, hd * B_NOPE:(hd + 1) * B_NOPE], qng_ref[...])
        qm_ref[hd] = jnp.concatenate(
            [(qn * scale).astype(BF16), (qr[:, hd * B_ROPE:(hd + 1) * B_ROPE] * scale).astype(BF16)], axis=1)

    c0 += Q_RANK
    ckv = _rms(proj(c0, c0 + KV_RANK), kvag_ref[...])
    c0 += KV_RANK
    kr64 = proj(c0, c0 + B_ROPE)
    kr_raw = jnp.concatenate([kr64, jnp.zeros_like(kr64)], axis=1)
    kr = kr_raw * lax.rsqrt(jnp.sum(kr_raw * kr_raw, axis=-1, keepdims=True) * (1.0 / B_ROPE) + EPS) * krg_ref[...]
    if sample:
        kr = _rope(kr, cos, sin)
    _expand_kv(ckv, kr, wukv_ref, kng_ref[...], km_ref, bv_ref)
    if not sample:
        ak32_ref, av32_ref, ckv32_ref, kr32_ref = refs
        ak32_ref[...] = ak
        av32_ref[...] = av
        ckv32_ref[...] = ckv
        kr32_ref[...] = kr[:, :B_ROPE]


def _in_proj(x, mod, sample, tiles_per_mod, w, tables):
    m = x.shape[0]
    tm = 256
    nt = m // tm
    row = lambda i: (i, 0)
    modspec = lambda k: pl.BlockSpec((None, 1, D_MODEL), lambda i: (i // tiles_per_mod, 0, k))
    in_specs = [pl.BlockSpec((tm, D_MODEL), row), modspec(0), modspec(1), _resident((1, D_MODEL)),
                _resident((D_MODEL, IN_COLS)), _resident((1, MXU_DIM)), _resident((1, MXU_DIM)),
                _resident((1, Q_RANK)), _resident((Q_RANK, B_HEADS * B_QK)), _resident((1, B_NOPE)),
                _resident((1, MXU_DIM)), _resident((1, KV_RANK)), _resident((KV_RANK, B_HEADS * (B_NOPE + B_DV))),
                _resident((1, B_NOPE)), _resident((1, LANES)), _resident((MXU_DIM, MXU_DIM))]
    args = [x, mod, mod, w["norm1_gain"], w["w_in"], w["a_q_gain"], w["a_k_gain"], w["q_a_gain"], w["w_uq"],
            w["b_qn_gain"], w["b_qr_gain"], w["kv_a_gain"], w["w_ukv"], w["b_kn_gain"], w["b_kr_gain"], w["seg64"]]
    if sample:
        per = tables[0].shape[0] // tm
        in_specs += [pl.BlockSpec((tm, LANES), lambda i: (i % per, 0))] * 2
        args += list(tables)
    out_shape = [jax.ShapeDtypeStruct((m, A_QK), BF16), jax.ShapeDtypeStruct((m, A_QK), BF16),
                 jax.ShapeDtypeStruct((m, A_V), BF16), jax.ShapeDtypeStruct((B_HEADS, m, B_QK), BF16),
                 jax.ShapeDtypeStruct((B_HEADS, m, B_QK), BF16), jax.ShapeDtypeStruct((B_HEADS, m, B_DV), BF16)]
    hspec = lambda d: pl.BlockSpec((B_HEADS, tm, d), lambda i: (0, i, 0))
    out_specs = [pl.BlockSpec((tm, A_QK), row), pl.BlockSpec((tm, A_QK), row), pl.BlockSpec((tm, A_V), row),
                 hspec(B_QK), hspec(B_QK), hspec(B_DV)]
    if not sample:
        out_shape += [jax.ShapeDtypeStruct((m, A_QK), F32), jax.ShapeDtypeStruct((m, A_V), F32),
                      jax.ShapeDtypeStruct((m, KV_RANK), F32), jax.ShapeDtypeStruct((m, B_ROPE), F32)]
        out_specs += [pl.BlockSpec((tm, A_QK), row), pl.BlockSpec((tm, A_V), row),
                      pl.BlockSpec((tm, KV_RANK), row), pl.BlockSpec((tm, B_ROPE), row)]
    return pl.pallas_call(
        functools.partial(_in_kernel, sample),
        grid=(nt,), in_specs=in_specs, out_specs=out_specs, out_shape=out_shape,
        compiler_params=_cparams(("arbitrary",)),
        name="in_proj_sample" if sample else "in_proj_prompt",
    )(*args)


def _cache_kv_kernel(ckv_ref, kr_ref, wukv_ref, kng_ref, km_ref, bv_ref):
    _expand_kv(ckv_ref[...], kr_ref[...], wukv_ref, kng_ref[...], km_ref, bv_ref)


def _cache_kv(ckv, kr, w):
    m = ckv.shape[0]
    return pl.pallas_call(
        _cache_kv_kernel,
        grid=(1,),
        in_specs=[_resident((m, KV_RANK)), _resident((m, B_ROPE)),
                  _resident((KV_RANK, B_HEADS * (B_NOPE + B_DV))), _resident((1, B_NOPE))],
        out_specs=[pl.BlockSpec((B_HEADS, m, B_QK), lambda i: (0, 0, 0)),
                   pl.BlockSpec((B_HEADS, m, B_DV), lambda i: (0, 0, 0))],
        out_shape=[jax.ShapeDtypeStruct((B_HEADS, m, B_QK), BF16), jax.ShapeDtypeStruct((B_HEADS, m, B_DV), BF16)],
        compiler_params=_cparams(("arbitrary",)),
        name="cache_kv",
    )(ckv, kr, w["w_ukv"], w["b_kn_gain"])


def _qk(q, k):
    return lax.dot_general(q, k, (((1,), (1,)), ((), ())), preferred_element_type=F32)


KEY_CHUNK = 512


def _scores(q, k, kc):
    s = _qk(q, k)
    return s if kc is None else jnp.concatenate([s, _qk(q, kc)], axis=1)


def _weighted(p, v, vc):
    if vc is None:
        return jnp.dot(p, v, preferred_element_type=F32)
    lk = v.shape[0]
    return (jnp.dot(p[:, :lk], v, preferred_element_type=F32)
            + jnp.dot(p[:, lk:], vc, preferred_element_type=F32))


def _diff_attn_kernel(hb, cached, q_ref, k_ref, v_ref, *refs):
    if cached:
        kc_ref, vc_ref = refs[:2]
        refs = refs[2:]
    lq1, lk1, lq2, lk2, sub_ref, o_ref = refs
    lam = (jnp.exp(jnp.sum(lq1[...] * lk1[...], axis=-1, keepdims=True))
           - jnp.exp(jnp.sum(lq2[...] * lk2[...], axis=-1, keepdims=True)) + LAM_INIT)
    for h in range(hb):
        sl = slice(h * LANES, (h + 1) * LANES)
        q, k, v = q_ref[:, sl], k_ref[:, sl], v_ref[:, sl]
        kc = kc_ref[:, sl].astype(BF16) if cached else None
        vc = vc_ref[:, sl].astype(BF16) if cached else None
        lane = lax.broadcasted_iota(jnp.int32, q.shape, 1)
        zero = jnp.zeros_like(q)
        s1 = _scores(jnp.where(lane < A_DH, q, zero), k, kc)
        s2 = _scores(jnp.where(lane >= A_DH, q, zero), k, kc)
        e1 = jnp.exp2(s1 - jnp.max(s1, axis=-1, keepdims=True))
        e2 = jnp.exp2(s2 - jnp.max(s2, axis=-1, keepdims=True))
        l1 = jnp.sum(e1, axis=-1, keepdims=True)
        l2 = jnp.sum(e2, axis=-1, keepdims=True)
        p = e1 - e2 * (lam * l1 / l2)
        o = _weighted(p.astype(BF16), v, vc) * (1.0 / l1)
        o_ref[:, sl] = (_rms(o, sub_ref[...]) * (1.0 - LAM_INIT)).astype(BF16)


def _diff_attn(q, k, v, cache, nb, lq, tq, hb, w):
    nq = lq // tq
    vec = _resident((1, A_DH))
    kv = lambda rows: pl.BlockSpec((rows, hb * LANES), lambda b, g, i: (b, g))
    in_specs = [pl.BlockSpec((tq, hb * LANES), lambda b, g, i: (b * nq + i, g)), kv(lq), kv(lq)]
    args = [q, k, v]
    if cache is not None:
        past = cache[0].shape[0] // nb
        in_specs += [kv(past), kv(past)]
        args += list(cache)
    return pl.pallas_call(
        functools.partial(_diff_attn_kernel, hb, cache is not None),
        grid=(nb, A_HEADS // hb, nq),
        in_specs=in_specs + [vec, vec, vec, vec, _resident((1, A_DV))],
        out_specs=pl.BlockSpec((tq, hb * LANES), lambda b, g, i: (b * nq + i, g)),
        out_shape=jax.ShapeDtypeStruct((nb * lq, A_V), BF16),
        compiler_params=_cparams(("arbitrary", "arbitrary", "arbitrary")),
        name="diff_attn_cached" if cache is not None else "diff_attn",
    )(*args, w["a_lambda_q1"], w["a_lambda_k1"], w["a_lambda_q2"], w["a_lambda_k2"], w["a_sub_gain"])


def _mla_attn_kernel(hb, cached, q_ref, k_ref, v_ref, *refs):
    if cached:
        kc_ref, vc_ref, o_ref = refs
    else:
        (o_ref,) = refs
    lk = k_ref.shape[1]
    ck = min(lk, KEY_CHUNK)
    for h in range(hb):
        q = q_ref[h]
        kv = [(k_ref[h, c * ck:(c + 1) * ck], v_ref[h, c * ck:(c + 1) * ck]) for c in range(lk // ck)]
        if cached:
            kv.append((kc_ref[h], vc_ref[h]))
        s = [_qk(q, k) for k, _ in kv]
        m = functools.reduce(jnp.maximum, [jnp.max(sc, axis=-1, keepdims=True) for sc in s])
        l = 0.0
        o = 0.0
        for sc, (_, v) in zip(s, kv):
            e = jnp.exp2(sc - m)
            l = l + jnp.sum(e, axis=-1, keepdims=True)
            o = o + jnp.dot(e.astype(BF16), v, preferred_element_type=F32)
        o_ref[:, h * LANES:(h + 1) * LANES] = (o * (1.0 / l)).astype(BF16)


def _mla_attn(q, k, v, cache, nb, lq, tq, hb):
    nq = lq // tq
    kv = lambda rows, d: pl.BlockSpec((hb, rows, d), lambda b, g, i: (g, b, 0))
    in_specs = [pl.BlockSpec((hb, tq, B_QK), lambda b, g, i: (g, b * nq + i, 0)), kv(lq, B_QK), kv(lq, B_DV)]
    args = [q, k, v]
    if cache is not None:
        past = cache[0].shape[1] // nb
        in_specs += [kv(past, B_QK), kv(past, B_DV)]
        args += list(cache)
    return pl.pallas_call(
        functools.partial(_mla_attn_kernel, hb, cache is not None),
        grid=(nb, B_HEADS // hb, nq),
        in_specs=in_specs,
        out_specs=pl.BlockSpec((tq, hb * LANES), lambda b, g, i: (b * nq + i, g)),
        out_shape=jax.ShapeDtypeStruct((nb * lq, B_HEADS * B_DV), BF16),
        compiler_params=_cparams(("arbitrary", "arbitrary", "arbitrary")),
        name="mla_attn_cached" if cache is not None else "mla_attn",
    )(*args)


def _out_kernel(oa_ref, ob_ref, x_ref, g1_ref, sh2_ref, sc2_ref, n2g_ref, woa_ref, wob_ref, wr_ref,
                x1_ref, h2_ref, aff_ref):
    o = (jnp.dot(oa_ref[...], woa_ref[...], preferred_element_type=F32)
         + jnp.dot(ob_ref[...], wob_ref[...], preferred_element_type=F32))
    x1 = x_ref[...] + g1_ref[...] * o
    x1_ref[...] = x1
    h2 = _rms(x1, n2g_ref[...]) * (1.0 + sc2_ref[...]) + sh2_ref[...]
    h2_ref[...] = h2
    logits = jnp.dot(h2.astype(BF16), wr_ref[...], preferred_element_type=F32)
    lane = lax.broadcasted_iota(jnp.int32, logits.shape, 1)
    logits = jnp.where(lane < N_EXPERTS, logits, -jnp.inf)
    e = jnp.exp(logits - jnp.max(logits, axis=-1, keepdims=True))
    aff_ref[...] = e / jnp.sum(e, axis=-1, keepdims=True)


def _out_proj(oa, ob, x, mod, tiles_per_mod, w, tag):
    m = x.shape[0]
    tm = 256
    row = lambda i: (i, 0)
    modspec = lambda k: pl.BlockSpec((None, 1, D_MODEL), lambda i: (i // tiles_per_mod, 0, k))
    return pl.pallas_call(
        _out_kernel,
        grid=(m // tm,),
        in_specs=[pl.BlockSpec((tm, A_V), row), pl.BlockSpec((tm, A_V), row), pl.BlockSpec((tm, D_MODEL), row),
                  modspec(2), modspec(3), modspec(4), _resident((1, D_MODEL)),
                  _resident((A_V, D_MODEL)), _resident((A_V, D_MODEL)), _resident((D_MODEL, LANES))],
        out_specs=[pl.BlockSpec((tm, D_MODEL), row), pl.BlockSpec((tm, D_MODEL), row),
                   pl.BlockSpec((tm, LANES), row)],
        out_shape=[jax.ShapeDtypeStruct((m, D_MODEL), F32), jax.ShapeDtypeStruct((m, D_MODEL), F32),
                   jax.ShapeDtypeStruct((m, LANES), F32)],
        compiler_params=_cparams(("arbitrary",)),
        name="out_proj_" + tag,
    )(oa, ob, x, mod, mod, mod, w["norm2_gain"], w["w_oa"], w["w_ob"], w["w_router"])


ROUTE_BLK = 256
COMBINE_T = 128
RANK_BITS = 4


def _route_kernel(cap, aff_ref, st_ref, cnt_ref, kmax_ref, posm_ref, pk_ref):
    n = aff_ref.shape[0]
    nblk = n // ROUTE_BLK
    aff = aff_ref[...]
    lane = lax.broadcasted_iota(jnp.int32, (n, LANES), 1)
    tok = lax.broadcasted_iota(jnp.int32, (n, LANES), 0)

    def narrow(carry):
        lo, hi = carry
        mid = lo + (hi - lo) * 0.5
        mid = jnp.where(mid < hi, mid, lo)
        above = aff > mid
        few = jnp.sum(above.astype(jnp.int32), axis=0, keepdims=True) < cap
        up = jnp.min(jnp.where(above, aff, jnp.inf), axis=0, keepdims=True)
        dn = jnp.max(jnp.where(above, -jnp.inf, aff), axis=0, keepdims=True)
        return jnp.where(few, lo, up), jnp.where(few, dn, hi)

    bounds = (jnp.min(aff, axis=0, keepdims=True), jnp.max(aff, axis=0, keepdims=True))
    thr, _ = lax.while_loop(lambda c: jnp.max((c[0] < c[1]).astype(jnp.int32)) > 0, narrow, bounds)
    gt = aff > thr
    eq = aff == thr
    need = cap - jnp.sum(gt.astype(jnp.int32), axis=0, keepdims=True)

    r = lax.broadcasted_iota(jnp.int32, (ROUTE_BLK, ROUTE_BLK), 0)
    c = lax.broadcasted_iota(jnp.int32, (ROUTE_BLK, ROUTE_BLK), 1)
    tri = (c < r).astype(BF16)

    def prefix(mask):
        m = mask.astype(F32)
        carry = jnp.zeros((1, LANES), F32)
        outs = []
        for b in range(nblk):
            mb = m[b * ROUTE_BLK:(b + 1) * ROUTE_BLK]
            outs.append(jnp.dot(tri, mb.astype(BF16), preferred_element_type=F32) + carry)
            carry = carry + jnp.sum(mb, axis=0, keepdims=True)
        return jnp.concatenate(outs, axis=0).astype(jnp.int32)

    sel = (gt | (eq & (prefix(eq) < need))) & (lane < N_EXPERTS)
    pos = prefix(sel)
    self_ = sel.astype(BF16)
    er = lax.broadcasted_iota(jnp.int32, (LANES, LANES), 0)
    ec = lax.broadcasted_iota(jnp.int32, (LANES, LANES), 1)
    rank = jnp.dot(self_, (er < ec).astype(BF16), preferred_element_type=F32).astype(jnp.int32)
    cnt = jnp.dot(self_, jnp.ones((LANES, LANES), BF16), preferred_element_type=F32).astype(jnp.int32)
    cnt_ref[...] = cnt
    nt = n // COMBINE_T
    kmax_ref[...] = jnp.max(cnt.reshape(nt, COMBINE_T, LANES), axis=1)

    posm_ref[...] = pos
    st_ref[...] = jnp.zeros(st_ref.shape, jnp.int32)
    st_ref[0:nt, :] = posm_ref[pl.ds(0, nt, stride=COMBINE_T), :]
    st_ref[nt:nt + 1, :] = jnp.full((1, LANES), cap, jnp.int32)
    posm_ref[...] = jnp.where(sel, pos, -1)
    pk_ref[...] = (tok << RANK_BITS) + rank


def _compact_kernel(cap, st_s, posm_ref, pk_ref, aff_ref, idx_ref, rk_ref, gate_ref, accp, accg):
    tt = COMBINE_T
    nt = posm_ref.shape[0] // tt
    accp[...] = jnp.zeros(accp.shape, jnp.int32)
    accg[...] = jnp.zeros(accg.shape, F32)
    jio = lax.broadcasted_iota(jnp.int32, (tt, LANES), 1)

    def tile_body(t, carry):
        rows = pl.ds(pl.multiple_of(t * tt, tt), tt)
        posm, pk, af = posm_ref[rows, :], pk_ref[rows, :], aff_ref[rows, :]
        for e in range(N_EXPERTS):
            g0 = jnp.minimum(lax.shift_right_logical(st_s[t * LANES + e], 7), cap // LANES - 1)
            rel = posm[:, e:e + 1] - g0 * LANES
            for d in range(2):
                oh = rel == jio + d * LANES
                accp[e, g0 + d] += jnp.sum(jnp.where(oh, pk[:, e:e + 1], 0).reshape(tt // 8, 8, LANES), axis=0)
                accg[e, g0 + d] += jnp.sum(jnp.where(oh, af[:, e:e + 1], 0.0).reshape(tt // 8, 8, LANES), axis=0)
        return carry

    lax.fori_loop(0, nt, tile_body, 0)
    for e in range(N_EXPERTS):
        packed = jnp.concatenate([jnp.sum(accp[e, g], axis=0, keepdims=True) for g in range(cap // LANES)], axis=1)
        idx_ref[e:e + 1, :] = packed >> RANK_BITS
        rk_ref[e:e + 1, :] = (packed & (N_EXPERTS - 1)) * tt + ((packed >> RANK_BITS) & (tt - 1))
        gate_ref[e:e + 1, :] = jnp.concatenate(
            [jnp.sum(accg[e, g], axis=0, keepdims=True) for g in range(cap // LANES)], axis=1)


def _route(aff, tag):
    n = aff.shape[0]
    cap = CAPACITY_FACTOR * n // N_EXPERTS
    nt = n // COMBINE_T
    full = lambda s: pl.BlockSpec(s, lambda i, *_: (0,) * len(s))
    i32 = lambda s: jax.ShapeDtypeStruct(s, jnp.int32)
    st, cnt, kmax, posm, pk = pl.pallas_call(
        functools.partial(_route_kernel, cap),
        grid=(1,),
        in_specs=[full((n, LANES))],
        out_specs=[full((nt + 8, LANES)), full((n, LANES)), full((nt, LANES)), full((n, LANES)), full((n, LANES))],
        out_shape=[i32((nt + 8, LANES)), i32((n, LANES)), i32((nt, LANES)), i32((n, LANES)), i32((n, LANES))],
        compiler_params=_cparams(("arbitrary",)),
        name="route_" + tag,
    )(aff)
    groups = cap // LANES + 1
    idx, rk, gate = pl.pallas_call(
        functools.partial(_compact_kernel, cap),
        grid_spec=pltpu.PrefetchScalarGridSpec(
            num_scalar_prefetch=1, grid=(1,),
            in_specs=[full((n, LANES)), full((n, LANES)), full((n, LANES))],
            out_specs=[full((N_EXPERTS, cap)), full((N_EXPERTS, cap)), full((N_EXPERTS, cap))],
            scratch_shapes=[pltpu.VMEM((N_EXPERTS, groups, 8, LANES), jnp.int32),
                            pltpu.VMEM((N_EXPERTS, groups, 8, LANES), F32)]),
        out_shape=[i32((N_EXPERTS, cap)), i32((N_EXPERTS, cap)), jax.ShapeDtypeStruct((N_EXPERTS, cap), F32)],
        compiler_params=_cparams(("arbitrary",)),
        name="compact_" + tag,
    )(st.reshape(-1), posm, pk, aff)
    return idx, rk, gate, st, cnt, kmax


FF_CHUNK = 256


def _ffn_kernel(cap, nf, per, idx_s, hp_hbm, hs_hbm, gate_ref, w1_ref, w3_ref, w2_ref, yp_hbm, ys_hbm,
                xg, xe, acc, gsem, osem):
    e = pl.program_id(0)
    f = pl.program_id(1)
    ne = pl.num_programs(0)
    srcs = (hp_hbm, hs_hbm)
    dsts = (yp_hbm, ys_hbm)
    slot = lax.rem(e, 2)

    def issue_part(ex, part, sl, live):
        for p in range(2):
            for i in range(per):
                j = part * per + i
                jc = jnp.minimum(j, cap - 1)
                tok = idx_s[(p * N_EXPERTS + ex) * cap + jc]

                @pl.when(live & (j < cap))
                def _():
                    pltpu.make_async_copy(srcs[p].at[pl.ds(tok, 1), :], xg.at[sl, pl.ds(p * cap + jc, 1), :],
                                          gsem.at[sl]).start()

    def out_copy(p, ex):
        return pltpu.make_async_copy(acc.at[pl.ds(p * cap, cap), :],
                                     dsts[p].at[pl.ds(pl.multiple_of(ex * cap, cap), cap), :], osem)

    @pl.when((e == 0) & (f == 0))
    def _prologue():
        def part_body(part, carry):
            issue_part(0, part, 0, True)
            return carry
        lax.fori_loop(0, nf, part_body, 0)

    @pl.when(f == 0)
    def _start_expert():
        for p in range(2):
            pltpu.make_async_copy(srcs[p].at[pl.ds(0, cap), :], xg.at[slot, pl.ds(p * cap, cap), :],
                                  gsem.at[slot]).wait()
        xe[...] = xg[slot].astype(BF16)

    issue_part(jnp.minimum(e + 1, ne - 1), f, 1 - slot, e + 1 < ne)

    x = xe[...]
    a = jnp.dot(x, w1_ref[...].astype(BF16), preferred_element_type=F32)
    b = jnp.dot(x, w3_ref[...].astype(BF16), preferred_element_type=F32)
    hid = ((a * jax.nn.sigmoid(a)) * b).astype(BF16)

    def down():
        return jnp.dot(hid, w2_ref[...].astype(BF16), preferred_element_type=F32)

    @pl.when(f == 0)
    def _first():
        @pl.when(e > 0)
        def _():
            for p in range(2):
                out_copy(p, e - 1).wait()
        acc[...] = down()

    @pl.when((f > 0) & (f < nf - 1))
    def _middle():
        acc[...] += down()

    @pl.when(f == nf - 1)
    def _last():
        acc[...] = (acc[...] + down()) * gate_ref[...]
        for p in range(2):
            out_copy(p, e).start()

        @pl.when(e == ne - 1)
        def _():
            for p in range(2):
                out_copy(p, e).wait()


def _ffn(idx_flat, h2p, h2s, gate, w1, w3, w2, cap):
    nf = EXPERT_FF // FF_CHUNK
    per = -(-cap // nf)
    grid_spec = pltpu.PrefetchScalarGridSpec(
        num_scalar_prefetch=1,
        grid=(N_EXPERTS, nf),
        in_specs=[pl.BlockSpec(memory_space=pl.ANY), pl.BlockSpec(memory_space=pl.ANY),
                  pl.BlockSpec((None, 2 * cap, 1), lambda e, f, *_: (e, 0, 0)),
                  pl.BlockSpec((None, D_MODEL, FF_CHUNK), lambda e, f, *_: (e, 0, f)),
                  pl.BlockSpec((None, D_MODEL, FF_CHUNK), lambda e, f, *_: (e, 0, f)),
                  pl.BlockSpec((None, FF_CHUNK, D_MODEL), lambda e, f, *_: (e, f, 0))],
        out_specs=[pl.BlockSpec(memory_space=pl.ANY), pl.BlockSpec(memory_space=pl.ANY)],
        scratch_shapes=[pltpu.VMEM((2, 2 * cap, D_MODEL), F32), pltpu.VMEM((2 * cap, D_MODEL), BF16),
                        pltpu.VMEM((2 * cap, D_MODEL), F32), pltpu.SemaphoreType.DMA((2,)),
                        pltpu.SemaphoreType.DMA(())])
    out = jax.ShapeDtypeStruct((N_EXPERTS * cap, D_MODEL), F32)
    return pl.pallas_call(
        functools.partial(_ffn_kernel, cap, nf, per),
        grid_spec=grid_spec, out_shape=[out, out],
        compiler_params=_cparams(("arbitrary", "arbitrary")),
        name="expert_ffn",
    )(idx_flat, h2p, h2s, gate, w1, w3, w2)


COMBINE_CW = 256
WAIT_ROWS = 32
ISSUE_UNROLL = 4


def _combine_kernel(cap, row_s, st_s, km_s, ye_hbm, x1_ref, g2_ref, cnt_ref, o_ref, buf, sem):
    t = pl.program_id(0)
    nt = pl.num_programs(0)
    tt = COMBINE_T
    slot = lax.rem(t, 2)

    def issue_tile(tile, sl):
        for e in range(N_EXPERTS):
            lo = st_s[tile * LANES + e]
            hi = st_s[(tile + 1) * LANES + e]

            def issue(g, carry):
                for u in range(ISSUE_UNROLL):
                    j = lo + g * ISSUE_UNROLL + u

                    @pl.when(j < hi)
                    def _():
                        pltpu.make_async_copy(ye_hbm.at[pl.ds(e * cap + j, 1), :],
                                              buf.at[sl, pl.ds(row_s[e * cap + j], 1), :], sem.at[sl]).start()
                return carry
            lax.fori_loop(0, lax.div(hi - lo + ISSUE_UNROLL - 1, ISSUE_UNROLL), issue, 0)

    def wait_tile(tile, sl):
        total = 0
        for e in range(N_EXPERTS):
            total = total + st_s[(tile + 1) * LANES + e] - st_s[tile * LANES + e]

        def wait_rows(n):
            def body(i, carry):
                pltpu.make_async_copy(ye_hbm.at[pl.ds(0, n), :], buf.at[sl, pl.ds(0, n), :], sem.at[sl]).wait()
                return carry
            return body
        lax.fori_loop(0, total // WAIT_ROWS, wait_rows(WAIT_ROWS), 0)
        lax.fori_loop(0, lax.rem(total, WAIT_ROWS), wait_rows(1), 0)

    @pl.when(t == 0)
    def _init():
        buf[...] = jnp.zeros(buf.shape, F32)
        issue_tile(0, 0)

    @pl.when(t + 1 < nt)
    def _prefetch():
        issue_tile(t + 1, 1 - slot)

    wait_tile(t, slot)

    kmax = km_s[t * LANES]
    cnt = jnp.tile(cnt_ref[...], (1, COMBINE_CW // LANES))
    for c in range(D_MODEL // COMBINE_CW):
        cs = slice(c * COMBINE_CW, (c + 1) * COMBINE_CW)

        def add_slot(k, a):
            v = buf[slot, pl.ds(pl.multiple_of(k * tt, tt), tt), cs]
            return a + jnp.where(cnt > k, v, 0.0)
        a = lax.fori_loop(0, kmax, add_slot, jnp.zeros((tt, COMBINE_CW), F32))
        o_ref[:, cs] = x1_ref[:, cs] + g2_ref[:, cs] * a


def _combine(rows, st, km, ye, x1, mod, tiles_per_mod, cnt, cap, tag):
    n = x1.shape[0]
    tt = COMBINE_T
    grid_spec = pltpu.PrefetchScalarGridSpec(
        num_scalar_prefetch=3,
        grid=(n // tt,),
        in_specs=[pl.BlockSpec(memory_space=pl.ANY),
                  pl.BlockSpec((tt, D_MODEL), lambda i, *_: (i, 0)),
                  pl.BlockSpec((None, 1, D_MODEL), lambda i, *_: (i // tiles_per_mod, 0, 5)),
                  pl.BlockSpec((tt, LANES), lambda i, *_: (i, 0))],
        out_specs=pl.BlockSpec((tt, D_MODEL), lambda i, *_: (i, 0)),
        scratch_shapes=[pltpu.VMEM((2, N_EXPERTS * tt, D_MODEL), F32), pltpu.SemaphoreType.DMA((2,))])
    return pl.pallas_call(
        functools.partial(_combine_kernel, cap),
        grid_spec=grid_spec, out_shape=jax.ShapeDtypeStruct((n, D_MODEL), F32),
        compiler_params=_cparams(("arbitrary",)),
        name="combine_" + tag,
    )(jnp.pad(rows.reshape(-1), (0, ISSUE_UNROLL)), st.reshape(-1), km.reshape(-1), ye, x1, mod, cnt)


def _prep_weights(norm1_gain, norm2_gain, w_in, a_q_gain, a_k_gain, a_lambda_q1, a_lambda_k1, a_lambda_q2,
                  a_lambda_k2, a_sub_gain, q_a_gain, w_uq, b_qn_gain, b_qr_gain, kv_a_gain, w_ukv, b_kn_gain,
                  b_kr_gain, w_o, w_router):
    tile = lambda g, width: jnp.tile(g.reshape(1, -1), (1, width // g.shape[-1]))
    wuq = w_uq[0].reshape(Q_RANK, B_HEADS, B_QK)
    wuq = jnp.concatenate([wuq[:, :, :B_NOPE].reshape(Q_RANK, -1), wuq[:, :, B_NOPE:].reshape(Q_RANK, -1)], axis=1)
    return {
        "norm1_gain": norm1_gain.reshape(1, -1), "norm2_gain": norm2_gain.reshape(1, -1),
        "w_in": w_in[0].astype(BF16),
        "a_q_gain": tile(a_q_gain, MXU_DIM), "a_k_gain": tile(a_k_gain, MXU_DIM),
        "a_lambda_q1": a_lambda_q1.reshape(1, -1), "a_lambda_k1": a_lambda_k1.reshape(1, -1),
        "a_lambda_q2": a_lambda_q2.reshape(1, -1), "a_lambda_k2": a_lambda_k2.reshape(1, -1),
        "a_sub_gain": a_sub_gain.reshape(1, -1), "q_a_gain": q_a_gain.reshape(1, -1),
        "w_uq": wuq.astype(BF16), "b_qn_gain": b_qn_gain.reshape(1, -1), "b_qr_gain": tile(b_qr_gain, MXU_DIM),
        "kv_a_gain": kv_a_gain.reshape(1, -1), "w_ukv": w_ukv[0].astype(BF16),
        "b_kn_gain": b_kn_gain.reshape(1, -1), "b_kr_gain": tile(b_kr_gain, LANES),
        "w_oa": w_o[0, :A_V].astype(BF16), "w_ob": w_o[0, A_V:].astype(BF16),
        "w_router": jnp.pad(w_router[0], ((0, 0), (0, LANES - N_EXPERTS))).astype(BF16),
        "seg64": _seg_matrix(A_DH),
    }


def kernel(x_prompt, x_sample, cache_diff_k, cache_diff_v, cache_mla_ckv, cache_mla_krope, c, c_ctx, w_ada, b_ada, norm1_gain, norm2_gain, w_in, a_q_gain, a_k_gain, a_lambda_q1, a_lambda_k1, a_lambda_q2, a_lambda_k2, a_sub_gain, q_a_gain, w_uq, b_qn_gain, b_qr_gain, kv_a_gain, w_ukv, b_kn_gain, b_kr_gain, w_o, w_router, w_exp1, w_exp3, w_exp2):
    nbp, lp, _ = x_prompt.shape
    nbs, ls, _ = x_sample.shape
    past = cache_diff_k.shape[2]
    w = _prep_weights(norm1_gain, norm2_gain, w_in, a_q_gain, a_k_gain, a_lambda_q1, a_lambda_k1, a_lambda_q2,
                      a_lambda_k2, a_sub_gain, q_a_gain, w_uq, b_qn_gain, b_qr_gain, kv_a_gain, w_ukv, b_kn_gain,
                      b_kr_gain, w_o, w_router)

    cvec = jnp.concatenate([c_ctx[None], c, jnp.zeros((8 - 1 - nbs, D_MODEL), F32)], axis=0)
    mod = _ada(cvec, w_ada[0], b_ada)
    mod_p = mod[0:1].reshape(1, 1, -1)
    mod_s = mod[1:1 + nbs].reshape(nbs, 1, -1)

    xp = x_prompt.reshape(nbp * lp, D_MODEL)
    xs = x_sample.reshape(nbs * ls, D_MODEL)
    tm = 256
    aq_p, ak_p, av_p, qm_p, km_p, bv_p, ak32, av32, ckv32, kr32 = _in_proj(xp, mod_p, False, nbp * lp // tm, w, None)
    aq_s, ak_s, av_s, qm_s, km_s, bv_s = _in_proj(xs, mod_s, True, ls // tm, w, _rope_tables(ls))
    km_c, bv_c = _cache_kv(cache_mla_ckv.reshape(nbs * past, KV_RANK), cache_mla_krope.reshape(nbs * past, B_ROPE), w)

    cache_a = (cache_diff_k.reshape(nbs * past, A_QK), cache_diff_v.reshape(nbs * past, A_V))

    oa_p = _diff_attn(aq_p, ak_p, av_p, None, nbp, lp, lp, A_HEADS, w)
    ob_p = _mla_attn(qm_p, km_p, bv_p, None, nbp, lp, lp, B_HEADS)
    oa_s = _diff_attn(aq_s, ak_s, av_s, cache_a, nbs, ls, 256, 2, w)
    ob_s = _mla_attn(qm_s, km_s, bv_s, (km_c, bv_c), nbs, ls, 256, 2)

    x1_p, h2_p, aff_p = _out_proj(oa_p, ob_p, xp, mod_p, nbp * lp // tm, w, "prompt")
    x1_s, h2_s, aff_s = _out_proj(oa_s, ob_s, xs, mod_s, ls // tm, w, "sample")

    idx_p, rk_p, gate_p, st_p, cnt_p, km_p_ = _route(aff_p, "prompt")
    idx_s, rk_s, gate_s, st_s, cnt_s, km_s_ = _route(aff_s, "sample")
    cap = idx_p.shape[1]
    idx_flat = jnp.stack([idx_p, idx_s]).reshape(-1)
    gate = jnp.concatenate([gate_p, gate_s], axis=1)[..., None]
    ye_p, ye_s = _ffn(idx_flat, h2_p, h2_s, gate, w_exp1[0], w_exp3[0], w_exp2[0], cap)

    y_p = _combine(rk_p, st_p, km_p_, ye_p, x1_p, mod_p, nbp * lp // COMBINE_T, cnt_p, cap, "prompt")
    y_s = _combine(rk_s, st_s, km_s_, ye_s, x1_s, mod_s, ls // COMBINE_T, cnt_s, cap, "sample")

    return (y_p.reshape(nbp, lp, D_MODEL), y_s.reshape(nbs, ls, D_MODEL),
            ak32.reshape(nbp, 1, lp, A_HEADS, 2, A_DH), av32.reshape(nbp, 1, lp, A_HEADS, A_DV),
            ckv32.reshape(nbp, 1, lp, KV_RANK), kr32.reshape(nbp, 1, lp, B_ROPE))
```

```python
import functools
import math

import numpy as np
import jax
import jax.numpy as jnp
from jax import lax
from jax.experimental import pallas as pl
from jax.experimental.pallas import tpu as pltpu

F32 = jnp.float32
BF16 = jnp.bfloat16

D_MODEL = 2048
GRID_W = 64
A_HEADS = 8
A_DH = 64
A_DV = 128
B_HEADS = 8
B_NOPE = 128
B_ROPE = 64
B_DV = 128
Q_RANK = 512
KV_RANK = 256
N_EXPERTS = 16
EXPERT_FF = 1536
CAPACITY_FACTOR = 2
ROPE_BASE = 10000.0
EPS = 1e-6
LAM_INIT = 0.8 - 0.6 * math.exp(-0.3 * 0)
LOG2E = math.log2(math.e)

A_QK = A_HEADS * 2 * A_DH
A_V = A_HEADS * A_DV
B_QK = B_NOPE + B_ROPE
IN_COLS = 2 * A_QK + A_V + Q_RANK + KV_RANK + B_ROPE
LANES = 128
MXU_DIM = 256
VMEM_LIMIT = 56 * 1024 * 1024


def _cparams(sem):
    return pltpu.CompilerParams(dimension_semantics=sem, vmem_limit_bytes=VMEM_LIMIT)


def _resident(shape):
    nd = len(shape)
    return pl.BlockSpec(shape, lambda *_: (0,) * nd, pipeline_mode=pl.Buffered(1))


def _rms(x, gain):
    return x * lax.rsqrt(jnp.mean(x * x, axis=-1, keepdims=True) + EPS) * gain


def _seg_rms(y, segmat, gain):
    y2 = (y * y).astype(BF16)
    outs = []
    for c in range(y.shape[1] // MXU_DIM):
        sl = slice(c * MXU_DIM, (c + 1) * MXU_DIM)
        ms = jnp.dot(y2[:, sl], segmat, preferred_element_type=F32)
        outs.append(y[:, sl] * lax.rsqrt(ms + EPS) * gain)
    return jnp.concatenate(outs, axis=1)


def _rope(y, cos, sin):
    outs = []
    for c in range(y.shape[1] // LANES):
        yc = y[:, c * LANES:(c + 1) * LANES]
        lane = lax.broadcasted_iota(jnp.int32, yc.shape, 1)
        partner = jnp.where((lane & 16) == 0, pltpu.roll(yc, LANES - 16, 1), pltpu.roll(yc, 16, 1))
        outs.append(yc * cos + partner * sin)
    return outs[0] if len(outs) == 1 else jnp.concatenate(outs, axis=1)


def _rope_tables(n_tok):
    t = np.arange(n_tok)
    row, col = t // GRID_W, t % GRID_W
    nf = A_DH // 4
    inv = ROPE_BASE ** (-np.arange(nf, dtype=np.float64) * 2.0 / (A_DH // 2))
    lane = np.arange(A_DH)
    pos = np.where(lane[None, :] < A_DH // 2, row[:, None], col[:, None]).astype(np.float64)
    ang = pos * inv[lane % nf][None, :]
    sign = np.where((lane % (2 * nf)) < nf, -1.0, 1.0)[None, :]
    cos = np.tile(np.cos(ang), (1, LANES // A_DH)).astype(np.float32)
    sin = np.tile(np.sin(ang) * sign, (1, LANES // A_DH)).astype(np.float32)
    return jnp.asarray(cos), jnp.asarray(sin)


def _seg_matrix(width):
    i = np.arange(MXU_DIM)
    return jnp.asarray(((i[:, None] // width) == (i[None, :] // width)).astype(np.float32) / width, dtype=BF16)


def _ada_kernel(c_ref, w_ref, b_ref, o_ref):
    c = c_ref[...]
    s = c * jax.nn.sigmoid(c)
    o_ref[...] = jnp.dot(s.astype(BF16), w_ref[...].astype(BF16), preferred_element_type=F32) + b_ref[...]


def _ada(cvec, w_ada, b_ada):
    tn = 1024
    n = w_ada.shape[1]
    return pl.pallas_call(
        _ada_kernel,
        grid=(n // tn,),
        in_specs=[pl.BlockSpec((8, D_MODEL), lambda j: (0, 0)),
                  pl.BlockSpec((D_MODEL, tn), lambda j: (0, j)),
                  pl.BlockSpec((1, tn), lambda j: (0, j))],
        out_specs=pl.BlockSpec((8, tn), lambda j: (0, j)),
        out_shape=jax.ShapeDtypeStruct((8, n), F32),
        compiler_params=_cparams(("arbitrary",)),
        name="ada_mod",
    )(cvec, w_ada, b_ada)


def _expand_kv(ckv_n, kr, wukv_ref, kng, km_ref, bv_ref):
    kv = jnp.dot(ckv_n.astype(BF16), wukv_ref[...], preferred_element_type=F32)
    krb = kr[:, :B_ROPE].astype(BF16)
    for h in range(B_HEADS):
        base = h * (B_NOPE + B_DV)
        kn = _rms(kv[:, base:base + B_NOPE], kng)
        km_ref[h] = jnp.concatenate([kn.astype(BF16), krb], axis=1)
        bv_ref[h] = kv[:, base + B_NOPE:base + B_NOPE + B_DV].astype(BF16)


def _in_kernel(sample, *refs):
    (x_ref, sh_ref, sc_ref, n1g_ref, win_ref, aqg_ref, akg_ref, qag_ref, wuq_ref, qng_ref, qrg_ref,
     kvag_ref, wukv_ref, kng_ref, krg_ref, seg_ref) = refs[:16]
    refs = refs[16:]
    if sample:
        cos_ref, sin_ref = refs[:2]
        refs = refs[2:]
        cos, sin = cos_ref[...], sin_ref[...]
    aq_ref, ak_ref, av_ref, qm_ref, km_ref, bv_ref = refs[:6]
    refs = refs[6:]

    x = x_ref[...]
    h = _rms(x, n1g_ref[...]) * (1.0 + sc_ref[...]) + sh_ref[...]
    hb = h.astype(BF16)

    def proj(c0, c1):
        return jnp.dot(hb, win_ref[:, c0:c1], preferred_element_type=F32)

    seg = seg_ref[...]
    aq = _seg_rms(proj(0, A_QK), seg, aqg_ref[...])
    ak = _seg_rms(proj(A_QK, 2 * A_QK), seg, akg_ref[...])
    if sample:
        aq = _rope(aq, cos, sin)
        ak = _rope(ak, cos, sin)
    aq_ref[...] = (aq * (A_DH ** -0.5 * LOG2E)).astype(BF16)
    ak_ref[...] = ak.astype(BF16)
    av = proj(2 * A_QK, 2 * A_QK + A_V)
    av_ref[...] = av.astype(BF16)

    c0 = 2 * A_QK + A_V
    q_lat = _rms(proj(c0, c0 + Q_RANK), qag_ref[...])
    qb = jnp.dot(q_lat.astype(BF16), wuq_ref[...], preferred_element_type=F32)
    qr = _seg_rms(qb[:, B_HEADS * B_NOPE:], seg, qrg_ref[...])
    if sample:
        qr = _rope(qr, cos, sin)
    scale = B_QK ** -0.5 * LOG2E
    for hd in range(B_HEADS):
        qn = _rms(qb[:, hd * B_NOPE:(hd + 1) * B_NOPE], qng_ref[...])
        qm_ref[hd] = jnp.concatenate(
            [(qn * scale).astype(BF16), (qr[:, hd * B_ROPE:(hd + 1) * B_ROPE] * scale).astype(BF16)], axis=1)

    c0 += Q_RANK
    ckv = _rms(proj(c0, c0 + KV_RANK), kvag_ref[...])
    c0 += KV_RANK
    kr64 = proj(c0, c0 + B_ROPE)
    kr_raw = jnp.concatenate([kr64, jnp.zeros_like(kr64)], axis=1)
    kr = kr_raw * lax.rsqrt(jnp.sum(kr_raw * kr_raw, axis=-1, keepdims=True) * (1.0 / B_ROPE) + EPS) * krg_ref[...]
    if sample:
        kr = _rope(kr, cos, sin)
    _expand_kv(ckv, kr, wukv_ref, kng_ref[...], km_ref, bv_ref)
    if not sample:
        ak32_ref, av32_ref, ckv32_ref, kr32_ref = refs
        ak32_ref[...] = ak.reshape(ak32_ref.shape)
        av32_ref[...] = av
        ckv32_ref[...] = ckv
        kr32_ref[...] = kr[:, :B_ROPE]


def _in_proj(x, mod, sample, tiles_per_mod, w, tables):
    m = x.shape[0]
    tm = 256
    nt = m // tm
    row = lambda i: (i, 0)
    modspec = lambda k: pl.BlockSpec((None, 1, D_MODEL), lambda i: (i // tiles_per_mod, 0, k))
    in_specs = [pl.BlockSpec((tm, D_MODEL), row), modspec(0), modspec(1), _resident((1, D_MODEL)),
                _resident((D_MODEL, IN_COLS)), _resident((1, MXU_DIM)), _resident((1, MXU_DIM)),
                _resident((1, Q_RANK)), _resident((Q_RANK, B_HEADS * B_QK)), _resident((1, B_NOPE)),
                _resident((1, MXU_DIM)), _resident((1, KV_RANK)), _resident((KV_RANK, B_HEADS * (B_NOPE + B_DV))),
                _resident((1, B_NOPE)), _resident((1, LANES)), _resident((MXU_DIM, MXU_DIM))]
    args = [x, mod, mod, w["norm1_gain"], w["w_in"], w["a_q_gain"], w["a_k_gain"], w["q_a_gain"], w["w_uq"],
            w["b_qn_gain"], w["b_qr_gain"], w["kv_a_gain"], w["w_ukv"], w["b_kn_gain"], w["b_kr_gain"], w["seg64"]]
    if sample:
        per = tables[0].shape[0] // tm
        in_specs += [pl.BlockSpec((tm, LANES), lambda i: (i % per, 0))] * 2
        args += list(tables)
    out_shape = [jax.ShapeDtypeStruct((m, A_QK), BF16), jax.ShapeDtypeStruct((m, A_QK), BF16),
                 jax.ShapeDtypeStruct((m, A_V), BF16), jax.ShapeDtypeStruct((B_HEADS, m, B_QK), BF16),
                 jax.ShapeDtypeStruct((B_HEADS, m, B_QK), BF16), jax.ShapeDtypeStruct((B_HEADS, m, B_DV), BF16)]
    hspec = lambda d: pl.BlockSpec((B_HEADS, tm, d), lambda i: (0, i, 0))
    out_specs = [pl.BlockSpec((tm, A_QK), row), pl.BlockSpec((tm, A_QK), row), pl.BlockSpec((tm, A_V), row),
                 hspec(B_QK), hspec(B_QK), hspec(B_DV)]
    if not sample:
        out_shape += [jax.ShapeDtypeStruct((nt, 1, tm, A_HEADS, 2, A_DH), F32), jax.ShapeDtypeStruct((m, A_V), F32),
                      jax.ShapeDtypeStruct((m, KV_RANK), F32), jax.ShapeDtypeStruct((m, B_ROPE), F32)]
        out_specs += [pl.BlockSpec((None, None, tm, A_HEADS, 2, A_DH), lambda i: (i, 0, 0, 0, 0, 0)),
                      pl.BlockSpec((tm, A_V), row),
                      pl.BlockSpec((tm, KV_RANK), row), pl.BlockSpec((tm, B_ROPE), row)]
    return pl.pallas_call(
        functools.partial(_in_kernel, sample),
        grid=(nt,), in_specs=in_specs, out_specs=out_specs, out_shape=out_shape,
        compiler_params=_cparams(("arbitrary",)),
        name="in_proj_sample" if sample else "in_proj_prompt",
    )(*args)


def _cache_kv_kernel(ckv_ref, kr_ref, wukv_ref, kng_ref, km_ref, bv_ref):
    _expand_kv(ckv_ref[...], kr_ref[...], wukv_ref, kng_ref[...], km_ref, bv_ref)


def _cache_kv(ckv, kr, w):
    m = ckv.shape[0]
    return pl.pallas_call(
        _cache_kv_kernel,
        grid=(1,),
        in_specs=[_resident((m, KV_RANK)), _resident((m, B_ROPE)),
                  _resident((KV_RANK, B_HEADS * (B_NOPE + B_DV))), _resident((1, B_NOPE))],
        out_specs=[pl.BlockSpec((B_HEADS, m, B_QK), lambda i: (0, 0, 0)),
                   pl.BlockSpec((B_HEADS, m, B_DV), lambda i: (0, 0, 0))],
        out_shape=[jax.ShapeDtypeStruct((B_HEADS, m, B_QK), BF16), jax.ShapeDtypeStruct((B_HEADS, m, B_DV), BF16)],
        compiler_params=_cparams(("arbitrary",)),
        name="cache_kv",
    )(ckv, kr, w["w_ukv"], w["b_kn_gain"])


def _qk(q, k):
    return lax.dot_general(q, k, (((1,), (1,)), ((), ())), preferred_element_type=F32)


KEY_CHUNK = 512


def _scores(q, k, kc):
    s = _qk(q, k)
    return s if kc is None else jnp.concatenate([s, _qk(q, kc)], axis=1)


def _weighted(p, v, vc):
    if vc is None:
        return jnp.dot(p, v, preferred_element_type=F32)
    lk = v.shape[0]
    return (jnp.dot(p[:, :lk], v, preferred_element_type=F32)
            + jnp.dot(p[:, lk:], vc, preferred_element_type=F32))


def _diff_attn_kernel(hb, cached, q_ref, k_ref, v_ref, *refs):
    if cached:
        kc_ref, vc_ref = refs[:2]
        refs = refs[2:]
    lq1, lk1, lq2, lk2, sub_ref, o_ref = refs
    lam = (jnp.exp(jnp.sum(lq1[...] * lk1[...], axis=-1, keepdims=True))
           - jnp.exp(jnp.sum(lq2[...] * lk2[...], axis=-1, keepdims=True)) + LAM_INIT)
    for h in range(hb):
        sl = slice(h * LANES, (h + 1) * LANES)
        q, k, v = q_ref[:, sl], k_ref[:, sl], v_ref[:, sl]
        kc = kc_ref[:, sl].astype(BF16) if cached else None
        vc = vc_ref[:, sl].astype(BF16) if cached else None
        lane = lax.broadcasted_iota(jnp.int32, q.shape, 1)
        zero = jnp.zeros_like(q)
        s1 = _scores(jnp.where(lane < A_DH, q, zero), k, kc)
        s2 = _scores(jnp.where(lane >= A_DH, q, zero), k, kc)
        e1 = jnp.exp2(s1 - jnp.max(s1, axis=-1, keepdims=True))
        e2 = jnp.exp2(s2 - jnp.max(s2, axis=-1, keepdims=True))
        l1 = jnp.sum(e1, axis=-1, keepdims=True)
        l2 = jnp.sum(e2, axis=-1, keepdims=True)
        p = e1 - e2 * (lam * l1 / l2)
        o = _weighted(p.astype(BF16), v, vc) * (1.0 / l1)
        o_ref[:, sl] = (_rms(o, sub_ref[...]) * (1.0 - LAM_INIT)).astype(BF16)


def _diff_attn(q, k, v, cache, nb, lq, tq, hb, w):
    nq = lq // tq
    vec = _resident((1, A_DH))
    kv = lambda rows: pl.BlockSpec((rows, hb * LANES), lambda b, g, i: (b, g))
    in_specs = [pl.BlockSpec((tq, hb * LANES), lambda b, g, i: (b * nq + i, g)), kv(lq), kv(lq)]
    args = [q, k, v]
    if cache is not None:
        past = cache[0].shape[0] // nb
        in_specs += [kv(past), kv(past)]
        args += list(cache)
    return pl.pallas_call(
        functools.partial(_diff_attn_kernel, hb, cache is not None),
        grid=(nb, A_HEADS // hb, nq),
        in_specs=in_specs + [vec, vec, vec, vec, _resident((1, A_DV))],
        out_specs=pl.BlockSpec((tq, hb * LANES), lambda b, g, i: (b * nq + i, g)),
        out_shape=jax.ShapeDtypeStruct((nb * lq, A_V), BF16),
        compiler_params=_cparams(("arbitrary", "arbitrary", "arbitrary")),
        name="diff_attn_cached" if cache is not None else "diff_attn",
    )(*args, w["a_lambda_q1"], w["a_lambda_k1"], w["a_lambda_q2"], w["a_lambda_k2"], w["a_sub_gain"])


def _mla_attn_kernel(hb, cached, q_ref, k_ref, v_ref, *refs):
    if cached:
        kc_ref, vc_ref, o_ref = refs
    else:
        (o_ref,) = refs
    lk = k_ref.shape[1]
    ck = min(lk, KEY_CHUNK)
    for h in range(hb):
        q = q_ref[h]
        kv = [(k_ref[h, c * ck:(c + 1) * ck], v_ref[h, c * ck:(c + 1) * ck]) for c in range(lk // ck)]
        if cached:
            kv.append((kc_ref[h], vc_ref[h]))
        s = [_qk(q, k) for k, _ in kv]
        m = functools.reduce(jnp.maximum, [jnp.max(sc, axis=-1, keepdims=True) for sc in s])
        l = 0.0
        o = 0.0
        for sc, (_, v) in zip(s, kv):
            e = jnp.exp2(sc - m)
            l = l + jnp.sum(e, axis=-1, keepdims=True)
            o = o + jnp.dot(e.astype(BF16), v, preferred_element_type=F32)
        o_ref[:, h * LANES:(h + 1) * LANES] = (o * (1.0 / l)).astype(BF16)


def _mla_attn(q, k, v, cache, nb, lq, tq, hb):
    nq = lq // tq
    kv = lambda rows, d: pl.BlockSpec((hb, rows, d), lambda b, g, i: (g, b, 0))
    in_specs = [pl.BlockSpec((hb, tq, B_QK), lambda b, g, i: (g, b * nq + i, 0)), kv(lq, B_QK), kv(lq, B_DV)]
    args = [q, k, v]
    if cache is not None:
        past = cache[0].shape[1] // nb
        in_specs += [kv(past, B_QK), kv(past, B_DV)]
        args += list(cache)
    return pl.pallas_call(
        functools.partial(_mla_attn_kernel, hb, cache is not None),
        grid=(nb, B_HEADS // hb, nq),
        in_specs=in_specs,
        out_specs=pl.BlockSpec((tq, hb * LANES), lambda b, g, i: (b * nq + i, g)),
        out_shape=jax.ShapeDtypeStruct((nb * lq, B_HEADS * B_DV), BF16),
        compiler_params=_cparams(("arbitrary", "arbitrary", "arbitrary")),
        name="mla_attn_cached" if cache is not None else "mla_attn",
    )(*args)


def _out_kernel(oa_ref, ob_ref, x_ref, g1_ref, sh2_ref, sc2_ref, n2g_ref, woa_ref, wob_ref, wr_ref,
                x1_ref, h2_ref, aff_ref):
    o = (jnp.dot(oa_ref[...], woa_ref[...], preferred_element_type=F32)
         + jnp.dot(ob_ref[...], wob_ref[...], preferred_element_type=F32))
    x1 = x_ref[...] + g1_ref[...] * o
    x1_ref[...] = x1
    h2 = _rms(x1, n2g_ref[...]) * (1.0 + sc2_ref[...]) + sh2_ref[...]
    h2_ref[...] = h2
    logits = jnp.dot(h2.astype(BF16), wr_ref[...], preferred_element_type=F32)
    lane = lax.broadcasted_iota(jnp.int32, logits.shape, 1)
    logits = jnp.where(lane < N_EXPERTS, logits, -jnp.inf)
    e = jnp.exp(logits - jnp.max(logits, axis=-1, keepdims=True))
    aff_ref[...] = e / jnp.sum(e, axis=-1, keepdims=True)


def _out_proj(oa, ob, x, mod, tiles_per_mod, w, tag):
    m = x.shape[0]
    tm = 256
    row = lambda i: (i, 0)
    modspec = lambda k: pl.BlockSpec((None, 1, D_MODEL), lambda i: (i // tiles_per_mod, 0, k))
    return pl.pallas_call(
        _out_kernel,
        grid=(m // tm,),
        in_specs=[pl.BlockSpec((tm, A_V), row), pl.BlockSpec((tm, A_V), row), pl.BlockSpec((tm, D_MODEL), row),
                  modspec(2), modspec(3), modspec(4), _resident((1, D_MODEL)),
                  _resident((A_V, D_MODEL)), _resident((A_V, D_MODEL)), _resident((D_MODEL, LANES))],
        out_specs=[pl.BlockSpec((tm, D_MODEL), row), pl.BlockSpec((tm, D_MODEL), row),
                   pl.BlockSpec((tm, LANES), row)],
        out_shape=[jax.ShapeDtypeStruct((m, D_MODEL), F32), jax.ShapeDtypeStruct((m, D_MODEL), F32),
                   jax.ShapeDtypeStruct((m, LANES), F32)],
        compiler_params=_cparams(("arbitrary",)),
        name="out_proj_" + tag,
    )(oa, ob, x, mod, mod, mod, w["norm2_gain"], w["w_oa"], w["w_ob"], w["w_router"])


ROUTE_BLK = 256
COMBINE_T = 128
RANK_BITS = 4


def _route_kernel(cap, aff_ref, st_ref, cnt_ref, kmax_ref, posm_ref, pk_ref):
    n = aff_ref.shape[0]
    nblk = n // ROUTE_BLK
    aff = aff_ref[...]
    lane = lax.broadcasted_iota(jnp.int32, (n, LANES), 1)
    tok = lax.broadcasted_iota(jnp.int32, (n, LANES), 0)

    def narrow(carry):
        lo, hi = carry
        mid = lo + (hi - lo) * 0.5
        mid = jnp.where(mid < hi, mid, lo)
        above = aff > mid
        few = jnp.sum(above.astype(jnp.int32), axis=0, keepdims=True) < cap
        up = jnp.min(jnp.where(above, aff, jnp.inf), axis=0, keepdims=True)
        dn = jnp.max(jnp.where(above, -jnp.inf, aff), axis=0, keepdims=True)
        return jnp.where(few, lo, up), jnp.where(few, dn, hi)

    bounds = (jnp.min(aff, axis=0, keepdims=True), jnp.max(aff, axis=0, keepdims=True))
    thr, _ = lax.while_loop(lambda c: jnp.max((c[0] < c[1]).astype(jnp.int32)) > 0, narrow, bounds)
    gt = aff > thr
    eq = aff == thr
    need = cap - jnp.sum(gt.astype(jnp.int32), axis=0, keepdims=True)

    r = lax.broadcasted_iota(jnp.int32, (ROUTE_BLK, ROUTE_BLK), 0)
    c = lax.broadcasted_iota(jnp.int32, (ROUTE_BLK, ROUTE_BLK), 1)
    tri = (c < r).astype(BF16)

    def prefix(mask):
        m = mask.astype(F32)
        carry = jnp.zeros((1, LANES), F32)
        outs = []
        for b in range(nblk):
            mb = m[b * ROUTE_BLK:(b + 1) * ROUTE_BLK]
            outs.append(jnp.dot(tri, mb.astype(BF16), preferred_element_type=F32) + carry)
            carry = carry + jnp.sum(mb, axis=0, keepdims=True)
        return jnp.concatenate(outs, axis=0).astype(jnp.int32)

    sel = (gt | (eq & (prefix(eq) < need))) & (lane < N_EXPERTS)
    pos = prefix(sel)
    self_ = sel.astype(BF16)
    er = lax.broadcasted_iota(jnp.int32, (LANES, LANES), 0)
    ec = lax.broadcasted_iota(jnp.int32, (LANES, LANES), 1)
    rank = jnp.dot(self_, (er < ec).astype(BF16), preferred_element_type=F32).astype(jnp.int32)
    cnt = jnp.dot(self_, jnp.ones((LANES, LANES), BF16), preferred_element_type=F32).astype(jnp.int32)
    cnt_ref[...] = cnt
    nt = n // COMBINE_T
    kmax_ref[...] = jnp.max(cnt.reshape(nt, COMBINE_T, LANES), axis=1)

    posm_ref[...] = pos
    st_ref[...] = jnp.zeros(st_ref.shape, jnp.int32)
    st_ref[0:nt, :] = posm_ref[pl.ds(0, nt, stride=COMBINE_T), :]
    st_ref[nt:nt + 1, :] = jnp.full((1, LANES), cap, jnp.int32)
    posm_ref[...] = jnp.where(sel, pos, -1)
    pk_ref[...] = (tok << RANK_BITS) + rank


def _compact_kernel(cap, st_s, posm_ref, pk_ref, aff_ref, idx_ref, rk_ref, gate_ref, accp, accg):
    tt = COMBINE_T
    nt = posm_ref.shape[0] // tt
    accp[...] = jnp.zeros(accp.shape, jnp.int32)
    accg[...] = jnp.zeros(accg.shape, F32)
    jio = lax.broadcasted_iota(jnp.int32, (tt, LANES), 1)

    def tile_body(t, carry):
        rows = pl.ds(pl.multiple_of(t * tt, tt), tt)
        posm, pk, af = posm_ref[rows, :], pk_ref[rows, :], aff_ref[rows, :]
        for e in range(N_EXPERTS):
            g0 = jnp.minimum(lax.shift_right_logical(st_s[t * LANES + e], 7), cap // LANES - 1)
            rel = posm[:, e:e + 1] - g0 * LANES
            for d in range(2):
                oh = rel == jio + d * LANES
                accp[e, g0 + d] += jnp.sum(jnp.where(oh, pk[:, e:e + 1], 0).reshape(tt // 8, 8, LANES), axis=0)
                accg[e, g0 + d] += jnp.sum(jnp.where(oh, af[:, e:e + 1], 0.0).reshape(tt // 8, 8, LANES), axis=0)
        return carry

    lax.fori_loop(0, nt, tile_body, 0)
    for e in range(N_EXPERTS):
        packed = jnp.concatenate([jnp.sum(accp[e, g], axis=0, keepdims=True) for g in range(cap // LANES)], axis=1)
        idx_ref[e:e + 1, :] = packed >> RANK_BITS
        rk_ref[e:e + 1, :] = (packed & (N_EXPERTS - 1)) * tt + ((packed >> RANK_BITS) & (tt - 1))
        gate_ref[e:e + 1, :] = jnp.concatenate(
            [jnp.sum(accg[e, g], axis=0, keepdims=True) for g in range(cap // LANES)], axis=1)


def _route(aff, tag):
    n = aff.shape[0]
    cap = CAPACITY_FACTOR * n // N_EXPERTS
    nt = n // COMBINE_T
    full = lambda s: pl.BlockSpec(s, lambda i, *_: (0,) * len(s))
    i32 = lambda s: jax.ShapeDtypeStruct(s, jnp.int32)
    st, cnt, kmax, posm, pk = pl.pallas_call(
        functools.partial(_route_kernel, cap),
        grid=(1,),
        in_specs=[full((n, LANES))],
        out_specs=[full((nt + 8, LANES)), full((n, LANES)), full((nt, LANES)), full((n, LANES)), full((n, LANES))],
        out_shape=[i32((nt + 8, LANES)), i32((n, LANES)), i32((nt, LANES)), i32((n, LANES)), i32((n, LANES))],
        compiler_params=_cparams(("arbitrary",)),
        name="route_" + tag,
    )(aff)
    groups = cap // LANES + 1
    idx, rk, gate = pl.pallas_call(
        functools.partial(_compact_kernel, cap),
        grid_spec=pltpu.PrefetchScalarGridSpec(
            num_scalar_prefetch=1, grid=(1,),
            in_specs=[full((n, LANES)), full((n, LANES)), full((n, LANES))],
            out_specs=[full((N_EXPERTS, cap)), full((N_EXPERTS, cap)), full((N_EXPERTS, cap))],
            scratch_shapes=[pltpu.VMEM((N_EXPERTS, groups, 8, LANES), jnp.int32),
                            pltpu.VMEM((N_EXPERTS, groups, 8, LANES), F32)]),
        out_shape=[i32((N_EXPERTS, cap)), i32((N_EXPERTS, cap)), jax.ShapeDtypeStruct((N_EXPERTS, cap), F32)],
        compiler_params=_cparams(("arbitrary",)),
        name="compact_" + tag,
    )(st.reshape(-1), posm, pk, aff)
    return idx, rk, gate, st, cnt, kmax


FF_CHUNK = 256


def _ffn_kernel(cap, nf, per, idx_s, hp_hbm, hs_hbm, gate_ref, w1_ref, w3_ref, w2_ref, yp_hbm, ys_hbm,
                xg, xe, acc, gsem, osem):
    e = pl.program_id(0)
    f = pl.program_id(1)
    ne = pl.num_programs(0)
    srcs = (hp_hbm, hs_hbm)
    dsts = (yp_hbm, ys_hbm)
    slot = lax.rem(e, 2)

    def issue_part(ex, part, sl, live):
        for p in range(2):
            for i in range(per):
                j = part * per + i
                jc = jnp.minimum(j, cap - 1)
                tok = idx_s[(p * N_EXPERTS + ex) * cap + jc]

                @pl.when(live & (j < cap))
                def _():
                    pltpu.make_async_copy(srcs[p].at[pl.ds(tok, 1), :], xg.at[sl, pl.ds(p * cap + jc, 1), :],
                                          gsem.at[sl]).start()

    def out_copy(p, ex):
        return pltpu.make_async_copy(acc.at[pl.ds(p * cap, cap), :],
                                     dsts[p].at[pl.ds(pl.multiple_of(ex * cap, cap), cap), :], osem)

    @pl.when((e == 0) & (f == 0))
    def _prologue():
        def part_body(part, carry):
            issue_part(0, part, 0, True)
            return carry
        lax.fori_loop(0, nf, part_body, 0)

    @pl.when(f == 0)
    def _start_expert():
        for p in range(2):
            pltpu.make_async_copy(srcs[p].at[pl.ds(0, cap), :], xg.at[slot, pl.ds(p * cap, cap), :],
                                  gsem.at[slot]).wait()
        xe[...] = xg[slot].astype(BF16)

    issue_part(jnp.minimum(e + 1, ne - 1), f, 1 - slot, e + 1 < ne)

    x = xe[...]
    a = jnp.dot(x, w1_ref[...].astype(BF16), preferred_element_type=F32)
    b = jnp.dot(x, w3_ref[...].astype(BF16), preferred_element_type=F32)
    hid = ((a * jax.nn.sigmoid(a)) * b).astype(BF16)

    def down():
        return jnp.dot(hid, w2_ref[...].astype(BF16), preferred_element_type=F32)

    @pl.when(f == 0)
    def _first():
        @pl.when(e > 0)
        def _():
            for p in range(2):
                out_copy(p, e - 1).wait()
        acc[...] = down()

    @pl.when((f > 0) & (f < nf - 1))
    def _middle():
        acc[...] += down()

    @pl.when(f == nf - 1)
    def _last():
        acc[...] = (acc[...] + down()) * gate_ref[...]
        for p in range(2):
            out_copy(p, e).start()

        @pl.when(e == ne - 1)
        def _():
            for p in range(2):
                out_copy(p, e).wait()


def _ffn(idx_flat, h2p, h2s, gate, w1, w3, w2, cap):
    nf = EXPERT_FF // FF_CHUNK
    per = -(-cap // nf)
    grid_spec = pltpu.PrefetchScalarGridSpec(
        num_scalar_prefetch=1,
        grid=(N_EXPERTS, nf),
        in_specs=[pl.BlockSpec(memory_space=pl.ANY), pl.BlockSpec(memory_space=pl.ANY),
                  pl.BlockSpec((None, 2 * cap, 1), lambda e, f, *_: (e, 0, 0)),
                  pl.BlockSpec((None, D_MODEL, FF_CHUNK), lambda e, f, *_: (e, 0, f)),
                  pl.BlockSpec((None, D_MODEL, FF_CHUNK), lambda e, f, *_: (e, 0, f)),
                  pl.BlockSpec((None, FF_CHUNK, D_MODEL), lambda e, f, *_: (e, f, 0))],
        out_specs=[pl.BlockSpec(memory_space=pl.ANY), pl.BlockSpec(memory_space=pl.ANY)],
        scratch_shapes=[pltpu.VMEM((2, 2 * cap, D_MODEL), F32), pltpu.VMEM((2 * cap, D_MODEL), BF16),
                        pltpu.VMEM((2 * cap, D_MODEL), F32), pltpu.SemaphoreType.DMA((2,)),
                        pltpu.SemaphoreType.DMA(())])
    out = jax.ShapeDtypeStruct((N_EXPERTS * cap, D_MODEL), F32)
    return pl.pallas_call(
        functools.partial(_ffn_kernel, cap, nf, per),
        grid_spec=grid_spec, out_shape=[out, out],
        compiler_params=_cparams(("arbitrary", "arbitrary")),
        name="expert_ffn",
    )(idx_flat, h2p, h2s, gate, w1, w3, w2)


COMBINE_CW = 256
WAIT_ROWS = 32
ISSUE_UNROLL = 4


def _combine_kernel(cap, row_s, st_s, km_s, ye_hbm, x1_ref, g2_ref, cnt_ref, o_ref, buf, sem):
    t = pl.program_id(0)
    nt = pl.num_programs(0)
    tt = COMBINE_T
    slot = lax.rem(t, 2)

    def issue_tile(tile, sl):
        for e in range(N_EXPERTS):
            lo = st_s[tile * LANES + e]
            hi = st_s[(tile + 1) * LANES + e]

            def issue(g, carry):
                for u in range(ISSUE_UNROLL):
                    j = lo + g * ISSUE_UNROLL + u

                    @pl.when(j < hi)
                    def _():
                        pltpu.make_async_copy(ye_hbm.at[pl.ds(e * cap + j, 1), :],
                                              buf.at[sl, pl.ds(row_s[e * cap + j], 1), :], sem.at[sl]).start()
                return carry
            lax.fori_loop(0, lax.div(hi - lo + ISSUE_UNROLL - 1, ISSUE_UNROLL), issue, 0)

    def wait_tile(tile, sl):
        total = 0
        for e in range(N_EXPERTS):
            total = total + st_s[(tile + 1) * LANES + e] - st_s[tile * LANES + e]

        def wait_rows(n):
            def body(i, carry):
                pltpu.make_async_copy(ye_hbm.at[pl.ds(0, n), :], buf.at[sl, pl.ds(0, n), :], sem.at[sl]).wait()
                return carry
            return body
        lax.fori_loop(0, total // WAIT_ROWS, wait_rows(WAIT_ROWS), 0)
        lax.fori_loop(0, lax.rem(total, WAIT_ROWS), wait_rows(1), 0)

    @pl.when(t == 0)
    def _init():
        buf[...] = jnp.zeros(buf.shape, F32)
        issue_tile(0, 0)

    @pl.when(t + 1 < nt)
    def _prefetch():
        issue_tile(t + 1, 1 - slot)

    wait_tile(t, slot)

    kmax = km_s[t * LANES]
    cnt = jnp.tile(cnt_ref[...], (1, COMBINE_CW // LANES))
    for c in range(D_MODEL // COMBINE_CW):
        cs = slice(c * COMBINE_CW, (c + 1) * COMBINE_CW)

        def add_slot(k, a):
            v = buf[slot, pl.ds(pl.multiple_of(k * tt, tt), tt), cs]
            return a + jnp.where(cnt > k, v, 0.0)
        a = lax.fori_loop(0, kmax, add_slot, jnp.zeros((tt, COMBINE_CW), F32))
        o_ref[:, cs] = x1_ref[:, cs] + g2_ref[:, cs] * a


def _combine(rows, st, km, ye, x1, mod, tiles_per_mod, cnt, cap, tag):
    n = x1.shape[0]
    tt = COMBINE_T
    grid_spec = pltpu.PrefetchScalarGridSpec(
        num_scalar_prefetch=3,
        grid=(n // tt,),
        in_specs=[pl.BlockSpec(memory_space=pl.ANY),
                  pl.BlockSpec((tt, D_MODEL), lambda i, *_: (i, 0)),
                  pl.BlockSpec((None, 1, D_MODEL), lambda i, *_: (i // tiles_per_mod, 0, 5)),
                  pl.BlockSpec((tt, LANES), lambda i, *_: (i, 0))],
        out_specs=pl.BlockSpec((tt, D_MODEL), lambda i, *_: (i, 0)),
        scratch_shapes=[pltpu.VMEM((2, N_EXPERTS * tt, D_MODEL), F32), pltpu.SemaphoreType.DMA((2,))])
    return pl.pallas_call(
        functools.partial(_combine_kernel, cap),
        grid_spec=grid_spec, out_shape=jax.ShapeDtypeStruct((n, D_MODEL), F32),
        compiler_params=_cparams(("arbitrary",)),
        name="combine_" + tag,
    )(jnp.pad(rows.reshape(-1), (0, ISSUE_UNROLL)), st.reshape(-1), km.reshape(-1), ye, x1, mod, cnt)


def _prep_weights(norm1_gain, norm2_gain, w_in, a_q_gain, a_k_gain, a_lambda_q1, a_lambda_k1, a_lambda_q2,
                  a_lambda_k2, a_sub_gain, q_a_gain, w_uq, b_qn_gain, b_qr_gain, kv_a_gain, w_ukv, b_kn_gain,
                  b_kr_gain, w_o, w_router):
    tile = lambda g, width: jnp.tile(g.reshape(1, -1), (1, width // g.shape[-1]))
    wuq = w_uq[0].reshape(Q_RANK, B_HEADS, B_QK)
    wuq = jnp.concatenate([wuq[:, :, :B_NOPE].reshape(Q_RANK, -1), wuq[:, :, B_NOPE:].reshape(Q_RANK, -1)], axis=1)
    return {
        "norm1_gain": norm1_gain.reshape(1, -1), "norm2_gain": norm2_gain.reshape(1, -1),
        "w_in": w_in[0].astype(BF16),
        "a_q_gain": tile(a_q_gain, MXU_DIM), "a_k_gain": tile(a_k_gain, MXU_DIM),
        "a_lambda_q1": a_lambda_q1.reshape(1, -1), "a_lambda_k1": a_lambda_k1.reshape(1, -1),
        "a_lambda_q2": a_lambda_q2.reshape(1, -1), "a_lambda_k2": a_lambda_k2.reshape(1, -1),
        "a_sub_gain": a_sub_gain.reshape(1, -1), "q_a_gain": q_a_gain.reshape(1, -1),
        "w_uq": wuq.astype(BF16), "b_qn_gain": b_qn_gain.reshape(1, -1), "b_qr_gain": tile(b_qr_gain, MXU_DIM),
        "kv_a_gain": kv_a_gain.reshape(1, -1), "w_ukv": w_ukv[0].astype(BF16),
        "b_kn_gain": b_kn_gain.reshape(1, -1), "b_kr_gain": tile(b_kr_gain, LANES),
        "w_oa": w_o[0, :A_V].astype(BF16), "w_ob": w_o[0, A_V:].astype(BF16),
        "w_router": jnp.pad(w_router[0], ((0, 0), (0, LANES - N_EXPERTS))).astype(BF16),
        "seg64": _seg_matrix(A_DH),
    }


def kernel(x_prompt, x_sample, cache_diff_k, cache_diff_v, cache_mla_ckv, cache_mla_krope, c, c_ctx, w_ada, b_ada, norm1_gain, norm2_gain, w_in, a_q_gain, a_k_gain, a_lambda_q1, a_lambda_k1, a_lambda_q2, a_lambda_k2, a_sub_gain, q_a_gain, w_uq, b_qn_gain, b_qr_gain, kv_a_gain, w_ukv, b_kn_gain, b_kr_gain, w_o, w_router, w_exp1, w_exp3, w_exp2):
    nbp, lp, _ = x_prompt.shape
    nbs, ls, _ = x_sample.shape
    past = cache_diff_k.shape[2]
    w = _prep_weights(norm1_gain, norm2_gain, w_in, a_q_gain, a_k_gain, a_lambda_q1, a_lambda_k1, a_lambda_q2,
                      a_lambda_k2, a_sub_gain, q_a_gain, w_uq, b_qn_gain, b_qr_gain, kv_a_gain, w_ukv, b_kn_gain,
                      b_kr_gain, w_o, w_router)

    cvec = jnp.concatenate([c_ctx[None], c, jnp.zeros((8 - 1 - nbs, D_MODEL), F32)], axis=0)
    mod = _ada(cvec, w_ada[0], b_ada)
    mod_p = mod[0:1].reshape(1, 1, -1)
    mod_s = mod[1:1 + nbs].reshape(nbs, 1, -1)

    xp = x_prompt.reshape(nbp * lp, D_MODEL)
    xs = x_sample.reshape(nbs * ls, D_MODEL)
    tm = 256
    assert lp == tm, "the prompt projection writes one batch of new_diff_k per row tile"
    aq_p, ak_p, av_p, qm_p, km_p, bv_p, ak32, av32, ckv32, kr32 = _in_proj(xp, mod_p, False, nbp * lp // tm, w, None)
    aq_s, ak_s, av_s, qm_s, km_s, bv_s = _in_proj(xs, mod_s, True, ls // tm, w, _rope_tables(ls))
    km_c, bv_c = _cache_kv(cache_mla_ckv.reshape(nbs * past, KV_RANK), cache_mla_krope.reshape(nbs * past, B_ROPE), w)

    cache_a = (cache_diff_k.reshape(nbs * past, A_QK), cache_diff_v.reshape(nbs * past, A_V))

    oa_p = _diff_attn(aq_p, ak_p, av_p, None, nbp, lp, lp, A_HEADS, w)
    ob_p = _mla_attn(qm_p, km_p, bv_p, None, nbp, lp, lp, B_HEADS)
    oa_s = _diff_attn(aq_s, ak_s, av_s, cache_a, nbs, ls, 256, 2, w)
    ob_s = _mla_attn(qm_s, km_s, bv_s, (km_c, bv_c), nbs, ls, 256, 2)

    x1_p, h2_p, aff_p = _out_proj(oa_p, ob_p, xp, mod_p, nbp * lp // tm, w, "prompt")
    x1_s, h2_s, aff_s = _out_proj(oa_s, ob_s, xs, mod_s, ls // tm, w, "sample")

    idx_p, rk_p, gate_p, st_p, cnt_p, km_p_ = _route(aff_p, "prompt")
    idx_s, rk_s, gate_s, st_s, cnt_s, km_s_ = _route(aff_s, "sample")
    cap = idx_p.shape[1]
    idx_flat = jnp.stack([idx_p, idx_s]).reshape(-1)
    gate = jnp.concatenate([gate_p, gate_s], axis=1)[..., None]
    ye_p, ye_s = _ffn(idx_flat, h2_p, h2_s, gate, w_exp1[0], w_exp3[0], w_exp2[0], cap)

    y_p = _combine(rk_p, st_p, km_p_, ye_p, x1_p, mod_p, nbp * lp // COMBINE_T, cnt_p, cap, "prompt")
    y_s = _combine(rk_s, st_s, km_s_, ye_s, x1_s, mod_s, ls // COMBINE_T, cnt_s, cap, "sample")

    return (y_p.reshape(nbp, lp, D_MODEL), y_s.reshape(nbs, ls, D_MODEL),
            ak32, av32.reshape(nbp, 1, lp, A_HEADS, A_DV),
            ckv32.reshape(nbp, 1, lp, KV_RANK), kr32.reshape(nbp, 1, lp, B_ROPE))
```

```python
import functools
import math

import numpy as np
import jax
import jax.numpy as jnp
from jax import lax
from jax.experimental import pallas as pl
from jax.experimental.pallas import tpu as pltpu

F32 = jnp.float32
BF16 = jnp.bfloat16

D_MODEL = 2048
GRID_W = 64
A_HEADS = 8
A_DH = 64
A_DV = 128
B_HEADS = 8
B_NOPE = 128
B_ROPE = 64
B_DV = 128
Q_RANK = 512
KV_RANK = 256
N_EXPERTS = 16
EXPERT_FF = 1536
CAPACITY_FACTOR = 2
ROPE_BASE = 10000.0
EPS = 1e-6
LAM_INIT = 0.8 - 0.6 * math.exp(-0.3 * 0)
LOG2E = math.log2(math.e)

A_QK = A_HEADS * 2 * A_DH
A_V = A_HEADS * A_DV
B_QK = B_NOPE + B_ROPE
IN_COLS = 2 * A_QK + A_V + Q_RANK + KV_RANK + B_ROPE
LANES = 128
MXU_DIM = 256
VMEM_LIMIT = 56 * 1024 * 1024


def _cparams(sem):
    return pltpu.CompilerParams(dimension_semantics=sem, vmem_limit_bytes=VMEM_LIMIT)


def _resident(shape):
    nd = len(shape)
    return pl.BlockSpec(shape, lambda *_: (0,) * nd, pipeline_mode=pl.Buffered(1))


def _rms(x, gain):
    return x * lax.rsqrt(jnp.mean(x * x, axis=-1, keepdims=True) + EPS) * gain


def _seg_rms(y, segmat, gain):
    y2 = (y * y).astype(BF16)
    outs = []
    for c in range(y.shape[1] // MXU_DIM):
        sl = slice(c * MXU_DIM, (c + 1) * MXU_DIM)
        ms = jnp.dot(y2[:, sl], segmat, preferred_element_type=F32)
        outs.append(y[:, sl] * lax.rsqrt(ms + EPS) * gain)
    return jnp.concatenate(outs, axis=1)


def _rope(y, cos, sin):
    outs = []
    for c in range(y.shape[1] // LANES):
        yc = y[:, c * LANES:(c + 1) * LANES]
        lane = lax.broadcasted_iota(jnp.int32, yc.shape, 1)
        partner = jnp.where((lane & 16) == 0, pltpu.roll(yc, LANES - 16, 1), pltpu.roll(yc, 16, 1))
        outs.append(yc * cos + partner * sin)
    return outs[0] if len(outs) == 1 else jnp.concatenate(outs, axis=1)


def _rope_tables(n_tok):
    t = np.arange(n_tok)
    row, col = t // GRID_W, t % GRID_W
    nf = A_DH // 4
    inv = ROPE_BASE ** (-np.arange(nf, dtype=np.float64) * 2.0 / (A_DH // 2))
    lane = np.arange(A_DH)
    pos = np.where(lane[None, :] < A_DH // 2, row[:, None], col[:, None]).astype(np.float64)
    ang = pos * inv[lane % nf][None, :]
    sign = np.where((lane % (2 * nf)) < nf, -1.0, 1.0)[None, :]
    cos = np.tile(np.cos(ang), (1, LANES // A_DH)).astype(np.float32)
    sin = np.tile(np.sin(ang) * sign, (1, LANES // A_DH)).astype(np.float32)
    return jnp.asarray(cos), jnp.asarray(sin)


def _seg_matrix(width):
    i = np.arange(MXU_DIM)
    return jnp.asarray(((i[:, None] // width) == (i[None, :] // width)).astype(np.float32) / width, dtype=BF16)


def _ada_kernel(c_ref, w_ref, b_ref, o_ref):
    c = c_ref[...]
    s = c * jax.nn.sigmoid(c)
    o_ref[...] = jnp.dot(s.astype(BF16), w_ref[...].astype(BF16), preferred_element_type=F32) + b_ref[...]


def _ada(cvec, w_ada, b_ada):
    tn = 1024
    n = w_ada.shape[1]
    return pl.pallas_call(
        _ada_kernel,
        grid=(n // tn,),
        in_specs=[pl.BlockSpec((8, D_MODEL), lambda j: (0, 0)),
                  pl.BlockSpec((D_MODEL, tn), lambda j: (0, j)),
                  pl.BlockSpec((1, tn), lambda j: (0, j))],
        out_specs=pl.BlockSpec((8, tn), lambda j: (0, j)),
        out_shape=jax.ShapeDtypeStruct((8, n), F32),
        compiler_params=_cparams(("arbitrary",)),
        name="ada_mod",
    )(cvec, w_ada, b_ada)


def _expand_kv(ckv_n, kr, wukv_ref, kng, km_ref, bv_ref):
    kv = jnp.dot(ckv_n.astype(BF16), wukv_ref[...], preferred_element_type=F32)
    krb = kr[:, :B_ROPE].astype(BF16)
    for h in range(B_HEADS):
        base = h * (B_NOPE + B_DV)
        kn = _rms(kv[:, base:base + B_NOPE], kng)
        km_ref[h] = jnp.concatenate([kn.astype(BF16), krb], axis=1)
        bv_ref[h] = kv[:, base + B_NOPE:base + B_NOPE + B_DV].astype(BF16)


def _in_kernel(sample, *refs):
    (x_ref, sh_ref, sc_ref, n1g_ref, win_ref, aqg_ref, akg_ref, qag_ref, wuq_ref, qng_ref, qrg_ref,
     kvag_ref, wukv_ref, kng_ref, krg_ref, seg_ref) = refs[:16]
    refs = refs[16:]
    if sample:
        cos_ref, sin_ref = refs[:2]
        refs = refs[2:]
        cos, sin = cos_ref[...], sin_ref[...]
    aq_ref, ak_ref, av_ref, qm_ref, km_ref, bv_ref = refs[:6]
    refs = refs[6:]

    x = x_ref[...]
    h = _rms(x, n1g_ref[...]) * (1.0 + sc_ref[...]) + sh_ref[...]
    hb = h.astype(BF16)

    def proj(c0, c1):
        return jnp.dot(hb, win_ref[:, c0:c1], preferred_element_type=F32)

    seg = seg_ref[...]
    aq = _seg_rms(proj(0, A_QK), seg, aqg_ref[...])
    ak = _seg_rms(proj(A_QK, 2 * A_QK), seg, akg_ref[...])
    if sample:
        aq = _rope(aq, cos, sin)
        ak = _rope(ak, cos, sin)
    aq_ref[...] = (aq * (A_DH ** -0.5 * LOG2E)).astype(BF16)
    ak_ref[...] = ak.astype(BF16)
    av = proj(2 * A_QK, 2 * A_QK + A_V)
    av_ref[...] = av.astype(BF16)

    c0 = 2 * A_QK + A_V
    q_lat = _rms(proj(c0, c0 + Q_RANK), qag_ref[...])
    qb = jnp.dot(q_lat.astype(BF16), wuq_ref[...], preferred_element_type=F32)
    qr = _seg_rms(qb[:, B_HEADS * B_NOPE:], seg, qrg_ref[...])
    if sample:
        qr = _rope(qr, cos, sin)
    scale = B_QK ** -0.5 * LOG2E
    for hd in range(B_HEADS):
        qn = _rms(qb[:, hd * B_NOPE:(hd + 1) * B_NOPE], qng_ref[...])
        qm_ref[hd] = jnp.concatenate(
            [(qn * scale).astype(BF16), (qr[:, hd * B_ROPE:(hd + 1) * B_ROPE] * scale).astype(BF16)], axis=1)

    c0 += Q_RANK
    ckv = _rms(proj(c0, c0 + KV_RANK), kvag_ref[...])
    c0 += KV_RANK
    kr64 = proj(c0, c0 + B_ROPE)
    kr_raw = jnp.concatenate([kr64, jnp.zeros_like(kr64)], axis=1)
    kr = kr_raw * lax.rsqrt(jnp.sum(kr_raw * kr_raw, axis=-1, keepdims=True) * (1.0 / B_ROPE) + EPS) * krg_ref[...]
    if sample:
        kr = _rope(kr, cos, sin)
    _expand_kv(ckv, kr, wukv_ref, kng_ref[...], km_ref, bv_ref)
    if not sample:
        ak32_ref, av32_ref, ckv32_ref, kr32_ref = refs
        ak32_ref[...] = ak.reshape(ak32_ref.shape)
        av32_ref[...] = av
        ckv32_ref[...] = ckv
        kr32_ref[...] = kr[:, :B_ROPE]


def _in_proj(x, mod, sample, tiles_per_mod, w, tables):
    m = x.shape[0]
    tm = 256
    nt = m // tm
    row = lambda i: (i, 0)
    modspec = lambda k: pl.BlockSpec((None, 1, D_MODEL), lambda i: (i // tiles_per_mod, 0, k))
    in_specs = [pl.BlockSpec((tm, D_MODEL), row), modspec(0), modspec(1), _resident((1, D_MODEL)),
                _resident((D_MODEL, IN_COLS)), _resident((1, MXU_DIM)), _resident((1, MXU_DIM)),
                _resident((1, Q_RANK)), _resident((Q_RANK, B_HEADS * B_QK)), _resident((1, B_NOPE)),
                _resident((1, MXU_DIM)), _resident((1, KV_RANK)), _resident((KV_RANK, B_HEADS * (B_NOPE + B_DV))),
                _resident((1, B_NOPE)), _resident((1, LANES)), _resident((MXU_DIM, MXU_DIM))]
    args = [x, mod, mod, w["norm1_gain"], w["w_in"], w["a_q_gain"], w["a_k_gain"], w["q_a_gain"], w["w_uq"],
            w["b_qn_gain"], w["b_qr_gain"], w["kv_a_gain"], w["w_ukv"], w["b_kn_gain"], w["b_kr_gain"], w["seg64"]]
    if sample:
        per = tables[0].shape[0] // tm
        in_specs += [pl.BlockSpec((tm, LANES), lambda i: (i % per, 0))] * 2
        args += list(tables)
    out_shape = [jax.ShapeDtypeStruct((m, A_QK), BF16), jax.ShapeDtypeStruct((m, A_QK), BF16),
                 jax.ShapeDtypeStruct((m, A_V), BF16), jax.ShapeDtypeStruct((B_HEADS, m, B_QK), BF16),
                 jax.ShapeDtypeStruct((B_HEADS, m, B_QK), BF16), jax.ShapeDtypeStruct((B_HEADS, m, B_DV), BF16)]
    hspec = lambda d: pl.BlockSpec((B_HEADS, tm, d), lambda i: (0, i, 0))
    out_specs = [pl.BlockSpec((tm, A_QK), row), pl.BlockSpec((tm, A_QK), row), pl.BlockSpec((tm, A_V), row),
                 hspec(B_QK), hspec(B_QK), hspec(B_DV)]
    if not sample:
        out_shape += [jax.ShapeDtypeStruct((nt, 1, tm, A_HEADS, 2, A_DH), F32), jax.ShapeDtypeStruct((m, A_V), F32),
                      jax.ShapeDtypeStruct((m, KV_RANK), F32), jax.ShapeDtypeStruct((m, B_ROPE), F32)]
        out_specs += [pl.BlockSpec((None, None, tm, A_HEADS, 2, A_DH), lambda i: (i, 0, 0, 0, 0, 0)),
                      pl.BlockSpec((tm, A_V), row),
                      pl.BlockSpec((tm, KV_RANK), row), pl.BlockSpec((tm, B_ROPE), row)]
    return pl.pallas_call(
        functools.partial(_in_kernel, sample),
        grid=(nt,), in_specs=in_specs, out_specs=out_specs, out_shape=out_shape,
        compiler_params=_cparams(("arbitrary",)),
        name="in_proj_sample" if sample else "in_proj_prompt",
    )(*args)


def _cache_kv_kernel(ckv_ref, kr_ref, wukv_ref, kng_ref, km_ref, bv_ref):
    _expand_kv(ckv_ref[...], kr_ref[...], wukv_ref, kng_ref[...], km_ref, bv_ref)


def _cache_kv(ckv, kr, w):
    m = ckv.shape[0]
    return pl.pallas_call(
        _cache_kv_kernel,
        grid=(1,),
        in_specs=[_resident((m, KV_RANK)), _resident((m, B_ROPE)),
                  _resident((KV_RANK, B_HEADS * (B_NOPE + B_DV))), _resident((1, B_NOPE))],
        out_specs=[pl.BlockSpec((B_HEADS, m, B_QK), lambda i: (0, 0, 0)),
                   pl.BlockSpec((B_HEADS, m, B_DV), lambda i: (0, 0, 0))],
        out_shape=[jax.ShapeDtypeStruct((B_HEADS, m, B_QK), BF16), jax.ShapeDtypeStruct((B_HEADS, m, B_DV), BF16)],
        compiler_params=_cparams(("arbitrary",)),
        name="cache_kv",
    )(ckv, kr, w["w_ukv"], w["b_kn_gain"])


def _qk(q, k):
    return lax.dot_general(q, k, (((1,), (1,)), ((), ())), preferred_element_type=F32)


KEY_CHUNK = 512
SCORE_AHEAD = 1


def _scores(q, k, kc):
    s = _qk(q, k)
    return s if kc is None else jnp.concatenate([s, _qk(q, kc)], axis=1)


def _weighted(p, v, vc):
    if vc is None:
        return jnp.dot(p, v, preferred_element_type=F32)
    lk = v.shape[0]
    return (jnp.dot(p[:, :lk], v, preferred_element_type=F32)
            + jnp.dot(p[:, lk:], vc, preferred_element_type=F32))


def _diff_attn_kernel(hb, cached, q_ref, k_ref, v_ref, *refs):
    if cached:
        kc_ref, vc_ref = refs[:2]
        refs = refs[2:]
    lq1, lk1, lq2, lk2, sub_ref, o_ref = refs
    lam = (jnp.exp(jnp.sum(lq1[...] * lk1[...], axis=-1, keepdims=True))
           - jnp.exp(jnp.sum(lq2[...] * lk2[...], axis=-1, keepdims=True)) + LAM_INIT)
    def scores(h, comp):
        sl = slice(h * LANES, (h + 1) * LANES)
        q, k = q_ref[:, sl], k_ref[:, sl]
        kc = kc_ref[:, sl].astype(BF16) if cached else None
        lane = lax.broadcasted_iota(jnp.int32, q.shape, 1)
        keep = (lane < A_DH) if comp == 0 else (lane >= A_DH)
        return _scores(jnp.where(keep, q, jnp.zeros_like(q)), k, kc)

    s1_next, s2_next = scores(0, 0), scores(0, 1)
    for h in range(hb):
        sl = slice(h * LANES, (h + 1) * LANES)
        v = v_ref[:, sl]
        vc = vc_ref[:, sl].astype(BF16) if cached else None
        s1, s2 = s1_next, s2_next
        if h + 1 < hb:
            s1_next = scores(h + 1, 0)
        e1 = jnp.exp2(s1 - jnp.max(s1, axis=-1, keepdims=True))
        l1 = jnp.sum(e1, axis=-1, keepdims=True)
        if h + 1 < hb:
            s2_next = scores(h + 1, 1)
        e2 = jnp.exp2(s2 - jnp.max(s2, axis=-1, keepdims=True))
        l2 = jnp.sum(e2, axis=-1, keepdims=True)
        p = e1 - e2 * (lam * l1 / l2)
        o = _weighted(p.astype(BF16), v, vc) * (1.0 / l1)
        o_ref[:, sl] = (_rms(o, sub_ref[...]) * (1.0 - LAM_INIT)).astype(BF16)


def _diff_attn(q, k, v, cache, nb, lq, tq, hb, w):
    nq = lq // tq
    vec = _resident((1, A_DH))
    kv = lambda rows: pl.BlockSpec((rows, hb * LANES), lambda b, g, i: (b, g))
    in_specs = [pl.BlockSpec((tq, hb * LANES), lambda b, g, i: (b * nq + i, g)), kv(lq), kv(lq)]
    args = [q, k, v]
    if cache is not None:
        past = cache[0].shape[0] // nb
        in_specs += [kv(past), kv(past)]
        args += list(cache)
    return pl.pallas_call(
        functools.partial(_diff_attn_kernel, hb, cache is not None),
        grid=(nb, A_HEADS // hb, nq),
        in_specs=in_specs + [vec, vec, vec, vec, _resident((1, A_DV))],
        out_specs=pl.BlockSpec((tq, hb * LANES), lambda b, g, i: (b * nq + i, g)),
        out_shape=jax.ShapeDtypeStruct((nb * lq, A_V), BF16),
        compiler_params=_cparams(("arbitrary", "arbitrary", "arbitrary")),
        name="diff_attn_cached" if cache is not None else "diff_attn",
    )(*args, w["a_lambda_q1"], w["a_lambda_k1"], w["a_lambda_q2"], w["a_lambda_k2"], w["a_sub_gain"])


def _mla_attn_kernel(hb, cached, q_ref, k_ref, v_ref, *refs):
    if cached:
        kc_ref, vc_ref, o_ref = refs
    else:
        (o_ref,) = refs
    lk = k_ref.shape[1]
    ck = min(lk, KEY_CHUNK)
    def head_scores(h):
        q = q_ref[h]
        s = [_qk(q, k_ref[h, c * ck:(c + 1) * ck]) for c in range(lk // ck)]
        return s + [_qk(q, kc_ref[h])] if cached else s

    ahead = [head_scores(i) for i in range(min(SCORE_AHEAD, hb))]
    for h in range(hb):
        vs = [v_ref[h, c * ck:(c + 1) * ck] for c in range(lk // ck)] + ([vc_ref[h]] if cached else [])
        s = ahead.pop(0)
        if h + SCORE_AHEAD < hb:
            ahead.append(head_scores(h + SCORE_AHEAD))
        m = functools.reduce(jnp.maximum, [jnp.max(sc, axis=-1, keepdims=True) for sc in s])
        l = 0.0
        o = 0.0
        for sc, v in zip(s, vs):
            e = jnp.exp2(sc - m)
            l = l + jnp.sum(e, axis=-1, keepdims=True)
            o = o + jnp.dot(e.astype(BF16), v, preferred_element_type=F32)
        o_ref[:, h * LANES:(h + 1) * LANES] = (o * (1.0 / l)).astype(BF16)


def _mla_attn(q, k, v, cache, nb, lq, tq, hb):
    nq = lq // tq
    kv = lambda rows, d: pl.BlockSpec((hb, rows, d), lambda b, g, i: (g, b, 0))
    in_specs = [pl.BlockSpec((hb, tq, B_QK), lambda b, g, i: (g, b * nq + i, 0)), kv(lq, B_QK), kv(lq, B_DV)]
    args = [q, k, v]
    if cache is not None:
        past = cache[0].shape[1] // nb
        in_specs += [kv(past, B_QK), kv(past, B_DV)]
        args += list(cache)
    return pl.pallas_call(
        functools.partial(_mla_attn_kernel, hb, cache is not None),
        grid=(nb, B_HEADS // hb, nq),
        in_specs=in_specs,
        out_specs=pl.BlockSpec((tq, hb * LANES), lambda b, g, i: (b * nq + i, g)),
        out_shape=jax.ShapeDtypeStruct((nb * lq, B_HEADS * B_DV), BF16),
        compiler_params=_cparams(("arbitrary", "arbitrary", "arbitrary")),
        name="mla_attn_cached" if cache is not None else "mla_attn",
    )(*args)


def _out_kernel(oa_ref, ob_ref, x_ref, g1_ref, sh2_ref, sc2_ref, n2g_ref, woa_ref, wob_ref, wr_ref,
                x1_ref, h2_ref, aff_ref):
    o = (jnp.dot(oa_ref[...], woa_ref[...], preferred_element_type=F32)
         + jnp.dot(ob_ref[...], wob_ref[...], preferred_element_type=F32))
    x1 = x_ref[...] + g1_ref[...] * o
    x1_ref[...] = x1
    h2 = _rms(x1, n2g_ref[...]) * (1.0 + sc2_ref[...]) + sh2_ref[...]
    h2_ref[...] = h2
    logits = jnp.dot(h2.astype(BF16), wr_ref[...], preferred_element_type=F32)
    lane = lax.broadcasted_iota(jnp.int32, logits.shape, 1)
    logits = jnp.where(lane < N_EXPERTS, logits, -jnp.inf)
    e = jnp.exp(logits - jnp.max(logits, axis=-1, keepdims=True))
    aff_ref[...] = e / jnp.sum(e, axis=-1, keepdims=True)


def _out_proj(oa, ob, x, mod, tiles_per_mod, w, tag):
    m = x.shape[0]
    tm = 256
    row = lambda i: (i, 0)
    modspec = lambda k: pl.BlockSpec((None, 1, D_MODEL), lambda i: (i // tiles_per_mod, 0, k))
    return pl.pallas_call(
        _out_kernel,
        grid=(m // tm,),
        in_specs=[pl.BlockSpec((tm, A_V), row), pl.BlockSpec((tm, A_V), row), pl.BlockSpec((tm, D_MODEL), row),
                  modspec(2), modspec(3), modspec(4), _resident((1, D_MODEL)),
                  _resident((A_V, D_MODEL)), _resident((A_V, D_MODEL)), _resident((D_MODEL, LANES))],
        out_specs=[pl.BlockSpec((tm, D_MODEL), row), pl.BlockSpec((tm, D_MODEL), row),
                   pl.BlockSpec((tm, LANES), row)],
        out_shape=[jax.ShapeDtypeStruct((m, D_MODEL), F32), jax.ShapeDtypeStruct((m, D_MODEL), F32),
                   jax.ShapeDtypeStruct((m, LANES), F32)],
        compiler_params=_cparams(("arbitrary",)),
        name="out_proj_" + tag,
    )(oa, ob, x, mod, mod, mod, w["norm2_gain"], w["w_oa"], w["w_ob"], w["w_router"])


ROUTE_BLK = 256
COMBINE_T = 128
RANK_BITS = 4


def _route_kernel(cap, aff_ref, st_ref, cnt_ref, kmax_ref, posm_ref, pk_ref):
    n = aff_ref.shape[0]
    nblk = n // ROUTE_BLK
    aff = aff_ref[...]
    lane = lax.broadcasted_iota(jnp.int32, (n, LANES), 1)
    tok = lax.broadcasted_iota(jnp.int32, (n, LANES), 0)

    def narrow(carry):
        lo, hi = carry
        mid = lo + (hi - lo) * 0.5
        mid = jnp.where(mid < hi, mid, lo)
        above = aff > mid
        few = jnp.sum(above.astype(jnp.int32), axis=0, keepdims=True) < cap
        up = jnp.min(jnp.where(above, aff, jnp.inf), axis=0, keepdims=True)
        dn = jnp.max(jnp.where(above, -jnp.inf, aff), axis=0, keepdims=True)
        return jnp.where(few, lo, up), jnp.where(few, dn, hi)

    bounds = (jnp.min(aff, axis=0, keepdims=True), jnp.max(aff, axis=0, keepdims=True))
    thr, _ = lax.while_loop(lambda c: jnp.max((c[0] < c[1]).astype(jnp.int32)) > 0, narrow, bounds)
    gt = aff > thr
    eq = aff == thr
    need = cap - jnp.sum(gt.astype(jnp.int32), axis=0, keepdims=True)

    r = lax.broadcasted_iota(jnp.int32, (ROUTE_BLK, ROUTE_BLK), 0)
    c = lax.broadcasted_iota(jnp.int32, (ROUTE_BLK, ROUTE_BLK), 1)
    tri = (c < r).astype(BF16)

    def prefix(mask):
        m = mask.astype(F32)
        carry = jnp.zeros((1, LANES), F32)
        outs = []
        for b in range(nblk):
            mb = m[b * ROUTE_BLK:(b + 1) * ROUTE_BLK]
            outs.append(jnp.dot(tri, mb.astype(BF16), preferred_element_type=F32) + carry)
            carry = carry + jnp.sum(mb, axis=0, keepdims=True)
        return jnp.concatenate(outs, axis=0).astype(jnp.int32)

    sel = (gt | (eq & (prefix(eq) < need))) & (lane < N_EXPERTS)
    pos = prefix(sel)
    self_ = sel.astype(BF16)
    er = lax.broadcasted_iota(jnp.int32, (LANES, LANES), 0)
    ec = lax.broadcasted_iota(jnp.int32, (LANES, LANES), 1)
    rank = jnp.dot(self_, (er < ec).astype(BF16), preferred_element_type=F32).astype(jnp.int32)
    cnt = jnp.dot(self_, jnp.ones((LANES, LANES), BF16), preferred_element_type=F32).astype(jnp.int32)
    cnt_ref[...] = cnt
    nt = n // COMBINE_T
    kmax_ref[...] = jnp.max(cnt.reshape(nt, COMBINE_T, LANES), axis=1)

    posm_ref[...] = pos
    st_ref[...] = jnp.zeros(st_ref.shape, jnp.int32)
    st_ref[0:nt, :] = posm_ref[pl.ds(0, nt, stride=COMBINE_T), :]
    st_ref[nt:nt + 1, :] = jnp.full((1, LANES), cap, jnp.int32)
    posm_ref[...] = jnp.where(sel, pos, -1)
    pk_ref[...] = (tok << RANK_BITS) + rank


def _compact_kernel(cap, st_s, posm_ref, pk_ref, aff_ref, idx_ref, rk_ref, gate_ref, accp, accg):
    tt = COMBINE_T
    nt = posm_ref.shape[0] // tt
    accp[...] = jnp.zeros(accp.shape, jnp.int32)
    accg[...] = jnp.zeros(accg.shape, F32)
    jio = lax.broadcasted_iota(jnp.int32, (tt, LANES), 1)

    def tile_body(t, carry):
        rows = pl.ds(pl.multiple_of(t * tt, tt), tt)
        posm, pk, af = posm_ref[rows, :], pk_ref[rows, :], aff_ref[rows, :]
        for e in range(N_EXPERTS):
            g0 = jnp.minimum(lax.shift_right_logical(st_s[t * LANES + e], 7), cap // LANES - 1)
            rel = posm[:, e:e + 1] - g0 * LANES
            for d in range(2):
                oh = rel == jio + d * LANES
                accp[e, g0 + d] += jnp.sum(jnp.where(oh, pk[:, e:e + 1], 0).reshape(tt // 8, 8, LANES), axis=0)
                accg[e, g0 + d] += jnp.sum(jnp.where(oh, af[:, e:e + 1], 0.0).reshape(tt // 8, 8, LANES), axis=0)
        return carry

    lax.fori_loop(0, nt, tile_body, 0)
    for e in range(N_EXPERTS):
        packed = jnp.concatenate([jnp.sum(accp[e, g], axis=0, keepdims=True) for g in range(cap // LANES)], axis=1)
        idx_ref[e:e + 1, :] = packed >> RANK_BITS
        rk_ref[e:e + 1, :] = (packed & (N_EXPERTS - 1)) * tt + ((packed >> RANK_BITS) & (tt - 1))
        gate_ref[e:e + 1, :] = jnp.concatenate(
            [jnp.sum(accg[e, g], axis=0, keepdims=True) for g in range(cap // LANES)], axis=1)


def _route(aff, tag):
    n = aff.shape[0]
    cap = CAPACITY_FACTOR * n // N_EXPERTS
    nt = n // COMBINE_T
    full = lambda s: pl.BlockSpec(s, lambda i, *_: (0,) * len(s))
    i32 = lambda s: jax.ShapeDtypeStruct(s, jnp.int32)
    st, cnt, kmax, posm, pk = pl.pallas_call(
        functools.partial(_route_kernel, cap),
        grid=(1,),
        in_specs=[full((n, LANES))],
        out_specs=[full((nt + 8, LANES)), full((n, LANES)), full((nt, LANES)), full((n, LANES)), full((n, LANES))],
        out_shape=[i32((nt + 8, LANES)), i32((n, LANES)), i32((nt, LANES)), i32((n, LANES)), i32((n, LANES))],
        compiler_params=_cparams(("arbitrary",)),
        name="route_" + tag,
    )(aff)
    groups = cap // LANES + 1
    idx, rk, gate = pl.pallas_call(
        functools.partial(_compact_kernel, cap),
        grid_spec=pltpu.PrefetchScalarGridSpec(
            num_scalar_prefetch=1, grid=(1,),
            in_specs=[full((n, LANES)), full((n, LANES)), full((n, LANES))],
            out_specs=[full((N_EXPERTS, cap)), full((N_EXPERTS, cap)), full((N_EXPERTS, cap))],
            scratch_shapes=[pltpu.VMEM((N_EXPERTS, groups, 8, LANES), jnp.int32),
                            pltpu.VMEM((N_EXPERTS, groups, 8, LANES), F32)]),
        out_shape=[i32((N_EXPERTS, cap)), i32((N_EXPERTS, cap)), jax.ShapeDtypeStruct((N_EXPERTS, cap), F32)],
        compiler_params=_cparams(("arbitrary",)),
        name="compact_" + tag,
    )(st.reshape(-1), posm, pk, aff)
    return idx, rk, gate, st, cnt, kmax


FF_CHUNK = 256


def _ffn_kernel(cap, nf, per, idx_s, hp_hbm, hs_hbm, gate_ref, w1_ref, w3_ref, w2_ref, yp_hbm, ys_hbm,
                xg, xe, acc, gsem, osem):
    e = pl.program_id(0)
    f = pl.program_id(1)
    ne = pl.num_programs(0)
    srcs = (hp_hbm, hs_hbm)
    dsts = (yp_hbm, ys_hbm)
    slot = lax.rem(e, 2)

    def issue_part(ex, part, sl, live):
        for p in range(2):
            for i in range(per):
                j = part * per + i
                jc = jnp.minimum(j, cap - 1)
                tok = idx_s[(p * N_EXPERTS + ex) * cap + jc]

                @pl.when(live & (j < cap))
                def _():
                    pltpu.make_async_copy(srcs[p].at[pl.ds(tok, 1), :], xg.at[sl, pl.ds(p * cap + jc, 1), :],
                                          gsem.at[sl]).start()

    def out_copy(p, ex):
        return pltpu.make_async_copy(acc.at[pl.ds(p * cap, cap), :],
                                     dsts[p].at[pl.ds(pl.multiple_of(ex * cap, cap), cap), :], osem)

    @pl.when((e == 0) & (f == 0))
    def _prologue():
        def part_body(part, carry):
            issue_part(0, part, 0, True)
            return carry
        lax.fori_loop(0, nf, part_body, 0)

    @pl.when(f == 0)
    def _start_expert():
        for p in range(2):
            pltpu.make_async_copy(srcs[p].at[pl.ds(0, cap), :], xg.at[slot, pl.ds(p * cap, cap), :],
                                  gsem.at[slot]).wait()
        xe[...] = xg[slot].astype(BF16)

    x = xe[...]
    a = jnp.dot(x, w1_ref[...].astype(BF16), preferred_element_type=F32)
    b = jnp.dot(x, w3_ref[...].astype(BF16), preferred_element_type=F32)
    hid = ((a * jax.nn.sigmoid(a)) * b).astype(BF16)

    issue_part(jnp.minimum(e + 1, ne - 1), f, 1 - slot, e + 1 < ne)

    def down():
        return jnp.dot(hid, w2_ref[...].astype(BF16), preferred_element_type=F32)

    @pl.when(f == 0)
    def _first():
        @pl.when(e > 0)
        def _():
            for p in range(2):
                out_copy(p, e - 1).wait()
        acc[...] = down()

    @pl.when((f > 0) & (f < nf - 1))
    def _middle():
        acc[...] += down()

    @pl.when(f == nf - 1)
    def _last():
        acc[...] = (acc[...] + down()) * gate_ref[...]
        for p in range(2):
            out_copy(p, e).start()

        @pl.when(e == ne - 1)
        def _():
            for p in range(2):
                out_copy(p, e).wait()


def _ffn(idx_flat, h2p, h2s, gate, w1, w3, w2, cap):
    nf = EXPERT_FF // FF_CHUNK
    per = -(-cap // nf)
    grid_spec = pltpu.PrefetchScalarGridSpec(
        num_scalar_prefetch=1,
        grid=(N_EXPERTS, nf),
        in_specs=[pl.BlockSpec(memory_space=pl.ANY), pl.BlockSpec(memory_space=pl.ANY),
                  pl.BlockSpec((None, 2 * cap, 1), lambda e, f, *_: (e, 0, 0)),
                  pl.BlockSpec((None, D_MODEL, FF_CHUNK), lambda e, f, *_: (e, 0, f)),
                  pl.BlockSpec((None, D_MODEL, FF_CHUNK), lambda e, f, *_: (e, 0, f)),
                  pl.BlockSpec((None, FF_CHUNK, D_MODEL), lambda e, f, *_: (e, f, 0))],
        out_specs=[pl.BlockSpec(memory_space=pl.ANY), pl.BlockSpec(memory_space=pl.ANY)],
        scratch_shapes=[pltpu.VMEM((2, 2 * cap, D_MODEL), F32), pltpu.VMEM((2 * cap, D_MODEL), BF16),
                        pltpu.VMEM((2 * cap, D_MODEL), F32), pltpu.SemaphoreType.DMA((2,)),
                        pltpu.SemaphoreType.DMA(())])
    out = jax.ShapeDtypeStruct((N_EXPERTS * cap, D_MODEL), F32)
    return pl.pallas_call(
        functools.partial(_ffn_kernel, cap, nf, per),
        grid_spec=grid_spec, out_shape=[out, out],
        compiler_params=_cparams(("arbitrary", "arbitrary")),
        name="expert_ffn",
    )(idx_flat, h2p, h2s, gate, w1, w3, w2)


COMBINE_CW = 256
WAIT_ROWS = 32
ISSUE_UNROLL = 4


def _combine_kernel(cap, row_s, st_s, km_s, ye_hbm, x1_ref, g2_ref, cnt_ref, o_ref, buf, sem):
    t = pl.program_id(0)
    nt = pl.num_programs(0)
    tt = COMBINE_T
    slot = lax.rem(t, 2)

    def issue_tile(tile, sl):
        for e in range(N_EXPERTS):
            lo = st_s[tile * LANES + e]
            hi = st_s[(tile + 1) * LANES + e]

            def issue(g, carry):
                for u in range(ISSUE_UNROLL):
                    j = lo + g * ISSUE_UNROLL + u

                    @pl.when(j < hi)
                    def _():
                        pltpu.make_async_copy(ye_hbm.at[pl.ds(e * cap + j, 1), :],
                                              buf.at[sl, pl.ds(row_s[e * cap + j], 1), :], sem.at[sl]).start()
                return carry
            lax.fori_loop(0, lax.div(hi - lo + ISSUE_UNROLL - 1, ISSUE_UNROLL), issue, 0)

    def wait_tile(tile, sl):
        total = 0
        for e in range(N_EXPERTS):
            total = total + st_s[(tile + 1) * LANES + e] - st_s[tile * LANES + e]

        def wait_rows(n):
            def body(i, carry):
                pltpu.make_async_copy(ye_hbm.at[pl.ds(0, n), :], buf.at[sl, pl.ds(0, n), :], sem.at[sl]).wait()
                return carry
            return body
        lax.fori_loop(0, total // WAIT_ROWS, wait_rows(WAIT_ROWS), 0)
        lax.fori_loop(0, lax.rem(total, WAIT_ROWS), wait_rows(1), 0)

    @pl.when(t == 0)
    def _init():
        buf[...] = jnp.zeros(buf.shape, F32)
        issue_tile(0, 0)

    @pl.when(t + 1 < nt)
    def _prefetch():
        issue_tile(t + 1, 1 - slot)

    wait_tile(t, slot)

    kmax = km_s[t * LANES]
    cnt = jnp.tile(cnt_ref[...], (1, COMBINE_CW // LANES))
    for c in range(D_MODEL // COMBINE_CW):
        cs = slice(c * COMBINE_CW, (c + 1) * COMBINE_CW)

        def add_slot(k, a):
            v = buf[slot, pl.ds(pl.multiple_of(k * tt, tt), tt), cs]
            return a + jnp.where(cnt > k, v, 0.0)
        a = lax.fori_loop(0, kmax, add_slot, jnp.zeros((tt, COMBINE_CW), F32))
        o_ref[:, cs] = x1_ref[:, cs] + g2_ref[:, cs] * a


def _combine(rows, st, km, ye, x1, mod, tiles_per_mod, cnt, cap, tag):
    n = x1.shape[0]
    tt = COMBINE_T
    grid_spec = pltpu.PrefetchScalarGridSpec(
        num_scalar_prefetch=3,
        grid=(n // tt,),
        in_specs=[pl.BlockSpec(memory_space=pl.ANY),
                  pl.BlockSpec((tt, D_MODEL), lambda i, *_: (i, 0)),
                  pl.BlockSpec((None, 1, D_MODEL), lambda i, *_: (i // tiles_per_mod, 0, 5)),
                  pl.BlockSpec((tt, LANES), lambda i, *_: (i, 0))],
        out_specs=pl.BlockSpec((tt, D_MODEL), lambda i, *_: (i, 0)),
        scratch_shapes=[pltpu.VMEM((2, N_EXPERTS * tt, D_MODEL), F32), pltpu.SemaphoreType.DMA((2,))])
    return pl.pallas_call(
        functools.partial(_combine_kernel, cap),
        grid_spec=grid_spec, out_shape=jax.ShapeDtypeStruct((n, D_MODEL), F32),
        compiler_params=_cparams(("arbitrary",)),
        name="combine_" + tag,
    )(jnp.pad(rows.reshape(-1), (0, ISSUE_UNROLL)), st.reshape(-1), km.reshape(-1), ye, x1, mod, cnt)


def _prep_weights(norm1_gain, norm2_gain, w_in, a_q_gain, a_k_gain, a_lambda_q1, a_lambda_k1, a_lambda_q2,
                  a_lambda_k2, a_sub_gain, q_a_gain, w_uq, b_qn_gain, b_qr_gain, kv_a_gain, w_ukv, b_kn_gain,
                  b_kr_gain, w_o, w_router):
    tile = lambda g, width: jnp.tile(g.reshape(1, -1), (1, width // g.shape[-1]))
    wuq = w_uq[0].reshape(Q_RANK, B_HEADS, B_QK)
    wuq = jnp.concatenate([wuq[:, :, :B_NOPE].reshape(Q_RANK, -1), wuq[:, :, B_NOPE:].reshape(Q_RANK, -1)], axis=1)
    return {
        "norm1_gain": norm1_gain.reshape(1, -1), "norm2_gain": norm2_gain.reshape(1, -1),
        "w_in": w_in[0].astype(BF16),
        "a_q_gain": tile(a_q_gain, MXU_DIM), "a_k_gain": tile(a_k_gain, MXU_DIM),
        "a_lambda_q1": a_lambda_q1.reshape(1, -1), "a_lambda_k1": a_lambda_k1.reshape(1, -1),
        "a_lambda_q2": a_lambda_q2.reshape(1, -1), "a_lambda_k2": a_lambda_k2.reshape(1, -1),
        "a_sub_gain": a_sub_gain.reshape(1, -1), "q_a_gain": q_a_gain.reshape(1, -1),
        "w_uq": wuq.astype(BF16), "b_qn_gain": b_qn_gain.reshape(1, -1), "b_qr_gain": tile(b_qr_gain, MXU_DIM),
        "kv_a_gain": kv_a_gain.reshape(1, -1), "w_ukv": w_ukv[0].astype(BF16),
        "b_kn_gain": b_kn_gain.reshape(1, -1), "b_kr_gain": tile(b_kr_gain, LANES),
        "w_oa": w_o[0, :A_V].astype(BF16), "w_ob": w_o[0, A_V:].astype(BF16),
        "w_router": jnp.pad(w_router[0], ((0, 0), (0, LANES - N_EXPERTS))).astype(BF16),
        "seg64": _seg_matrix(A_DH),
    }


def kernel(x_prompt, x_sample, cache_diff_k, cache_diff_v, cache_mla_ckv, cache_mla_krope, c, c_ctx, w_ada, b_ada, norm1_gain, norm2_gain, w_in, a_q_gain, a_k_gain, a_lambda_q1, a_lambda_k1, a_lambda_q2, a_lambda_k2, a_sub_gain, q_a_gain, w_uq, b_qn_gain, b_qr_gain, kv_a_gain, w_ukv, b_kn_gain, b_kr_gain, w_o, w_router, w_exp1, w_exp3, w_exp2):
    nbp, lp, _ = x_prompt.shape
    nbs, ls, _ = x_sample.shape
    past = cache_diff_k.shape[2]
    w = _prep_weights(norm1_gain, norm2_gain, w_in, a_q_gain, a_k_gain, a_lambda_q1, a_lambda_k1, a_lambda_q2,
                      a_lambda_k2, a_sub_gain, q_a_gain, w_uq, b_qn_gain, b_qr_gain, kv_a_gain, w_ukv, b_kn_gain,
                      b_kr_gain, w_o, w_router)

    cvec = jnp.concatenate([c_ctx[None], c, jnp.zeros((8 - 1 - nbs, D_MODEL), F32)], axis=0)
    mod = _ada(cvec, w_ada[0], b_ada)
    mod_p = mod[0:1].reshape(1, 1, -1)
    mod_s = mod[1:1 + nbs].reshape(nbs, 1, -1)

    xp = x_prompt.reshape(nbp * lp, D_MODEL)
    xs = x_sample.reshape(nbs * ls, D_MODEL)
    tm = 256
    assert lp == tm, "the prompt projection writes one batch of new_diff_k per row tile"
    aq_p, ak_p, av_p, qm_p, km_p, bv_p, ak32, av32, ckv32, kr32 = _in_proj(xp, mod_p, False, nbp * lp // tm, w, None)
    aq_s, ak_s, av_s, qm_s, km_s, bv_s = _in_proj(xs, mod_s, True, ls // tm, w, _rope_tables(ls))
    km_c, bv_c = _cache_kv(cache_mla_ckv.reshape(nbs * past, KV_RANK), cache_mla_krope.reshape(nbs * past, B_ROPE), w)

    cache_a = (cache_diff_k.reshape(nbs * past, A_QK), cache_diff_v.reshape(nbs * past, A_V))

    oa_p = _diff_attn(aq_p, ak_p, av_p, None, nbp, lp, lp, A_HEADS, w)
    ob_p = _mla_attn(qm_p, km_p, bv_p, None, nbp, lp, lp, B_HEADS)
    oa_s = _diff_attn(aq_s, ak_s, av_s, cache_a, nbs, ls, 256, 4, w)
    ob_s = _mla_attn(qm_s, km_s, bv_s, (km_c, bv_c), nbs, ls, 256, 4)

    x1_p, h2_p, aff_p = _out_proj(oa_p, ob_p, xp, mod_p, nbp * lp // tm, w, "prompt")
    x1_s, h2_s, aff_s = _out_proj(oa_s, ob_s, xs, mod_s, ls // tm, w, "sample")

    idx_p, rk_p, gate_p, st_p, cnt_p, km_p_ = _route(aff_p, "prompt")
    idx_s, rk_s, gate_s, st_s, cnt_s, km_s_ = _route(aff_s, "sample")
    cap = idx_p.shape[1]
    idx_flat = jnp.stack([idx_p, idx_s]).reshape(-1)
    gate = jnp.concatenate([gate_p, gate_s], axis=1)[..., None]
    ye_p, ye_s = _ffn(idx_flat, h2_p, h2_s, gate, w_exp1[0], w_exp3[0], w_exp2[0], cap)

    y_p = _combine(rk_p, st_p, km_p_, ye_p, x1_p, mod_p, nbp * lp // COMBINE_T, cnt_p, cap, "prompt")
    y_s = _combine(rk_s, st_s, km_s_, ye_s, x1_s, mod_s, ls // COMBINE_T, cnt_s, cap, "sample")

    return (y_p.reshape(nbp, lp, D_MODEL), y_s.reshape(nbs, ls, D_MODEL),
            ak32, av32.reshape(nbp, 1, lp, A_HEADS, A_DV),
            ckv32.reshape(nbp, 1, lp, KV_RANK), kr32.reshape(nbp, 1, lp, B_ROPE))
```

```python
import functools
import math

import numpy as np
import jax
import jax.numpy as jnp
from jax import lax
from jax.experimental import pallas as pl
from jax.experimental.pallas import tpu as pltpu

F32 = jnp.float32
BF16 = jnp.bfloat16

D_MODEL = 2048
GRID_W = 64
A_HEADS = 8
A_DH = 64
A_DV = 128
B_HEADS = 8
B_NOPE = 128
B_ROPE = 64
B_DV = 128
Q_RANK = 512
KV_RANK = 256
N_EXPERTS = 16
EXPERT_FF = 1536
CAPACITY_FACTOR = 2
ROPE_BASE = 10000.0
EPS = 1e-6
LAM_INIT = 0.8 - 0.6 * math.exp(-0.3 * 0)
LOG2E = math.log2(math.e)

A_QK = A_HEADS * 2 * A_DH
A_V = A_HEADS * A_DV
B_QK = B_NOPE + B_ROPE
IN_COLS = 2 * A_QK + A_V + Q_RANK + KV_RANK + B_ROPE
LANES = 128
MXU_DIM = 256
VMEM_LIMIT = 56 * 1024 * 1024


def _cparams(sem):
    return pltpu.CompilerParams(dimension_semantics=sem, vmem_limit_bytes=VMEM_LIMIT)


def _resident(shape):
    nd = len(shape)
    return pl.BlockSpec(shape, lambda *_: (0,) * nd, pipeline_mode=pl.Buffered(1))


def _rms(x, gain):
    return x * lax.rsqrt(jnp.mean(x * x, axis=-1, keepdims=True) + EPS) * gain


def _seg_rms(y, segmat, gain):
    y2 = (y * y).astype(BF16)
    outs = []
    for c in range(y.shape[1] // MXU_DIM):
        sl = slice(c * MXU_DIM, (c + 1) * MXU_DIM)
        ms = jnp.dot(y2[:, sl], segmat, preferred_element_type=F32)
        outs.append(y[:, sl] * lax.rsqrt(ms + EPS) * gain)
    return jnp.concatenate(outs, axis=1)


def _rope(y, cos, sin):
    outs = []
    for c in range(y.shape[1] // LANES):
        yc = y[:, c * LANES:(c + 1) * LANES]
        lane = lax.broadcasted_iota(jnp.int32, yc.shape, 1)
        partner = jnp.where((lane & 16) == 0, pltpu.roll(yc, LANES - 16, 1), pltpu.roll(yc, 16, 1))
        outs.append(yc * cos + partner * sin)
    return outs[0] if len(outs) == 1 else jnp.concatenate(outs, axis=1)


def _rope_tables(n_tok):
    t = np.arange(n_tok)
    row, col = t // GRID_W, t % GRID_W
    nf = A_DH // 4
    inv = ROPE_BASE ** (-np.arange(nf, dtype=np.float64) * 2.0 / (A_DH // 2))
    lane = np.arange(A_DH)
    pos = np.where(lane[None, :] < A_DH // 2, row[:, None], col[:, None]).astype(np.float64)
    ang = pos * inv[lane % nf][None, :]
    sign = np.where((lane % (2 * nf)) < nf, -1.0, 1.0)[None, :]
    cos = np.tile(np.cos(ang), (1, LANES // A_DH)).astype(np.float32)
    sin = np.tile(np.sin(ang) * sign, (1, LANES // A_DH)).astype(np.float32)
    return jnp.asarray(cos), jnp.asarray(sin)


def _seg_matrix(width):
    i = np.arange(MXU_DIM)
    return jnp.asarray(((i[:, None] // width) == (i[None, :] // width)).astype(np.float32) / width, dtype=BF16)


def _ada_kernel(c_ref, w_ref, b_ref, o_ref):
    c = c_ref[...]
    s = c * jax.nn.sigmoid(c)
    o_ref[...] = jnp.dot(s.astype(BF16), w_ref[...].astype(BF16), preferred_element_type=F32) + b_ref[...]


def _ada(cvec, w_ada, b_ada):
    tn = 1024
    n = w_ada.shape[1]
    return pl.pallas_call(
        _ada_kernel,
        grid=(n // tn,),
        in_specs=[pl.BlockSpec((8, D_MODEL), lambda j: (0, 0)),
                  pl.BlockSpec((D_MODEL, tn), lambda j: (0, j)),
                  pl.BlockSpec((1, tn), lambda j: (0, j))],
        out_specs=pl.BlockSpec((8, tn), lambda j: (0, j)),
        out_shape=jax.ShapeDtypeStruct((8, n), F32),
        compiler_params=_cparams(("arbitrary",)),
        name="ada_mod",
    )(cvec, w_ada, b_ada)


def _expand_kv(ckv_n, kr, wukv_ref, kng, km_ref, bv_ref):
    kv = jnp.dot(ckv_n.astype(BF16), wukv_ref[...], preferred_element_type=F32)
    krb = kr[:, :B_ROPE].astype(BF16)
    for h in range(B_HEADS):
        base = h * (B_NOPE + B_DV)
        kn = _rms(kv[:, base:base + B_NOPE], kng)
        km_ref[h] = jnp.concatenate([kn.astype(BF16), krb], axis=1)
        bv_ref[h] = kv[:, base + B_NOPE:base + B_NOPE + B_DV].astype(BF16)


def _in_kernel(sample, *refs):
    (x_ref, sh_ref, sc_ref, n1g_ref, win_ref, aqg_ref, akg_ref, qag_ref, wuq_ref, qng_ref, qrg_ref,
     kvag_ref, wukv_ref, kng_ref, krg_ref, seg_ref) = refs[:16]
    refs = refs[16:]
    if sample:
        cos_ref, sin_ref = refs[:2]
        refs = refs[2:]
        cos, sin = cos_ref[...], sin_ref[...]
    aq_ref, ak_ref, av_ref, qm_ref, km_ref, bv_ref = refs[:6]
    refs = refs[6:]

    x = x_ref[...]
    h = _rms(x, n1g_ref[...]) * (1.0 + sc_ref[...]) + sh_ref[...]
    hb = h.astype(BF16)

    def proj(c0, c1):
        return jnp.dot(hb, win_ref[:, c0:c1], preferred_element_type=F32)

    seg = seg_ref[...]
    c0 = 2 * A_QK + A_V
    y_q = proj(c0, c0 + Q_RANK)
    y_ckv = proj(c0 + Q_RANK, c0 + Q_RANK + KV_RANK)
    kr64 = proj(c0 + Q_RANK + KV_RANK, c0 + Q_RANK + KV_RANK + B_ROPE)
    q_lat = _rms(y_q, qag_ref[...])
    qb = jnp.dot(q_lat.astype(BF16), wuq_ref[...], preferred_element_type=F32)
    ckv = _rms(y_ckv, kvag_ref[...])
    kr_raw = jnp.concatenate([kr64, jnp.zeros_like(kr64)], axis=1)
    kr = kr_raw * lax.rsqrt(jnp.sum(kr_raw * kr_raw, axis=-1, keepdims=True) * (1.0 / B_ROPE) + EPS) * krg_ref[...]
    if sample:
        kr = _rope(kr, cos, sin)
    _expand_kv(ckv, kr, wukv_ref, kng_ref[...], km_ref, bv_ref)
    y_aq = proj(0, A_QK)
    qr = _seg_rms(qb[:, B_HEADS * B_NOPE:], seg, qrg_ref[...])
    if sample:
        qr = _rope(qr, cos, sin)
    scale = B_QK ** -0.5 * LOG2E
    for hd in range(B_HEADS):
        qn = _rms(qb[:, hd * B_NOPE:(hd + 1) * B_NOPE], qng_ref[...])
        qm_ref[hd] = jnp.concatenate(
            [(qn * scale).astype(BF16), (qr[:, hd * B_ROPE:(hd + 1) * B_ROPE] * scale).astype(BF16)], axis=1)

    y_ak = proj(A_QK, 2 * A_QK)
    aq = _seg_rms(y_aq, seg, aqg_ref[...])
    if sample:
        aq = _rope(aq, cos, sin)
    aq_ref[...] = (aq * (A_DH ** -0.5 * LOG2E)).astype(BF16)
    av = proj(2 * A_QK, 2 * A_QK + A_V)
    ak = _seg_rms(y_ak, seg, akg_ref[...])
    if sample:
        ak = _rope(ak, cos, sin)
    ak_ref[...] = ak.astype(BF16)
    av_ref[...] = av.astype(BF16)
    if not sample:
        ak32_ref, av32_ref, ckv32_ref, kr32_ref = refs
        ak32_ref[...] = ak.reshape(ak32_ref.shape)
        av32_ref[...] = av
        ckv32_ref[...] = ckv
        kr32_ref[...] = kr[:, :B_ROPE]


def _in_proj(x, mod, sample, tiles_per_mod, w, tables):
    m = x.shape[0]
    tm = 256
    nt = m // tm
    row = lambda i: (i, 0)
    modspec = lambda k: pl.BlockSpec((None, 1, D_MODEL), lambda i: (i // tiles_per_mod, 0, k))
    in_specs = [pl.BlockSpec((tm, D_MODEL), row), modspec(0), modspec(1), _resident((1, D_MODEL)),
                _resident((D_MODEL, IN_COLS)), _resident((1, MXU_DIM)), _resident((1, MXU_DIM)),
                _resident((1, Q_RANK)), _resident((Q_RANK, B_HEADS * B_QK)), _resident((1, B_NOPE)),
                _resident((1, MXU_DIM)), _resident((1, KV_RANK)), _resident((KV_RANK, B_HEADS * (B_NOPE + B_DV))),
                _resident((1, B_NOPE)), _resident((1, LANES)), _resident((MXU_DIM, MXU_DIM))]
    args = [x, mod, mod, w["norm1_gain"], w["w_in"], w["a_q_gain"], w["a_k_gain"], w["q_a_gain"], w["w_uq"],
            w["b_qn_gain"], w["b_qr_gain"], w["kv_a_gain"], w["w_ukv"], w["b_kn_gain"], w["b_kr_gain"], w["seg64"]]
    if sample:
        per = tables[0].shape[0] // tm
        in_specs += [pl.BlockSpec((tm, LANES), lambda i: (i % per, 0))] * 2
        args += list(tables)
    out_shape = [jax.ShapeDtypeStruct((m, A_QK), BF16), jax.ShapeDtypeStruct((m, A_QK), BF16),
                 jax.ShapeDtypeStruct((m, A_V), BF16), jax.ShapeDtypeStruct((B_HEADS, m, B_QK), BF16),
                 jax.ShapeDtypeStruct((B_HEADS, m, B_QK), BF16), jax.ShapeDtypeStruct((B_HEADS, m, B_DV), BF16)]
    hspec = lambda d: pl.BlockSpec((B_HEADS, tm, d), lambda i: (0, i, 0))
    out_specs = [pl.BlockSpec((tm, A_QK), row), pl.BlockSpec((tm, A_QK), row), pl.BlockSpec((tm, A_V), row),
                 hspec(B_QK), hspec(B_QK), hspec(B_DV)]
    if not sample:
        out_shape += [jax.ShapeDtypeStruct((nt, 1, tm, A_HEADS, 2, A_DH), F32), jax.ShapeDtypeStruct((m, A_V), F32),
                      jax.ShapeDtypeStruct((m, KV_RANK), F32), jax.ShapeDtypeStruct((m, B_ROPE), F32)]
        out_specs += [pl.BlockSpec((None, None, tm, A_HEADS, 2, A_DH), lambda i: (i, 0, 0, 0, 0, 0)),
                      pl.BlockSpec((tm, A_V), row),
                      pl.BlockSpec((tm, KV_RANK), row), pl.BlockSpec((tm, B_ROPE), row)]
    return pl.pallas_call(
        functools.partial(_in_kernel, sample),
        grid=(nt,), in_specs=in_specs, out_specs=out_specs, out_shape=out_shape,
        compiler_params=_cparams(("arbitrary",)),
        name="in_proj_sample" if sample else "in_proj_prompt",
    )(*args)


def _cache_kv_kernel(ckv_ref, kr_ref, wukv_ref, kng_ref, km_ref, bv_ref):
    _expand_kv(ckv_ref[...], kr_ref[...], wukv_ref, kng_ref[...], km_ref, bv_ref)


def _cache_kv(ckv, kr, w):
    m = ckv.shape[0]
    return pl.pallas_call(
        _cache_kv_kernel,
        grid=(1,),
        in_specs=[_resident((m, KV_RANK)), _resident((m, B_ROPE)),
                  _resident((KV_RANK, B_HEADS * (B_NOPE + B_DV))), _resident((1, B_NOPE))],
        out_specs=[pl.BlockSpec((B_HEADS, m, B_QK), lambda i: (0, 0, 0)),
                   pl.BlockSpec((B_HEADS, m, B_DV), lambda i: (0, 0, 0))],
        out_shape=[jax.ShapeDtypeStruct((B_HEADS, m, B_QK), BF16), jax.ShapeDtypeStruct((B_HEADS, m, B_DV), BF16)],
        compiler_params=_cparams(("arbitrary",)),
        name="cache_kv",
    )(ckv, kr, w["w_ukv"], w["b_kn_gain"])


def _qk(q, k):
    return lax.dot_general(q, k, (((1,), (1,)), ((), ())), preferred_element_type=F32)


KEY_CHUNK = 512
SCORE_AHEAD = 1


def _scores(q, k, kc):
    s = _qk(q, k)
    return s if kc is None else jnp.concatenate([s, _qk(q, kc)], axis=1)


def _weighted(p, v, vc):
    if vc is None:
        return jnp.dot(p, v, preferred_element_type=F32)
    lk = v.shape[0]
    return (jnp.dot(p[:, :lk], v, preferred_element_type=F32)
            + jnp.dot(p[:, lk:], vc, preferred_element_type=F32))


def _diff_attn_kernel(hb, cached, q_ref, k_ref, v_ref, *refs):
    if cached:
        kc_ref, vc_ref = refs[:2]
        refs = refs[2:]
    lq1, lk1, lq2, lk2, sub_ref, o_ref = refs
    lam = (jnp.exp(jnp.sum(lq1[...] * lk1[...], axis=-1, keepdims=True))
           - jnp.exp(jnp.sum(lq2[...] * lk2[...], axis=-1, keepdims=True)) + LAM_INIT)
    def scores(h, comp):
        sl = slice(h * LANES, (h + 1) * LANES)
        q, k = q_ref[:, sl], k_ref[:, sl]
        kc = kc_ref[:, sl].astype(BF16) if cached else None
        lane = lax.broadcasted_iota(jnp.int32, q.shape, 1)
        keep = (lane < A_DH) if comp == 0 else (lane >= A_DH)
        return _scores(jnp.where(keep, q, jnp.zeros_like(q)), k, kc)

    s1_next, s2_next = scores(0, 0), scores(0, 1)
    for h in range(hb):
        sl = slice(h * LANES, (h + 1) * LANES)
        v = v_ref[:, sl]
        vc = vc_ref[:, sl].astype(BF16) if cached else None
        s1, s2 = s1_next, s2_next
        if h + 1 < hb:
            s1_next = scores(h + 1, 0)
        e1 = jnp.exp2(s1 - jnp.max(s1, axis=-1, keepdims=True))
        l1 = jnp.sum(e1, axis=-1, keepdims=True)
        if h + 1 < hb:
            s2_next = scores(h + 1, 1)
        e2 = jnp.exp2(s2 - jnp.max(s2, axis=-1, keepdims=True))
        l2 = jnp.sum(e2, axis=-1, keepdims=True)
        p = e1 - e2 * (lam * l1 / l2)
        o = _weighted(p.astype(BF16), v, vc) * (1.0 / l1)
        o_ref[:, sl] = (_rms(o, sub_ref[...]) * (1.0 - LAM_INIT)).astype(BF16)


def _diff_attn(q, k, v, cache, nb, lq, tq, hb, w):
    nq = lq // tq
    vec = _resident((1, A_DH))
    kv = lambda rows: pl.BlockSpec((rows, hb * LANES), lambda b, g, i: (b, g))
    in_specs = [pl.BlockSpec((tq, hb * LANES), lambda b, g, i: (b * nq + i, g)), kv(lq), kv(lq)]
    args = [q, k, v]
    if cache is not None:
        past = cache[0].shape[0] // nb
        in_specs += [kv(past), kv(past)]
        args += list(cache)
    return pl.pallas_call(
        functools.partial(_diff_attn_kernel, hb, cache is not None),
        grid=(nb, A_HEADS // hb, nq),
        in_specs=in_specs + [vec, vec, vec, vec, _resident((1, A_DV))],
        out_specs=pl.BlockSpec((tq, hb * LANES), lambda b, g, i: (b * nq + i, g)),
        out_shape=jax.ShapeDtypeStruct((nb * lq, A_V), BF16),
        compiler_params=_cparams(("arbitrary", "arbitrary", "arbitrary")),
        name="diff_attn_cached" if cache is not None else "diff_attn",
    )(*args, w["a_lambda_q1"], w["a_lambda_k1"], w["a_lambda_q2"], w["a_lambda_k2"], w["a_sub_gain"])


def _mla_attn_kernel(hb, cached, q_ref, k_ref, v_ref, *refs):
    if cached:
        kc_ref, vc_ref, o_ref = refs
    else:
        (o_ref,) = refs
    lk = k_ref.shape[1]
    ck = min(lk, KEY_CHUNK)
    def head_scores(h):
        q = q_ref[h]
        s = [_qk(q, k_ref[h, c * ck:(c + 1) * ck]) for c in range(lk // ck)]
        return s + [_qk(q, kc_ref[h])] if cached else s

    ahead = [head_scores(i) for i in range(min(SCORE_AHEAD, hb))]
    for h in range(hb):
        vs = [v_ref[h, c * ck:(c + 1) * ck] for c in range(lk // ck)] + ([vc_ref[h]] if cached else [])
        s = ahead.pop(0)
        if h + SCORE_AHEAD < hb:
            ahead.append(head_scores(h + SCORE_AHEAD))
        m = functools.reduce(jnp.maximum, [jnp.max(sc, axis=-1, keepdims=True) for sc in s])
        l = 0.0
        o = 0.0
        for sc, v in zip(s, vs):
            e = jnp.exp2(sc - m)
            l = l + jnp.sum(e, axis=-1, keepdims=True)
            o = o + jnp.dot(e.astype(BF16), v, preferred_element_type=F32)
        o_ref[:, h * LANES:(h + 1) * LANES] = (o * (1.0 / l)).astype(BF16)


def _mla_attn(q, k, v, cache, nb, lq, tq, hb):
    nq = lq // tq
    kv = lambda rows, d: pl.BlockSpec((hb, rows, d), lambda b, g, i: (g, b, 0))
    in_specs = [pl.BlockSpec((hb, tq, B_QK), lambda b, g, i: (g, b * nq + i, 0)), kv(lq, B_QK), kv(lq, B_DV)]
    args = [q, k, v]
    if cache is not None:
        past = cache[0].shape[1] // nb
        in_specs += [kv(past, B_QK), kv(past, B_DV)]
        args += list(cache)
    return pl.pallas_call(
        functools.partial(_mla_attn_kernel, hb, cache is not None),
        grid=(nb, B_HEADS // hb, nq),
        in_specs=in_specs,
        out_specs=pl.BlockSpec((tq, hb * LANES), lambda b, g, i: (b * nq + i, g)),
        out_shape=jax.ShapeDtypeStruct((nb * lq, B_HEADS * B_DV), BF16),
        compiler_params=_cparams(("arbitrary", "arbitrary", "arbitrary")),
        name="mla_attn_cached" if cache is not None else "mla_attn",
    )(*args)


def _out_kernel(oa_ref, ob_ref, x_ref, g1_ref, sh2_ref, sc2_ref, n2g_ref, woa_ref, wob_ref, wr_ref,
                x1_ref, h2_ref, aff_ref):
    sub = OUT_SUB
    rows = [pl.ds(r * sub, sub) for r in range(x_ref.shape[0] // sub)]
    os_ = [jnp.dot(oa_ref[r, :], woa_ref[...], preferred_element_type=F32)
           + jnp.dot(ob_ref[r, :], wob_ref[...], preferred_element_type=F32) for r in rows]
    for r, o in zip(rows, os_):
        x1 = x_ref[r, :] + g1_ref[...] * o
        x1_ref[r, :] = x1
        h2 = _rms(x1, n2g_ref[...]) * (1.0 + sc2_ref[...]) + sh2_ref[...]
        h2_ref[r, :] = h2
        logits = jnp.dot(h2.astype(BF16), wr_ref[...], preferred_element_type=F32)
        lane = lax.broadcasted_iota(jnp.int32, logits.shape, 1)
        logits = jnp.where(lane < N_EXPERTS, logits, -jnp.inf)
        e = jnp.exp(logits - jnp.max(logits, axis=-1, keepdims=True))
        aff_ref[r, :] = e / jnp.sum(e, axis=-1, keepdims=True)


OUT_SUB = 256


def _out_proj(oa, ob, x, mod, tiles_per_mod, w, tag):
    m = x.shape[0]
    tm = 512
    tiles_per_mod = tiles_per_mod * 256 // tm
    row = lambda i: (i, 0)
    modspec = lambda k: pl.BlockSpec((None, 1, D_MODEL), lambda i: (i // tiles_per_mod, 0, k))
    return pl.pallas_call(
        _out_kernel,
        grid=(m // tm,),
        in_specs=[pl.BlockSpec((tm, A_V), row), pl.BlockSpec((tm, A_V), row), pl.BlockSpec((tm, D_MODEL), row),
                  modspec(2), modspec(3), modspec(4), _resident((1, D_MODEL)),
                  _resident((A_V, D_MODEL)), _resident((A_V, D_MODEL)), _resident((D_MODEL, LANES))],
        out_specs=[pl.BlockSpec((tm, D_MODEL), row), pl.BlockSpec((tm, D_MODEL), row),
                   pl.BlockSpec((tm, LANES), row)],
        out_shape=[jax.ShapeDtypeStruct((m, D_MODEL), F32), jax.ShapeDtypeStruct((m, D_MODEL), F32),
                   jax.ShapeDtypeStruct((m, LANES), F32)],
        compiler_params=_cparams(("arbitrary",)),
        name="out_proj_" + tag,
    )(oa, ob, x, mod, mod, mod, w["norm2_gain"], w["w_oa"], w["w_ob"], w["w_router"])


ROUTE_BLK = 256
COMBINE_T = 128
RANK_BITS = 4
KEY_SHIFT = 16


def _route_kernel(cap, aff_ref, st_ref, cnt_ref, kmax_ref, key_ref):
    n = aff_ref.shape[0]
    nblk = n // ROUTE_BLK
    aff = aff_ref[...]
    lane = lax.broadcasted_iota(jnp.int32, (n, LANES), 1)
    tok = lax.broadcasted_iota(jnp.int32, (n, LANES), 0)

    def narrow(carry):
        lo, hi = carry
        mid = lo + (hi - lo) * 0.5
        mid = jnp.where(mid < hi, mid, lo)
        above = aff > mid
        few = jnp.sum(above.astype(jnp.int32), axis=0, keepdims=True) < cap
        up = jnp.min(jnp.where(above, aff, jnp.inf), axis=0, keepdims=True)
        dn = jnp.max(jnp.where(above, -jnp.inf, aff), axis=0, keepdims=True)
        return jnp.where(few, lo, up), jnp.where(few, dn, hi)

    bounds = (jnp.min(aff, axis=0, keepdims=True), jnp.max(aff, axis=0, keepdims=True))
    thr, _ = lax.while_loop(lambda c: jnp.max((c[0] < c[1]).astype(jnp.int32)) > 0, narrow, bounds)
    gt = aff > thr
    eq = aff == thr
    need = cap - jnp.sum(gt.astype(jnp.int32), axis=0, keepdims=True)

    r = lax.broadcasted_iota(jnp.int32, (ROUTE_BLK, ROUTE_BLK), 0)
    c = lax.broadcasted_iota(jnp.int32, (ROUTE_BLK, ROUTE_BLK), 1)
    tri = (c < r).astype(BF16)

    def prefix(mask):
        m = mask.astype(F32)
        carry = jnp.zeros((1, LANES), F32)
        outs = []
        for b in range(nblk):
            mb = m[b * ROUTE_BLK:(b + 1) * ROUTE_BLK]
            outs.append(jnp.dot(tri, mb.astype(BF16), preferred_element_type=F32) + carry)
            carry = carry + jnp.sum(mb, axis=0, keepdims=True)
        return jnp.concatenate(outs, axis=0).astype(jnp.int32)

    sel = (gt | (eq & (prefix(eq) < need))) & (lane < N_EXPERTS)
    pos = prefix(sel)
    self_ = sel.astype(BF16)
    er = lax.broadcasted_iota(jnp.int32, (LANES, LANES), 0)
    ec = lax.broadcasted_iota(jnp.int32, (LANES, LANES), 1)
    rank = jnp.dot(self_, (er < ec).astype(BF16), preferred_element_type=F32).astype(jnp.int32)
    cnt = jnp.dot(self_, jnp.ones((LANES, LANES), BF16), preferred_element_type=F32).astype(jnp.int32)
    cnt_ref[...] = cnt
    nt = n // COMBINE_T
    kmax_ref[...] = jnp.max(cnt.reshape(nt, COMBINE_T, LANES), axis=1)

    key_ref[...] = pos
    st_ref[...] = jnp.zeros(st_ref.shape, jnp.int32)
    st_ref[0:nt, :] = key_ref[pl.ds(0, nt, stride=COMBINE_T), :]
    st_ref[nt:nt + 1, :] = jnp.full((1, LANES), cap, jnp.int32)
    key_ref[...] = (jnp.where(sel, pos + 1, 0) << KEY_SHIFT) | (tok << RANK_BITS) | rank


def _compact_kernel(cap, st_s, key_ref, aff_ref, idx_ref, rk_ref, gate_ref, accp, accg):
    tt = COMBINE_T
    nt = key_ref.shape[0] // tt
    accp[...] = jnp.zeros(accp.shape, jnp.int32)
    accg[...] = jnp.zeros(accg.shape, F32)
    jio = lax.broadcasted_iota(jnp.int32, (tt, LANES), 1)

    def tile_body(t, carry):
        rows = pl.ds(pl.multiple_of(t * tt, tt), tt)
        key, af = key_ref[rows, :], aff_ref[rows, :]
        for e in range(N_EXPERTS):
            g0 = jnp.minimum(lax.shift_right_logical(st_s[t * LANES + e], 7), cap // LANES - 1)
            kb = jnp.broadcast_to(key[:, e:e + 1], (tt, LANES))
            ab = jnp.broadcast_to(af[:, e:e + 1], (tt, LANES))
            rel = (kb >> KEY_SHIFT) - 1 - g0 * LANES
            pk = kb & ((1 << KEY_SHIFT) - 1)
            for d in range(2):
                oh = rel == jio + d * LANES
                accp[e, g0 + d] += jnp.sum(jnp.where(oh, pk, 0).reshape(tt // 8, 8, LANES), axis=0)
                accg[e, g0 + d] += jnp.sum(jnp.where(oh, ab, 0.0).reshape(tt // 8, 8, LANES), axis=0)
        return carry

    lax.fori_loop(0, nt, tile_body, 0)
    for e in range(N_EXPERTS):
        packed = jnp.concatenate([jnp.sum(accp[e, g], axis=0, keepdims=True) for g in range(cap // LANES)], axis=1)
        idx_ref[e:e + 1, :] = packed >> RANK_BITS
        rk_ref[e:e + 1, :] = (packed & (N_EXPERTS - 1)) * tt + ((packed >> RANK_BITS) & (tt - 1))
        gate_ref[e:e + 1, :] = jnp.concatenate(
            [jnp.sum(accg[e, g], axis=0, keepdims=True) for g in range(cap // LANES)], axis=1)


def _route(aff, tag):
    n = aff.shape[0]
    cap = CAPACITY_FACTOR * n // N_EXPERTS
    nt = n // COMBINE_T
    full = lambda s: pl.BlockSpec(s, lambda i, *_: (0,) * len(s))
    i32 = lambda s: jax.ShapeDtypeStruct(s, jnp.int32)
    assert (n << RANK_BITS) <= (1 << KEY_SHIFT) and cap + 1 < (1 << (31 - KEY_SHIFT))
    st, cnt, kmax, key = pl.pallas_call(
        functools.partial(_route_kernel, cap),
        grid=(1,),
        in_specs=[full((n, LANES))],
        out_specs=[full((nt + 8, LANES)), full((n, LANES)), full((nt, LANES)), full((n, LANES))],
        out_shape=[i32((nt + 8, LANES)), i32((n, LANES)), i32((nt, LANES)), i32((n, LANES))],
        compiler_params=_cparams(("arbitrary",)),
        name="route_" + tag,
    )(aff)
    groups = cap // LANES + 1
    idx, rk, gate = pl.pallas_call(
        functools.partial(_compact_kernel, cap),
        grid_spec=pltpu.PrefetchScalarGridSpec(
            num_scalar_prefetch=1, grid=(1,),
            in_specs=[full((n, LANES)), full((n, LANES))],
            out_specs=[full((N_EXPERTS, cap)), full((N_EXPERTS, cap)), full((N_EXPERTS, cap))],
            scratch_shapes=[pltpu.VMEM((N_EXPERTS, groups, 8, LANES), jnp.int32),
                            pltpu.VMEM((N_EXPERTS, groups, 8, LANES), F32)]),
        out_shape=[i32((N_EXPERTS, cap)), i32((N_EXPERTS, cap)), jax.ShapeDtypeStruct((N_EXPERTS, cap), F32)],
        compiler_params=_cparams(("arbitrary",)),
        name="compact_" + tag,
    )(st.reshape(-1), key, aff)
    return idx, rk, gate, st, cnt, kmax


FF_CHUNK = 256


def _ffn_kernel(cap, nf, per, idx_s, hp_hbm, hs_hbm, gate_ref, w1_ref, w3_ref, w2_ref, yp_hbm, ys_hbm,
                xg, xe, acc, gsem, osem):
    e = pl.program_id(0)
    f = pl.program_id(1)
    ne = pl.num_programs(0)
    srcs = (hp_hbm, hs_hbm)
    dsts = (yp_hbm, ys_hbm)
    slot = lax.rem(e, 2)

    def issue_part(ex, part, sl, live):
        tail = cap - (nf - 1) * per
        live_tail = live & (part < nf - 1)
        for p in range(2):
            for i in range(per):
                j = part * per + i
                jc = j if i < tail else jnp.minimum(j, cap - 1)
                tok = idx_s[(p * N_EXPERTS + ex) * cap + jc]

                @pl.when(live if i < tail else live_tail)
                def _():
                    pltpu.make_async_copy(srcs[p].at[pl.ds(tok, 1), :], xg.at[sl, pl.ds(p * cap + jc, 1), :],
                                          gsem.at[sl]).start()

    def out_copy(p, ex):
        return pltpu.make_async_copy(acc.at[pl.ds(p * cap, cap), :],
                                     dsts[p].at[pl.ds(pl.multiple_of(ex * cap, cap), cap), :], osem)

    @pl.when((e == 0) & (f == 0))
    def _prologue():
        def part_body(part, carry):
            issue_part(0, part, 0, True)
            return carry
        lax.fori_loop(0, nf, part_body, 0)

    @pl.when(f == 0)
    def _start_expert():
        for p in range(2):
            pltpu.make_async_copy(srcs[p].at[pl.ds(0, cap), :], xg.at[slot, pl.ds(p * cap, cap), :],
                                  gsem.at[slot]).wait()
        xe[...] = xg[slot].astype(BF16)

    x = xe[...]
    a = jnp.dot(x, w1_ref[...].astype(BF16), preferred_element_type=F32)
    b = jnp.dot(x, w3_ref[...].astype(BF16), preferred_element_type=F32)
    hid = ((a * jax.nn.sigmoid(a)) * b).astype(BF16)

    issue_part(jnp.minimum(e + 1, ne - 1), f, 1 - slot, e + 1 < ne)

    def down():
        return jnp.dot(hid, w2_ref[...].astype(BF16), preferred_element_type=F32)

    @pl.when(f == 0)
    def _first():
        @pl.when(e > 0)
        def _():
            for p in range(2):
                out_copy(p, e - 1).wait()
        acc[...] = down()

    @pl.when((f > 0) & (f < nf - 1))
    def _middle():
        acc[...] += down()

    @pl.when(f == nf - 1)
    def _last():
        acc[...] = (acc[...] + down()) * gate_ref[...]
        for p in range(2):
            out_copy(p, e).start()

        @pl.when(e == ne - 1)
        def _():
            for p in range(2):
                out_copy(p, e).wait()


def _ffn(idx_flat, h2p, h2s, gate, w1, w3, w2, cap):
    nf = EXPERT_FF // FF_CHUNK
    per = -(-cap // nf)
    grid_spec = pltpu.PrefetchScalarGridSpec(
        num_scalar_prefetch=1,
        grid=(N_EXPERTS, nf),
        in_specs=[pl.BlockSpec(memory_space=pl.ANY), pl.BlockSpec(memory_space=pl.ANY),
                  pl.BlockSpec((None, 2 * cap, 1), lambda e, f, *_: (e, 0, 0)),
                  pl.BlockSpec((None, D_MODEL, FF_CHUNK), lambda e, f, *_: (e, 0, f)),
                  pl.BlockSpec((None, D_MODEL, FF_CHUNK), lambda e, f, *_: (e, 0, f)),
                  pl.BlockSpec((None, FF_CHUNK, D_MODEL), lambda e, f, *_: (e, f, 0))],
        out_specs=[pl.BlockSpec(memory_space=pl.ANY), pl.BlockSpec(memory_space=pl.ANY)],
        scratch_shapes=[pltpu.VMEM((2, 2 * cap, D_MODEL), F32), pltpu.VMEM((2 * cap, D_MODEL), BF16),
                        pltpu.VMEM((2 * cap, D_MODEL), F32), pltpu.SemaphoreType.DMA((2,)),
                        pltpu.SemaphoreType.DMA(())])
    out = jax.ShapeDtypeStruct((N_EXPERTS * cap, D_MODEL), F32)
    return pl.pallas_call(
        functools.partial(_ffn_kernel, cap, nf, per),
        grid_spec=grid_spec, out_shape=[out, out],
        compiler_params=_cparams(("arbitrary", "arbitrary")),
        name="expert_ffn",
    )(idx_flat, h2p, h2s, gate, w1, w3, w2)


COMBINE_CW = 256
WAIT_ROWS = 32
ISSUE_UNROLL = 4


def _combine_kernel(cap, row_s, st_s, km_s, ye_hbm, x1_ref, g2_ref, cnt_ref, o_ref, buf, sem):
    t = pl.program_id(0)
    nt = pl.num_programs(0)
    tt = COMBINE_T
    slot = lax.rem(t, 2)

    def issue_tile(tile, sl):
        for e in range(N_EXPERTS):
            lo = st_s[tile * LANES + e]
            hi = st_s[(tile + 1) * LANES + e]

            def issue(g, carry):
                for u in range(ISSUE_UNROLL):
                    j = lo + g * ISSUE_UNROLL + u

                    @pl.when(j < hi)
                    def _():
                        pltpu.make_async_copy(ye_hbm.at[pl.ds(e * cap + j, 1), :],
                                              buf.at[sl, pl.ds(row_s[e * cap + j], 1), :], sem.at[sl]).start()
                return carry
            lax.fori_loop(0, lax.div(hi - lo + ISSUE_UNROLL - 1, ISSUE_UNROLL), issue, 0)

    def wait_tile(tile, sl):
        total = 0
        for e in range(N_EXPERTS):
            total = total + st_s[(tile + 1) * LANES + e] - st_s[tile * LANES + e]

        def wait_rows(n):
            def body(i, carry):
                pltpu.make_async_copy(ye_hbm.at[pl.ds(0, n), :], buf.at[sl, pl.ds(0, n), :], sem.at[sl]).wait()
                return carry
            return body
        lax.fori_loop(0, total // WAIT_ROWS, wait_rows(WAIT_ROWS), 0)
        lax.fori_loop(0, lax.rem(total, WAIT_ROWS), wait_rows(1), 0)

    @pl.when(t == 0)
    def _init():
        buf[...] = jnp.zeros(buf.shape, F32)
        issue_tile(0, 0)

    @pl.when(t + 1 < nt)
    def _prefetch():
        issue_tile(t + 1, 1 - slot)

    wait_tile(t, slot)

    kmax = km_s[t * LANES]
    cnt = jnp.tile(cnt_ref[...], (1, COMBINE_CW // LANES))
    for c in range(D_MODEL // COMBINE_CW):
        cs = slice(c * COMBINE_CW, (c + 1) * COMBINE_CW)

        def add_slot(k, a):
            v = buf[slot, pl.ds(pl.multiple_of(k * tt, tt), tt), cs]
            return a + jnp.where(cnt > k, v, 0.0)
        a = lax.fori_loop(0, kmax, add_slot, jnp.zeros((tt, COMBINE_CW), F32))
        o_ref[:, cs] = x1_ref[:, cs] + g2_ref[:, cs] * a


def _combine(rows, st, km, ye, x1, mod, tiles_per_mod, cnt, cap, tag):
    n = x1.shape[0]
    tt = COMBINE_T
    grid_spec = pltpu.PrefetchScalarGridSpec(
        num_scalar_prefetch=3,
        grid=(n // tt,),
        in_specs=[pl.BlockSpec(memory_space=pl.ANY),
                  pl.BlockSpec((tt, D_MODEL), lambda i, *_: (i, 0)),
                  pl.BlockSpec((None, 1, D_MODEL), lambda i, *_: (i // tiles_per_mod, 0, 5)),
                  pl.BlockSpec((tt, LANES), lambda i, *_: (i, 0))],
        out_specs=pl.BlockSpec((tt, D_MODEL), lambda i, *_: (i, 0)),
        scratch_shapes=[pltpu.VMEM((2, N_EXPERTS * tt, D_MODEL), F32), pltpu.SemaphoreType.DMA((2,))])
    return pl.pallas_call(
        functools.partial(_combine_kernel, cap),
        grid_spec=grid_spec, out_shape=jax.ShapeDtypeStruct((n, D_MODEL), F32),
        compiler_params=_cparams(("arbitrary",)),
        name="combine_" + tag,
    )(jnp.pad(rows.reshape(-1), (0, ISSUE_UNROLL)), st.reshape(-1), km.reshape(-1), ye, x1, mod, cnt)


def _prep_weights(norm1_gain, norm2_gain, w_in, a_q_gain, a_k_gain, a_lambda_q1, a_lambda_k1, a_lambda_q2,
                  a_lambda_k2, a_sub_gain, q_a_gain, w_uq, b_qn_gain, b_qr_gain, kv_a_gain, w_ukv, b_kn_gain,
                  b_kr_gain, w_o, w_router):
    tile = lambda g, width: jnp.tile(g.reshape(1, -1), (1, width // g.shape[-1]))
    wuq = w_uq[0].reshape(Q_RANK, B_HEADS, B_QK)
    wuq = jnp.concatenate([wuq[:, :, :B_NOPE].reshape(Q_RANK, -1), wuq[:, :, B_NOPE:].reshape(Q_RANK, -1)], axis=1)
    return {
        "norm1_gain": norm1_gain.reshape(1, -1), "norm2_gain": norm2_gain.reshape(1, -1),
        "w_in": w_in[0].astype(BF16),
        "a_q_gain": tile(a_q_gain, MXU_DIM), "a_k_gain": tile(a_k_gain, MXU_DIM),
        "a_lambda_q1": a_lambda_q1.reshape(1, -1), "a_lambda_k1": a_lambda_k1.reshape(1, -1),
        "a_lambda_q2": a_lambda_q2.reshape(1, -1), "a_lambda_k2": a_lambda_k2.reshape(1, -1),
        "a_sub_gain": a_sub_gain.reshape(1, -1), "q_a_gain": q_a_gain.reshape(1, -1),
        "w_uq": wuq.astype(BF16), "b_qn_gain": b_qn_gain.reshape(1, -1), "b_qr_gain": tile(b_qr_gain, MXU_DIM),
        "kv_a_gain": kv_a_gain.reshape(1, -1), "w_ukv": w_ukv[0].astype(BF16),
        "b_kn_gain": b_kn_gain.reshape(1, -1), "b_kr_gain": tile(b_kr_gain, LANES),
        "w_oa": w_o[0, :A_V].astype(BF16), "w_ob": w_o[0, A_V:].astype(BF16),
        "w_router": jnp.pad(w_router[0], ((0, 0), (0, LANES - N_EXPERTS))).astype(BF16),
        "seg64": _seg_matrix(A_DH),
    }


def kernel(x_prompt, x_sample, cache_diff_k, cache_diff_v, cache_mla_ckv, cache_mla_krope, c, c_ctx, w_ada, b_ada, norm1_gain, norm2_gain, w_in, a_q_gain, a_k_gain, a_lambda_q1, a_lambda_k1, a_lambda_q2, a_lambda_k2, a_sub_gain, q_a_gain, w_uq, b_qn_gain, b_qr_gain, kv_a_gain, w_ukv, b_kn_gain, b_kr_gain, w_o, w_router, w_exp1, w_exp3, w_exp2):
    nbp, lp, _ = x_prompt.shape
    nbs, ls, _ = x_sample.shape
    past = cache_diff_k.shape[2]
    w = _prep_weights(norm1_gain, norm2_gain, w_in, a_q_gain, a_k_gain, a_lambda_q1, a_lambda_k1, a_lambda_q2,
                      a_lambda_k2, a_sub_gain, q_a_gain, w_uq, b_qn_gain, b_qr_gain, kv_a_gain, w_ukv, b_kn_gain,
                      b_kr_gain, w_o, w_router)

    cvec = jnp.concatenate([c_ctx[None], c, jnp.zeros((8 - 1 - nbs, D_MODEL), F32)], axis=0)
    mod = _ada(cvec, w_ada[0], b_ada)
    mod_p = mod[0:1].reshape(1, 1, -1)
    mod_s = mod[1:1 + nbs].reshape(nbs, 1, -1)

    xp = x_prompt.reshape(nbp * lp, D_MODEL)
    xs = x_sample.reshape(nbs * ls, D_MODEL)
    tm = 256
    assert lp == tm, "the prompt projection writes one batch of new_diff_k per row tile"
    aq_p, ak_p, av_p, qm_p, km_p, bv_p, ak32, av32, ckv32, kr32 = _in_proj(xp, mod_p, False, nbp * lp // tm, w, None)
    aq_s, ak_s, av_s, qm_s, km_s, bv_s = _in_proj(xs, mod_s, True, ls // tm, w, _rope_tables(ls))
    km_c, bv_c = _cache_kv(cache_mla_ckv.reshape(nbs * past, KV_RANK), cache_mla_krope.reshape(nbs * past, B_ROPE), w)

    cache_a = (cache_diff_k.reshape(nbs * past, A_QK), cache_diff_v.reshape(nbs * past, A_V))

    oa_p = _diff_attn(aq_p, ak_p, av_p, None, nbp, lp, lp, A_HEADS, w)
    ob_p = _mla_attn(qm_p, km_p, bv_p, None, nbp, lp, lp, B_HEADS)
    oa_s = _diff_attn(aq_s, ak_s, av_s, cache_a, nbs, ls, 256, 4, w)
    ob_s = _mla_attn(qm_s, km_s, bv_s, (km_c, bv_c), nbs, ls, 256, 4)

    x1_p, h2_p, aff_p = _out_proj(oa_p, ob_p, xp, mod_p, nbp * lp // tm, w, "prompt")
    x1_s, h2_s, aff_s = _out_proj(oa_s, ob_s, xs, mod_s, ls // tm, w, "sample")

    idx_p, rk_p, gate_p, st_p, cnt_p, km_p_ = _route(aff_p, "prompt")
    idx_s, rk_s, gate_s, st_s, cnt_s, km_s_ = _route(aff_s, "sample")
    cap = idx_p.shape[1]
    idx_flat = jnp.stack([idx_p, idx_s]).reshape(-1)
    gate = jnp.concatenate([gate_p, gate_s], axis=1)[..., None]
    ye_p, ye_s = _ffn(idx_flat, h2_p, h2_s, gate, w_exp1[0], w_exp3[0], w_exp2[0], cap)

    y_p = _combine(rk_p, st_p, km_p_, ye_p, x1_p, mod_p, nbp * lp // COMBINE_T, cnt_p, cap, "prompt")
    y_s = _combine(rk_s, st_s, km_s_, ye_s, x1_s, mod_s, ls // COMBINE_T, cnt_s, cap, "sample")

    return (y_p.reshape(nbp, lp, D_MODEL), y_s.reshape(nbs, ls, D_MODEL),
            ak32, av32.reshape(nbp, 1, lp, A_HEADS, A_DV),
            ckv32.reshape(nbp, 1, lp, KV_RANK), kr32.reshape(nbp, 1, lp, B_ROPE))
```

```python
import functools
import math

import numpy as np
import jax
import jax.numpy as jnp
from jax import lax
from jax.experimental import pallas as pl
from jax.experimental.pallas import tpu as pltpu

F32 = jnp.float32
BF16 = jnp.bfloat16

D_MODEL = 2048
GRID_W = 64
A_HEADS = 8
A_DH = 64
A_DV = 128
B_HEADS = 8
B_NOPE = 128
B_ROPE = 64
B_DV = 128
Q_RANK = 512
KV_RANK = 256
N_EXPERTS = 16
EXPERT_FF = 1536
CAPACITY_FACTOR = 2
ROPE_BASE = 10000.0
EPS = 1e-6
LAM_INIT = 0.8 - 0.6 * math.exp(-0.3 * 0)
LOG2E = math.log2(math.e)

A_QK = A_HEADS * 2 * A_DH
A_V = A_HEADS * A_DV
B_QK = B_NOPE + B_ROPE
IN_COLS = 2 * A_QK + A_V + Q_RANK + KV_RANK + B_ROPE
LANES = 128
MXU_DIM = 256
VMEM_LIMIT = 56 * 1024 * 1024


def _cparams(sem):
    return pltpu.CompilerParams(dimension_semantics=sem, vmem_limit_bytes=VMEM_LIMIT)


def _resident(shape):
    nd = len(shape)
    return pl.BlockSpec(shape, lambda *_: (0,) * nd, pipeline_mode=pl.Buffered(1))


def _rms(x, gain):
    return x * lax.rsqrt(jnp.mean(x * x, axis=-1, keepdims=True) + EPS) * gain


def _seg_rms(y, segmat, gain):
    y2 = (y * y).astype(BF16)
    outs = []
    for c in range(y.shape[1] // MXU_DIM):
        sl = slice(c * MXU_DIM, (c + 1) * MXU_DIM)
        ms = jnp.dot(y2[:, sl], segmat, preferred_element_type=F32)
        outs.append(y[:, sl] * lax.rsqrt(ms + EPS) * gain)
    return jnp.concatenate(outs, axis=1)


def _rope(y, cos, sin):
    outs = []
    for c in range(y.shape[1] // LANES):
        yc = y[:, c * LANES:(c + 1) * LANES]
        lane = lax.broadcasted_iota(jnp.int32, yc.shape, 1)
        partner = jnp.where((lane & 16) == 0, pltpu.roll(yc, LANES - 16, 1), pltpu.roll(yc, 16, 1))
        outs.append(yc * cos + partner * sin)
    return outs[0] if len(outs) == 1 else jnp.concatenate(outs, axis=1)


def _rope_tables(n_tok):
    t = np.arange(n_tok)
    row, col = t // GRID_W, t % GRID_W
    nf = A_DH // 4
    inv = ROPE_BASE ** (-np.arange(nf, dtype=np.float64) * 2.0 / (A_DH // 2))
    lane = np.arange(A_DH)
    pos = np.where(lane[None, :] < A_DH // 2, row[:, None], col[:, None]).astype(np.float64)
    ang = pos * inv[lane % nf][None, :]
    sign = np.where((lane % (2 * nf)) < nf, -1.0, 1.0)[None, :]
    cos = np.tile(np.cos(ang), (1, LANES // A_DH)).astype(np.float32)
    sin = np.tile(np.sin(ang) * sign, (1, LANES // A_DH)).astype(np.float32)
    return jnp.asarray(cos), jnp.asarray(sin)


def _seg_matrix(width):
    i = np.arange(MXU_DIM)
    return jnp.asarray(((i[:, None] // width) == (i[None, :] // width)).astype(np.float32) / width, dtype=BF16)


def _ada_kernel(c_ref, w_ref, b_ref, o_ref):
    c = c_ref[...]
    s = c * jax.nn.sigmoid(c)
    o_ref[...] = jnp.dot(s.astype(BF16), w_ref[...].astype(BF16), preferred_element_type=F32) + b_ref[...]


def _ada(cvec, w_ada, b_ada):
    tn = 1024
    n = w_ada.shape[1]
    return pl.pallas_call(
        _ada_kernel,
        grid=(n // tn,),
        in_specs=[pl.BlockSpec((8, D_MODEL), lambda j: (0, 0)),
                  pl.BlockSpec((D_MODEL, tn), lambda j: (0, j)),
                  pl.BlockSpec((1, tn), lambda j: (0, j))],
        out_specs=pl.BlockSpec((8, tn), lambda j: (0, j)),
        out_shape=jax.ShapeDtypeStruct((8, n), F32),
        compiler_params=_cparams(("arbitrary",)),
        name="ada_mod",
    )(cvec, w_ada, b_ada)


def _expand_kv(ckv_n, kr, wukv_ref, kng, km_ref, bv_ref):
    kv = jnp.dot(ckv_n.astype(BF16), wukv_ref[...], preferred_element_type=F32)
    krb = kr[:, :B_ROPE].astype(BF16)
    for h in range(B_HEADS):
        base = h * (B_NOPE + B_DV)
        kn = _rms(kv[:, base:base + B_NOPE], kng)
        km_ref[h] = jnp.concatenate([kn.astype(BF16), krb], axis=1)
        bv_ref[h] = kv[:, base + B_NOPE:base + B_NOPE + B_DV].astype(BF16)


def _in_kernel(sample, *refs):
    (x_ref, sh_ref, sc_ref, n1g_ref, win_ref, aqg_ref, akg_ref, qag_ref, wuq_ref, qng_ref, qrg_ref,
     kvag_ref, wukv_ref, kng_ref, krg_ref, seg_ref) = refs[:16]
    refs = refs[16:]
    if sample:
        cos_ref, sin_ref = refs[:2]
        refs = refs[2:]
        cos, sin = cos_ref[...], sin_ref[...]
    aq_ref, ak_ref, av_ref, qm_ref, km_ref, bv_ref = refs[:6]
    refs = refs[6:]

    x = x_ref[...]
    h = _rms(x, n1g_ref[...]) * (1.0 + sc_ref[...]) + sh_ref[...]
    hb = h.astype(BF16)

    def proj(c0, c1):
        return jnp.dot(hb, win_ref[:, c0:c1], preferred_element_type=F32)

    seg = seg_ref[...]
    c0 = 2 * A_QK + A_V
    y_q = proj(c0, c0 + Q_RANK)
    y_ckv = proj(c0 + Q_RANK, c0 + Q_RANK + KV_RANK)
    kr64 = proj(c0 + Q_RANK + KV_RANK, c0 + Q_RANK + KV_RANK + B_ROPE)
    q_lat = _rms(y_q, qag_ref[...])
    qb = jnp.dot(q_lat.astype(BF16), wuq_ref[...], preferred_element_type=F32)
    ckv = _rms(y_ckv, kvag_ref[...])
    kr_raw = jnp.concatenate([kr64, jnp.zeros_like(kr64)], axis=1)
    kr = kr_raw * lax.rsqrt(jnp.sum(kr_raw * kr_raw, axis=-1, keepdims=True) * (1.0 / B_ROPE) + EPS) * krg_ref[...]
    if sample:
        kr = _rope(kr, cos, sin)
    _expand_kv(ckv, kr, wukv_ref, kng_ref[...], km_ref, bv_ref)
    y_aq = proj(0, A_QK)
    qr = _seg_rms(qb[:, B_HEADS * B_NOPE:], seg, qrg_ref[...])
    if sample:
        qr = _rope(qr, cos, sin)
    scale = B_QK ** -0.5 * LOG2E
    for hd in range(B_HEADS):
        qn = _rms(qb[:, hd * B_NOPE:(hd + 1) * B_NOPE], qng_ref[...])
        qm_ref[hd] = jnp.concatenate(
            [(qn * scale).astype(BF16), (qr[:, hd * B_ROPE:(hd + 1) * B_ROPE] * scale).astype(BF16)], axis=1)

    y_ak = proj(A_QK, 2 * A_QK)
    aq = _seg_rms(y_aq, seg, aqg_ref[...])
    if sample:
        aq = _rope(aq, cos, sin)
    aq_ref[...] = (aq * (A_DH ** -0.5 * LOG2E)).astype(BF16)
    av = proj(2 * A_QK, 2 * A_QK + A_V)
    ak = _seg_rms(y_ak, seg, akg_ref[...])
    if sample:
        ak = _rope(ak, cos, sin)
    ak_ref[...] = ak.astype(BF16)
    av_ref[...] = av.astype(BF16)
    if not sample:
        ak32_ref, av32_ref, ckv32_ref, kr32_ref = refs
        ak32_ref[...] = ak.reshape(ak32_ref.shape)
        av32_ref[...] = av
        ckv32_ref[...] = ckv
        kr32_ref[...] = kr[:, :B_ROPE]


def _in_proj(x, mod, sample, tiles_per_mod, w, tables):
    m = x.shape[0]
    tm = 256
    nt = m // tm
    row = lambda i: (i, 0)
    modspec = lambda k: pl.BlockSpec((None, 1, D_MODEL), lambda i: (i // tiles_per_mod, 0, k))
    in_specs = [pl.BlockSpec((tm, D_MODEL), row), modspec(0), modspec(1), _resident((1, D_MODEL)),
                _resident((D_MODEL, IN_COLS)), _resident((1, MXU_DIM)), _resident((1, MXU_DIM)),
                _resident((1, Q_RANK)), _resident((Q_RANK, B_HEADS * B_QK)), _resident((1, B_NOPE)),
                _resident((1, MXU_DIM)), _resident((1, KV_RANK)), _resident((KV_RANK, B_HEADS * (B_NOPE + B_DV))),
                _resident((1, B_NOPE)), _resident((1, LANES)), _resident((MXU_DIM, MXU_DIM))]
    args = [x, mod, mod, w["norm1_gain"], w["w_in"], w["a_q_gain"], w["a_k_gain"], w["q_a_gain"], w["w_uq"],
            w["b_qn_gain"], w["b_qr_gain"], w["kv_a_gain"], w["w_ukv"], w["b_kn_gain"], w["b_kr_gain"], w["seg64"]]
    if sample:
        per = tables[0].shape[0] // tm
        in_specs += [pl.BlockSpec((tm, LANES), lambda i: (i % per, 0))] * 2
        args += list(tables)
    out_shape = [jax.ShapeDtypeStruct((m, A_QK), BF16), jax.ShapeDtypeStruct((m, A_QK), BF16),
                 jax.ShapeDtypeStruct((m, A_V), BF16), jax.ShapeDtypeStruct((B_HEADS, m, B_QK), BF16),
                 jax.ShapeDtypeStruct((B_HEADS, m, B_QK), BF16), jax.ShapeDtypeStruct((B_HEADS, m, B_DV), BF16)]
    hspec = lambda d: pl.BlockSpec((B_HEADS, tm, d), lambda i: (0, i, 0))
    out_specs = [pl.BlockSpec((tm, A_QK), row), pl.BlockSpec((tm, A_QK), row), pl.BlockSpec((tm, A_V), row),
                 hspec(B_QK), hspec(B_QK), hspec(B_DV)]
    if not sample:
        out_shape += [jax.ShapeDtypeStruct((nt, 1, tm, A_HEADS, 2, A_DH), F32), jax.ShapeDtypeStruct((m, A_V), F32),
                      jax.ShapeDtypeStruct((m, KV_RANK), F32), jax.ShapeDtypeStruct((m, B_ROPE), F32)]
        out_specs += [pl.BlockSpec((None, None, tm, A_HEADS, 2, A_DH), lambda i: (i, 0, 0, 0, 0, 0)),
                      pl.BlockSpec((tm, A_V), row),
                      pl.BlockSpec((tm, KV_RANK), row), pl.BlockSpec((tm, B_ROPE), row)]
    return pl.pallas_call(
        functools.partial(_in_kernel, sample),
        grid=(nt,), in_specs=in_specs, out_specs=out_specs, out_shape=out_shape,
        compiler_params=_cparams(("arbitrary",)),
        name="in_proj_sample" if sample else "in_proj_prompt",
    )(*args)


def _cache_kv_kernel(ckv_ref, kr_ref, wukv_ref, kng_ref, km_ref, bv_ref):
    _expand_kv(ckv_ref[...], kr_ref[...], wukv_ref, kng_ref[...], km_ref, bv_ref)


def _cache_kv(ckv, kr, w):
    m = ckv.shape[0]
    return pl.pallas_call(
        _cache_kv_kernel,
        grid=(1,),
        in_specs=[_resident((m, KV_RANK)), _resident((m, B_ROPE)),
                  _resident((KV_RANK, B_HEADS * (B_NOPE + B_DV))), _resident((1, B_NOPE))],
        out_specs=[pl.BlockSpec((B_HEADS, m, B_QK), lambda i: (0, 0, 0)),
                   pl.BlockSpec((B_HEADS, m, B_DV), lambda i: (0, 0, 0))],
        out_shape=[jax.ShapeDtypeStruct((B_HEADS, m, B_QK), BF16), jax.ShapeDtypeStruct((B_HEADS, m, B_DV), BF16)],
        compiler_params=_cparams(("arbitrary",)),
        name="cache_kv",
    )(ckv, kr, w["w_ukv"], w["b_kn_gain"])


def _qk(q, k):
    return lax.dot_general(q, k, (((1,), (1,)), ((), ())), preferred_element_type=F32)


KEY_CHUNK = 512
SCORE_AHEAD = 1


def _scores(q, k, kc):
    s = _qk(q, k)
    return s if kc is None else jnp.concatenate([s, _qk(q, kc)], axis=1)


def _weighted(p, v, vc):
    if vc is None:
        return jnp.dot(p, v, preferred_element_type=F32)
    lk = v.shape[0]
    return (jnp.dot(p[:, :lk], v, preferred_element_type=F32)
            + jnp.dot(p[:, lk:], vc, preferred_element_type=F32))


def _diff_attn_kernel(hb, cached, q_ref, k_ref, v_ref, *refs):
    if cached:
        kc_ref, vc_ref = refs[:2]
        refs = refs[2:]
    lq1, lk1, lq2, lk2, sub_ref, o_ref = refs
    lam = (jnp.exp(jnp.sum(lq1[...] * lk1[...], axis=-1, keepdims=True))
           - jnp.exp(jnp.sum(lq2[...] * lk2[...], axis=-1, keepdims=True)) + LAM_INIT)
    def scores(h, comp):
        sl = slice(h * LANES, (h + 1) * LANES)
        q, k = q_ref[:, sl], k_ref[:, sl]
        kc = kc_ref[:, sl].astype(BF16) if cached else None
        lane = lax.broadcasted_iota(jnp.int32, q.shape, 1)
        keep = (lane < A_DH) if comp == 0 else (lane >= A_DH)
        return _scores(jnp.where(keep, q, jnp.zeros_like(q)), k, kc)

    s1_next, s2_next = scores(0, 0), scores(0, 1)
    for h in range(hb):
        sl = slice(h * LANES, (h + 1) * LANES)
        s1, s2 = s1_next, s2_next
        if h + 1 < hb:
            s1_next = scores(h + 1, 0)
        e1 = jnp.exp2(s1 - jnp.max(s1, axis=-1, keepdims=True))
        if h + 1 < hb:
            s2_next = scores(h + 1, 1)
        e2 = jnp.exp2(s2 - jnp.max(s2, axis=-1, keepdims=True))
        if cached:
            ones = lambda x: jnp.concatenate([x, jnp.ones_like(x)], axis=1)
            v, vc = ones(v_ref[:, sl]), ones(vc_ref[:, sl].astype(BF16))
            n1 = _weighted(e1.astype(BF16), v, vc)
            n2 = _weighted(e2.astype(BF16), v, vc)
            o = n1[:, :LANES] * (1.0 / n1[:, LANES:LANES + 1]) - n2[:, :LANES] * (lam / n2[:, LANES:LANES + 1])
        else:
            l1 = jnp.sum(e1, axis=-1, keepdims=True)
            l2 = jnp.sum(e2, axis=-1, keepdims=True)
            p = e1 - e2 * (lam * l1 / l2)
            o = jnp.dot(p.astype(BF16), v_ref[:, sl], preferred_element_type=F32) * (1.0 / l1)
        o_ref[:, sl] = (_rms(o, sub_ref[...]) * (1.0 - LAM_INIT)).astype(BF16)


def _diff_attn(q, k, v, cache, nb, lq, tq, hb, w):
    nq = lq // tq
    vec = _resident((1, A_DH))
    kv = lambda rows: pl.BlockSpec((rows, hb * LANES), lambda b, g, i: (b, g))
    in_specs = [pl.BlockSpec((tq, hb * LANES), lambda b, g, i: (b * nq + i, g)), kv(lq), kv(lq)]
    args = [q, k, v]
    if cache is not None:
        past = cache[0].shape[0] // nb
        in_specs += [kv(past), kv(past)]
        args += list(cache)
    return pl.pallas_call(
        functools.partial(_diff_attn_kernel, hb, cache is not None),
        grid=(nb, A_HEADS // hb, nq),
        in_specs=in_specs + [vec, vec, vec, vec, _resident((1, A_DV))],
        out_specs=pl.BlockSpec((tq, hb * LANES), lambda b, g, i: (b * nq + i, g)),
        out_shape=jax.ShapeDtypeStruct((nb * lq, A_V), BF16),
        compiler_params=_cparams(("arbitrary", "arbitrary", "arbitrary")),
        name="diff_attn_cached" if cache is not None else "diff_attn",
    )(*args, w["a_lambda_q1"], w["a_lambda_k1"], w["a_lambda_q2"], w["a_lambda_k2"], w["a_sub_gain"])


def _mla_attn_kernel(hb, cached, q_ref, k_ref, v_ref, *refs):
    if cached:
        kc_ref, vc_ref, o_ref = refs
    else:
        (o_ref,) = refs
    lk = k_ref.shape[1]
    ck = min(lk, KEY_CHUNK)
    def head_scores(h):
        q = q_ref[h]
        s = [_qk(q, k_ref[h, c * ck:(c + 1) * ck]) for c in range(lk // ck)]
        return s + [_qk(q, kc_ref[h])] if cached else s

    ahead = [head_scores(i) for i in range(min(SCORE_AHEAD, hb))]
    for h in range(hb):
        vs = [v_ref[h, c * ck:(c + 1) * ck] for c in range(lk // ck)] + ([vc_ref[h]] if cached else [])
        s = ahead.pop(0)
        if h + SCORE_AHEAD < hb:
            ahead.append(head_scores(h + SCORE_AHEAD))
        m = functools.reduce(jnp.maximum, [jnp.max(sc, axis=-1, keepdims=True) for sc in s])
        l = 0.0
        o = 0.0
        for sc, v in zip(s, vs):
            e = jnp.exp2(sc - m)
            if cached:
                v = jnp.concatenate([v, jnp.ones_like(v)], axis=1)
            else:
                l = l + jnp.sum(e, axis=-1, keepdims=True)
            o = o + jnp.dot(e.astype(BF16), v, preferred_element_type=F32)
        if cached:
            o, l = o[:, :LANES], o[:, LANES:LANES + 1]
        o_ref[:, h * LANES:(h + 1) * LANES] = (o * (1.0 / l)).astype(BF16)


def _mla_attn(q, k, v, cache, nb, lq, tq, hb):
    nq = lq // tq
    kv = lambda rows, d: pl.BlockSpec((hb, rows, d), lambda b, g, i: (g, b, 0))
    in_specs = [pl.BlockSpec((hb, tq, B_QK), lambda b, g, i: (g, b * nq + i, 0)), kv(lq, B_QK), kv(lq, B_DV)]
    args = [q, k, v]
    if cache is not None:
        past = cache[0].shape[1] // nb
        in_specs += [kv(past, B_QK), kv(past, B_DV)]
        args += list(cache)
    return pl.pallas_call(
        functools.partial(_mla_attn_kernel, hb, cache is not None),
        grid=(nb, B_HEADS // hb, nq),
        in_specs=in_specs,
        out_specs=pl.BlockSpec((tq, hb * LANES), lambda b, g, i: (b * nq + i, g)),
        out_shape=jax.ShapeDtypeStruct((nb * lq, B_HEADS * B_DV), BF16),
        compiler_params=_cparams(("arbitrary", "arbitrary", "arbitrary")),
        name="mla_attn_cached" if cache is not None else "mla_attn",
    )(*args)


def _out_kernel(oa_ref, ob_ref, x_ref, g1_ref, sh2_ref, sc2_ref, n2g_ref, woa_ref, wob_ref, wr_ref,
                x1_ref, h2_ref, aff_ref):
    sub = OUT_SUB
    rows = [pl.ds(r * sub, sub) for r in range(x_ref.shape[0] // sub)]
    os_ = [jnp.dot(oa_ref[r, :], woa_ref[...], preferred_element_type=F32)
           + jnp.dot(ob_ref[r, :], wob_ref[...], preferred_element_type=F32) for r in rows]
    for r, o in zip(rows, os_):
        x1 = x_ref[r, :] + g1_ref[...] * o
        x1_ref[r, :] = x1
        h2 = _rms(x1, n2g_ref[...]) * (1.0 + sc2_ref[...]) + sh2_ref[...]
        h2_ref[r, :] = h2
        logits = jnp.dot(h2.astype(BF16), wr_ref[...], preferred_element_type=F32)
        lane = lax.broadcasted_iota(jnp.int32, logits.shape, 1)
        logits = jnp.where(lane < N_EXPERTS, logits, -jnp.inf)
        e = jnp.exp(logits - jnp.max(logits, axis=-1, keepdims=True))
        aff_ref[r, :] = e / jnp.sum(e, axis=-1, keepdims=True)


OUT_SUB = 256


def _out_proj(oa, ob, x, mod, tiles_per_mod, w, tag):
    m = x.shape[0]
    tm = 512
    tiles_per_mod = tiles_per_mod * 256 // tm
    row = lambda i: (i, 0)
    modspec = lambda k: pl.BlockSpec((None, 1, D_MODEL), lambda i: (i // tiles_per_mod, 0, k))
    return pl.pallas_call(
        _out_kernel,
        grid=(m // tm,),
        in_specs=[pl.BlockSpec((tm, A_V), row), pl.BlockSpec((tm, A_V), row), pl.BlockSpec((tm, D_MODEL), row),
                  modspec(2), modspec(3), modspec(4), _resident((1, D_MODEL)),
                  _resident((A_V, D_MODEL)), _resident((A_V, D_MODEL)), _resident((D_MODEL, LANES))],
        out_specs=[pl.BlockSpec((tm, D_MODEL), row), pl.BlockSpec((tm, D_MODEL), row),
                   pl.BlockSpec((tm, LANES), row)],
        out_shape=[jax.ShapeDtypeStruct((m, D_MODEL), F32), jax.ShapeDtypeStruct((m, D_MODEL), F32),
                   jax.ShapeDtypeStruct((m, LANES), F32)],
        compiler_params=_cparams(("arbitrary",)),
        name="out_proj_" + tag,
    )(oa, ob, x, mod, mod, mod, w["norm2_gain"], w["w_oa"], w["w_ob"], w["w_router"])


ROUTE_BLK = 256
COMBINE_T = 128
RANK_BITS = 4
KEY_SHIFT = 16


def _route_kernel(cap, aff_ref, st_ref, cnt_ref, kmax_ref, key_ref):
    n = aff_ref.shape[0]
    nblk = n // ROUTE_BLK
    aff = aff_ref[...]
    lane = lax.broadcasted_iota(jnp.int32, (n, LANES), 1)
    tok = lax.broadcasted_iota(jnp.int32, (n, LANES), 0)

    def narrow(carry):
        lo, hi = carry
        mid = lo + (hi - lo) * 0.5
        mid = jnp.where(mid < hi, mid, lo)
        above = aff > mid
        few = jnp.sum(above.astype(jnp.int32), axis=0, keepdims=True) < cap
        up = jnp.min(jnp.where(above, aff, jnp.inf), axis=0, keepdims=True)
        dn = jnp.max(jnp.where(above, -jnp.inf, aff), axis=0, keepdims=True)
        return jnp.where(few, lo, up), jnp.where(few, dn, hi)

    bounds = (jnp.min(aff, axis=0, keepdims=True), jnp.max(aff, axis=0, keepdims=True))
    thr, _ = lax.while_loop(lambda c: jnp.max((c[0] < c[1]).astype(jnp.int32)) > 0, narrow, bounds)
    gt = aff > thr
    eq = aff == thr
    need = cap - jnp.sum(gt.astype(jnp.int32), axis=0, keepdims=True)

    r = lax.broadcasted_iota(jnp.int32, (ROUTE_BLK, ROUTE_BLK), 0)
    c = lax.broadcasted_iota(jnp.int32, (ROUTE_BLK, ROUTE_BLK), 1)
    tri = (c < r).astype(BF16)

    def prefix(mask):
        m = mask.astype(F32)
        carry = jnp.zeros((1, LANES), F32)
        outs = []
        for b in range(nblk):
            mb = m[b * ROUTE_BLK:(b + 1) * ROUTE_BLK]
            outs.append(jnp.dot(tri, mb.astype(BF16), preferred_element_type=F32) + carry)
            carry = carry + jnp.sum(mb, axis=0, keepdims=True)
        return jnp.concatenate(outs, axis=0).astype(jnp.int32)

    sel = (gt | (eq & (prefix(eq) < need))) & (lane < N_EXPERTS)
    pos = prefix(sel)
    self_ = sel.astype(BF16)
    er = lax.broadcasted_iota(jnp.int32, (LANES, LANES), 0)
    ec = lax.broadcasted_iota(jnp.int32, (LANES, LANES), 1)
    rank = jnp.dot(self_, (er < ec).astype(BF16), preferred_element_type=F32).astype(jnp.int32)
    cnt = jnp.dot(self_, jnp.ones((LANES, LANES), BF16), preferred_element_type=F32).astype(jnp.int32)
    cnt_ref[...] = cnt
    nt = n // COMBINE_T
    kmax_ref[...] = jnp.max(cnt.reshape(nt, COMBINE_T, LANES), axis=1)

    key_ref[...] = pos
    st_ref[...] = jnp.zeros(st_ref.shape, jnp.int32)
    st_ref[0:nt, :] = key_ref[pl.ds(0, nt, stride=COMBINE_T), :]
    st_ref[nt:nt + 1, :] = jnp.full((1, LANES), cap, jnp.int32)
    key_ref[...] = (jnp.where(sel, pos + 1, 0) << KEY_SHIFT) | (tok << RANK_BITS) | rank


def _compact_kernel(cap, st_s, key_ref, aff_ref, idx_ref, rk_ref, gate_ref, accp, accg):
    tt = COMBINE_T
    nt = key_ref.shape[0] // tt
    accp[...] = jnp.zeros(accp.shape, jnp.int32)
    accg[...] = jnp.zeros(accg.shape, F32)
    jio = lax.broadcasted_iota(jnp.int32, (tt, LANES), 1)

    def tile_body(t, carry):
        rows = pl.ds(pl.multiple_of(t * tt, tt), tt)
        key, af = key_ref[rows, :], aff_ref[rows, :]
        for e in range(N_EXPERTS):
            g0 = jnp.minimum(lax.shift_right_logical(st_s[t * LANES + e], 7), cap // LANES - 1)
            kb = jnp.broadcast_to(key[:, e:e + 1], (tt, LANES))
            ab = jnp.broadcast_to(af[:, e:e + 1], (tt, LANES))
            rel = (kb >> KEY_SHIFT) - 1 - g0 * LANES
            pk = kb & ((1 << KEY_SHIFT) - 1)
            for d in range(2):
                oh = rel == jio + d * LANES
                accp[e, g0 + d] += jnp.sum(jnp.where(oh, pk, 0).reshape(tt // 8, 8, LANES), axis=0)
                accg[e, g0 + d] += jnp.sum(jnp.where(oh, ab, 0.0).reshape(tt // 8, 8, LANES), axis=0)
        return carry

    lax.fori_loop(0, nt, tile_body, 0)
    for e in range(N_EXPERTS):
        packed = jnp.concatenate([jnp.sum(accp[e, g], axis=0, keepdims=True) for g in range(cap // LANES)], axis=1)
        idx_ref[e:e + 1, :] = packed >> RANK_BITS
        rk_ref[e:e + 1, :] = (packed & (N_EXPERTS - 1)) * tt + ((packed >> RANK_BITS) & (tt - 1))
        gate_ref[e:e + 1, :] = jnp.concatenate(
            [jnp.sum(accg[e, g], axis=0, keepdims=True) for g in range(cap // LANES)], axis=1)


def _route(aff, tag):
    n = aff.shape[0]
    cap = CAPACITY_FACTOR * n // N_EXPERTS
    nt = n // COMBINE_T
    full = lambda s: pl.BlockSpec(s, lambda i, *_: (0,) * len(s))
    i32 = lambda s: jax.ShapeDtypeStruct(s, jnp.int32)
    assert (n << RANK_BITS) <= (1 << KEY_SHIFT) and cap + 1 < (1 << (31 - KEY_SHIFT))
    st, cnt, kmax, key = pl.pallas_call(
        functools.partial(_route_kernel, cap),
        grid=(1,),
        in_specs=[full((n, LANES))],
        out_specs=[full((nt + 8, LANES)), full((n, LANES)), full((nt, LANES)), full((n, LANES))],
        out_shape=[i32((nt + 8, LANES)), i32((n, LANES)), i32((nt, LANES)), i32((n, LANES))],
        compiler_params=_cparams(("arbitrary",)),
        name="route_" + tag,
    )(aff)
    groups = cap // LANES + 1
    idx, rk, gate = pl.pallas_call(
        functools.partial(_compact_kernel, cap),
        grid_spec=pltpu.PrefetchScalarGridSpec(
            num_scalar_prefetch=1, grid=(1,),
            in_specs=[full((n, LANES)), full((n, LANES))],
            out_specs=[full((N_EXPERTS, cap)), full((N_EXPERTS, cap)), full((N_EXPERTS, cap))],
            scratch_shapes=[pltpu.VMEM((N_EXPERTS, groups, 8, LANES), jnp.int32),
                            pltpu.VMEM((N_EXPERTS, groups, 8, LANES), F32)]),
        out_shape=[i32((N_EXPERTS, cap)), i32((N_EXPERTS, cap)), jax.ShapeDtypeStruct((N_EXPERTS, cap), F32)],
        compiler_params=_cparams(("arbitrary",)),
        name="compact_" + tag,
    )(st.reshape(-1), key, aff)
    return idx, rk, gate, st, cnt, kmax


FF_CHUNK = 256


def _ffn_kernel(cap, nf, per, idx_s, hp_hbm, hs_hbm, gate_ref, w1_ref, w3_ref, w2_ref, yp_hbm, ys_hbm,
                xg, xe, acc, gsem, osem):
    e = pl.program_id(0)
    f = pl.program_id(1)
    ne = pl.num_programs(0)
    srcs = (hp_hbm, hs_hbm)
    dsts = (yp_hbm, ys_hbm)
    slot = lax.rem(e, 2)

    def issue_part(ex, part, sl, live):
        tail = cap - (nf - 1) * per
        live_tail = live & (part < nf - 1)
        for p in range(2):
            for i in range(per):
                j = part * per + i
                jc = j if i < tail else jnp.minimum(j, cap - 1)
                tok = idx_s[(p * N_EXPERTS + ex) * cap + jc]

                @pl.when(live if i < tail else live_tail)
                def _():
                    pltpu.make_async_copy(srcs[p].at[pl.ds(tok, 1), :], xg.at[sl, pl.ds(p * cap + jc, 1), :],
                                          gsem.at[sl]).start()

    def out_copy(p, ex):
        return pltpu.make_async_copy(acc.at[pl.ds(p * cap, cap), :],
                                     dsts[p].at[pl.ds(pl.multiple_of(ex * cap, cap), cap), :], osem)

    @pl.when((e == 0) & (f == 0))
    def _prologue():
        def part_body(part, carry):
            issue_part(0, part, 0, True)
            return carry
        lax.fori_loop(0, nf, part_body, 0)

    @pl.when(f == 0)
    def _start_expert():
        for p in range(2):
            pltpu.make_async_copy(srcs[p].at[pl.ds(0, cap), :], xg.at[slot, pl.ds(p * cap, cap), :],
                                  gsem.at[slot]).wait()
        xe[...] = xg[slot].astype(BF16)

    x = xe[...]
    a = jnp.dot(x, w1_ref[...].astype(BF16), preferred_element_type=F32)
    b = jnp.dot(x, w3_ref[...].astype(BF16), preferred_element_type=F32)
    hid = ((a * jax.nn.sigmoid(a)) * b).astype(BF16)

    issue_part(jnp.minimum(e + 1, ne - 1), f, 1 - slot, e + 1 < ne)

    def down():
        return jnp.dot(hid, w2_ref[...].astype(BF16), preferred_element_type=F32)

    @pl.when(f == 0)
    def _first():
        @pl.when(e > 0)
        def _():
            for p in range(2):
                out_copy(p, e - 1).wait()
        acc[...] = down()

    @pl.when((f > 0) & (f < nf - 1))
    def _middle():
        acc[...] += down()

    @pl.when(f == nf - 1)
    def _last():
        acc[...] = (acc[...] + down()) * gate_ref[...]
        for p in range(2):
            out_copy(p, e).start()

        @pl.when(e == ne - 1)
        def _():
            for p in range(2):
                out_copy(p, e).wait()


def _ffn(idx_flat, h2p, h2s, gate, w1, w3, w2, cap):
    nf = EXPERT_FF // FF_CHUNK
    per = -(-cap // nf)
    grid_spec = pltpu.PrefetchScalarGridSpec(
        num_scalar_prefetch=1,
        grid=(N_EXPERTS, nf),
        in_specs=[pl.BlockSpec(memory_space=pl.ANY), pl.BlockSpec(memory_space=pl.ANY),
                  pl.BlockSpec((None, 2 * cap, 1), lambda e, f, *_: (e, 0, 0)),
                  pl.BlockSpec((None, D_MODEL, FF_CHUNK), lambda e, f, *_: (e, 0, f)),
                  pl.BlockSpec((None, D_MODEL, FF_CHUNK), lambda e, f, *_: (e, 0, f)),
                  pl.BlockSpec((None, FF_CHUNK, D_MODEL), lambda e, f, *_: (e, f, 0))],
        out_specs=[pl.BlockSpec(memory_space=pl.ANY), pl.BlockSpec(memory_space=pl.ANY)],
        scratch_shapes=[pltpu.VMEM((2, 2 * cap, D_MODEL), F32), pltpu.VMEM((2 * cap, D_MODEL), BF16),
                        pltpu.VMEM((2 * cap, D_MODEL), F32), pltpu.SemaphoreType.DMA((2,)),
                        pltpu.SemaphoreType.DMA(())])
    out = jax.ShapeDtypeStruct((N_EXPERTS * cap, D_MODEL), F32)
    return pl.pallas_call(
        functools.partial(_ffn_kernel, cap, nf, per),
        grid_spec=grid_spec, out_shape=[out, out],
        compiler_params=_cparams(("arbitrary", "arbitrary")),
        name="expert_ffn",
    )(idx_flat, h2p, h2s, gate, w1, w3, w2)


COMBINE_CW = 256
WAIT_ROWS = 32
ISSUE_UNROLL = 4


def _combine_kernel(cap, row_s, st_s, km_s, ye_hbm, x1_ref, g2_ref, cnt_ref, o_ref, buf, sem):
    t = pl.program_id(0)
    nt = pl.num_programs(0)
    tt = COMBINE_T
    slot = lax.rem(t, 2)

    def issue_tile(tile, sl):
        for e in range(N_EXPERTS):
            lo = st_s[tile * LANES + e]
            hi = st_s[(tile + 1) * LANES + e]

            def issue(g, carry):
                for u in range(ISSUE_UNROLL):
                    j = lo + g * ISSUE_UNROLL + u

                    @pl.when(j < hi)
                    def _():
                        pltpu.make_async_copy(ye_hbm.at[pl.ds(e * cap + j, 1), :],
                                              buf.at[sl, pl.ds(row_s[e * cap + j], 1), :], sem.at[sl]).start()
                return carry
            lax.fori_loop(0, lax.div(hi - lo + ISSUE_UNROLL - 1, ISSUE_UNROLL), issue, 0)

    def wait_tile(tile, sl):
        total = 0
        for e in range(N_EXPERTS):
            total = total + st_s[(tile + 1) * LANES + e] - st_s[tile * LANES + e]

        def wait_rows(n):
            def body(i, carry):
                pltpu.make_async_copy(ye_hbm.at[pl.ds(0, n), :], buf.at[sl, pl.ds(0, n), :], sem.at[sl]).wait()
                return carry
            return body
        lax.fori_loop(0, total // WAIT_ROWS, wait_rows(WAIT_ROWS), 0)
        lax.fori_loop(0, lax.rem(total, WAIT_ROWS), wait_rows(1), 0)

    @pl.when(t == 0)
    def _init():
        buf[...] = jnp.zeros(buf.shape, F32)
        issue_tile(0, 0)

    @pl.when(t + 1 < nt)
    def _prefetch():
        issue_tile(t + 1, 1 - slot)

    wait_tile(t, slot)

    kmax = km_s[t * LANES]
    cnt = jnp.tile(cnt_ref[...], (1, COMBINE_CW // LANES))
    for c in range(D_MODEL // COMBINE_CW):
        cs = slice(c * COMBINE_CW, (c + 1) * COMBINE_CW)

        def add_slot(k, a):
            v = buf[slot, pl.ds(pl.multiple_of(k * tt, tt), tt), cs]
            return a + jnp.where(cnt > k, v, 0.0)
        a = lax.fori_loop(0, kmax, add_slot, jnp.zeros((tt, COMBINE_CW), F32))
        o_ref[:, cs] = x1_ref[:, cs] + g2_ref[:, cs] * a


def _combine(rows, st, km, ye, x1, mod, tiles_per_mod, cnt, cap, tag):
    n = x1.shape[0]
    tt = COMBINE_T
    grid_spec = pltpu.PrefetchScalarGridSpec(
        num_scalar_prefetch=3,
        grid=(n // tt,),
        in_specs=[pl.BlockSpec(memory_space=pl.ANY),
                  pl.BlockSpec((tt, D_MODEL), lambda i, *_: (i, 0)),
                  pl.BlockSpec((None, 1, D_MODEL), lambda i, *_: (i // tiles_per_mod, 0, 5)),
                  pl.BlockSpec((tt, LANES), lambda i, *_: (i, 0))],
        out_specs=pl.BlockSpec((tt, D_MODEL), lambda i, *_: (i, 0)),
        scratch_shapes=[pltpu.VMEM((2, N_EXPERTS * tt, D_MODEL), F32), pltpu.SemaphoreType.DMA((2,))])
    return pl.pallas_call(
        functools.partial(_combine_kernel, cap),
        grid_spec=grid_spec, out_shape=jax.ShapeDtypeStruct((n, D_MODEL), F32),
        compiler_params=_cparams(("arbitrary",)),
        name="combine_" + tag,
    )(jnp.pad(rows.reshape(-1), (0, ISSUE_UNROLL)), st.reshape(-1), km.reshape(-1), ye, x1, mod, cnt)


def _prep_weights(norm1_gain, norm2_gain, w_in, a_q_gain, a_k_gain, a_lambda_q1, a_lambda_k1, a_lambda_q2,
                  a_lambda_k2, a_sub_gain, q_a_gain, w_uq, b_qn_gain, b_qr_gain, kv_a_gain, w_ukv, b_kn_gain,
                  b_kr_gain, w_o, w_router):
    tile = lambda g, width: jnp.tile(g.reshape(1, -1), (1, width // g.shape[-1]))
    wuq = w_uq[0].reshape(Q_RANK, B_HEADS, B_QK)
    wuq = jnp.concatenate([wuq[:, :, :B_NOPE].reshape(Q_RANK, -1), wuq[:, :, B_NOPE:].reshape(Q_RANK, -1)], axis=1)
    return {
        "norm1_gain": norm1_gain.reshape(1, -1), "norm2_gain": norm2_gain.reshape(1, -1),
        "w_in": w_in[0].astype(BF16),
        "a_q_gain": tile(a_q_gain, MXU_DIM), "a_k_gain": tile(a_k_gain, MXU_DIM),
        "a_lambda_q1": a_lambda_q1.reshape(1, -1), "a_lambda_k1": a_lambda_k1.reshape(1, -1),
        "a_lambda_q2": a_lambda_q2.reshape(1, -1), "a_lambda_k2": a_lambda_k2.reshape(1, -1),
        "a_sub_gain": a_sub_gain.reshape(1, -1), "q_a_gain": q_a_gain.reshape(1, -1),
        "w_uq": wuq.astype(BF16), "b_qn_gain": b_qn_gain.reshape(1, -1), "b_qr_gain": tile(b_qr_gain, MXU_DIM),
        "kv_a_gain": kv_a_gain.reshape(1, -1), "w_ukv": w_ukv[0].astype(BF16),
        "b_kn_gain": b_kn_gain.reshape(1, -1), "b_kr_gain": tile(b_kr_gain, LANES),
        "w_oa": w_o[0, :A_V].astype(BF16), "w_ob": w_o[0, A_V:].astype(BF16),
        "w_router": jnp.pad(w_router[0], ((0, 0), (0, LANES - N_EXPERTS))).astype(BF16),
        "seg64": _seg_matrix(A_DH),
    }


def kernel(x_prompt, x_sample, cache_diff_k, cache_diff_v, cache_mla_ckv, cache_mla_krope, c, c_ctx, w_ada, b_ada, norm1_gain, norm2_gain, w_in, a_q_gain, a_k_gain, a_lambda_q1, a_lambda_k1, a_lambda_q2, a_lambda_k2, a_sub_gain, q_a_gain, w_uq, b_qn_gain, b_qr_gain, kv_a_gain, w_ukv, b_kn_gain, b_kr_gain, w_o, w_router, w_exp1, w_exp3, w_exp2):
    nbp, lp, _ = x_prompt.shape
    nbs, ls, _ = x_sample.shape
    past = cache_diff_k.shape[2]
    w = _prep_weights(norm1_gain, norm2_gain, w_in, a_q_gain, a_k_gain, a_lambda_q1, a_lambda_k1, a_lambda_q2,
                      a_lambda_k2, a_sub_gain, q_a_gain, w_uq, b_qn_gain, b_qr_gain, kv_a_gain, w_ukv, b_kn_gain,
                      b_kr_gain, w_o, w_router)

    cvec = jnp.concatenate([c_ctx[None], c, jnp.zeros((8 - 1 - nbs, D_MODEL), F32)], axis=0)
    mod = _ada(cvec, w_ada[0], b_ada)
    mod_p = mod[0:1].reshape(1, 1, -1)
    mod_s = mod[1:1 + nbs].reshape(nbs, 1, -1)

    xp = x_prompt.reshape(nbp * lp, D_MODEL)
    xs = x_sample.reshape(nbs * ls, D_MODEL)
    tm = 256
    assert lp == tm, "the prompt projection writes one batch of new_diff_k per row tile"
    aq_p, ak_p, av_p, qm_p, km_p, bv_p, ak32, av32, ckv32, kr32 = _in_proj(xp, mod_p, False, nbp * lp // tm, w, None)
    aq_s, ak_s, av_s, qm_s, km_s, bv_s = _in_proj(xs, mod_s, True, ls // tm, w, _rope_tables(ls))
    km_c, bv_c = _cache_kv(cache_mla_ckv.reshape(nbs * past, KV_RANK), cache_mla_krope.reshape(nbs * past, B_ROPE), w)

    cache_a = (cache_diff_k.reshape(nbs * past, A_QK), cache_diff_v.reshape(nbs * past, A_V))

    oa_p = _diff_attn(aq_p, ak_p, av_p, None, nbp, lp, lp, A_HEADS, w)
    ob_p = _mla_attn(qm_p, km_p, bv_p, None, nbp, lp, lp, B_HEADS)
    oa_s = _diff_attn(aq_s, ak_s, av_s, cache_a, nbs, ls, 256, 4, w)
    ob_s = _mla_attn(qm_s, km_s, bv_s, (km_c, bv_c), nbs, ls, 256, 4)

    x1_p, h2_p, aff_p = _out_proj(oa_p, ob_p, xp, mod_p, nbp * lp // tm, w, "prompt")
    x1_s, h2_s, aff_s = _out_proj(oa_s, ob_s, xs, mod_s, ls // tm, w, "sample")

    idx_p, rk_p, gate_p, st_p, cnt_p, km_p_ = _route(aff_p, "prompt")
    idx_s, rk_s, gate_s, st_s, cnt_s, km_s_ = _route(aff_s, "sample")
    cap = idx_p.shape[1]
    idx_flat = jnp.stack([idx_p, idx_s]).reshape(-1)
    gate = jnp.concatenate([gate_p, gate_s], axis=1)[..., None]
    ye_p, ye_s = _ffn(idx_flat, h2_p, h2_s, gate, w_exp1[0], w_exp3[0], w_exp2[0], cap)

    y_p = _combine(rk_p, st_p, km_p_, ye_p, x1_p, mod_p, nbp * lp // COMBINE_T, cnt_p, cap, "prompt")
    y_s = _combine(rk_s, st_s, km_s_, ye_s, x1_s, mod_s, ls // COMBINE_T, cnt_s, cap, "sample")

    return (y_p.reshape(nbp, lp, D_MODEL), y_s.reshape(nbs, ls, D_MODEL),
            ak32, av32.reshape(nbp, 1, lp, A_HEADS, A_DV),
            ckv32.reshape(nbp, 1, lp, KV_RANK), kr32.reshape(nbp, 1, lp, B_ROPE))
```

```python
import functools
import math

import numpy as np
import jax
import jax.numpy as jnp
from jax import lax
from jax.experimental import pallas as pl
from jax.experimental.pallas import tpu as pltpu

F32 = jnp.float32
BF16 = jnp.bfloat16

D_MODEL = 2048
GRID_W = 64
A_HEADS = 8
A_DH = 64
A_DV = 128
B_HEADS = 8
B_NOPE = 128
B_ROPE = 64
B_DV = 128
Q_RANK = 512
KV_RANK = 256
N_EXPERTS = 16
EXPERT_FF = 1536
CAPACITY_FACTOR = 2
ROPE_BASE = 10000.0
EPS = 1e-6
LAM_INIT = 0.8 - 0.6 * math.exp(-0.3 * 0)
LOG2E = math.log2(math.e)

A_QK = A_HEADS * 2 * A_DH
A_V = A_HEADS * A_DV
B_QK = B_NOPE + B_ROPE
IN_COLS = 2 * A_QK + A_V + Q_RANK + KV_RANK + B_ROPE
LANES = 128
MXU_DIM = 256
VMEM_LIMIT = 56 * 1024 * 1024


def _cparams(sem):
    return pltpu.CompilerParams(dimension_semantics=sem, vmem_limit_bytes=VMEM_LIMIT)


def _resident(shape):
    nd = len(shape)
    return pl.BlockSpec(shape, lambda *_: (0,) * nd, pipeline_mode=pl.Buffered(1))


def _rms(x, gain):
    return x * lax.rsqrt(jnp.mean(x * x, axis=-1, keepdims=True) + EPS) * gain


def _seg_rms(y, segmat, gain):
    y2 = (y * y).astype(BF16)
    outs = []
    for c in range(y.shape[1] // MXU_DIM):
        sl = slice(c * MXU_DIM, (c + 1) * MXU_DIM)
        ms = jnp.dot(y2[:, sl], segmat, preferred_element_type=F32)
        outs.append(y[:, sl] * lax.rsqrt(ms + EPS) * gain)
    return jnp.concatenate(outs, axis=1)


def _rope(y, cos, sin):
    outs = []
    for c in range(y.shape[1] // LANES):
        yc = y[:, c * LANES:(c + 1) * LANES]
        lane = lax.broadcasted_iota(jnp.int32, yc.shape, 1)
        partner = jnp.where((lane & 16) == 0, pltpu.roll(yc, LANES - 16, 1), pltpu.roll(yc, 16, 1))
        outs.append(yc * cos + partner * sin)
    return outs[0] if len(outs) == 1 else jnp.concatenate(outs, axis=1)


def _rope_tables(n_tok):
    t = np.arange(n_tok)
    row, col = t // GRID_W, t % GRID_W
    nf = A_DH // 4
    inv = ROPE_BASE ** (-np.arange(nf, dtype=np.float64) * 2.0 / (A_DH // 2))
    lane = np.arange(A_DH)
    pos = np.where(lane[None, :] < A_DH // 2, row[:, None], col[:, None]).astype(np.float64)
    ang = pos * inv[lane % nf][None, :]
    sign = np.where((lane % (2 * nf)) < nf, -1.0, 1.0)[None, :]
    cos = np.tile(np.cos(ang), (1, LANES // A_DH)).astype(np.float32)
    sin = np.tile(np.sin(ang) * sign, (1, LANES // A_DH)).astype(np.float32)
    return jnp.asarray(cos), jnp.asarray(sin)


def _seg_matrix(width):
    i = np.arange(MXU_DIM)
    return jnp.asarray(((i[:, None] // width) == (i[None, :] // width)).astype(np.float32) / width, dtype=BF16)


def _ada_kernel(c_ref, w_ref, b_ref, o_ref):
    c = c_ref[...]
    s = c * jax.nn.sigmoid(c)
    o_ref[...] = jnp.dot(s.astype(BF16), w_ref[...].astype(BF16), preferred_element_type=F32) + b_ref[...]


def _ada(cvec, w_ada, b_ada):
    tn = 1024
    n = w_ada.shape[1]
    return pl.pallas_call(
        _ada_kernel,
        grid=(n // tn,),
        in_specs=[pl.BlockSpec((8, D_MODEL), lambda j: (0, 0)),
                  pl.BlockSpec((D_MODEL, tn), lambda j: (0, j)),
                  pl.BlockSpec((1, tn), lambda j: (0, j))],
        out_specs=pl.BlockSpec((8, tn), lambda j: (0, j)),
        out_shape=jax.ShapeDtypeStruct((8, n), F32),
        compiler_params=_cparams(("arbitrary",)),
        name="ada_mod",
    )(cvec, w_ada, b_ada)


def _expand_kv(ckv_n, kr, wukv_ref, kng, km_ref, bv_ref):
    kv = jnp.dot(ckv_n.astype(BF16), wukv_ref[...], preferred_element_type=F32)
    krb = kr[:, :B_ROPE].astype(BF16)
    for h in range(B_HEADS):
        base = h * (B_NOPE + B_DV)
        kn = _rms(kv[:, base:base + B_NOPE], kng)
        km_ref[h] = jnp.concatenate([kn.astype(BF16), krb], axis=1)
        bv_ref[h] = kv[:, base + B_NOPE:base + B_NOPE + B_DV].astype(BF16)


def _in_kernel(sample, *refs):
    (x_ref, sh_ref, sc_ref, n1g_ref, win_ref, aqg_ref, akg_ref, qag_ref, wuq_ref, qng_ref, qrg_ref,
     kvag_ref, wukv_ref, kng_ref, krg_ref, seg_ref) = refs[:16]
    refs = refs[16:]
    if sample:
        cos_ref, sin_ref = refs[:2]
        refs = refs[2:]
        cos, sin = cos_ref[...], sin_ref[...]
    aq_ref, ak_ref, av_ref, qm_ref, km_ref, bv_ref = refs[:6]
    refs = refs[6:]

    x = x_ref[...]
    h = _rms(x, n1g_ref[...]) * (1.0 + sc_ref[...]) + sh_ref[...]
    hb = h.astype(BF16)

    def proj(c0, c1):
        return jnp.dot(hb, win_ref[:, c0:c1], preferred_element_type=F32)

    seg = seg_ref[...]
    c0 = 2 * A_QK + A_V
    y_q = proj(c0, c0 + Q_RANK)
    y_ckv = proj(c0 + Q_RANK, c0 + Q_RANK + KV_RANK)
    kr64 = proj(c0 + Q_RANK + KV_RANK, c0 + Q_RANK + KV_RANK + B_ROPE)
    q_lat = _rms(y_q, qag_ref[...])
    qb = jnp.dot(q_lat.astype(BF16), wuq_ref[...], preferred_element_type=F32)
    ckv = _rms(y_ckv, kvag_ref[...])
    kr_raw = jnp.concatenate([kr64, jnp.zeros_like(kr64)], axis=1)
    kr = kr_raw * lax.rsqrt(jnp.sum(kr_raw * kr_raw, axis=-1, keepdims=True) * (1.0 / B_ROPE) + EPS) * krg_ref[...]
    if sample:
        kr = _rope(kr, cos, sin)
    _expand_kv(ckv, kr, wukv_ref, kng_ref[...], km_ref, bv_ref)
    y_aq = proj(0, A_QK)
    qr = _seg_rms(qb[:, B_HEADS * B_NOPE:], seg, qrg_ref[...])
    if sample:
        qr = _rope(qr, cos, sin)
    scale = B_QK ** -0.5 * LOG2E
    for hd in range(B_HEADS):
        qn = _rms(qb[:, hd * B_NOPE:(hd + 1) * B_NOPE], qng_ref[...])
        qm_ref[hd] = jnp.concatenate(
            [(qn * scale).astype(BF16), (qr[:, hd * B_ROPE:(hd + 1) * B_ROPE] * scale).astype(BF16)], axis=1)

    y_ak = proj(A_QK, 2 * A_QK)
    aq = _seg_rms(y_aq, seg, aqg_ref[...])
    if sample:
        aq = _rope(aq, cos, sin)
    aq_ref[...] = (aq * (A_DH ** -0.5 * LOG2E)).astype(BF16)
    av = proj(2 * A_QK, 2 * A_QK + A_V)
    ak = _seg_rms(y_ak, seg, akg_ref[...])
    if sample:
        ak = _rope(ak, cos, sin)
    ak_ref[...] = ak.astype(BF16)
    av_ref[...] = av.astype(BF16)
    if not sample:
        ak32_ref, av32_ref, ckv32_ref, kr32_ref = refs
        ak32_ref[...] = ak.reshape(ak32_ref.shape)
        av32_ref[...] = av
        ckv32_ref[...] = ckv
        kr32_ref[...] = kr[:, :B_ROPE]


def _in_proj(x, mod, sample, tiles_per_mod, w, tables):
    m = x.shape[0]
    tm = 256
    nt = m // tm
    row = lambda i: (i, 0)
    modspec = lambda k: pl.BlockSpec((None, 1, D_MODEL), lambda i: (i // tiles_per_mod, 0, k))
    in_specs = [pl.BlockSpec((tm, D_MODEL), row), modspec(0), modspec(1), _resident((1, D_MODEL)),
                _resident((D_MODEL, IN_COLS)), _resident((1, MXU_DIM)), _resident((1, MXU_DIM)),
                _resident((1, Q_RANK)), _resident((Q_RANK, B_HEADS * B_QK)), _resident((1, B_NOPE)),
                _resident((1, MXU_DIM)), _resident((1, KV_RANK)), _resident((KV_RANK, B_HEADS * (B_NOPE + B_DV))),
                _resident((1, B_NOPE)), _resident((1, LANES)), _resident((MXU_DIM, MXU_DIM))]
    args = [x, mod, mod, w["norm1_gain"], w["w_in"], w["a_q_gain"], w["a_k_gain"], w["q_a_gain"], w["w_uq"],
            w["b_qn_gain"], w["b_qr_gain"], w["kv_a_gain"], w["w_ukv"], w["b_kn_gain"], w["b_kr_gain"], w["seg64"]]
    if sample:
        per = tables[0].shape[0] // tm
        in_specs += [pl.BlockSpec((tm, LANES), lambda i: (i % per, 0))] * 2
        args += list(tables)
    out_shape = [jax.ShapeDtypeStruct((m, A_QK), BF16), jax.ShapeDtypeStruct((m, A_QK), BF16),
                 jax.ShapeDtypeStruct((m, A_V), BF16), jax.ShapeDtypeStruct((B_HEADS, m, B_QK), BF16),
                 jax.ShapeDtypeStruct((B_HEADS, m, B_QK), BF16), jax.ShapeDtypeStruct((B_HEADS, m, B_DV), BF16)]
    hspec = lambda d: pl.BlockSpec((B_HEADS, tm, d), lambda i: (0, i, 0))
    out_specs = [pl.BlockSpec((tm, A_QK), row), pl.BlockSpec((tm, A_QK), row), pl.BlockSpec((tm, A_V), row),
                 hspec(B_QK), hspec(B_QK), hspec(B_DV)]
    if not sample:
        out_shape += [jax.ShapeDtypeStruct((nt, 1, tm, A_HEADS, 2, A_DH), F32), jax.ShapeDtypeStruct((m, A_V), F32),
                      jax.ShapeDtypeStruct((m, KV_RANK), F32), jax.ShapeDtypeStruct((m, B_ROPE), F32)]
        out_specs += [pl.BlockSpec((None, None, tm, A_HEADS, 2, A_DH), lambda i: (i, 0, 0, 0, 0, 0)),
                      pl.BlockSpec((tm, A_V), row),
                      pl.BlockSpec((tm, KV_RANK), row), pl.BlockSpec((tm, B_ROPE), row)]
    return pl.pallas_call(
        functools.partial(_in_kernel, sample),
        grid=(nt,), in_specs=in_specs, out_specs=out_specs, out_shape=out_shape,
        compiler_params=_cparams(("arbitrary",)),
        name="in_proj_sample" if sample else "in_proj_prompt",
    )(*args)


def _cache_kv_kernel(ckv_ref, kr_ref, wukv_ref, kng_ref, km_ref, bv_ref):
    _expand_kv(ckv_ref[...], kr_ref[...], wukv_ref, kng_ref[...], km_ref, bv_ref)


def _cache_kv(ckv, kr, w):
    m = ckv.shape[0]
    return pl.pallas_call(
        _cache_kv_kernel,
        grid=(1,),
        in_specs=[_resident((m, KV_RANK)), _resident((m, B_ROPE)),
                  _resident((KV_RANK, B_HEADS * (B_NOPE + B_DV))), _resident((1, B_NOPE))],
        out_specs=[pl.BlockSpec((B_HEADS, m, B_QK), lambda i: (0, 0, 0)),
                   pl.BlockSpec((B_HEADS, m, B_DV), lambda i: (0, 0, 0))],
        out_shape=[jax.ShapeDtypeStruct((B_HEADS, m, B_QK), BF16), jax.ShapeDtypeStruct((B_HEADS, m, B_DV), BF16)],
        compiler_params=_cparams(("arbitrary",)),
        name="cache_kv",
    )(ckv, kr, w["w_ukv"], w["b_kn_gain"])


def _qk(q, k):
    return lax.dot_general(q, k, (((1,), (1,)), ((), ())), preferred_element_type=F32)


KEY_CHUNK = 512
SCORE_AHEAD = 1


def _scores(q, k, kc):
    s = _qk(q, k)
    return s if kc is None else jnp.concatenate([s, _qk(q, kc)], axis=1)


def _weighted(p, v, vc):
    if vc is None:
        return jnp.dot(p, v, preferred_element_type=F32)
    lk = v.shape[0]
    return (jnp.dot(p[:, :lk], v, preferred_element_type=F32)
            + jnp.dot(p[:, lk:], vc, preferred_element_type=F32))


def _diff_attn_kernel(hb, cached, q_ref, k_ref, v_ref, *refs):
    if cached:
        kc_ref, vc_ref = refs[:2]
        refs = refs[2:]
    lq1, lk1, lq2, lk2, sub_ref, o_ref = refs
    lam = (jnp.exp(jnp.sum(lq1[...] * lk1[...], axis=-1, keepdims=True))
           - jnp.exp(jnp.sum(lq2[...] * lk2[...], axis=-1, keepdims=True)) + LAM_INIT)
    def scores(h, comp):
        sl = slice(h * LANES, (h + 1) * LANES)
        q, k = q_ref[:, sl], k_ref[:, sl]
        kc = kc_ref[:, sl].astype(BF16) if cached else None
        lane = lax.broadcasted_iota(jnp.int32, q.shape, 1)
        keep = (lane < A_DH) if comp == 0 else (lane >= A_DH)
        return _scores(jnp.where(keep, q, jnp.zeros_like(q)), k, kc)

    s1_next, s2_next = scores(0, 0), scores(0, 1)
    for h in range(hb):
        sl = slice(h * LANES, (h + 1) * LANES)
        s1, s2 = s1_next, s2_next
        if h + 1 < hb:
            s1_next = scores(h + 1, 0)
        e1 = jnp.exp2(s1 - jnp.max(s1, axis=-1, keepdims=True))
        if h + 1 < hb:
            s2_next = scores(h + 1, 1)
        e2 = jnp.exp2(s2 - jnp.max(s2, axis=-1, keepdims=True))
        if cached:
            ones = lambda x: jnp.concatenate([x, jnp.ones_like(x)], axis=1)
            v, vc = ones(v_ref[:, sl]), ones(vc_ref[:, sl].astype(BF16))
            n1 = _weighted(e1.astype(BF16), v, vc)
            n2 = _weighted(e2.astype(BF16), v, vc)
            o = n1[:, :LANES] * (1.0 / n1[:, LANES:LANES + 1]) - n2[:, :LANES] * (lam / n2[:, LANES:LANES + 1])
        else:
            l1 = jnp.sum(e1, axis=-1, keepdims=True)
            l2 = jnp.sum(e2, axis=-1, keepdims=True)
            p = e1 - e2 * (lam * l1 / l2)
            o = jnp.dot(p.astype(BF16), v_ref[:, sl], preferred_element_type=F32) * (1.0 / l1)
        o_ref[:, sl] = (_rms(o, sub_ref[...]) * (1.0 - LAM_INIT)).astype(BF16)


def _diff_attn(q, k, v, cache, nb, lq, tq, hb, w):
    nq = lq // tq
    vec = _resident((1, A_DH))
    kv = lambda rows: pl.BlockSpec((rows, hb * LANES), lambda b, g, i: (b, g))
    in_specs = [pl.BlockSpec((tq, hb * LANES), lambda b, g, i: (b * nq + i, g)), kv(lq), kv(lq)]
    args = [q, k, v]
    if cache is not None:
        past = cache[0].shape[0] // nb
        in_specs += [kv(past), kv(past)]
        args += list(cache)
    return pl.pallas_call(
        functools.partial(_diff_attn_kernel, hb, cache is not None),
        grid=(nb, A_HEADS // hb, nq),
        in_specs=in_specs + [vec, vec, vec, vec, _resident((1, A_DV))],
        out_specs=pl.BlockSpec((tq, hb * LANES), lambda b, g, i: (b * nq + i, g)),
        out_shape=jax.ShapeDtypeStruct((nb * lq, A_V), BF16),
        compiler_params=_cparams(("arbitrary", "arbitrary", "arbitrary")),
        name="diff_attn_cached" if cache is not None else "diff_attn",
    )(*args, w["a_lambda_q1"], w["a_lambda_k1"], w["a_lambda_q2"], w["a_lambda_k2"], w["a_sub_gain"])


def _mla_attn_kernel(hb, cached, q_ref, k_ref, v_ref, *refs):
    if cached:
        kc_ref, vc_ref, o_ref = refs
    else:
        (o_ref,) = refs
    lk = k_ref.shape[1]
    ck = min(lk, KEY_CHUNK)
    def head_scores(h):
        q = q_ref[h]
        s = [_qk(q, k_ref[h, c * ck:(c + 1) * ck]) for c in range(lk // ck)]
        return s + [_qk(q, kc_ref[h])] if cached else s

    ahead = [head_scores(i) for i in range(min(SCORE_AHEAD, hb))]
    for h in range(hb):
        vs = [v_ref[h, c * ck:(c + 1) * ck] for c in range(lk // ck)] + ([vc_ref[h]] if cached else [])
        s = ahead.pop(0)
        if h + SCORE_AHEAD < hb:
            ahead.append(head_scores(h + SCORE_AHEAD))
        m = functools.reduce(jnp.maximum, [jnp.max(sc, axis=-1, keepdims=True) for sc in s])
        l = 0.0
        o = 0.0
        for sc, v in zip(s, vs):
            e = jnp.exp2(sc - m)
            if cached:
                v = jnp.concatenate([v, jnp.ones_like(v)], axis=1)
            else:
                l = l + jnp.sum(e, axis=-1, keepdims=True)
            o = o + jnp.dot(e.astype(BF16), v, preferred_element_type=F32)
        if cached:
            o, l = o[:, :LANES], o[:, LANES:LANES + 1]
        o_ref[:, h * LANES:(h + 1) * LANES] = (o * (1.0 / l)).astype(BF16)


def _mla_attn(q, k, v, cache, nb, lq, tq, hb):
    nq = lq // tq
    kv = lambda rows, d: pl.BlockSpec((hb, rows, d), lambda b, g, i: (g, b, 0))
    in_specs = [pl.BlockSpec((hb, tq, B_QK), lambda b, g, i: (g, b * nq + i, 0)), kv(lq, B_QK), kv(lq, B_DV)]
    args = [q, k, v]
    if cache is not None:
        past = cache[0].shape[1] // nb
        in_specs += [kv(past, B_QK), kv(past, B_DV)]
        args += list(cache)
    return pl.pallas_call(
        functools.partial(_mla_attn_kernel, hb, cache is not None),
        grid=(nb, B_HEADS // hb, nq),
        in_specs=in_specs,
        out_specs=pl.BlockSpec((tq, hb * LANES), lambda b, g, i: (b * nq + i, g)),
        out_shape=jax.ShapeDtypeStruct((nb * lq, B_HEADS * B_DV), BF16),
        compiler_params=_cparams(("arbitrary", "arbitrary", "arbitrary")),
        name="mla_attn_cached" if cache is not None else "mla_attn",
    )(*args)


def _out_kernel(oa_ref, ob_ref, x_ref, g1_ref, sh2_ref, sc2_ref, n2g_ref, woa_ref, wob_ref, wr_ref,
                x1_ref, h2_ref, aff_ref):
    sub = OUT_SUB
    rows = [pl.ds(r * sub, sub) for r in range(x_ref.shape[0] // sub)]
    os_ = [jnp.dot(oa_ref[r, :], woa_ref[...], preferred_element_type=F32)
           + jnp.dot(ob_ref[r, :], wob_ref[...], preferred_element_type=F32) for r in rows]
    for r, o in zip(rows, os_):
        x1 = x_ref[r, :] + g1_ref[...] * o
        x1_ref[r, :] = x1
        h2 = _rms(x1, n2g_ref[...]) * (1.0 + sc2_ref[...]) + sh2_ref[...]
        h2_ref[r, :] = h2
        logits = jnp.dot(h2.astype(BF16), wr_ref[...], preferred_element_type=F32)
        lane = lax.broadcasted_iota(jnp.int32, logits.shape, 1)
        logits = jnp.where(lane < N_EXPERTS, logits, -jnp.inf)
        e = jnp.exp(logits - jnp.max(logits, axis=-1, keepdims=True))
        aff_ref[r, :] = e / jnp.sum(e, axis=-1, keepdims=True)


OUT_SUB = 256


def _out_proj(oa, ob, x, mod, tiles_per_mod, w, tag):
    m = x.shape[0]
    tm = 512
    tiles_per_mod = tiles_per_mod * 256 // tm
    row = lambda i: (i, 0)
    modspec = lambda k: pl.BlockSpec((None, 1, D_MODEL), lambda i: (i // tiles_per_mod, 0, k))
    return pl.pallas_call(
        _out_kernel,
        grid=(m // tm,),
        in_specs=[pl.BlockSpec((tm, A_V), row), pl.BlockSpec((tm, A_V), row), pl.BlockSpec((tm, D_MODEL), row),
                  modspec(2), modspec(3), modspec(4), _resident((1, D_MODEL)),
                  _resident((A_V, D_MODEL)), _resident((A_V, D_MODEL)), _resident((D_MODEL, LANES))],
        out_specs=[pl.BlockSpec((tm, D_MODEL), row), pl.BlockSpec((tm, D_MODEL), row),
                   pl.BlockSpec((tm, LANES), row)],
        out_shape=[jax.ShapeDtypeStruct((m, D_MODEL), F32), jax.ShapeDtypeStruct((m, D_MODEL), F32),
                   jax.ShapeDtypeStruct((m, LANES), F32)],
        compiler_params=_cparams(("arbitrary",)),
        name="out_proj_" + tag,
    )(oa, ob, x, mod, mod, mod, w["norm2_gain"], w["w_oa"], w["w_ob"], w["w_router"])


ROUTE_BLK = 256
COMBINE_T = 128
KEY_SHIFT = 16
GROUP_BITS = 3
GROUP = 1 << GROUP_BITS
PLAN_NG_SHIFT = 10
PLAN_GS_SHIFT = 15


def _route_kernel(cap, aff_ref, st_ref, key_ref, plan_ref, off_ref):
    n = aff_ref.shape[0]
    nblk = n // ROUTE_BLK
    aff = aff_ref[...]
    lane = lax.broadcasted_iota(jnp.int32, (n, LANES), 1)
    tok = lax.broadcasted_iota(jnp.int32, (n, LANES), 0)

    def narrow(carry):
        lo, hi = carry
        mid = lo + (hi - lo) * 0.5
        mid = jnp.where(mid < hi, mid, lo)
        above = aff > mid
        few = jnp.sum(above.astype(jnp.int32), axis=0, keepdims=True) < cap
        up = jnp.min(jnp.where(above, aff, jnp.inf), axis=0, keepdims=True)
        dn = jnp.max(jnp.where(above, -jnp.inf, aff), axis=0, keepdims=True)
        return jnp.where(few, lo, up), jnp.where(few, dn, hi)

    bounds = (jnp.min(aff, axis=0, keepdims=True), jnp.max(aff, axis=0, keepdims=True))
    thr, _ = lax.while_loop(lambda c: jnp.max((c[0] < c[1]).astype(jnp.int32)) > 0, narrow, bounds)
    gt = aff > thr
    eq = aff == thr
    need = cap - jnp.sum(gt.astype(jnp.int32), axis=0, keepdims=True)

    r = lax.broadcasted_iota(jnp.int32, (ROUTE_BLK, ROUTE_BLK), 0)
    c = lax.broadcasted_iota(jnp.int32, (ROUTE_BLK, ROUTE_BLK), 1)
    tri = (c < r).astype(BF16)

    def prefix(mask):
        m = mask.astype(F32)
        carry = jnp.zeros((1, LANES), F32)
        outs = []
        for b in range(nblk):
            mb = m[b * ROUTE_BLK:(b + 1) * ROUTE_BLK]
            outs.append(jnp.dot(tri, mb.astype(BF16), preferred_element_type=F32) + carry)
            carry = carry + jnp.sum(mb, axis=0, keepdims=True)
        return jnp.concatenate(outs, axis=0).astype(jnp.int32)

    sel = (gt | (eq & (prefix(eq) < need))) & (lane < N_EXPERTS)
    pos = prefix(sel)
    nt = n // COMBINE_T
    key_ref[...] = pos
    st_ref[...] = jnp.zeros(st_ref.shape, jnp.int32)
    st_ref[0:nt, :] = key_ref[pl.ds(0, nt, stride=COMBINE_T), :]
    st_ref[nt:nt + 1, :] = jnp.full((1, LANES), cap, jnp.int32)
    key_ref[...] = (jnp.where(sel, pos + 1, 0) << KEY_SHIFT) | tok

    lo = st_ref[0:nt, :]
    hi = st_ref[pl.ds(1, nt), :]
    a = lo - (lo & (GROUP - 1))
    ng = jnp.where(hi > lo, lax.shift_right_logical(hi - a + (GROUP - 1), GROUP_BITS), 0)
    er = lax.broadcasted_iota(jnp.int32, (LANES, LANES), 0)
    ec = lax.broadcasted_iota(jnp.int32, (LANES, LANES), 1)
    gs = jnp.dot(ng.astype(BF16), (er < ec).astype(BF16), preferred_element_type=F32).astype(jnp.int32)
    elane = lax.broadcasted_iota(jnp.int32, (nt, LANES), 1)
    plan_ref[...] = jnp.where(elane == N_EXPERTS, gs, a | (ng << PLAN_NG_SHIFT) | (gs << PLAN_GS_SHIFT))
    off_ref[...] = gs * GROUP - a


def _compact_kernel(cap, st_s, key_ref, aff_ref, idx_ref, gate_ref, accp, accg):
    tt = COMBINE_T
    nt = key_ref.shape[0] // tt
    accp[...] = jnp.zeros(accp.shape, jnp.int32)
    accg[...] = jnp.zeros(accg.shape, F32)
    jio = lax.broadcasted_iota(jnp.int32, (tt, LANES), 1)

    def tile_body(t, carry):
        rows = pl.ds(pl.multiple_of(t * tt, tt), tt)
        key, af = key_ref[rows, :], aff_ref[rows, :]
        for e in range(N_EXPERTS):
            g0 = jnp.minimum(lax.shift_right_logical(st_s[t * LANES + e], 7), cap // LANES - 1)
            kb = jnp.broadcast_to(key[:, e:e + 1], (tt, LANES))
            ab = jnp.broadcast_to(af[:, e:e + 1], (tt, LANES))
            rel = (kb >> KEY_SHIFT) - 1 - g0 * LANES
            pk = kb & ((1 << KEY_SHIFT) - 1)
            for d in range(2):
                oh = rel == jio + d * LANES
                accp[e, g0 + d] += jnp.sum(jnp.where(oh, pk, 0).reshape(tt // 8, 8, LANES), axis=0)
                accg[e, g0 + d] += jnp.sum(jnp.where(oh, ab, 0.0).reshape(tt // 8, 8, LANES), axis=0)
        return carry

    lax.fori_loop(0, nt, tile_body, 0)
    for e in range(N_EXPERTS):
        idx_ref[e:e + 1, :] = jnp.concatenate(
            [jnp.sum(accp[e, g], axis=0, keepdims=True) for g in range(cap // LANES)], axis=1)
        gate_ref[e:e + 1, :] = jnp.concatenate(
            [jnp.sum(accg[e, g], axis=0, keepdims=True) for g in range(cap // LANES)], axis=1)


def _route(aff, tag):
    n = aff.shape[0]
    cap = CAPACITY_FACTOR * n // N_EXPERTS
    nt = n // COMBINE_T
    full = lambda s: pl.BlockSpec(s, lambda i, *_: (0,) * len(s))
    i32 = lambda s: jax.ShapeDtypeStruct(s, jnp.int32)
    assert n <= (1 << KEY_SHIFT) and cap + 1 < (1 << (31 - KEY_SHIFT))
    assert cap < (1 << PLAN_NG_SHIFT) and COMBINE_T // GROUP + 1 < (1 << (PLAN_GS_SHIFT - PLAN_NG_SHIFT))
    st, key, plan, off = pl.pallas_call(
        functools.partial(_route_kernel, cap),
        grid=(1,),
        in_specs=[full((n, LANES))],
        out_specs=[full((nt + 8, LANES)), full((n, LANES)), full((nt, LANES)), full((nt, LANES))],
        out_shape=[i32((nt + 8, LANES)), i32((n, LANES)), i32((nt, LANES)), i32((nt, LANES))],
        compiler_params=_cparams(("arbitrary",)),
        name="route_" + tag,
    )(aff)
    groups = cap // LANES + 1
    idx, gate = pl.pallas_call(
        functools.partial(_compact_kernel, cap),
        grid_spec=pltpu.PrefetchScalarGridSpec(
            num_scalar_prefetch=1, grid=(1,),
            in_specs=[full((n, LANES)), full((n, LANES))],
            out_specs=[full((N_EXPERTS, cap)), full((N_EXPERTS, cap))],
            scratch_shapes=[pltpu.VMEM((N_EXPERTS, groups, 8, LANES), jnp.int32),
                            pltpu.VMEM((N_EXPERTS, groups, 8, LANES), F32)]),
        out_shape=[i32((N_EXPERTS, cap)), jax.ShapeDtypeStruct((N_EXPERTS, cap), F32)],
        compiler_params=_cparams(("arbitrary",)),
        name="compact_" + tag,
    )(st.reshape(-1), key, aff)
    return idx, gate, key, plan, off


FF_CHUNK = 256


def _ffn_kernel(cap, nf, per, idx_s, hp_hbm, hs_hbm, gate_ref, w1_ref, w3_ref, w2_ref, yp_hbm, ys_hbm,
                xg, xe, acc, gsem, osem):
    e = pl.program_id(0)
    f = pl.program_id(1)
    ne = pl.num_programs(0)
    srcs = (hp_hbm, hs_hbm)
    dsts = (yp_hbm, ys_hbm)
    slot = lax.rem(e, 2)

    def issue_part(ex, part, sl, live):
        tail = cap - (nf - 1) * per
        live_tail = live & (part < nf - 1)
        for p in range(2):
            for i in range(per):
                j = part * per + i
                jc = j if i < tail else jnp.minimum(j, cap - 1)
                tok = idx_s[(p * N_EXPERTS + ex) * cap + jc]

                @pl.when(live if i < tail else live_tail)
                def _():
                    pltpu.make_async_copy(srcs[p].at[pl.ds(tok, 1), :], xg.at[sl, pl.ds(p * cap + jc, 1), :],
                                          gsem.at[sl]).start()

    def out_copy(p, ex):
        return pltpu.make_async_copy(acc.at[pl.ds(p * cap, cap), :],
                                     dsts[p].at[pl.ds(pl.multiple_of(ex * cap, cap), cap), :], osem)

    @pl.when((e == 0) & (f == 0))
    def _prologue():
        def part_body(part, carry):
            issue_part(0, part, 0, True)
            return carry
        lax.fori_loop(0, nf, part_body, 0)

    @pl.when(f == 0)
    def _start_expert():
        for p in range(2):
            pltpu.make_async_copy(srcs[p].at[pl.ds(0, cap), :], xg.at[slot, pl.ds(p * cap, cap), :],
                                  gsem.at[slot]).wait()
        xe[...] = xg[slot].astype(BF16)

    x = xe[...]
    a = jnp.dot(x, w1_ref[...].astype(BF16), preferred_element_type=F32)
    b = jnp.dot(x, w3_ref[...].astype(BF16), preferred_element_type=F32)
    hid = ((a * jax.nn.sigmoid(a)) * b).astype(BF16)

    issue_part(jnp.minimum(e + 1, ne - 1), f, 1 - slot, e + 1 < ne)

    def down():
        return jnp.dot(hid, w2_ref[...].astype(BF16), preferred_element_type=F32)

    @pl.when(f == 0)
    def _first():
        @pl.when(e > 0)
        def _():
            for p in range(2):
                out_copy(p, e - 1).wait()
        acc[...] = down()

    @pl.when((f > 0) & (f < nf - 1))
    def _middle():
        acc[...] += down()

    @pl.when(f == nf - 1)
    def _last():
        acc[...] = (acc[...] + down()) * gate_ref[...]
        for p in range(2):
            out_copy(p, e).start()

        @pl.when(e == ne - 1)
        def _():
            for p in range(2):
                out_copy(p, e).wait()


def _ffn(idx_flat, h2p, h2s, gate, w1, w3, w2, cap):
    nf = EXPERT_FF // FF_CHUNK
    per = -(-cap // nf)
    grid_spec = pltpu.PrefetchScalarGridSpec(
        num_scalar_prefetch=1,
        grid=(N_EXPERTS, nf),
        in_specs=[pl.BlockSpec(memory_space=pl.ANY), pl.BlockSpec(memory_space=pl.ANY),
                  pl.BlockSpec((None, 2 * cap, 1), lambda e, f, *_: (e, 0, 0)),
                  pl.BlockSpec((None, D_MODEL, FF_CHUNK), lambda e, f, *_: (e, 0, f)),
                  pl.BlockSpec((None, D_MODEL, FF_CHUNK), lambda e, f, *_: (e, 0, f)),
                  pl.BlockSpec((None, FF_CHUNK, D_MODEL), lambda e, f, *_: (e, f, 0))],
        out_specs=[pl.BlockSpec(memory_space=pl.ANY), pl.BlockSpec(memory_space=pl.ANY)],
        scratch_shapes=[pltpu.VMEM((2, 2 * cap, D_MODEL), F32), pltpu.VMEM((2 * cap, D_MODEL), BF16),
                        pltpu.VMEM((2 * cap, D_MODEL), F32), pltpu.SemaphoreType.DMA((2,)),
                        pltpu.SemaphoreType.DMA(())])
    out = jax.ShapeDtypeStruct((N_EXPERTS * cap, D_MODEL), F32)
    return pl.pallas_call(
        functools.partial(_ffn_kernel, cap, nf, per),
        grid_spec=grid_spec, out_shape=[out, out],
        compiler_params=_cparams(("arbitrary", "arbitrary")),
        name="expert_ffn",
    )(idx_flat, h2p, h2s, gate, w1, w3, w2)


WAIT_GROUPS = 8
STAGE_ROWS = -(-(N_EXPERTS * (COMBINE_T // GROUP + 1) * GROUP) // MXU_DIM) * MXU_DIM


def _combine_kernel(cap, plan_s, ye_hbm, x1_ref, g2_ref, key_ref, off_ref, o_ref, stage, acc, sem):
    t = pl.program_id(0)
    nt = pl.num_programs(0)
    tt = COMBINE_T
    slot = lax.rem(t, 2)

    def issue_tile(tile, sl):
        for e in range(N_EXPERTS):
            w = plan_s[tile * LANES + e]
            a = w & ((1 << PLAN_NG_SHIFT) - 1)
            ng = lax.shift_right_logical(w, PLAN_NG_SHIFT) & ((1 << (PLAN_GS_SHIFT - PLAN_NG_SHIFT)) - 1)
            gs = lax.shift_right_logical(w, PLAN_GS_SHIFT)

            def issue(g, carry):
                src = pl.multiple_of(e * cap + a + g * GROUP, GROUP)
                dst = pl.multiple_of((gs + g) * GROUP, GROUP)
                pltpu.make_async_copy(ye_hbm.at[pl.ds(src, GROUP), :], stage.at[sl, pl.ds(dst, GROUP), :],
                                      sem.at[sl]).start()
                return carry
            lax.fori_loop(0, ng, issue, 0)

    def wait_tile(tile, sl):
        total = plan_s[tile * LANES + N_EXPERTS]

        def wait_groups(n):
            def body(i, carry):
                pltpu.make_async_copy(ye_hbm.at[pl.ds(0, n * GROUP), :], stage.at[sl, pl.ds(0, n * GROUP), :],
                                      sem.at[sl]).wait()
                return carry
            return body
        lax.fori_loop(0, total // WAIT_GROUPS, wait_groups(WAIT_GROUPS), 0)
        lax.fori_loop(0, lax.rem(total, WAIT_GROUPS), wait_groups(1), 0)

    @pl.when(t == 0)
    def _init():
        stage[...] = jnp.zeros(stage.shape, F32)
        issue_tile(0, 0)

    @pl.when(t + 1 < nt)
    def _prefetch():
        issue_tile(t + 1, 1 - slot)

    wait_tile(t, slot)

    slot1 = lax.shift_right_logical(key_ref[...], KEY_SHIFT)
    srow = jnp.where(slot1 > 0, slot1 - 1 + off_ref[pl.ds(t, 1), :], -1)
    cio = lax.broadcasted_iota(jnp.int32, (tt, MXU_DIM), 1)
    acc[...] = jnp.zeros(acc.shape, F32)

    def add_chunk(kc, carry):
        base = pl.multiple_of(kc * MXU_DIM, MXU_DIM)
        pick = jnp.zeros((tt, MXU_DIM), F32)
        for e in range(N_EXPERTS):
            pick = pick + jnp.where(jnp.broadcast_to(srow[:, e:e + 1], (tt, MXU_DIM)) - base == cio, 1.0, 0.0)
        rows = stage[slot, pl.ds(base, MXU_DIM), :]
        acc[...] += jnp.dot(pick.astype(BF16), rows.astype(BF16), preferred_element_type=F32)
        return carry

    total_rows = plan_s[t * LANES + N_EXPERTS] * GROUP
    lax.fori_loop(0, lax.div(total_rows + MXU_DIM - 1, MXU_DIM), add_chunk, 0)
    o_ref[...] = x1_ref[...] + g2_ref[...] * acc[...]


def _combine(plan, off, key, ye, x1, mod, tiles_per_mod, cap, tag):
    n = x1.shape[0]
    tt = COMBINE_T
    nt = n // tt
    grid_spec = pltpu.PrefetchScalarGridSpec(
        num_scalar_prefetch=1,
        grid=(nt,),
        in_specs=[pl.BlockSpec(memory_space=pl.ANY),
                  pl.BlockSpec((tt, D_MODEL), lambda i, *_: (i, 0)),
                  pl.BlockSpec((None, 1, D_MODEL), lambda i, *_: (i // tiles_per_mod, 0, 5)),
                  pl.BlockSpec((tt, LANES), lambda i, *_: (i, 0)),
                  pl.BlockSpec((nt, LANES), lambda i, *_: (0, 0))],
        out_specs=pl.BlockSpec((tt, D_MODEL), lambda i, *_: (i, 0)),
        scratch_shapes=[pltpu.VMEM((2, STAGE_ROWS, D_MODEL), F32), pltpu.VMEM((tt, D_MODEL), F32),
                        pltpu.SemaphoreType.DMA((2,))])
    return pl.pallas_call(
        functools.partial(_combine_kernel, cap),
        grid_spec=grid_spec, out_shape=jax.ShapeDtypeStruct((n, D_MODEL), F32),
        compiler_params=_cparams(("arbitrary",)),
        name="combine_" + tag,
    )(plan.reshape(-1), ye, x1, mod, key, off)


def _prep_weights(norm1_gain, norm2_gain, w_in, a_q_gain, a_k_gain, a_lambda_q1, a_lambda_k1, a_lambda_q2,
                  a_lambda_k2, a_sub_gain, q_a_gain, w_uq, b_qn_gain, b_qr_gain, kv_a_gain, w_ukv, b_kn_gain,
                  b_kr_gain, w_o, w_router):
    tile = lambda g, width: jnp.tile(g.reshape(1, -1), (1, width // g.shape[-1]))
    wuq = w_uq[0].reshape(Q_RANK, B_HEADS, B_QK)
    wuq = jnp.concatenate([wuq[:, :, :B_NOPE].reshape(Q_RANK, -1), wuq[:, :, B_NOPE:].reshape(Q_RANK, -1)], axis=1)
    return {
        "norm1_gain": norm1_gain.reshape(1, -1), "norm2_gain": norm2_gain.reshape(1, -1),
        "w_in": w_in[0].astype(BF16),
        "a_q_gain": tile(a_q_gain, MXU_DIM), "a_k_gain": tile(a_k_gain, MXU_DIM),
        "a_lambda_q1": a_lambda_q1.reshape(1, -1), "a_lambda_k1": a_lambda_k1.reshape(1, -1),
        "a_lambda_q2": a_lambda_q2.reshape(1, -1), "a_lambda_k2": a_lambda_k2.reshape(1, -1),
        "a_sub_gain": a_sub_gain.reshape(1, -1), "q_a_gain": q_a_gain.reshape(1, -1),
        "w_uq": wuq.astype(BF16), "b_qn_gain": b_qn_gain.reshape(1, -1), "b_qr_gain": tile(b_qr_gain, MXU_DIM),
        "kv_a_gain": kv_a_gain.reshape(1, -1), "w_ukv": w_ukv[0].astype(BF16),
        "b_kn_gain": b_kn_gain.reshape(1, -1), "b_kr_gain": tile(b_kr_gain, LANES),
        "w_oa": w_o[0, :A_V].astype(BF16), "w_ob": w_o[0, A_V:].astype(BF16),
        "w_router": jnp.pad(w_router[0], ((0, 0), (0, LANES - N_EXPERTS))).astype(BF16),
        "seg64": _seg_matrix(A_DH),
    }


def kernel(x_prompt, x_sample, cache_diff_k, cache_diff_v, cache_mla_ckv, cache_mla_krope, c, c_ctx, w_ada, b_ada, norm1_gain, norm2_gain, w_in, a_q_gain, a_k_gain, a_lambda_q1, a_lambda_k1, a_lambda_q2, a_lambda_k2, a_sub_gain, q_a_gain, w_uq, b_qn_gain, b_qr_gain, kv_a_gain, w_ukv, b_kn_gain, b_kr_gain, w_o, w_router, w_exp1, w_exp3, w_exp2):
    nbp, lp, _ = x_prompt.shape
    nbs, ls, _ = x_sample.shape
    past = cache_diff_k.shape[2]
    w = _prep_weights(norm1_gain, norm2_gain, w_in, a_q_gain, a_k_gain, a_lambda_q1, a_lambda_k1, a_lambda_q2,
                      a_lambda_k2, a_sub_gain, q_a_gain, w_uq, b_qn_gain, b_qr_gain, kv_a_gain, w_ukv, b_kn_gain,
                      b_kr_gain, w_o, w_router)

    cvec = jnp.concatenate([c_ctx[None], c, jnp.zeros((8 - 1 - nbs, D_MODEL), F32)], axis=0)
    mod = _ada(cvec, w_ada[0], b_ada)
    mod_p = mod[0:1].reshape(1, 1, -1)
    mod_s = mod[1:1 + nbs].reshape(nbs, 1, -1)

    xp = x_prompt.reshape(nbp * lp, D_MODEL)
    xs = x_sample.reshape(nbs * ls, D_MODEL)
    tm = 256
    assert lp == tm, "the prompt projection writes one batch of new_diff_k per row tile"
    aq_p, ak_p, av_p, qm_p, km_p, bv_p, ak32, av32, ckv32, kr32 = _in_proj(xp, mod_p, False, nbp * lp // tm, w, None)
    aq_s, ak_s, av_s, qm_s, km_s, bv_s = _in_proj(xs, mod_s, True, ls // tm, w, _rope_tables(ls))
    km_c, bv_c = _cache_kv(cache_mla_ckv.reshape(nbs * past, KV_RANK), cache_mla_krope.reshape(nbs * past, B_ROPE), w)

    cache_a = (cache_diff_k.reshape(nbs * past, A_QK), cache_diff_v.reshape(nbs * past, A_V))

    oa_p = _diff_attn(aq_p, ak_p, av_p, None, nbp, lp, lp, A_HEADS, w)
    ob_p = _mla_attn(qm_p, km_p, bv_p, None, nbp, lp, lp, B_HEADS)
    oa_s = _diff_attn(aq_s, ak_s, av_s, cache_a, nbs, ls, 256, 4, w)
    ob_s = _mla_attn(qm_s, km_s, bv_s, (km_c, bv_c), nbs, ls, 256, 4)

    x1_p, h2_p, aff_p = _out_proj(oa_p, ob_p, xp, mod_p, nbp * lp // tm, w, "prompt")
    x1_s, h2_s, aff_s = _out_proj(oa_s, ob_s, xs, mod_s, ls // tm, w, "sample")

    idx_p, gate_p, key_p, plan_p, off_p = _route(aff_p, "prompt")
    idx_s, gate_s, key_s, plan_s, off_s = _route(aff_s, "sample")
    cap = idx_p.shape[1]
    idx_flat = jnp.stack([idx_p, idx_s]).reshape(-1)
    gate = jnp.concatenate([gate_p, gate_s], axis=1)[..., None]
    ye_p, ye_s = _ffn(idx_flat, h2_p, h2_s, gate, w_exp1[0], w_exp3[0], w_exp2[0], cap)

    y_p = _combine(plan_p, off_p, key_p, ye_p, x1_p, mod_p, nbp * lp // COMBINE_T, cap, "prompt")
    y_s = _combine(plan_s, off_s, key_s, ye_s, x1_s, mod_s, ls // COMBINE_T, cap, "sample")

    return (y_p.reshape(nbp, lp, D_MODEL), y_s.reshape(nbs, ls, D_MODEL),
            ak32, av32.reshape(nbp, 1, lp, A_HEADS, A_DV),
            ckv32.reshape(nbp, 1, lp, KV_RANK), kr32.reshape(nbp, 1, lp, B_ROPE))
```

```python
import functools
import math

import numpy as np
import jax
import jax.numpy as jnp
from jax import lax
from jax.experimental import pallas as pl
from jax.experimental.pallas import tpu as pltpu

F32 = jnp.float32
BF16 = jnp.bfloat16

D_MODEL = 2048
GRID_W = 64
A_HEADS = 8
A_DH = 64
A_DV = 128
B_HEADS = 8
B_NOPE = 128
B_ROPE = 64
B_DV = 128
Q_RANK = 512
KV_RANK = 256
N_EXPERTS = 16
EXPERT_FF = 1536
CAPACITY_FACTOR = 2
ROPE_BASE = 10000.0
EPS = 1e-6
LAM_INIT = 0.8 - 0.6 * math.exp(-0.3 * 0)
LOG2E = math.log2(math.e)

A_QK = A_HEADS * 2 * A_DH
A_V = A_HEADS * A_DV
B_QK = B_NOPE + B_ROPE
IN_COLS = 2 * A_QK + A_V + Q_RANK + KV_RANK + B_ROPE
LANES = 128
MXU_DIM = 256
VMEM_LIMIT = 56 * 1024 * 1024


def _cparams(sem):
    return pltpu.CompilerParams(dimension_semantics=sem, vmem_limit_bytes=VMEM_LIMIT)


def _resident(shape):
    nd = len(shape)
    return pl.BlockSpec(shape, lambda *_: (0,) * nd, pipeline_mode=pl.Buffered(1))


def _rms(x, gain):
    return x * lax.rsqrt(jnp.mean(x * x, axis=-1, keepdims=True) + EPS) * gain


def _seg_rms(y, segmat, gain):
    y2 = (y * y).astype(BF16)
    outs = []
    for c in range(y.shape[1] // MXU_DIM):
        sl = slice(c * MXU_DIM, (c + 1) * MXU_DIM)
        ms = jnp.dot(y2[:, sl], segmat, preferred_element_type=F32)
        outs.append(y[:, sl] * lax.rsqrt(ms + EPS) * gain)
    return jnp.concatenate(outs, axis=1)


def _rope(y, cos, sin):
    outs = []
    for c in range(y.shape[1] // LANES):
        yc = y[:, c * LANES:(c + 1) * LANES]
        lane = lax.broadcasted_iota(jnp.int32, yc.shape, 1)
        partner = jnp.where((lane & 16) == 0, pltpu.roll(yc, LANES - 16, 1), pltpu.roll(yc, 16, 1))
        outs.append(yc * cos + partner * sin)
    return outs[0] if len(outs) == 1 else jnp.concatenate(outs, axis=1)


def _rope_tables(n_tok):
    t = np.arange(n_tok)
    row, col = t // GRID_W, t % GRID_W
    nf = A_DH // 4
    inv = ROPE_BASE ** (-np.arange(nf, dtype=np.float64) * 2.0 / (A_DH // 2))
    lane = np.arange(A_DH)
    pos = np.where(lane[None, :] < A_DH // 2, row[:, None], col[:, None]).astype(np.float64)
    ang = pos * inv[lane % nf][None, :]
    sign = np.where((lane % (2 * nf)) < nf, -1.0, 1.0)[None, :]
    cos = np.tile(np.cos(ang), (1, LANES // A_DH)).astype(np.float32)
    sin = np.tile(np.sin(ang) * sign, (1, LANES // A_DH)).astype(np.float32)
    return jnp.asarray(cos), jnp.asarray(sin)


def _seg_matrix(width):
    i = np.arange(MXU_DIM)
    return jnp.asarray(((i[:, None] // width) == (i[None, :] // width)).astype(np.float32) / width, dtype=BF16)


def _ada_kernel(c_ref, w_ref, b_ref, o_ref):
    c = c_ref[...]
    s = c * jax.nn.sigmoid(c)
    o_ref[...] = jnp.dot(s.astype(BF16), w_ref[...].astype(BF16), preferred_element_type=F32) + b_ref[...]


def _ada(cvec, w_ada, b_ada):
    tn = 1024
    n = w_ada.shape[1]
    return pl.pallas_call(
        _ada_kernel,
        grid=(n // tn,),
        in_specs=[pl.BlockSpec((8, D_MODEL), lambda j: (0, 0)),
                  pl.BlockSpec((D_MODEL, tn), lambda j: (0, j)),
                  pl.BlockSpec((1, tn), lambda j: (0, j))],
        out_specs=pl.BlockSpec((8, tn), lambda j: (0, j)),
        out_shape=jax.ShapeDtypeStruct((8, n), F32),
        compiler_params=_cparams(("arbitrary",)),
        name="ada_mod",
    )(cvec, w_ada, b_ada)


def _expand_kv(ckv_n, kr, wukv_ref, kng, km_ref, bv_ref):
    kv = jnp.dot(ckv_n.astype(BF16), wukv_ref[...], preferred_element_type=F32)
    krb = kr[:, :B_ROPE].astype(BF16)
    for h in range(B_HEADS):
        base = h * (B_NOPE + B_DV)
        kn = _rms(kv[:, base:base + B_NOPE], kng)
        km_ref[h] = jnp.concatenate([kn.astype(BF16), krb], axis=1)
        bv_ref[h] = kv[:, base + B_NOPE:base + B_NOPE + B_DV].astype(BF16)


def _in_kernel(sample, *refs):
    (x_ref, sh_ref, sc_ref, n1g_ref, win_ref, aqg_ref, akg_ref, qag_ref, wuq_ref, qng_ref, qrg_ref,
     kvag_ref, wukv_ref, kng_ref, krg_ref, seg_ref) = refs[:16]
    refs = refs[16:]
    if sample:
        cos_ref, sin_ref = refs[:2]
        refs = refs[2:]
        cos, sin = cos_ref[...], sin_ref[...]
    aq_ref, ak_ref, av_ref, qm_ref, km_ref, bv_ref = refs[:6]
    refs = refs[6:]

    x = x_ref[...]
    h = _rms(x, n1g_ref[...]) * (1.0 + sc_ref[...]) + sh_ref[...]
    hb = h.astype(BF16)

    def proj(c0, c1):
        return jnp.dot(hb, win_ref[:, c0:c1], preferred_element_type=F32)

    seg = seg_ref[...]
    c0 = 2 * A_QK + A_V
    y_q = proj(c0, c0 + Q_RANK)
    y_ckv = proj(c0 + Q_RANK, c0 + Q_RANK + KV_RANK)
    kr64 = proj(c0 + Q_RANK + KV_RANK, c0 + Q_RANK + KV_RANK + B_ROPE)
    q_lat = _rms(y_q, qag_ref[...])
    qb = jnp.dot(q_lat.astype(BF16), wuq_ref[...], preferred_element_type=F32)
    ckv = _rms(y_ckv, kvag_ref[...])
    kr_raw = jnp.concatenate([kr64, jnp.zeros_like(kr64)], axis=1)
    kr = kr_raw * lax.rsqrt(jnp.sum(kr_raw * kr_raw, axis=-1, keepdims=True) * (1.0 / B_ROPE) + EPS) * krg_ref[...]
    if sample:
        kr = _rope(kr, cos, sin)
    _expand_kv(ckv, kr, wukv_ref, kng_ref[...], km_ref, bv_ref)
    y_aq = proj(0, A_QK)
    qr = _seg_rms(qb[:, B_HEADS * B_NOPE:], seg, qrg_ref[...])
    if sample:
        qr = _rope(qr, cos, sin)
    scale = B_QK ** -0.5 * LOG2E
    for hd in range(B_HEADS):
        qn = _rms(qb[:, hd * B_NOPE:(hd + 1) * B_NOPE], qng_ref[...])
        qm_ref[hd] = jnp.concatenate(
            [(qn * scale).astype(BF16), (qr[:, hd * B_ROPE:(hd + 1) * B_ROPE] * scale).astype(BF16)], axis=1)

    y_ak = proj(A_QK, 2 * A_QK)
    aq = _seg_rms(y_aq, seg, aqg_ref[...])
    if sample:
        aq = _rope(aq, cos, sin)
    aq_ref[...] = (aq * (A_DH ** -0.5 * LOG2E)).astype(BF16)
    av = proj(2 * A_QK, 2 * A_QK + A_V)
    ak = _seg_rms(y_ak, seg, akg_ref[...])
    if sample:
        ak = _rope(ak, cos, sin)
    ak_ref[...] = ak.astype(BF16)
    av_ref[...] = av.astype(BF16)
    if not sample:
        ak32_ref, av32_ref, ckv32_ref, kr32_ref = refs
        ak32_ref[...] = ak.reshape(ak32_ref.shape)
        av32_ref[...] = av
        ckv32_ref[...] = ckv
        kr32_ref[...] = kr[:, :B_ROPE]


def _in_proj(x, mod, sample, tiles_per_mod, w, tables):
    m = x.shape[0]
    tm = 256
    nt = m // tm
    row = lambda i: (i, 0)
    modspec = lambda k: pl.BlockSpec((None, 1, D_MODEL), lambda i: (i // tiles_per_mod, 0, k))
    in_specs = [pl.BlockSpec((tm, D_MODEL), row), modspec(0), modspec(1), _resident((1, D_MODEL)),
                _resident((D_MODEL, IN_COLS)), _resident((1, MXU_DIM)), _resident((1, MXU_DIM)),
                _resident((1, Q_RANK)), _resident((Q_RANK, B_HEADS * B_QK)), _resident((1, B_NOPE)),
                _resident((1, MXU_DIM)), _resident((1, KV_RANK)), _resident((KV_RANK, B_HEADS * (B_NOPE + B_DV))),
                _resident((1, B_NOPE)), _resident((1, LANES)), _resident((MXU_DIM, MXU_DIM))]
    args = [x, mod, mod, w["norm1_gain"], w["w_in"], w["a_q_gain"], w["a_k_gain"], w["q_a_gain"], w["w_uq"],
            w["b_qn_gain"], w["b_qr_gain"], w["kv_a_gain"], w["w_ukv"], w["b_kn_gain"], w["b_kr_gain"], w["seg64"]]
    if sample:
        per = tables[0].shape[0] // tm
        in_specs += [pl.BlockSpec((tm, LANES), lambda i: (i % per, 0))] * 2
        args += list(tables)
    out_shape = [jax.ShapeDtypeStruct((m, A_QK), BF16), jax.ShapeDtypeStruct((m, A_QK), BF16),
                 jax.ShapeDtypeStruct((m, A_V), BF16), jax.ShapeDtypeStruct((B_HEADS, m, B_QK), BF16),
                 jax.ShapeDtypeStruct((B_HEADS, m, B_QK), BF16), jax.ShapeDtypeStruct((B_HEADS, m, B_DV), BF16)]
    hspec = lambda d: pl.BlockSpec((B_HEADS, tm, d), lambda i: (0, i, 0))
    out_specs = [pl.BlockSpec((tm, A_QK), row), pl.BlockSpec((tm, A_QK), row), pl.BlockSpec((tm, A_V), row),
                 hspec(B_QK), hspec(B_QK), hspec(B_DV)]
    if not sample:
        out_shape += [jax.ShapeDtypeStruct((nt, 1, tm, A_HEADS, 2, A_DH), F32), jax.ShapeDtypeStruct((m, A_V), F32),
                      jax.ShapeDtypeStruct((m, KV_RANK), F32), jax.ShapeDtypeStruct((m, B_ROPE), F32)]
        out_specs += [pl.BlockSpec((None, None, tm, A_HEADS, 2, A_DH), lambda i: (i, 0, 0, 0, 0, 0)),
                      pl.BlockSpec((tm, A_V), row),
                      pl.BlockSpec((tm, KV_RANK), row), pl.BlockSpec((tm, B_ROPE), row)]
    return pl.pallas_call(
        functools.partial(_in_kernel, sample),
        grid=(nt,), in_specs=in_specs, out_specs=out_specs, out_shape=out_shape,
        compiler_params=_cparams(("arbitrary",)),
        name="in_proj_sample" if sample else "in_proj_prompt",
    )(*args)


def _cache_kv_kernel(ckv_ref, kr_ref, wukv_ref, kng_ref, km_ref, bv_ref):
    _expand_kv(ckv_ref[...], kr_ref[...], wukv_ref, kng_ref[...], km_ref, bv_ref)


def _cache_kv(ckv, kr, w):
    m = ckv.shape[0]
    return pl.pallas_call(
        _cache_kv_kernel,
        grid=(1,),
        in_specs=[_resident((m, KV_RANK)), _resident((m, B_ROPE)),
                  _resident((KV_RANK, B_HEADS * (B_NOPE + B_DV))), _resident((1, B_NOPE))],
        out_specs=[pl.BlockSpec((B_HEADS, m, B_QK), lambda i: (0, 0, 0)),
                   pl.BlockSpec((B_HEADS, m, B_DV), lambda i: (0, 0, 0))],
        out_shape=[jax.ShapeDtypeStruct((B_HEADS, m, B_QK), BF16), jax.ShapeDtypeStruct((B_HEADS, m, B_DV), BF16)],
        compiler_params=_cparams(("arbitrary",)),
        name="cache_kv",
    )(ckv, kr, w["w_ukv"], w["b_kn_gain"])


def _qk(q, k):
    return lax.dot_general(q, k, (((1,), (1,)), ((), ())), preferred_element_type=F32)


KEY_CHUNK = 512
SCORE_AHEAD = 1


def _scores(q, k, kc):
    s = _qk(q, k)
    return s if kc is None else jnp.concatenate([s, _qk(q, kc)], axis=1)


def _weighted(p, v, vc):
    if vc is None:
        return jnp.dot(p, v, preferred_element_type=F32)
    lk = v.shape[0]
    return (jnp.dot(p[:, :lk], v, preferred_element_type=F32)
            + jnp.dot(p[:, lk:], vc, preferred_element_type=F32))


def _diff_attn_kernel(hb, cached, q_ref, k_ref, v_ref, *refs):
    if cached:
        kc_ref, vc_ref = refs[:2]
        refs = refs[2:]
    lq1, lk1, lq2, lk2, sub_ref, o_ref = refs
    lam = (jnp.exp(jnp.sum(lq1[...] * lk1[...], axis=-1, keepdims=True))
           - jnp.exp(jnp.sum(lq2[...] * lk2[...], axis=-1, keepdims=True)) + LAM_INIT)
    def scores(h, comp):
        sl = slice(h * LANES, (h + 1) * LANES)
        q, k = q_ref[:, sl], k_ref[:, sl]
        kc = kc_ref[:, sl].astype(BF16) if cached else None
        lane = lax.broadcasted_iota(jnp.int32, q.shape, 1)
        keep = (lane < A_DH) if comp == 0 else (lane >= A_DH)
        return _scores(jnp.where(keep, q, jnp.zeros_like(q)), k, kc)

    s1_next, s2_next = scores(0, 0), scores(0, 1)
    for h in range(hb):
        sl = slice(h * LANES, (h + 1) * LANES)
        s1, s2 = s1_next, s2_next
        if h + 1 < hb:
            s1_next = scores(h + 1, 0)
        e1 = jnp.exp2(s1 - jnp.max(s1, axis=-1, keepdims=True))
        if h + 1 < hb:
            s2_next = scores(h + 1, 1)
        e2 = jnp.exp2(s2 - jnp.max(s2, axis=-1, keepdims=True))
        if cached:
            ones = lambda x: jnp.concatenate([x, jnp.ones_like(x)], axis=1)
            v, vc = ones(v_ref[:, sl]), ones(vc_ref[:, sl].astype(BF16))
            n1 = _weighted(e1.astype(BF16), v, vc)
            n2 = _weighted(e2.astype(BF16), v, vc)
            o = n1[:, :LANES] * (1.0 / n1[:, LANES:LANES + 1]) - n2[:, :LANES] * (lam / n2[:, LANES:LANES + 1])
        else:
            l1 = jnp.sum(e1, axis=-1, keepdims=True)
            l2 = jnp.sum(e2, axis=-1, keepdims=True)
            p = e1 - e2 * (lam * l1 / l2)
            o = jnp.dot(p.astype(BF16), v_ref[:, sl], preferred_element_type=F32) * (1.0 / l1)
        o_ref[:, sl] = (_rms(o, sub_ref[...]) * (1.0 - LAM_INIT)).astype(BF16)


def _diff_attn(q, k, v, cache, nb, lq, tq, hb, w):
    nq = lq // tq
    vec = _resident((1, A_DH))
    kv = lambda rows: pl.BlockSpec((rows, hb * LANES), lambda b, g, i: (b, g))
    in_specs = [pl.BlockSpec((tq, hb * LANES), lambda b, g, i: (b * nq + i, g)), kv(lq), kv(lq)]
    args = [q, k, v]
    if cache is not None:
        past = cache[0].shape[0] // nb
        in_specs += [kv(past), kv(past)]
        args += list(cache)
    return pl.pallas_call(
        functools.partial(_diff_attn_kernel, hb, cache is not None),
        grid=(nb, A_HEADS // hb, nq),
        in_specs=in_specs + [vec, vec, vec, vec, _resident((1, A_DV))],
        out_specs=pl.BlockSpec((tq, hb * LANES), lambda b, g, i: (b * nq + i, g)),
        out_shape=jax.ShapeDtypeStruct((nb * lq, A_V), BF16),
        compiler_params=_cparams(("arbitrary", "arbitrary", "arbitrary")),
        name="diff_attn_cached" if cache is not None else "diff_attn",
    )(*args, w["a_lambda_q1"], w["a_lambda_k1"], w["a_lambda_q2"], w["a_lambda_k2"], w["a_sub_gain"])


def _mla_attn_kernel(hb, cached, q_ref, k_ref, v_ref, *refs):
    if cached:
        kc_ref, vc_ref, o_ref = refs
    else:
        (o_ref,) = refs
    lk = k_ref.shape[1]
    ck = min(lk, KEY_CHUNK)
    def head_scores(h):
        q = q_ref[h]
        s = [_qk(q, k_ref[h, c * ck:(c + 1) * ck]) for c in range(lk // ck)]
        return s + [_qk(q, kc_ref[h])] if cached else s

    ahead = [head_scores(i) for i in range(min(SCORE_AHEAD, hb))]
    for h in range(hb):
        vs = [v_ref[h, c * ck:(c + 1) * ck] for c in range(lk // ck)] + ([vc_ref[h]] if cached else [])
        s = ahead.pop(0)
        if h + SCORE_AHEAD < hb:
            ahead.append(head_scores(h + SCORE_AHEAD))
        m = functools.reduce(jnp.maximum, [jnp.max(sc, axis=-1, keepdims=True) for sc in s])
        l = 0.0
        o = 0.0
        for sc, v in zip(s, vs):
            e = jnp.exp2(sc - m)
            if cached:
                v = jnp.concatenate([v, jnp.ones_like(v)], axis=1)
            else:
                l = l + jnp.sum(e, axis=-1, keepdims=True)
            o = o + jnp.dot(e.astype(BF16), v, preferred_element_type=F32)
        if cached:
            o, l = o[:, :LANES], o[:, LANES:LANES + 1]
        o_ref[:, h * LANES:(h + 1) * LANES] = (o * (1.0 / l)).astype(BF16)


def _mla_attn(q, k, v, cache, nb, lq, tq, hb):
    nq = lq // tq
    kv = lambda rows, d: pl.BlockSpec((hb, rows, d), lambda b, g, i: (g, b, 0))
    in_specs = [pl.BlockSpec((hb, tq, B_QK), lambda b, g, i: (g, b * nq + i, 0)), kv(lq, B_QK), kv(lq, B_DV)]
    args = [q, k, v]
    if cache is not None:
        past = cache[0].shape[1] // nb
        in_specs += [kv(past, B_QK), kv(past, B_DV)]
        args += list(cache)
    return pl.pallas_call(
        functools.partial(_mla_attn_kernel, hb, cache is not None),
        grid=(nb, B_HEADS // hb, nq),
        in_specs=in_specs,
        out_specs=pl.BlockSpec((tq, hb * LANES), lambda b, g, i: (b * nq + i, g)),
        out_shape=jax.ShapeDtypeStruct((nb * lq, B_HEADS * B_DV), BF16),
        compiler_params=_cparams(("arbitrary", "arbitrary", "arbitrary")),
        name="mla_attn_cached" if cache is not None else "mla_attn",
    )(*args)


def _out_kernel(oa_ref, ob_ref, x_ref, g1_ref, sh2_ref, sc2_ref, n2g_ref, wo_ref, wr_ref,
                x1_ref, h2_ref, aff_ref):
    sub = OUT_SUB
    rows = [pl.ds(r * sub, sub) for r in range(x_ref.shape[0] // sub)]
    os_ = [jnp.dot(oa_ref[r, :], wo_ref[:A_V, :], preferred_element_type=F32)
           + jnp.dot(ob_ref[r, :], wo_ref[A_V:, :], preferred_element_type=F32) for r in rows]
    for r, o in zip(rows, os_):
        x1 = x_ref[r, :] + g1_ref[...] * o
        x1_ref[r, :] = x1
        h2 = _rms(x1, n2g_ref[...]) * (1.0 + sc2_ref[...]) + sh2_ref[...]
        h2_ref[r, :] = h2
        logits = jnp.dot(h2.astype(BF16), wr_ref[...], preferred_element_type=F32)
        lane = lax.broadcasted_iota(jnp.int32, logits.shape, 1)
        logits = jnp.where(lane < N_EXPERTS, logits, -jnp.inf)
        e = jnp.exp(logits - jnp.max(logits, axis=-1, keepdims=True))
        aff_ref[r, :] = e / jnp.sum(e, axis=-1, keepdims=True)


OUT_SUB = 256


def _out_proj(oa, ob, x, mod, tiles_per_mod, w, tag):
    m = x.shape[0]
    tm = 512
    tiles_per_mod = tiles_per_mod * 256 // tm
    row = lambda i: (i, 0)
    modspec = lambda k: pl.BlockSpec((None, 1, D_MODEL), lambda i: (i // tiles_per_mod, 0, k))
    return pl.pallas_call(
        _out_kernel,
        grid=(m // tm,),
        in_specs=[pl.BlockSpec((tm, A_V), row), pl.BlockSpec((tm, A_V), row), pl.BlockSpec((tm, D_MODEL), row),
                  modspec(2), modspec(3), modspec(4), _resident((1, D_MODEL)),
                  _resident((2 * A_V, D_MODEL)), _resident((D_MODEL, LANES))],
        out_specs=[pl.BlockSpec((tm, D_MODEL), row), pl.BlockSpec((tm, D_MODEL), row),
                   pl.BlockSpec((tm, LANES), row)],
        out_shape=[jax.ShapeDtypeStruct((m, D_MODEL), F32), jax.ShapeDtypeStruct((m, D_MODEL), F32),
                   jax.ShapeDtypeStruct((m, LANES), F32)],
        compiler_params=_cparams(("arbitrary",)),
        name="out_proj_" + tag,
    )(oa, ob, x, mod, mod, mod, w["norm2_gain"], w["w_o"], w["w_router"])


ROUTE_BLK = 256
COMBINE_T = 128
KEY_SHIFT = 16
GROUP_BITS = 3
GROUP = 1 << GROUP_BITS
PLAN_NG_SHIFT = 10
PLAN_GS_SHIFT = 15


def _route_kernel(cap, aff_ref, st_ref, key_ref, plan_ref, off_ref):
    n = aff_ref.shape[0]
    nblk = n // ROUTE_BLK
    aff = aff_ref[...]
    lane = lax.broadcasted_iota(jnp.int32, (n, LANES), 1)
    tok = lax.broadcasted_iota(jnp.int32, (n, LANES), 0)

    def narrow(carry):
        lo, hi = carry
        mid = lo + (hi - lo) * 0.5
        mid = jnp.where(mid < hi, mid, lo)
        above = aff > mid
        few = jnp.sum(above.astype(jnp.int32), axis=0, keepdims=True) < cap
        up = jnp.min(jnp.where(above, aff, jnp.inf), axis=0, keepdims=True)
        dn = jnp.max(jnp.where(above, -jnp.inf, aff), axis=0, keepdims=True)
        return jnp.where(few, lo, up), jnp.where(few, dn, hi)

    bounds = (jnp.min(aff, axis=0, keepdims=True), jnp.max(aff, axis=0, keepdims=True))
    thr, _ = lax.while_loop(lambda c: jnp.max((c[0] < c[1]).astype(jnp.int32)) > 0, narrow, bounds)
    gt = aff > thr
    eq = aff == thr
    need = cap - jnp.sum(gt.astype(jnp.int32), axis=0, keepdims=True)

    r = lax.broadcasted_iota(jnp.int32, (ROUTE_BLK, ROUTE_BLK), 0)
    c = lax.broadcasted_iota(jnp.int32, (ROUTE_BLK, ROUTE_BLK), 1)
    tri = (c < r).astype(BF16)

    def prefix(mask):
        m = mask.astype(F32)
        carry = jnp.zeros((1, LANES), F32)
        outs = []
        for b in range(nblk):
            mb = m[b * ROUTE_BLK:(b + 1) * ROUTE_BLK]
            outs.append(jnp.dot(tri, mb.astype(BF16), preferred_element_type=F32) + carry)
            carry = carry + jnp.sum(mb, axis=0, keepdims=True)
        return jnp.concatenate(outs, axis=0).astype(jnp.int32)

    sel = (gt | (eq & (prefix(eq) < need))) & (lane < N_EXPERTS)
    pos = prefix(sel)
    nt = n // COMBINE_T
    key_ref[...] = pos
    st_ref[...] = jnp.zeros(st_ref.shape, jnp.int32)
    st_ref[0:nt, :] = key_ref[pl.ds(0, nt, stride=COMBINE_T), :]
    st_ref[nt:nt + 1, :] = jnp.full((1, LANES), cap, jnp.int32)
    key_ref[...] = (jnp.where(sel, pos + 1, 0) << KEY_SHIFT) | tok

    lo = st_ref[0:nt, :]
    hi = st_ref[pl.ds(1, nt), :]
    a = lo - (lo & (GROUP - 1))
    ng = jnp.where(hi > lo, lax.shift_right_logical(hi - a + (GROUP - 1), GROUP_BITS), 0)
    er = lax.broadcasted_iota(jnp.int32, (LANES, LANES), 0)
    ec = lax.broadcasted_iota(jnp.int32, (LANES, LANES), 1)
    gs = jnp.dot(ng.astype(BF16), (er < ec).astype(BF16), preferred_element_type=F32).astype(jnp.int32)
    elane = lax.broadcasted_iota(jnp.int32, (nt, LANES), 1)
    plan_ref[...] = jnp.where(elane == N_EXPERTS, gs, a | (ng << PLAN_NG_SHIFT) | (gs << PLAN_GS_SHIFT))
    off_ref[...] = gs * GROUP - a


def _compact_kernel(cap, st_s, key_ref, aff_ref, idx_ref, gate_ref, accp, accg):
    tt = COMBINE_T
    nt = key_ref.shape[0] // tt
    accp[...] = jnp.zeros(accp.shape, jnp.int32)
    accg[...] = jnp.zeros(accg.shape, F32)
    jio = lax.broadcasted_iota(jnp.int32, (tt, LANES), 1)

    def tile_body(t, carry):
        rows = pl.ds(pl.multiple_of(t * tt, tt), tt)
        key, af = key_ref[rows, :], aff_ref[rows, :]
        for e in range(N_EXPERTS):
            g0 = jnp.minimum(lax.shift_right_logical(st_s[t * LANES + e], 7), cap // LANES - 1)
            kb = jnp.broadcast_to(key[:, e:e + 1], (tt, LANES))
            ab = jnp.broadcast_to(af[:, e:e + 1], (tt, LANES))
            rel = (kb >> KEY_SHIFT) - 1 - g0 * LANES
            pk = kb & ((1 << KEY_SHIFT) - 1)
            for d in range(2):
                oh = rel == jio + d * LANES
                accp[e, g0 + d] += jnp.sum(jnp.where(oh, pk, 0).reshape(tt // 8, 8, LANES), axis=0)
                accg[e, g0 + d] += jnp.sum(jnp.where(oh, ab, 0.0).reshape(tt // 8, 8, LANES), axis=0)
        return carry

    lax.fori_loop(0, nt, tile_body, 0)
    for e in range(N_EXPERTS):
        idx_ref[e:e + 1, :] = jnp.concatenate(
            [jnp.sum(accp[e, g], axis=0, keepdims=True) for g in range(cap // LANES)], axis=1)
        gate_ref[e:e + 1, :] = jnp.concatenate(
            [jnp.sum(accg[e, g], axis=0, keepdims=True) for g in range(cap // LANES)], axis=1)


def _route(aff, tag):
    n = aff.shape[0]
    cap = CAPACITY_FACTOR * n // N_EXPERTS
    nt = n // COMBINE_T
    full = lambda s: pl.BlockSpec(s, lambda i, *_: (0,) * len(s))
    i32 = lambda s: jax.ShapeDtypeStruct(s, jnp.int32)
    assert n <= (1 << KEY_SHIFT) and cap + 1 < (1 << (31 - KEY_SHIFT))
    assert cap < (1 << PLAN_NG_SHIFT) and COMBINE_T // GROUP + 1 < (1 << (PLAN_GS_SHIFT - PLAN_NG_SHIFT))
    st, key, plan, off = pl.pallas_call(
        functools.partial(_route_kernel, cap),
        grid=(1,),
        in_specs=[full((n, LANES))],
        out_specs=[full((nt + 8, LANES)), full((n, LANES)), full((nt, LANES)), full((nt, LANES))],
        out_shape=[i32((nt + 8, LANES)), i32((n, LANES)), i32((nt, LANES)), i32((nt, LANES))],
        compiler_params=_cparams(("arbitrary",)),
        name="route_" + tag,
    )(aff)
    groups = cap // LANES + 1
    idx, gate = pl.pallas_call(
        functools.partial(_compact_kernel, cap),
        grid_spec=pltpu.PrefetchScalarGridSpec(
            num_scalar_prefetch=1, grid=(1,),
            in_specs=[full((n, LANES)), full((n, LANES))],
            out_specs=[full((N_EXPERTS, cap)), full((N_EXPERTS, cap))],
            scratch_shapes=[pltpu.VMEM((N_EXPERTS, groups, 8, LANES), jnp.int32),
                            pltpu.VMEM((N_EXPERTS, groups, 8, LANES), F32)]),
        out_shape=[i32((N_EXPERTS, cap)), jax.ShapeDtypeStruct((N_EXPERTS, cap), F32)],
        compiler_params=_cparams(("arbitrary",)),
        name="compact_" + tag,
    )(st.reshape(-1), key, aff)
    return idx, gate, key, plan, off


FF_CHUNK = 256


def _ffn_kernel(cap, nf, per, idx_s, hp_hbm, hs_hbm, gate_ref, w1_ref, w3_ref, w2_ref, yp_hbm, ys_hbm,
                xg, xe, acc, gsem, osem):
    e = pl.program_id(0)
    f = pl.program_id(1)
    ne = pl.num_programs(0)
    srcs = (hp_hbm, hs_hbm)
    dsts = (yp_hbm, ys_hbm)
    slot = lax.rem(e, 2)

    def issue_part(ex, part, sl, live):
        tail = cap - (nf - 1) * per
        live_tail = live & (part < nf - 1)
        j0 = pl.multiple_of(part * per, GROUP) if per % GROUP == 0 else part * per
        for p in range(2):
            for i in range(per):
                j = j0 + i
                jc = j if i < tail else jnp.minimum(j, cap - 1)
                tok = idx_s[(p * N_EXPERTS + ex) * cap + jc]

                @pl.when(live if i < tail else live_tail)
                def _():
                    pltpu.make_async_copy(srcs[p].at[pl.ds(tok, 1), :], xg.at[sl, pl.ds(p * cap + jc, 1), :],
                                          gsem.at[sl]).start()

    def out_copy(p, ex):
        return pltpu.make_async_copy(acc.at[pl.ds(p * cap, cap), :],
                                     dsts[p].at[pl.ds(pl.multiple_of(ex * cap, cap), cap), :], osem)

    @pl.when((e == 0) & (f == 0))
    def _prologue():
        def part_body(part, carry):
            issue_part(0, part, 0, True)
            return carry
        lax.fori_loop(0, nf, part_body, 0)

    @pl.when(f == 0)
    def _start_expert():
        for p in range(2):
            pltpu.make_async_copy(srcs[p].at[pl.ds(0, cap), :], xg.at[slot, pl.ds(p * cap, cap), :],
                                  gsem.at[slot]).wait()
        xe[...] = xg[slot].astype(BF16)

    x = xe[...]
    a = jnp.dot(x, w1_ref[...].astype(BF16), preferred_element_type=F32)
    b = jnp.dot(x, w3_ref[...].astype(BF16), preferred_element_type=F32)
    hid = ((a * jax.nn.sigmoid(a)) * b).astype(BF16)

    issue_part(jnp.minimum(e + 1, ne - 1), f, 1 - slot, e + 1 < ne)

    def down():
        return jnp.dot(hid, w2_ref[...].astype(BF16), preferred_element_type=F32)

    @pl.when(f == 0)
    def _first():
        @pl.when(e > 0)
        def _():
            for p in range(2):
                out_copy(p, e - 1).wait()
        acc[...] = down()

    @pl.when((f > 0) & (f < nf - 1))
    def _middle():
        acc[...] += down()

    @pl.when(f == nf - 1)
    def _last():
        acc[...] = (acc[...] + down()) * gate_ref[...]
        for p in range(2):
            out_copy(p, e).start()

        @pl.when(e == ne - 1)
        def _():
            for p in range(2):
                out_copy(p, e).wait()


def _ffn(idx_flat, h2p, h2s, gate, w1, w3, w2, cap):
    nf = EXPERT_FF // FF_CHUNK
    per = -(-cap // nf)
    grid_spec = pltpu.PrefetchScalarGridSpec(
        num_scalar_prefetch=1,
        grid=(N_EXPERTS, nf),
        in_specs=[pl.BlockSpec(memory_space=pl.ANY), pl.BlockSpec(memory_space=pl.ANY),
                  pl.BlockSpec((None, 2 * cap, 1), lambda e, f, *_: (e, 0, 0)),
                  pl.BlockSpec((None, D_MODEL, FF_CHUNK), lambda e, f, *_: (e, 0, f)),
                  pl.BlockSpec((None, D_MODEL, FF_CHUNK), lambda e, f, *_: (e, 0, f)),
                  pl.BlockSpec((None, FF_CHUNK, D_MODEL), lambda e, f, *_: (e, f, 0))],
        out_specs=[pl.BlockSpec(memory_space=pl.ANY), pl.BlockSpec(memory_space=pl.ANY)],
        scratch_shapes=[pltpu.VMEM((2, 2 * cap, D_MODEL), F32), pltpu.VMEM((2 * cap, D_MODEL), BF16),
                        pltpu.VMEM((2 * cap, D_MODEL), F32), pltpu.SemaphoreType.DMA((2,)),
                        pltpu.SemaphoreType.DMA(())])
    out = jax.ShapeDtypeStruct((N_EXPERTS * cap, D_MODEL), F32)
    return pl.pallas_call(
        functools.partial(_ffn_kernel, cap, nf, per),
        grid_spec=grid_spec, out_shape=[out, out],
        compiler_params=_cparams(("arbitrary", "arbitrary")),
        name="expert_ffn",
    )(idx_flat, h2p, h2s, gate, w1, w3, w2)


WAIT_GROUPS = 8
STAGE_ROWS = -(-(N_EXPERTS * (COMBINE_T // GROUP + 1) * GROUP) // MXU_DIM) * MXU_DIM


def _combine_kernel(cap, plan_s, ye_hbm, x1_ref, g2_ref, key_ref, off_ref, o_ref, stage, acc, sem):
    t = pl.program_id(0)
    nt = pl.num_programs(0)
    tt = COMBINE_T
    slot = lax.rem(t, 2)

    def issue_tile(tile, sl):
        for e in range(N_EXPERTS):
            w = plan_s[tile * LANES + e]
            a = w & ((1 << PLAN_NG_SHIFT) - 1)
            ng = lax.shift_right_logical(w, PLAN_NG_SHIFT) & ((1 << (PLAN_GS_SHIFT - PLAN_NG_SHIFT)) - 1)
            gs = lax.shift_right_logical(w, PLAN_GS_SHIFT)

            def issue(g, carry):
                src = pl.multiple_of(e * cap + a + g * GROUP, GROUP)
                dst = pl.multiple_of((gs + g) * GROUP, GROUP)
                pltpu.make_async_copy(ye_hbm.at[pl.ds(src, GROUP), :], stage.at[sl, pl.ds(dst, GROUP), :],
                                      sem.at[sl]).start()
                return carry
            lax.fori_loop(0, ng, issue, 0)

    def wait_tile(tile, sl):
        total = plan_s[tile * LANES + N_EXPERTS]

        def wait_groups(n):
            def body(i, carry):
                pltpu.make_async_copy(ye_hbm.at[pl.ds(0, n * GROUP), :], stage.at[sl, pl.ds(0, n * GROUP), :],
                                      sem.at[sl]).wait()
                return carry
            return body
        lax.fori_loop(0, total // WAIT_GROUPS, wait_groups(WAIT_GROUPS), 0)
        lax.fori_loop(0, lax.rem(total, WAIT_GROUPS), wait_groups(1), 0)

    @pl.when(t == 0)
    def _init():
        stage[...] = jnp.zeros(stage.shape, F32)
        issue_tile(0, 0)

    @pl.when(t + 1 < nt)
    def _prefetch():
        issue_tile(t + 1, 1 - slot)

    wait_tile(t, slot)

    slot1 = lax.shift_right_logical(key_ref[...], KEY_SHIFT)
    srow = jnp.where(slot1 > 0, slot1 - 1 + off_ref[pl.ds(t, 1), :], -1)
    cio = lax.broadcasted_iota(jnp.int32, (tt, MXU_DIM), 1)
    acc[...] = jnp.zeros(acc.shape, F32)

    def add_chunk(kc, carry):
        base = pl.multiple_of(kc * MXU_DIM, MXU_DIM)
        rel = srow - base
        pick = jnp.zeros((tt, MXU_DIM), F32)
        for e in range(N_EXPERTS):
            pick = jnp.where(jnp.broadcast_to(rel[:, e:e + 1], (tt, MXU_DIM)) == cio, 1.0, pick)
        rows = stage[slot, pl.ds(base, MXU_DIM), :]
        acc[...] += jnp.dot(pick.astype(BF16), rows.astype(BF16), preferred_element_type=F32)
        return carry

    total_rows = plan_s[t * LANES + N_EXPERTS] * GROUP
    lax.fori_loop(0, lax.div(total_rows + MXU_DIM - 1, MXU_DIM), add_chunk, 0)
    o_ref[...] = x1_ref[...] + g2_ref[...] * acc[...]


def _combine(plan, off, key, ye, x1, mod, tiles_per_mod, cap, tag):
    n = x1.shape[0]
    tt = COMBINE_T
    nt = n // tt
    grid_spec = pltpu.PrefetchScalarGridSpec(
        num_scalar_prefetch=1,
        grid=(nt,),
        in_specs=[pl.BlockSpec(memory_space=pl.ANY),
                  pl.BlockSpec((tt, D_MODEL), lambda i, *_: (i, 0)),
                  pl.BlockSpec((None, 1, D_MODEL), lambda i, *_: (i // tiles_per_mod, 0, 5)),
                  pl.BlockSpec((tt, LANES), lambda i, *_: (i, 0)),
                  pl.BlockSpec((nt, LANES), lambda i, *_: (0, 0))],
        out_specs=pl.BlockSpec((tt, D_MODEL), lambda i, *_: (i, 0)),
        scratch_shapes=[pltpu.VMEM((2, STAGE_ROWS, D_MODEL), F32), pltpu.VMEM((tt, D_MODEL), F32),
                        pltpu.SemaphoreType.DMA((2,))])
    return pl.pallas_call(
        functools.partial(_combine_kernel, cap),
        grid_spec=grid_spec, out_shape=jax.ShapeDtypeStruct((n, D_MODEL), F32),
        compiler_params=_cparams(("arbitrary",)),
        name="combine_" + tag,
    )(plan.reshape(-1), ye, x1, mod, key, off)


def _prep_weights(norm1_gain, norm2_gain, w_in, a_q_gain, a_k_gain, a_lambda_q1, a_lambda_k1, a_lambda_q2,
                  a_lambda_k2, a_sub_gain, q_a_gain, w_uq, b_qn_gain, b_qr_gain, kv_a_gain, w_ukv, b_kn_gain,
                  b_kr_gain, w_o, w_router):
    tile = lambda g, width: jnp.tile(g.reshape(1, -1), (1, width // g.shape[-1]))
    wuq = w_uq[0].reshape(Q_RANK, B_HEADS, B_QK)
    wuq = jnp.concatenate([wuq[:, :, :B_NOPE].reshape(Q_RANK, -1), wuq[:, :, B_NOPE:].reshape(Q_RANK, -1)], axis=1)
    return {
        "norm1_gain": norm1_gain.reshape(1, -1), "norm2_gain": norm2_gain.reshape(1, -1),
        "w_in": w_in[0].astype(BF16),
        "a_q_gain": tile(a_q_gain, MXU_DIM), "a_k_gain": tile(a_k_gain, MXU_DIM),
        "a_lambda_q1": a_lambda_q1.reshape(1, -1), "a_lambda_k1": a_lambda_k1.reshape(1, -1),
        "a_lambda_q2": a_lambda_q2.reshape(1, -1), "a_lambda_k2": a_lambda_k2.reshape(1, -1),
        "a_sub_gain": a_sub_gain.reshape(1, -1), "q_a_gain": q_a_gain.reshape(1, -1),
        "w_uq": wuq.astype(BF16), "b_qn_gain": b_qn_gain.reshape(1, -1), "b_qr_gain": tile(b_qr_gain, MXU_DIM),
        "kv_a_gain": kv_a_gain.reshape(1, -1), "w_ukv": w_ukv[0].astype(BF16),
        "b_kn_gain": b_kn_gain.reshape(1, -1), "b_kr_gain": tile(b_kr_gain, LANES),
        "w_o": w_o[0].astype(BF16),
        "w_router": jnp.pad(w_router[0], ((0, 0), (0, LANES - N_EXPERTS))).astype(BF16),
        "seg64": _seg_matrix(A_DH),
    }


def kernel(x_prompt, x_sample, cache_diff_k, cache_diff_v, cache_mla_ckv, cache_mla_krope, c, c_ctx, w_ada, b_ada, norm1_gain, norm2_gain, w_in, a_q_gain, a_k_gain, a_lambda_q1, a_lambda_k1, a_lambda_q2, a_lambda_k2, a_sub_gain, q_a_gain, w_uq, b_qn_gain, b_qr_gain, kv_a_gain, w_ukv, b_kn_gain, b_kr_gain, w_o, w_router, w_exp1, w_exp3, w_exp2):
    nbp, lp, _ = x_prompt.shape
    nbs, ls, _ = x_sample.shape
    past = cache_diff_k.shape[2]
    w = _prep_weights(norm1_gain, norm2_gain, w_in, a_q_gain, a_k_gain, a_lambda_q1, a_lambda_k1, a_lambda_q2,
                      a_lambda_k2, a_sub_gain, q_a_gain, w_uq, b_qn_gain, b_qr_gain, kv_a_gain, w_ukv, b_kn_gain,
                      b_kr_gain, w_o, w_router)

    cvec = jnp.concatenate([c_ctx[None], c, jnp.zeros((8 - 1 - nbs, D_MODEL), F32)], axis=0)
    mod = _ada(cvec, w_ada[0], b_ada)
    mod_p = mod[0:1].reshape(1, 1, -1)
    mod_s = mod[1:1 + nbs].reshape(nbs, 1, -1)

    xp = x_prompt.reshape(nbp * lp, D_MODEL)
    xs = x_sample.reshape(nbs * ls, D_MODEL)
    tm = 256
    assert lp == tm, "the prompt projection writes one batch of new_diff_k per row tile"
    aq_p, ak_p, av_p, qm_p, km_p, bv_p, ak32, av32, ckv32, kr32 = _in_proj(xp, mod_p, False, nbp * lp // tm, w, None)
    aq_s, ak_s, av_s, qm_s, km_s, bv_s = _in_proj(xs, mod_s, True, ls // tm, w, _rope_tables(ls))
    km_c, bv_c = _cache_kv(cache_mla_ckv.reshape(nbs * past, KV_RANK), cache_mla_krope.reshape(nbs * past, B_ROPE), w)

    cache_a = (cache_diff_k.reshape(nbs * past, A_QK), cache_diff_v.reshape(nbs * past, A_V))

    oa_p = _diff_attn(aq_p, ak_p, av_p, None, nbp, lp, lp, A_HEADS, w)
    ob_p = _mla_attn(qm_p, km_p, bv_p, None, nbp, lp, lp, B_HEADS)
    oa_s = _diff_attn(aq_s, ak_s, av_s, cache_a, nbs, ls, 256, 4, w)
    ob_s = _mla_attn(qm_s, km_s, bv_s, (km_c, bv_c), nbs, ls, 256, 4)

    x1_p, h2_p, aff_p = _out_proj(oa_p, ob_p, xp, mod_p, nbp * lp // tm, w, "prompt")
    x1_s, h2_s, aff_s = _out_proj(oa_s, ob_s, xs, mod_s, ls // tm, w, "sample")

    idx_p, gate_p, key_p, plan_p, off_p = _route(aff_p, "prompt")
    idx_s, gate_s, key_s, plan_s, off_s = _route(aff_s, "sample")
    cap = idx_p.shape[1]
    idx_flat = jnp.stack([idx_p, idx_s]).reshape(-1)
    gate = jnp.concatenate([gate_p, gate_s], axis=1)[..., None]
    ye_p, ye_s = _ffn(idx_flat, h2_p, h2_s, gate, w_exp1[0], w_exp3[0], w_exp2[0], cap)

    y_p = _combine(plan_p, off_p, key_p, ye_p, x1_p, mod_p, nbp * lp // COMBINE_T, cap, "prompt")
    y_s = _combine(plan_s, off_s, key_s, ye_s, x1_s, mod_s, ls // COMBINE_T, cap, "sample")

    return (y_p.reshape(nbp, lp, D_MODEL), y_s.reshape(nbs, ls, D_MODEL),
            ak32, av32.reshape(nbp, 1, lp, A_HEADS, A_DV),
            ckv32.reshape(nbp, 1, lp, KV_RANK), kr32.reshape(nbp, 1, lp, B_ROPE))
```

```python
import functools
import math

import numpy as np
import jax
import jax.numpy as jnp
from jax import lax
from jax.experimental import pallas as pl
from jax.experimental.pallas import tpu as pltpu

F32 = jnp.float32
BF16 = jnp.bfloat16

D_MODEL = 2048
GRID_W = 64
A_HEADS = 8
A_DH = 64
A_DV = 128
B_HEADS = 8
B_NOPE = 128
B_ROPE = 64
B_DV = 128
Q_RANK = 512
KV_RANK = 256
N_EXPERTS = 16
EXPERT_FF = 1536
CAPACITY_FACTOR = 2
ROPE_BASE = 10000.0
EPS = 1e-6
LAM_INIT = 0.8 - 0.6 * math.exp(-0.3 * 0)
LOG2E = math.log2(math.e)

A_QK = A_HEADS * 2 * A_DH
A_V = A_HEADS * A_DV
B_QK = B_NOPE + B_ROPE
IN_COLS = 2 * A_QK + A_V + Q_RANK + KV_RANK + B_ROPE
LANES = 128
MXU_DIM = 256
VMEM_LIMIT = 56 * 1024 * 1024


def _cparams(sem):
    return pltpu.CompilerParams(dimension_semantics=sem, vmem_limit_bytes=VMEM_LIMIT)


def _resident(shape):
    nd = len(shape)
    return pl.BlockSpec(shape, lambda *_: (0,) * nd, pipeline_mode=pl.Buffered(1))


def _rms(x, gain):
    return x * lax.rsqrt(jnp.mean(x * x, axis=-1, keepdims=True) + EPS) * gain


def _seg_rms(y, segmat, gain):
    y2 = (y * y).astype(BF16)
    outs = []
    for c in range(y.shape[1] // MXU_DIM):
        sl = slice(c * MXU_DIM, (c + 1) * MXU_DIM)
        ms = jnp.dot(y2[:, sl], segmat, preferred_element_type=F32)
        outs.append(y[:, sl] * lax.rsqrt(ms + EPS) * gain)
    return jnp.concatenate(outs, axis=1)


def _rope(y, cos, sin):
    outs = []
    for c in range(y.shape[1] // LANES):
        yc = y[:, c * LANES:(c + 1) * LANES]
        lane = lax.broadcasted_iota(jnp.int32, yc.shape, 1)
        partner = jnp.where((lane & 16) == 0, pltpu.roll(yc, LANES - 16, 1), pltpu.roll(yc, 16, 1))
        outs.append(yc * cos + partner * sin)
    return outs[0] if len(outs) == 1 else jnp.concatenate(outs, axis=1)


def _rope_tables(n_tok):
    t = np.arange(n_tok)
    row, col = t // GRID_W, t % GRID_W
    nf = A_DH // 4
    inv = ROPE_BASE ** (-np.arange(nf, dtype=np.float64) * 2.0 / (A_DH // 2))
    lane = np.arange(A_DH)
    pos = np.where(lane[None, :] < A_DH // 2, row[:, None], col[:, None]).astype(np.float64)
    ang = pos * inv[lane % nf][None, :]
    sign = np.where((lane % (2 * nf)) < nf, -1.0, 1.0)[None, :]
    cos = np.tile(np.cos(ang), (1, LANES // A_DH)).astype(np.float32)
    sin = np.tile(np.sin(ang) * sign, (1, LANES // A_DH)).astype(np.float32)
    return jnp.asarray(cos), jnp.asarray(sin)


def _seg_matrix(width):
    i = np.arange(MXU_DIM)
    return jnp.asarray(((i[:, None] // width) == (i[None, :] // width)).astype(np.float32) / width, dtype=BF16)


def _ada_kernel(c_ref, w_ref, b_ref, o_ref):
    c = c_ref[...]
    s = c * jax.nn.sigmoid(c)
    o_ref[...] = jnp.dot(s.astype(BF16), w_ref[...].astype(BF16), preferred_element_type=F32) + b_ref[...]


def _ada(cvec, w_ada, b_ada):
    tn = 1024
    n = w_ada.shape[1]
    return pl.pallas_call(
        _ada_kernel,
        grid=(n // tn,),
        in_specs=[pl.BlockSpec((8, D_MODEL), lambda j: (0, 0)),
                  pl.BlockSpec((D_MODEL, tn), lambda j: (0, j)),
                  pl.BlockSpec((1, tn), lambda j: (0, j))],
        out_specs=pl.BlockSpec((8, tn), lambda j: (0, j)),
        out_shape=jax.ShapeDtypeStruct((8, n), F32),
        compiler_params=_cparams(("arbitrary",)),
        name="ada_mod",
    )(cvec, w_ada, b_ada)


def _expand_kv(ckv_n, kr, wukv_ref, kng, km_ref, bv_ref):
    kv = jnp.dot(ckv_n.astype(BF16), wukv_ref[...], preferred_element_type=F32)
    krb = kr[:, :B_ROPE].astype(BF16)
    for h in range(B_HEADS):
        base = h * (B_NOPE + B_DV)
        kn = _rms(kv[:, base:base + B_NOPE], kng)
        km_ref[h] = jnp.concatenate([kn.astype(BF16), krb], axis=1)
        bv_ref[h] = kv[:, base + B_NOPE:base + B_NOPE + B_DV].astype(BF16)


def _in_kernel(sample, *refs):
    (x_ref, sh_ref, sc_ref, n1g_ref, win_ref, aqg_ref, akg_ref, qag_ref, wuq_ref, qng_ref, qrg_ref,
     kvag_ref, wukv_ref, kng_ref, krg_ref, seg_ref) = refs[:16]
    refs = refs[16:]
    if sample:
        cos_ref, sin_ref = refs[:2]
        refs = refs[2:]
        cos, sin = cos_ref[...], sin_ref[...]
    aq_ref, ak_ref, av_ref, qm_ref, km_ref, bv_ref = refs[:6]
    refs = refs[6:]

    x = x_ref[...]
    h = _rms(x, n1g_ref[...]) * (1.0 + sc_ref[...]) + sh_ref[...]
    hb = h.astype(BF16)

    def proj(c0, c1):
        return jnp.dot(hb, win_ref[:, c0:c1], preferred_element_type=F32)

    seg = seg_ref[...]
    c0 = 2 * A_QK + A_V
    y_q = proj(c0, c0 + Q_RANK)
    y_ckv = proj(c0 + Q_RANK, c0 + Q_RANK + KV_RANK)
    kr64 = proj(c0 + Q_RANK + KV_RANK, c0 + Q_RANK + KV_RANK + B_ROPE)
    q_lat = _rms(y_q, qag_ref[...])
    qb = jnp.dot(q_lat.astype(BF16), wuq_ref[...], preferred_element_type=F32)
    ckv = _rms(y_ckv, kvag_ref[...])
    kr_raw = jnp.concatenate([kr64, jnp.zeros_like(kr64)], axis=1)
    kr = kr_raw * lax.rsqrt(jnp.sum(kr_raw * kr_raw, axis=-1, keepdims=True) * (1.0 / B_ROPE) + EPS) * krg_ref[...]
    if sample:
        kr = _rope(kr, cos, sin)
    _expand_kv(ckv, kr, wukv_ref, kng_ref[...], km_ref, bv_ref)
    y_aq = proj(0, A_QK)
    qr = _seg_rms(qb[:, B_HEADS * B_NOPE:], seg, qrg_ref[...])
    if sample:
        qr = _rope(qr, cos, sin)
    scale = B_QK ** -0.5 * LOG2E
    for hd in range(B_HEADS):
        qn = _rms(qb[:, hd * B_NOPE:(hd + 1) * B_NOPE], qng_ref[...])
        qm_ref[hd] = jnp.concatenate(
            [(qn * scale).astype(BF16), (qr[:, hd * B_ROPE:(hd + 1) * B_ROPE] * scale).astype(BF16)], axis=1)

    y_ak = proj(A_QK, 2 * A_QK)
    aq = _seg_rms(y_aq, seg, aqg_ref[...])
    if sample:
        aq = _rope(aq, cos, sin)
    aq_ref[...] = (aq * (A_DH ** -0.5 * LOG2E)).astype(BF16)
    av = proj(2 * A_QK, 2 * A_QK + A_V)
    ak = _seg_rms(y_ak, seg, akg_ref[...])
    if sample:
        ak = _rope(ak, cos, sin)
    ak_ref[...] = ak.astype(BF16)
    av_ref[...] = av.astype(BF16)
    if not sample:
        ak32_ref, av32_ref, ckv32_ref, kr32_ref = refs
        ak32_ref[...] = ak.reshape(ak32_ref.shape)
        av32_ref[...] = av
        ckv32_ref[...] = ckv
        kr32_ref[...] = kr[:, :B_ROPE]


IN_SUB = 256


def _in_proj(x, mod, sample, rows_per_mod, w, tables):
    m = x.shape[0]
    tm = IN_SUB
    nt = m // tm
    tiles_per_mod = rows_per_mod // tm
    row = lambda i: (i, 0)
    modspec = lambda k: pl.BlockSpec((None, 1, D_MODEL), lambda i: (i // tiles_per_mod, 0, k))
    in_specs = [pl.BlockSpec((tm, D_MODEL), row), modspec(0), modspec(1), _resident((1, D_MODEL)),
                _resident((D_MODEL, IN_COLS)), _resident((1, MXU_DIM)), _resident((1, MXU_DIM)),
                _resident((1, Q_RANK)), _resident((Q_RANK, B_HEADS * B_QK)), _resident((1, B_NOPE)),
                _resident((1, MXU_DIM)), _resident((1, KV_RANK)), _resident((KV_RANK, B_HEADS * (B_NOPE + B_DV))),
                _resident((1, B_NOPE)), _resident((1, LANES)), _resident((MXU_DIM, MXU_DIM))]
    args = [x, mod, mod, w["norm1_gain"], w["w_in"], w["a_q_gain"], w["a_k_gain"], w["q_a_gain"], w["w_uq"],
            w["b_qn_gain"], w["b_qr_gain"], w["kv_a_gain"], w["w_ukv"], w["b_kn_gain"], w["b_kr_gain"], w["seg64"]]
    if sample:
        per = tables[0].shape[0] // tm
        in_specs += [pl.BlockSpec((tm, LANES), lambda i: (i % per, 0))] * 2
        args += list(tables)
    out_shape = [jax.ShapeDtypeStruct((m, A_QK), BF16), jax.ShapeDtypeStruct((m, A_QK), BF16),
                 jax.ShapeDtypeStruct((m, A_V), BF16), jax.ShapeDtypeStruct((B_HEADS, m, B_QK), BF16),
                 jax.ShapeDtypeStruct((B_HEADS, m, B_QK), BF16), jax.ShapeDtypeStruct((B_HEADS, m, B_DV), BF16)]
    hspec = lambda d: pl.BlockSpec((B_HEADS, tm, d), lambda i: (0, i, 0))
    out_specs = [pl.BlockSpec((tm, A_QK), row), pl.BlockSpec((tm, A_QK), row), pl.BlockSpec((tm, A_V), row),
                 hspec(B_QK), hspec(B_QK), hspec(B_DV)]
    if not sample:
        out_shape += [jax.ShapeDtypeStruct((nt, 1, tm, A_HEADS, 2, A_DH), F32), jax.ShapeDtypeStruct((m, A_V), F32),
                      jax.ShapeDtypeStruct((m, KV_RANK), F32), jax.ShapeDtypeStruct((m, B_ROPE), F32)]
        out_specs += [pl.BlockSpec((None, None, tm, A_HEADS, 2, A_DH), lambda i: (i, 0, 0, 0, 0, 0)),
                      pl.BlockSpec((tm, A_V), row),
                      pl.BlockSpec((tm, KV_RANK), row), pl.BlockSpec((tm, B_ROPE), row)]
    return pl.pallas_call(
        functools.partial(_in_kernel, sample),
        grid=(nt,), in_specs=in_specs, out_specs=out_specs, out_shape=out_shape,
        compiler_params=_cparams(("arbitrary",)),
        name="in_proj_sample" if sample else "in_proj_prompt",
    )(*args)


def _cache_kv_kernel(ckv_ref, kr_ref, wukv_ref, kng_ref, km_ref, bv_ref):
    _expand_kv(ckv_ref[...], kr_ref[...], wukv_ref, kng_ref[...], km_ref, bv_ref)


def _cache_kv(ckv, kr, w):
    m = ckv.shape[0]
    return pl.pallas_call(
        _cache_kv_kernel,
        grid=(1,),
        in_specs=[_resident((m, KV_RANK)), _resident((m, B_ROPE)),
                  _resident((KV_RANK, B_HEADS * (B_NOPE + B_DV))), _resident((1, B_NOPE))],
        out_specs=[pl.BlockSpec((B_HEADS, m, B_QK), lambda i: (0, 0, 0)),
                   pl.BlockSpec((B_HEADS, m, B_DV), lambda i: (0, 0, 0))],
        out_shape=[jax.ShapeDtypeStruct((B_HEADS, m, B_QK), BF16), jax.ShapeDtypeStruct((B_HEADS, m, B_DV), BF16)],
        compiler_params=_cparams(("arbitrary",)),
        name="cache_kv",
    )(ckv, kr, w["w_ukv"], w["b_kn_gain"])


def _qk(q, k):
    return lax.dot_general(q, k, (((1,), (1,)), ((), ())), preferred_element_type=F32)


KEY_CHUNK = 512
SCORE_AHEAD = 1


def _scores(q, k, kc):
    s = _qk(q, k)
    return s if kc is None else jnp.concatenate([s, _qk(q, kc)], axis=1)


def _weighted(p, v, vc):
    if vc is None:
        return jnp.dot(p, v, preferred_element_type=F32)
    lk = v.shape[0]
    return (jnp.dot(p[:, :lk], v, preferred_element_type=F32)
            + jnp.dot(p[:, lk:], vc, preferred_element_type=F32))


def _diff_attn_kernel(hb, cached, q_ref, k_ref, v_ref, *refs):
    if cached:
        kc_ref, vc_ref = refs[:2]
        refs = refs[2:]
    lq1, lk1, lq2, lk2, sub_ref, o_ref = refs
    lam = (jnp.exp(jnp.sum(lq1[...] * lk1[...], axis=-1, keepdims=True))
           - jnp.exp(jnp.sum(lq2[...] * lk2[...], axis=-1, keepdims=True)) + LAM_INIT)
    def scores(h, comp):
        sl = slice(h * LANES, (h + 1) * LANES)
        q, k = q_ref[:, sl], k_ref[:, sl]
        kc = kc_ref[:, sl].astype(BF16) if cached else None
        lane = lax.broadcasted_iota(jnp.int32, q.shape, 1)
        keep = (lane < A_DH) if comp == 0 else (lane >= A_DH)
        return _scores(jnp.where(keep, q, jnp.zeros_like(q)), k, kc)

    s1_next, s2_next = scores(0, 0), scores(0, 1)
    for h in range(hb):
        sl = slice(h * LANES, (h + 1) * LANES)
        s1, s2 = s1_next, s2_next
        if h + 1 < hb:
            s1_next = scores(h + 1, 0)
        e1 = jnp.exp2(s1 - jnp.max(s1, axis=-1, keepdims=True))
        if h + 1 < hb:
            s2_next = scores(h + 1, 1)
        e2 = jnp.exp2(s2 - jnp.max(s2, axis=-1, keepdims=True))
        if cached:
            ones = lambda x: jnp.concatenate([x, jnp.ones_like(x)], axis=1)
            v, vc = ones(v_ref[:, sl]), ones(vc_ref[:, sl].astype(BF16))
            n1 = _weighted(e1.astype(BF16), v, vc)
            n2 = _weighted(e2.astype(BF16), v, vc)
            o = n1[:, :LANES] * (1.0 / n1[:, LANES:LANES + 1]) - n2[:, :LANES] * (lam / n2[:, LANES:LANES + 1])
        else:
            l1 = jnp.sum(e1, axis=-1, keepdims=True)
            l2 = jnp.sum(e2, axis=-1, keepdims=True)
            p = e1 - e2 * (lam * l1 / l2)
            o = jnp.dot(p.astype(BF16), v_ref[:, sl], preferred_element_type=F32) * (1.0 / l1)
        o_ref[:, sl] = (_rms(o, sub_ref[...]) * (1.0 - LAM_INIT)).astype(BF16)


def _diff_attn(q, k, v, cache, nb, lq, tq, hb, w):
    nq = lq // tq
    vec = _resident((1, A_DH))
    kv = lambda rows: pl.BlockSpec((rows, hb * LANES), lambda b, g, i: (b, g))
    in_specs = [pl.BlockSpec((tq, hb * LANES), lambda b, g, i: (b * nq + i, g)), kv(lq), kv(lq)]
    args = [q, k, v]
    if cache is not None:
        past = cache[0].shape[0] // nb
        in_specs += [kv(past), kv(past)]
        args += list(cache)
    return pl.pallas_call(
        functools.partial(_diff_attn_kernel, hb, cache is not None),
        grid=(nb, A_HEADS // hb, nq),
        in_specs=in_specs + [vec, vec, vec, vec, _resident((1, A_DV))],
        out_specs=pl.BlockSpec((tq, hb * LANES), lambda b, g, i: (b * nq + i, g)),
        out_shape=jax.ShapeDtypeStruct((nb * lq, A_V), BF16),
        compiler_params=_cparams(("arbitrary", "arbitrary", "arbitrary")),
        name="diff_attn_cached" if cache is not None else "diff_attn",
    )(*args, w["a_lambda_q1"], w["a_lambda_k1"], w["a_lambda_q2"], w["a_lambda_k2"], w["a_sub_gain"])


def _mla_attn_kernel(hb, cached, q_ref, k_ref, v_ref, *refs):
    if cached:
        kc_ref, vc_ref, o_ref = refs
    else:
        (o_ref,) = refs
    lk = k_ref.shape[1]
    ck = min(lk, KEY_CHUNK)
    def head_scores(h):
        q = q_ref[h]
        s = [_qk(q, k_ref[h, c * ck:(c + 1) * ck]) for c in range(lk // ck)]
        return s + [_qk(q, kc_ref[h])] if cached else s

    ahead = [head_scores(i) for i in range(min(SCORE_AHEAD, hb))]
    for h in range(hb):
        vs = [v_ref[h, c * ck:(c + 1) * ck] for c in range(lk // ck)] + ([vc_ref[h]] if cached else [])
        s = ahead.pop(0)
        if h + SCORE_AHEAD < hb:
            ahead.append(head_scores(h + SCORE_AHEAD))
        m = functools.reduce(jnp.maximum, [jnp.max(sc, axis=-1, keepdims=True) for sc in s])
        l = 0.0
        o = 0.0
        for sc, v in zip(s, vs):
            e = jnp.exp2(sc - m)
            if cached:
                v = jnp.concatenate([v, jnp.ones_like(v)], axis=1)
            else:
                l = l + jnp.sum(e, axis=-1, keepdims=True)
            o = o + jnp.dot(e.astype(BF16), v, preferred_element_type=F32)
        if cached:
            o, l = o[:, :LANES], o[:, LANES:LANES + 1]
        o_ref[:, h * LANES:(h + 1) * LANES] = (o * (1.0 / l)).astype(BF16)


def _mla_attn(q, k, v, cache, nb, lq, tq, hb):
    nq = lq // tq
    kv = lambda rows, d: pl.BlockSpec((hb, rows, d), lambda b, g, i: (g, b, 0))
    in_specs = [pl.BlockSpec((hb, tq, B_QK), lambda b, g, i: (g, b * nq + i, 0)), kv(lq, B_QK), kv(lq, B_DV)]
    args = [q, k, v]
    if cache is not None:
        past = cache[0].shape[1] // nb
        in_specs += [kv(past, B_QK), kv(past, B_DV)]
        args += list(cache)
    return pl.pallas_call(
        functools.partial(_mla_attn_kernel, hb, cache is not None),
        grid=(nb, B_HEADS // hb, nq),
        in_specs=in_specs,
        out_specs=pl.BlockSpec((tq, hb * LANES), lambda b, g, i: (b * nq + i, g)),
        out_shape=jax.ShapeDtypeStruct((nb * lq, B_HEADS * B_DV), BF16),
        compiler_params=_cparams(("arbitrary", "arbitrary", "arbitrary")),
        name="mla_attn_cached" if cache is not None else "mla_attn",
    )(*args)


def _out_kernel(oa_ref, ob_ref, x_ref, g1_ref, sh2_ref, sc2_ref, n2g_ref, wo_ref, wr_ref,
                x1_ref, h2_ref, aff_ref):
    sub = OUT_SUB
    rows = [pl.ds(r * sub, sub) for r in range(x_ref.shape[0] // sub)]
    os_ = [jnp.dot(oa_ref[r, :], wo_ref[:A_V, :], preferred_element_type=F32)
           + jnp.dot(ob_ref[r, :], wo_ref[A_V:, :], preferred_element_type=F32) for r in rows]
    for r, o in zip(rows, os_):
        x1 = x_ref[r, :] + g1_ref[...] * o
        x1_ref[r, :] = x1
        h2 = _rms(x1, n2g_ref[...]) * (1.0 + sc2_ref[...]) + sh2_ref[...]
        h2_ref[r, :] = h2
        logits = jnp.dot(h2.astype(BF16), wr_ref[...], preferred_element_type=F32)
        lane = lax.broadcasted_iota(jnp.int32, logits.shape, 1)
        logits = jnp.where(lane < N_EXPERTS, logits, -jnp.inf)
        e = jnp.exp(logits - jnp.max(logits, axis=-1, keepdims=True))
        aff_ref[r, :] = e / jnp.sum(e, axis=-1, keepdims=True)


OUT_SUB = 256


def _out_proj(oa, ob, x, mod, tiles_per_mod, w, tag):
    m = x.shape[0]
    tm = 512
    tiles_per_mod = tiles_per_mod * 256 // tm
    row = lambda i: (i, 0)
    modspec = lambda k: pl.BlockSpec((None, 1, D_MODEL), lambda i: (i // tiles_per_mod, 0, k))
    return pl.pallas_call(
        _out_kernel,
        grid=(m // tm,),
        in_specs=[pl.BlockSpec((tm, A_V), row), pl.BlockSpec((tm, A_V), row), pl.BlockSpec((tm, D_MODEL), row),
                  modspec(2), modspec(3), modspec(4), _resident((1, D_MODEL)),
                  _resident((2 * A_V, D_MODEL)), _resident((D_MODEL, LANES))],
        out_specs=[pl.BlockSpec((tm, D_MODEL), row), pl.BlockSpec((tm, D_MODEL), row),
                   pl.BlockSpec((tm, LANES), row)],
        out_shape=[jax.ShapeDtypeStruct((m, D_MODEL), F32), jax.ShapeDtypeStruct((m, D_MODEL), F32),
                   jax.ShapeDtypeStruct((m, LANES), F32)],
        compiler_params=_cparams(("arbitrary",)),
        name="out_proj_" + tag,
    )(oa, ob, x, mod, mod, mod, w["norm2_gain"], w["w_o"], w["w_router"])


ROUTE_BLK = 256
COMBINE_T = 128
KEY_SHIFT = 16
GROUP_BITS = 3
GROUP = 1 << GROUP_BITS
PLAN_NG_SHIFT = 10
PLAN_GS_SHIFT = 15


def _route_kernel(cap, aff_ref, st_ref, key_ref, plan_ref, off_ref):
    n = aff_ref.shape[0]
    nblk = n // ROUTE_BLK
    aff = aff_ref[...]
    lane = lax.broadcasted_iota(jnp.int32, (n, LANES), 1)
    tok = lax.broadcasted_iota(jnp.int32, (n, LANES), 0)

    aff_t = aff.T[:N_EXPERTS]

    def narrow(carry):
        lo, hi = carry
        mid = lo + (hi - lo) * 0.5
        mid = jnp.where(mid < hi, mid, lo)
        above = aff_t > mid
        few = jnp.sum(above.astype(jnp.int32), axis=1, keepdims=True) < cap
        up = jnp.min(jnp.where(above, aff_t, jnp.inf), axis=1, keepdims=True)
        dn = jnp.max(jnp.where(above, -jnp.inf, aff_t), axis=1, keepdims=True)
        return jnp.where(few, lo, up), jnp.where(few, dn, hi)

    bounds = (jnp.min(aff_t, axis=1, keepdims=True), jnp.max(aff_t, axis=1, keepdims=True))
    thr_t, _ = lax.while_loop(lambda c: jnp.max((c[0] < c[1]).astype(jnp.int32)) > 0, narrow, bounds)
    thr_sq = jnp.broadcast_to(jnp.concatenate([thr_t, jnp.zeros((LANES - N_EXPERTS, 1), F32)], axis=0), (LANES, LANES))
    thr = thr_sq.T[0:1, :]
    gt = aff > thr
    eq = aff == thr
    need = cap - jnp.sum(gt.astype(jnp.int32), axis=0, keepdims=True)

    r = lax.broadcasted_iota(jnp.int32, (ROUTE_BLK, ROUTE_BLK), 0)
    c = lax.broadcasted_iota(jnp.int32, (ROUTE_BLK, ROUTE_BLK), 1)
    tri = (c < r).astype(BF16)

    def prefix(mask):
        m = mask.astype(F32)
        carry = jnp.zeros((1, LANES), F32)
        outs = []
        for b in range(nblk):
            mb = m[b * ROUTE_BLK:(b + 1) * ROUTE_BLK]
            outs.append(jnp.dot(tri, mb.astype(BF16), preferred_element_type=F32) + carry)
            carry = carry + jnp.sum(mb, axis=0, keepdims=True)
        return jnp.concatenate(outs, axis=0).astype(jnp.int32)

    sel = (gt | (eq & (prefix(eq) < need))) & (lane < N_EXPERTS)
    pos = prefix(sel)
    nt = n // COMBINE_T
    key_ref[...] = pos
    st_ref[...] = jnp.zeros(st_ref.shape, jnp.int32)
    st_ref[0:nt, :] = key_ref[pl.ds(0, nt, stride=COMBINE_T), :]
    st_ref[nt:nt + 1, :] = jnp.full((1, LANES), cap, jnp.int32)
    key_ref[...] = (jnp.where(sel, pos + 1, 0) << KEY_SHIFT) | tok

    lo = st_ref[0:nt, :]
    hi = st_ref[pl.ds(1, nt), :]
    a = lo - (lo & (GROUP - 1))
    ng = jnp.where(hi > lo, lax.shift_right_logical(hi - a + (GROUP - 1), GROUP_BITS), 0)
    er = lax.broadcasted_iota(jnp.int32, (LANES, LANES), 0)
    ec = lax.broadcasted_iota(jnp.int32, (LANES, LANES), 1)
    gs = jnp.dot(ng.astype(BF16), (er < ec).astype(BF16), preferred_element_type=F32).astype(jnp.int32)
    elane = lax.broadcasted_iota(jnp.int32, (nt, LANES), 1)
    plan_ref[...] = jnp.where(elane == N_EXPERTS, gs, a | (ng << PLAN_NG_SHIFT) | (gs << PLAN_GS_SHIFT))
    off_ref[...] = gs * GROUP - a


def _compact_kernel(cap, st_s, key_ref, aff_ref, idx_ref, gate_ref, accp, accg):
    tt = COMBINE_T
    nt = key_ref.shape[0] // tt
    accp[...] = jnp.zeros(accp.shape, jnp.int32)
    accg[...] = jnp.zeros(accg.shape, F32)
    jio = lax.broadcasted_iota(jnp.int32, (tt, LANES), 1)

    def tile_body(t, carry):
        rows = pl.ds(pl.multiple_of(t * tt, tt), tt)
        key, af = key_ref[rows, :], aff_ref[rows, :]
        for e in range(N_EXPERTS):
            g0 = jnp.minimum(lax.shift_right_logical(st_s[t * LANES + e], 7), cap // LANES - 1)
            kb = jnp.broadcast_to(key[:, e:e + 1], (tt, LANES))
            ab = jnp.broadcast_to(af[:, e:e + 1], (tt, LANES))
            rel = (kb >> KEY_SHIFT) - 1 - g0 * LANES
            pk = kb & ((1 << KEY_SHIFT) - 1)
            for d in range(2):
                oh = rel == jio + d * LANES
                accp[e, g0 + d] += jnp.sum(jnp.where(oh, pk, 0).reshape(tt // 8, 8, LANES), axis=0)
                accg[e, g0 + d] += jnp.sum(jnp.where(oh, ab, 0.0).reshape(tt // 8, 8, LANES), axis=0)
        return carry

    lax.fori_loop(0, nt, tile_body, 0)
    for e in range(N_EXPERTS):
        idx_ref[e:e + 1, :] = jnp.concatenate(
            [jnp.sum(accp[e, g], axis=0, keepdims=True) for g in range(cap // LANES)], axis=1)
        gate_ref[e:e + 1, :] = jnp.concatenate(
            [jnp.sum(accg[e, g], axis=0, keepdims=True) for g in range(cap // LANES)], axis=1)


def _route(aff, tag):
    n = aff.shape[0]
    cap = CAPACITY_FACTOR * n // N_EXPERTS
    nt = n // COMBINE_T
    full = lambda s: pl.BlockSpec(s, lambda i, *_: (0,) * len(s))
    i32 = lambda s: jax.ShapeDtypeStruct(s, jnp.int32)
    assert n <= (1 << KEY_SHIFT) and cap + 1 < (1 << (31 - KEY_SHIFT))
    assert cap < (1 << PLAN_NG_SHIFT) and COMBINE_T // GROUP + 1 < (1 << (PLAN_GS_SHIFT - PLAN_NG_SHIFT))
    st, key, plan, off = pl.pallas_call(
        functools.partial(_route_kernel, cap),
        grid=(1,),
        in_specs=[full((n, LANES))],
        out_specs=[full((nt + 8, LANES)), full((n, LANES)), full((nt, LANES)), full((nt, LANES))],
        out_shape=[i32((nt + 8, LANES)), i32((n, LANES)), i32((nt, LANES)), i32((nt, LANES))],
        compiler_params=_cparams(("arbitrary",)),
        name="route_" + tag,
    )(aff)
    groups = cap // LANES + 1
    idx, gate = pl.pallas_call(
        functools.partial(_compact_kernel, cap),
        grid_spec=pltpu.PrefetchScalarGridSpec(
            num_scalar_prefetch=1, grid=(1,),
            in_specs=[full((n, LANES)), full((n, LANES))],
            out_specs=[full((N_EXPERTS, cap)), full((N_EXPERTS, cap))],
            scratch_shapes=[pltpu.VMEM((N_EXPERTS, groups, 8, LANES), jnp.int32),
                            pltpu.VMEM((N_EXPERTS, groups, 8, LANES), F32)]),
        out_shape=[i32((N_EXPERTS, cap)), jax.ShapeDtypeStruct((N_EXPERTS, cap), F32)],
        compiler_params=_cparams(("arbitrary",)),
        name="compact_" + tag,
    )(st.reshape(-1), key, aff)
    return idx, gate, key, plan, off


FF_CHUNK = 256


def _ffn_kernel(cap, nf, per, idx_s, hp_hbm, hs_hbm, gate_ref, w1_ref, w3_ref, w2_ref, yp_hbm, ys_hbm,
                xg, xe, acc, gsem, osem):
    e = pl.program_id(0)
    f = pl.program_id(1)
    ne = pl.num_programs(0)
    srcs = (hp_hbm, hs_hbm)
    dsts = (yp_hbm, ys_hbm)
    slot = lax.rem(e, 2)

    def issue_part(ex, part, sl, live):
        tail = cap - (nf - 1) * per
        live_tail = live & (part < nf - 1)
        j0 = pl.multiple_of(part * per, GROUP) if per % GROUP == 0 else part * per
        for p in range(2):
            for i in range(per):
                j = j0 + i
                jc = j if i < tail else jnp.minimum(j, cap - 1)
                tok = idx_s[(p * N_EXPERTS + ex) * cap + jc]

                @pl.when(live if i < tail else live_tail)
                def _():
                    pltpu.make_async_copy(srcs[p].at[pl.ds(tok, 1), :], xg.at[sl, pl.ds(p * cap + jc, 1), :],
                                          gsem.at[sl]).start()

    def out_copy(p, ex):
        return pltpu.make_async_copy(acc.at[pl.ds(p * cap, cap), :],
                                     dsts[p].at[pl.ds(pl.multiple_of(ex * cap, cap), cap), :], osem)

    @pl.when((e == 0) & (f == 0))
    def _prologue():
        def part_body(part, carry):
            issue_part(0, part, 0, True)
            return carry
        lax.fori_loop(0, nf, part_body, 0)

    @pl.when(f == 0)
    def _start_expert():
        for p in range(2):
            pltpu.make_async_copy(srcs[p].at[pl.ds(0, cap), :], xg.at[slot, pl.ds(p * cap, cap), :],
                                  gsem.at[slot]).wait()
        xe[...] = xg[slot].astype(BF16)

    x = xe[...]
    a = jnp.dot(x, w1_ref[...].astype(BF16), preferred_element_type=F32)
    b = jnp.dot(x, w3_ref[...].astype(BF16), preferred_element_type=F32)
    hid = ((a * jax.nn.sigmoid(a)) * b).astype(BF16)

    issue_part(jnp.minimum(e + 1, ne - 1), f, 1 - slot, e + 1 < ne)

    def down():
        return jnp.dot(hid, w2_ref[...].astype(BF16), preferred_element_type=F32)

    @pl.when(f == 0)
    def _first():
        @pl.when(e > 0)
        def _():
            for p in range(2):
                out_copy(p, e - 1).wait()
        acc[...] = down()

    @pl.when((f > 0) & (f < nf - 1))
    def _middle():
        acc[...] += down()

    @pl.when(f == nf - 1)
    def _last():
        acc[...] = (acc[...] + down()) * gate_ref[...]
        for p in range(2):
            out_copy(p, e).start()

        @pl.when(e == ne - 1)
        def _():
            for p in range(2):
                out_copy(p, e).wait()


def _ffn(idx_flat, h2p, h2s, gate, w1, w3, w2, cap):
    nf = EXPERT_FF // FF_CHUNK
    per = -(-cap // nf)
    grid_spec = pltpu.PrefetchScalarGridSpec(
        num_scalar_prefetch=1,
        grid=(N_EXPERTS, nf),
        in_specs=[pl.BlockSpec(memory_space=pl.ANY), pl.BlockSpec(memory_space=pl.ANY),
                  pl.BlockSpec((None, 2 * cap, 1), lambda e, f, *_: (e, 0, 0)),
                  pl.BlockSpec((None, D_MODEL, FF_CHUNK), lambda e, f, *_: (e, 0, f)),
                  pl.BlockSpec((None, D_MODEL, FF_CHUNK), lambda e, f, *_: (e, 0, f)),
                  pl.BlockSpec((None, FF_CHUNK, D_MODEL), lambda e, f, *_: (e, f, 0))],
        out_specs=[pl.BlockSpec(memory_space=pl.ANY), pl.BlockSpec(memory_space=pl.ANY)],
        scratch_shapes=[pltpu.VMEM((2, 2 * cap, D_MODEL), F32), pltpu.VMEM((2 * cap, D_MODEL), BF16),
                        pltpu.VMEM((2 * cap, D_MODEL), F32), pltpu.SemaphoreType.DMA((2,)),
                        pltpu.SemaphoreType.DMA(())])
    out = jax.ShapeDtypeStruct((N_EXPERTS * cap, D_MODEL), F32)
    return pl.pallas_call(
        functools.partial(_ffn_kernel, cap, nf, per),
        grid_spec=grid_spec, out_shape=[out, out],
        compiler_params=_cparams(("arbitrary", "arbitrary")),
        name="expert_ffn",
    )(idx_flat, h2p, h2s, gate, w1, w3, w2)


WAIT_GROUPS = 8
STAGE_ROWS = -(-(N_EXPERTS * (COMBINE_T // GROUP + 1) * GROUP) // MXU_DIM) * MXU_DIM


def _combine_kernel(cap, plan_s, ye_hbm, x1_ref, g2_ref, key_ref, off_ref, o_ref, stage, acc, sem):
    t = pl.program_id(0)
    nt = pl.num_programs(0)
    tt = COMBINE_T
    slot = lax.rem(t, 2)

    def issue_tile(tile, sl):
        for e in range(N_EXPERTS):
            w = plan_s[tile * LANES + e]
            a = w & ((1 << PLAN_NG_SHIFT) - 1)
            ng = lax.shift_right_logical(w, PLAN_NG_SHIFT) & ((1 << (PLAN_GS_SHIFT - PLAN_NG_SHIFT)) - 1)
            gs = lax.shift_right_logical(w, PLAN_GS_SHIFT)

            def issue(g, carry):
                src = pl.multiple_of(e * cap + a + g * GROUP, GROUP)
                dst = pl.multiple_of((gs + g) * GROUP, GROUP)
                pltpu.make_async_copy(ye_hbm.at[pl.ds(src, GROUP), :], stage.at[sl, pl.ds(dst, GROUP), :],
                                      sem.at[sl]).start()
                return carry
            lax.fori_loop(0, ng, issue, 0)

    def wait_tile(tile, sl):
        total = plan_s[tile * LANES + N_EXPERTS]

        def wait_groups(n):
            def body(i, carry):
                pltpu.make_async_copy(ye_hbm.at[pl.ds(0, n * GROUP), :], stage.at[sl, pl.ds(0, n * GROUP), :],
                                      sem.at[sl]).wait()
                return carry
            return body
        lax.fori_loop(0, total // WAIT_GROUPS, wait_groups(WAIT_GROUPS), 0)
        lax.fori_loop(0, lax.rem(total, WAIT_GROUPS), wait_groups(1), 0)

    @pl.when(t == 0)
    def _init():
        stage[...] = jnp.zeros(stage.shape, F32)
        issue_tile(0, 0)

    @pl.when(t + 1 < nt)
    def _prefetch():
        issue_tile(t + 1, 1 - slot)

    wait_tile(t, slot)

    slot1 = lax.shift_right_logical(key_ref[...], KEY_SHIFT)
    srow = jnp.where(slot1 > 0, slot1 - 1 + off_ref[pl.ds(t, 1), :], -1)
    cio = lax.broadcasted_iota(jnp.int32, (tt, MXU_DIM), 1)
    acc[...] = jnp.zeros(acc.shape, F32)

    def add_chunk(kc, carry):
        base = pl.multiple_of(kc * MXU_DIM, MXU_DIM)
        rel = srow - base
        pick = jnp.zeros((tt, MXU_DIM), F32)
        for e in range(N_EXPERTS):
            pick = jnp.where(jnp.broadcast_to(rel[:, e:e + 1], (tt, MXU_DIM)) == cio, 1.0, pick)
        rows = stage[slot, pl.ds(base, MXU_DIM), :]
        acc[...] += jnp.dot(pick.astype(BF16), rows.astype(BF16), preferred_element_type=F32)
        return carry

    total_rows = plan_s[t * LANES + N_EXPERTS] * GROUP
    lax.fori_loop(0, lax.div(total_rows + MXU_DIM - 1, MXU_DIM), add_chunk, 0)
    o_ref[...] = x1_ref[...] + g2_ref[...] * acc[...]


def _combine(plan, off, key, ye, x1, mod, tiles_per_mod, cap, tag):
    n = x1.shape[0]
    tt = COMBINE_T
    nt = n // tt
    grid_spec = pltpu.PrefetchScalarGridSpec(
        num_scalar_prefetch=1,
        grid=(nt,),
        in_specs=[pl.BlockSpec(memory_space=pl.ANY),
                  pl.BlockSpec((tt, D_MODEL), lambda i, *_: (i, 0)),
                  pl.BlockSpec((None, 1, D_MODEL), lambda i, *_: (i // tiles_per_mod, 0, 5)),
                  pl.BlockSpec((tt, LANES), lambda i, *_: (i, 0)),
                  pl.BlockSpec((nt, LANES), lambda i, *_: (0, 0))],
        out_specs=pl.BlockSpec((tt, D_MODEL), lambda i, *_: (i, 0)),
        scratch_shapes=[pltpu.VMEM((2, STAGE_ROWS, D_MODEL), F32), pltpu.VMEM((tt, D_MODEL), F32),
                        pltpu.SemaphoreType.DMA((2,))])
    return pl.pallas_call(
        functools.partial(_combine_kernel, cap),
        grid_spec=grid_spec, out_shape=jax.ShapeDtypeStruct((n, D_MODEL), F32),
        compiler_params=_cparams(("arbitrary",)),
        name="combine_" + tag,
    )(plan.reshape(-1), ye, x1, mod, key, off)


def _prep_weights(norm1_gain, norm2_gain, w_in, a_q_gain, a_k_gain, a_lambda_q1, a_lambda_k1, a_lambda_q2,
                  a_lambda_k2, a_sub_gain, q_a_gain, w_uq, b_qn_gain, b_qr_gain, kv_a_gain, w_ukv, b_kn_gain,
                  b_kr_gain, w_o, w_router):
    tile = lambda g, width: jnp.tile(g.reshape(1, -1), (1, width // g.shape[-1]))
    wuq = w_uq[0].reshape(Q_RANK, B_HEADS, B_QK)
    wuq = jnp.concatenate([wuq[:, :, :B_NOPE].reshape(Q_RANK, -1), wuq[:, :, B_NOPE:].reshape(Q_RANK, -1)], axis=1)
    return {
        "norm1_gain": norm1_gain.reshape(1, -1), "norm2_gain": norm2_gain.reshape(1, -1),
        "w_in": w_in[0].astype(BF16),
        "a_q_gain": tile(a_q_gain, MXU_DIM), "a_k_gain": tile(a_k_gain, MXU_DIM),
        "a_lambda_q1": a_lambda_q1.reshape(1, -1), "a_lambda_k1": a_lambda_k1.reshape(1, -1),
        "a_lambda_q2": a_lambda_q2.reshape(1, -1), "a_lambda_k2": a_lambda_k2.reshape(1, -1),
        "a_sub_gain": a_sub_gain.reshape(1, -1), "q_a_gain": q_a_gain.reshape(1, -1),
        "w_uq": wuq.astype(BF16), "b_qn_gain": b_qn_gain.reshape(1, -1), "b_qr_gain": tile(b_qr_gain, MXU_DIM),
        "kv_a_gain": kv_a_gain.reshape(1, -1), "w_ukv": w_ukv[0].astype(BF16),
        "b_kn_gain": b_kn_gain.reshape(1, -1), "b_kr_gain": tile(b_kr_gain, LANES),
        "w_o": w_o[0].astype(BF16),
        "w_router": jnp.pad(w_router[0], ((0, 0), (0, LANES - N_EXPERTS))).astype(BF16),
        "seg64": _seg_matrix(A_DH),
    }


def kernel(x_prompt, x_sample, cache_diff_k, cache_diff_v, cache_mla_ckv, cache_mla_krope, c, c_ctx, w_ada, b_ada, norm1_gain, norm2_gain, w_in, a_q_gain, a_k_gain, a_lambda_q1, a_lambda_k1, a_lambda_q2, a_lambda_k2, a_sub_gain, q_a_gain, w_uq, b_qn_gain, b_qr_gain, kv_a_gain, w_ukv, b_kn_gain, b_kr_gain, w_o, w_router, w_exp1, w_exp3, w_exp2):
    nbp, lp, _ = x_prompt.shape
    nbs, ls, _ = x_sample.shape
    past = cache_diff_k.shape[2]
    w = _prep_weights(norm1_gain, norm2_gain, w_in, a_q_gain, a_k_gain, a_lambda_q1, a_lambda_k1, a_lambda_q2,
                      a_lambda_k2, a_sub_gain, q_a_gain, w_uq, b_qn_gain, b_qr_gain, kv_a_gain, w_ukv, b_kn_gain,
                      b_kr_gain, w_o, w_router)

    cvec = jnp.concatenate([c_ctx[None], c, jnp.zeros((8 - 1 - nbs, D_MODEL), F32)], axis=0)
    mod = _ada(cvec, w_ada[0], b_ada)
    mod_p = mod[0:1].reshape(1, 1, -1)
    mod_s = mod[1:1 + nbs].reshape(nbs, 1, -1)

    xp = x_prompt.reshape(nbp * lp, D_MODEL)
    xs = x_sample.reshape(nbs * ls, D_MODEL)
    tm = 256
    assert lp == IN_SUB, "the prompt projection writes one batch of new_diff_k per row tile"
    aq_p, ak_p, av_p, qm_p, km_p, bv_p, ak32, av32, ckv32, kr32 = _in_proj(xp, mod_p, False, nbp * lp, w, None)
    aq_s, ak_s, av_s, qm_s, km_s, bv_s = _in_proj(xs, mod_s, True, ls, w, _rope_tables(ls))
    km_c, bv_c = _cache_kv(cache_mla_ckv.reshape(nbs * past, KV_RANK), cache_mla_krope.reshape(nbs * past, B_ROPE), w)

    cache_a = (cache_diff_k.reshape(nbs * past, A_QK), cache_diff_v.reshape(nbs * past, A_V))

    oa_p = _diff_attn(aq_p, ak_p, av_p, None, nbp, lp, lp, A_HEADS, w)
    ob_p = _mla_attn(qm_p, km_p, bv_p, None, nbp, lp, lp, B_HEADS)
    oa_s = _diff_attn(aq_s, ak_s, av_s, cache_a, nbs, ls, 256, 4, w)
    ob_s = _mla_attn(qm_s, km_s, bv_s, (km_c, bv_c), nbs, ls, 256, 4)

    x1_p, h2_p, aff_p = _out_proj(oa_p, ob_p, xp, mod_p, nbp * lp // tm, w, "prompt")
    x1_s, h2_s, aff_s = _out_proj(oa_s, ob_s, xs, mod_s, ls // tm, w, "sample")

    idx_p, gate_p, key_p, plan_p, off_p = _route(aff_p, "prompt")
    idx_s, gate_s, key_s, plan_s, off_s = _route(aff_s, "sample")
    cap = idx_p.shape[1]
    idx_flat = jnp.stack([idx_p, idx_s]).reshape(-1)
    gate = jnp.concatenate([gate_p, gate_s], axis=1)[..., None]
    ye_p, ye_s = _ffn(idx_flat, h2_p, h2_s, gate, w_exp1[0], w_exp3[0], w_exp2[0], cap)

    y_p = _combine(plan_p, off_p, key_p, ye_p, x1_p, mod_p, nbp * lp // COMBINE_T, cap, "prompt")
    y_s = _combine(plan_s, off_s, key_s, ye_s, x1_s, mod_s, ls // COMBINE_T, cap, "sample")

    return (y_p.reshape(nbp, lp, D_MODEL), y_s.reshape(nbs, ls, D_MODEL),
            ak32, av32.reshape(nbp, 1, lp, A_HEADS, A_DV),
            ckv32.reshape(nbp, 1, lp, KV_RANK), kr32.reshape(nbp, 1, lp, B_ROPE))
```

```python
import functools
import math

import numpy as np
import jax
import jax.numpy as jnp
from jax import lax
from jax.experimental import pallas as pl
from jax.experimental.pallas import tpu as pltpu

F32 = jnp.float32
BF16 = jnp.bfloat16

D_MODEL = 2048
GRID_W = 64
A_HEADS = 8
A_DH = 64
A_DV = 128
B_HEADS = 8
B_NOPE = 128
B_ROPE = 64
B_DV = 128
Q_RANK = 512
KV_RANK = 256
N_EXPERTS = 16
EXPERT_FF = 1536
CAPACITY_FACTOR = 2
ROPE_BASE = 10000.0
EPS = 1e-6
LAM_INIT = 0.8 - 0.6 * math.exp(-0.3 * 0)
LOG2E = math.log2(math.e)

A_QK = A_HEADS * 2 * A_DH
A_V = A_HEADS * A_DV
B_QK = B_NOPE + B_ROPE
IN_COLS = 2 * A_QK + A_V + Q_RANK + KV_RANK + B_ROPE
LANES = 128
MXU_DIM = 256
VMEM_LIMIT = 56 * 1024 * 1024


def _cparams(sem):
    return pltpu.CompilerParams(dimension_semantics=sem, vmem_limit_bytes=VMEM_LIMIT)


def _resident(shape):
    nd = len(shape)
    return pl.BlockSpec(shape, lambda *_: (0,) * nd, pipeline_mode=pl.Buffered(1))


def _rms(x, gain):
    return x * lax.rsqrt(jnp.mean(x * x, axis=-1, keepdims=True) + EPS) * gain


def _seg_rms(y, segmat, gain):
    y2 = (y * y).astype(BF16)
    outs = []
    for c in range(y.shape[1] // MXU_DIM):
        sl = slice(c * MXU_DIM, (c + 1) * MXU_DIM)
        ms = jnp.dot(y2[:, sl], segmat, preferred_element_type=F32)
        outs.append(y[:, sl] * lax.rsqrt(ms + EPS) * gain)
    return jnp.concatenate(outs, axis=1)


def _rope(y, cos, sin):
    outs = []
    for c in range(y.shape[1] // LANES):
        yc = y[:, c * LANES:(c + 1) * LANES]
        lane = lax.broadcasted_iota(jnp.int32, yc.shape, 1)
        partner = jnp.where((lane & 16) == 0, pltpu.roll(yc, LANES - 16, 1), pltpu.roll(yc, 16, 1))
        outs.append(yc * cos + partner * sin)
    return outs[0] if len(outs) == 1 else jnp.concatenate(outs, axis=1)


def _rope_tables(n_tok):
    t = np.arange(n_tok)
    row, col = t // GRID_W, t % GRID_W
    nf = A_DH // 4
    inv = ROPE_BASE ** (-np.arange(nf, dtype=np.float64) * 2.0 / (A_DH // 2))
    lane = np.arange(A_DH)
    pos = np.where(lane[None, :] < A_DH // 2, row[:, None], col[:, None]).astype(np.float64)
    ang = pos * inv[lane % nf][None, :]
    sign = np.where((lane % (2 * nf)) < nf, -1.0, 1.0)[None, :]
    cos = np.tile(np.cos(ang), (1, LANES // A_DH)).astype(np.float32)
    sin = np.tile(np.sin(ang) * sign, (1, LANES // A_DH)).astype(np.float32)
    return jnp.asarray(cos), jnp.asarray(sin)


def _seg_matrix(width):
    i = np.arange(MXU_DIM)
    return jnp.asarray(((i[:, None] // width) == (i[None, :] // width)).astype(np.float32) / width, dtype=BF16)


def _ada_kernel(c_ref, w_ref, b_ref, o_ref):
    c = c_ref[...]
    s = c * jax.nn.sigmoid(c)
    o_ref[...] = jnp.dot(s.astype(BF16), w_ref[...].astype(BF16), preferred_element_type=F32) + b_ref[...]


def _ada(cvec, w_ada, b_ada):
    tn = 1024
    n = w_ada.shape[1]
    return pl.pallas_call(
        _ada_kernel,
        grid=(n // tn,),
        in_specs=[pl.BlockSpec((8, D_MODEL), lambda j: (0, 0)),
                  pl.BlockSpec((D_MODEL, tn), lambda j: (0, j)),
                  pl.BlockSpec((1, tn), lambda j: (0, j))],
        out_specs=pl.BlockSpec((8, tn), lambda j: (0, j)),
        out_shape=jax.ShapeDtypeStruct((8, n), F32),
        compiler_params=_cparams(("arbitrary",)),
        name="ada_mod",
    )(cvec, w_ada, b_ada)


def _expand_kv(ckv_n, kr, wukv_ref, kng, km_ref, bv_ref):
    kv = jnp.dot(ckv_n.astype(BF16), wukv_ref[...], preferred_element_type=F32)
    krb = kr[:, :B_ROPE].astype(BF16)
    for h in range(B_HEADS):
        base = h * (B_NOPE + B_DV)
        kn = _rms(kv[:, base:base + B_NOPE], kng)
        km_ref[h] = jnp.concatenate([kn.astype(BF16), krb], axis=1)
        bv_ref[h] = kv[:, base + B_NOPE:base + B_NOPE + B_DV].astype(BF16)


def _in_kernel(sample, *refs):
    (x_ref, sh_ref, sc_ref, n1g_ref, win_ref, aqg_ref, akg_ref, qag_ref, wuq_ref, qng_ref, qrg_ref,
     kvag_ref, wukv_ref, kng_ref, krg_ref, seg_ref) = refs[:16]
    refs = refs[16:]
    if sample:
        cos_ref, sin_ref = refs[:2]
        refs = refs[2:]
        cos, sin = cos_ref[...], sin_ref[...]
    aq_ref, ak_ref, av_ref, qm_ref, km_ref, bv_ref = refs[:6]
    refs = refs[6:]

    x = x_ref[...]
    h = _rms(x, n1g_ref[...]) * (1.0 + sc_ref[...]) + sh_ref[...]
    hb = h.astype(BF16)

    def proj(c0, c1):
        return jnp.dot(hb, win_ref[:, c0:c1], preferred_element_type=F32)

    seg = seg_ref[...]
    c0 = 2 * A_QK + A_V
    y_q = proj(c0, c0 + Q_RANK)
    y_ckv = proj(c0 + Q_RANK, c0 + Q_RANK + KV_RANK)
    kr64 = proj(c0 + Q_RANK + KV_RANK, c0 + Q_RANK + KV_RANK + B_ROPE)
    q_lat = _rms(y_q, qag_ref[...])
    qb = jnp.dot(q_lat.astype(BF16), wuq_ref[...], preferred_element_type=F32)
    ckv = _rms(y_ckv, kvag_ref[...])
    kr_raw = jnp.concatenate([kr64, jnp.zeros_like(kr64)], axis=1)
    kr = kr_raw * lax.rsqrt(jnp.sum(kr_raw * kr_raw, axis=-1, keepdims=True) * (1.0 / B_ROPE) + EPS) * krg_ref[...]
    if sample:
        kr = _rope(kr, cos, sin)
    _expand_kv(ckv, kr, wukv_ref, kng_ref[...], km_ref, bv_ref)
    y_aq = proj(0, A_QK)
    qr = _seg_rms(qb[:, B_HEADS * B_NOPE:], seg, qrg_ref[...])
    if sample:
        qr = _rope(qr, cos, sin)
    scale = B_QK ** -0.5 * LOG2E
    for hd in range(B_HEADS):
        qn = _rms(qb[:, hd * B_NOPE:(hd + 1) * B_NOPE], qng_ref[...])
        qm_ref[hd] = jnp.concatenate(
            [(qn * scale).astype(BF16), (qr[:, hd * B_ROPE:(hd + 1) * B_ROPE] * scale).astype(BF16)], axis=1)

    y_ak = proj(A_QK, 2 * A_QK)
    aq = _seg_rms(y_aq, seg, aqg_ref[...])
    if sample:
        aq = _rope(aq, cos, sin)
    aq_ref[...] = (aq * (A_DH ** -0.5 * LOG2E)).astype(BF16)
    av = proj(2 * A_QK, 2 * A_QK + A_V)
    ak = _seg_rms(y_ak, seg, akg_ref[...])
    if sample:
        ak = _rope(ak, cos, sin)
    ak_ref[...] = ak.astype(BF16)
    av_ref[...] = av.astype(BF16)
    if not sample:
        ak32_ref, av32_ref, ckv32_ref, kr32_ref = refs
        ak32_ref[...] = ak.reshape(ak32_ref.shape)
        av32_ref[...] = av
        ckv32_ref[...] = ckv
        kr32_ref[...] = kr[:, :B_ROPE]


IN_SUB = 256


def _in_proj(x, mod, sample, rows_per_mod, w, tables):
    m = x.shape[0]
    tm = IN_SUB
    nt = m // tm
    tiles_per_mod = rows_per_mod // tm
    row = lambda i: (i, 0)
    modspec = lambda k: pl.BlockSpec((None, 1, D_MODEL), lambda i: (i // tiles_per_mod, 0, k))
    in_specs = [pl.BlockSpec((tm, D_MODEL), row), modspec(0), modspec(1), _resident((1, D_MODEL)),
                _resident((D_MODEL, IN_COLS)), _resident((1, MXU_DIM)), _resident((1, MXU_DIM)),
                _resident((1, Q_RANK)), _resident((Q_RANK, B_HEADS * B_QK)), _resident((1, B_NOPE)),
                _resident((1, MXU_DIM)), _resident((1, KV_RANK)), _resident((KV_RANK, B_HEADS * (B_NOPE + B_DV))),
                _resident((1, B_NOPE)), _resident((1, LANES)), _resident((MXU_DIM, MXU_DIM))]
    args = [x, mod, mod, w["norm1_gain"], w["w_in"], w["a_q_gain"], w["a_k_gain"], w["q_a_gain"], w["w_uq"],
            w["b_qn_gain"], w["b_qr_gain"], w["kv_a_gain"], w["w_ukv"], w["b_kn_gain"], w["b_kr_gain"], w["seg64"]]
    if sample:
        per = tables[0].shape[0] // tm
        in_specs += [pl.BlockSpec((tm, LANES), lambda i: (i % per, 0))] * 2
        args += list(tables)
    out_shape = [jax.ShapeDtypeStruct((m, A_QK), BF16), jax.ShapeDtypeStruct((m, A_QK), BF16),
                 jax.ShapeDtypeStruct((m, A_V), BF16), jax.ShapeDtypeStruct((B_HEADS, m, B_QK), BF16),
                 jax.ShapeDtypeStruct((B_HEADS, m, B_QK), BF16), jax.ShapeDtypeStruct((B_HEADS, m, B_DV), BF16)]
    hspec = lambda d: pl.BlockSpec((B_HEADS, tm, d), lambda i: (0, i, 0))
    out_specs = [pl.BlockSpec((tm, A_QK), row), pl.BlockSpec((tm, A_QK), row), pl.BlockSpec((tm, A_V), row),
                 hspec(B_QK), hspec(B_QK), hspec(B_DV)]
    if not sample:
        out_shape += [jax.ShapeDtypeStruct((nt, 1, tm, A_HEADS, 2, A_DH), F32), jax.ShapeDtypeStruct((m, A_V), F32),
                      jax.ShapeDtypeStruct((m, KV_RANK), F32), jax.ShapeDtypeStruct((m, B_ROPE), F32)]
        out_specs += [pl.BlockSpec((None, None, tm, A_HEADS, 2, A_DH), lambda i: (i, 0, 0, 0, 0, 0)),
                      pl.BlockSpec((tm, A_V), row),
                      pl.BlockSpec((tm, KV_RANK), row), pl.BlockSpec((tm, B_ROPE), row)]
    return pl.pallas_call(
        functools.partial(_in_kernel, sample),
        grid=(nt,), in_specs=in_specs, out_specs=out_specs, out_shape=out_shape,
        compiler_params=_cparams(("arbitrary",)),
        name="in_proj_sample" if sample else "in_proj_prompt",
    )(*args)


def _cache_kv_kernel(ckv_ref, kr_ref, wukv_ref, kng_ref, km_ref, bv_ref):
    _expand_kv(ckv_ref[...], kr_ref[...], wukv_ref, kng_ref[...], km_ref, bv_ref)


def _cache_kv(ckv, kr, w):
    m = ckv.shape[0]
    return pl.pallas_call(
        _cache_kv_kernel,
        grid=(1,),
        in_specs=[_resident((m, KV_RANK)), _resident((m, B_ROPE)),
                  _resident((KV_RANK, B_HEADS * (B_NOPE + B_DV))), _resident((1, B_NOPE))],
        out_specs=[pl.BlockSpec((B_HEADS, m, B_QK), lambda i: (0, 0, 0)),
                   pl.BlockSpec((B_HEADS, m, B_DV), lambda i: (0, 0, 0))],
        out_shape=[jax.ShapeDtypeStruct((B_HEADS, m, B_QK), BF16), jax.ShapeDtypeStruct((B_HEADS, m, B_DV), BF16)],
        compiler_params=_cparams(("arbitrary",)),
        name="cache_kv",
    )(ckv, kr, w["w_ukv"], w["b_kn_gain"])


def _qk(q, k):
    return lax.dot_general(q, k, (((1,), (1,)), ((), ())), preferred_element_type=F32)


KEY_CHUNK = 512
SCORE_AHEAD = 1


def _scores(q, k, kc):
    s = _qk(q, k)
    return s if kc is None else jnp.concatenate([s, _qk(q, kc)], axis=1)


def _weighted(p, v, vc):
    if vc is None:
        return jnp.dot(p, v, preferred_element_type=F32)
    lk = v.shape[0]
    return (jnp.dot(p[:, :lk], v, preferred_element_type=F32)
            + jnp.dot(p[:, lk:], vc, preferred_element_type=F32))


def _diff_attn_kernel(hb, cached, q_ref, k_ref, v_ref, *refs):
    if cached:
        kc_ref, vc_ref = refs[:2]
        refs = refs[2:]
    lq1, lk1, lq2, lk2, sub_ref, o_ref = refs
    lam = (jnp.exp(jnp.sum(lq1[...] * lk1[...], axis=-1, keepdims=True))
           - jnp.exp(jnp.sum(lq2[...] * lk2[...], axis=-1, keepdims=True)) + LAM_INIT)
    def scores(h, comp):
        sl = slice(h * LANES, (h + 1) * LANES)
        q, k = q_ref[:, sl], k_ref[:, sl]
        kc = kc_ref[:, sl].astype(BF16) if cached else None
        lane = lax.broadcasted_iota(jnp.int32, q.shape, 1)
        keep = (lane < A_DH) if comp == 0 else (lane >= A_DH)
        return _scores(jnp.where(keep, q, jnp.zeros_like(q)), k, kc)

    s1_next, s2_next = scores(0, 0), scores(0, 1)
    for h in range(hb):
        sl = slice(h * LANES, (h + 1) * LANES)
        s1, s2 = s1_next, s2_next
        if h + 1 < hb:
            s1_next = scores(h + 1, 0)
        e1 = jnp.exp2(s1 - jnp.max(s1, axis=-1, keepdims=True))
        if h + 1 < hb:
            s2_next = scores(h + 1, 1)
        e2 = jnp.exp2(s2 - jnp.max(s2, axis=-1, keepdims=True))
        if cached:
            ones = lambda x: jnp.concatenate([x, jnp.ones_like(x)], axis=1)
            v, vc = ones(v_ref[:, sl]), ones(vc_ref[:, sl].astype(BF16))
            n1 = _weighted(e1.astype(BF16), v, vc)
            n2 = _weighted(e2.astype(BF16), v, vc)
            o = n1[:, :LANES] * (1.0 / n1[:, LANES:LANES + 1]) - n2[:, :LANES] * (lam / n2[:, LANES:LANES + 1])
        else:
            l1 = jnp.sum(e1, axis=-1, keepdims=True)
            l2 = jnp.sum(e2, axis=-1, keepdims=True)
            p = e1 - e2 * (lam * l1 / l2)
            o = jnp.dot(p.astype(BF16), v_ref[:, sl], preferred_element_type=F32) * (1.0 / l1)
        o_ref[:, sl] = (_rms(o, sub_ref[...]) * (1.0 - LAM_INIT)).astype(BF16)


def _diff_attn(q, k, v, cache, nb, lq, tq, hb, w):
    nq = lq // tq
    vec = _resident((1, A_DH))
    kv = lambda rows: pl.BlockSpec((rows, hb * LANES), lambda b, g, i: (b, g))
    in_specs = [pl.BlockSpec((tq, hb * LANES), lambda b, g, i: (b * nq + i, g)), kv(lq), kv(lq)]
    args = [q, k, v]
    if cache is not None:
        past = cache[0].shape[0] // nb
        in_specs += [kv(past), kv(past)]
        args += list(cache)
    return pl.pallas_call(
        functools.partial(_diff_attn_kernel, hb, cache is not None),
        grid=(nb, A_HEADS // hb, nq),
        in_specs=in_specs + [vec, vec, vec, vec, _resident((1, A_DV))],
        out_specs=pl.BlockSpec((tq, hb * LANES), lambda b, g, i: (b * nq + i, g)),
        out_shape=jax.ShapeDtypeStruct((nb * lq, A_V), BF16),
        compiler_params=_cparams(("arbitrary", "arbitrary", "arbitrary")),
        name="diff_attn_cached" if cache is not None else "diff_attn",
    )(*args, w["a_lambda_q1"], w["a_lambda_k1"], w["a_lambda_q2"], w["a_lambda_k2"], w["a_sub_gain"])


def _mla_attn_kernel(hb, cached, q_ref, k_ref, v_ref, *refs):
    if cached:
        kc_ref, vc_ref, o_ref = refs
    else:
        (o_ref,) = refs
    lk = k_ref.shape[1]
    ck = min(lk, KEY_CHUNK)
    def head_scores(h):
        q = q_ref[h]
        s = [_qk(q, k_ref[h, c * ck:(c + 1) * ck]) for c in range(lk // ck)]
        return s + [_qk(q, kc_ref[h])] if cached else s

    ahead = [head_scores(i) for i in range(min(SCORE_AHEAD, hb))]
    for h in range(hb):
        vs = [v_ref[h, c * ck:(c + 1) * ck] for c in range(lk // ck)] + ([vc_ref[h]] if cached else [])
        s = ahead.pop(0)
        if h + SCORE_AHEAD < hb:
            ahead.append(head_scores(h + SCORE_AHEAD))
        m = functools.reduce(jnp.maximum, [jnp.max(sc, axis=-1, keepdims=True) for sc in s])
        l = 0.0
        o = 0.0
        for sc, v in zip(s, vs):
            e = jnp.exp2(sc - m)
            if cached:
                v = jnp.concatenate([v, jnp.ones_like(v)], axis=1)
            else:
                l = l + jnp.sum(e, axis=-1, keepdims=True)
            o = o + jnp.dot(e.astype(BF16), v, preferred_element_type=F32)
        if cached:
            o, l = o[:, :LANES], o[:, LANES:LANES + 1]
        o_ref[:, h * LANES:(h + 1) * LANES] = (o * (1.0 / l)).astype(BF16)


def _mla_attn(q, k, v, cache, nb, lq, tq, hb):
    nq = lq // tq
    kv = lambda rows, d: pl.BlockSpec((hb, rows, d), lambda b, g, i: (g, b, 0))
    in_specs = [pl.BlockSpec((hb, tq, B_QK), lambda b, g, i: (g, b * nq + i, 0)), kv(lq, B_QK), kv(lq, B_DV)]
    args = [q, k, v]
    if cache is not None:
        past = cache[0].shape[1] // nb
        in_specs += [kv(past, B_QK), kv(past, B_DV)]
        args += list(cache)
    return pl.pallas_call(
        functools.partial(_mla_attn_kernel, hb, cache is not None),
        grid=(nb, B_HEADS // hb, nq),
        in_specs=in_specs,
        out_specs=pl.BlockSpec((tq, hb * LANES), lambda b, g, i: (b * nq + i, g)),
        out_shape=jax.ShapeDtypeStruct((nb * lq, B_HEADS * B_DV), BF16),
        compiler_params=_cparams(("arbitrary", "arbitrary", "arbitrary")),
        name="mla_attn_cached" if cache is not None else "mla_attn",
    )(*args)


def _out_kernel(oa_ref, ob_ref, x_ref, g1_ref, sh2_ref, sc2_ref, n2g_ref, wo_ref, wr_ref,
                x1_ref, h2_ref, aff_ref):
    sub = OUT_SUB
    rows = [pl.ds(r * sub, sub) for r in range(x_ref.shape[0] // sub)]
    os_ = [jnp.dot(oa_ref[r, :], wo_ref[:A_V, :], preferred_element_type=F32)
           + jnp.dot(ob_ref[r, :], wo_ref[A_V:, :], preferred_element_type=F32) for r in rows]
    for r, o in zip(rows, os_):
        x1 = x_ref[r, :] + g1_ref[...] * o
        x1_ref[r, :] = x1
        h2 = _rms(x1, n2g_ref[...]) * (1.0 + sc2_ref[...]) + sh2_ref[...]
        h2_ref[r, :] = h2
        logits = jnp.dot(h2.astype(BF16), wr_ref[...], preferred_element_type=F32)
        lane = lax.broadcasted_iota(jnp.int32, logits.shape, 1)
        logits = jnp.where(lane < N_EXPERTS, logits, -jnp.inf)
        e = jnp.exp(logits - jnp.max(logits, axis=-1, keepdims=True))
        aff_ref[r, :] = e / jnp.sum(e, axis=-1, keepdims=True)


OUT_SUB = 256


def _out_proj(oa, ob, x, mod, tiles_per_mod, w, tag):
    m = x.shape[0]
    tm = 512
    tiles_per_mod = tiles_per_mod * 256 // tm
    row = lambda i: (i, 0)
    modspec = lambda k: pl.BlockSpec((None, 1, D_MODEL), lambda i: (i // tiles_per_mod, 0, k))
    return pl.pallas_call(
        _out_kernel,
        grid=(m // tm,),
        in_specs=[pl.BlockSpec((tm, A_V), row), pl.BlockSpec((tm, A_V), row), pl.BlockSpec((tm, D_MODEL), row),
                  modspec(2), modspec(3), modspec(4), _resident((1, D_MODEL)),
                  _resident((2 * A_V, D_MODEL)), _resident((D_MODEL, LANES))],
        out_specs=[pl.BlockSpec((tm, D_MODEL), row), pl.BlockSpec((tm, D_MODEL), row),
                   pl.BlockSpec((tm, LANES), row)],
        out_shape=[jax.ShapeDtypeStruct((m, D_MODEL), F32), jax.ShapeDtypeStruct((m, D_MODEL), F32),
                   jax.ShapeDtypeStruct((m, LANES), F32)],
        compiler_params=_cparams(("arbitrary",)),
        name="out_proj_" + tag,
    )(oa, ob, x, mod, mod, mod, w["norm2_gain"], w["w_o"], w["w_router"])


ROUTE_BLK = 256
COMBINE_T = 128
GROUP_BITS = 3
GROUP = 1 << GROUP_BITS
PLAN_NG_SHIFT = 10
PLAN_GS_SHIFT = 15


def _route_kernel(cap, aff_ref, st_ref, key_ref, plan_ref, off_ref, slot_t_ref, aff_t_ref):
    n = aff_ref.shape[0]
    nblk = n // ROUTE_BLK
    aff = aff_ref[...]
    lane = lax.broadcasted_iota(jnp.int32, (n, LANES), 1)

    aff_t = aff.T[:N_EXPERTS]

    def narrow(carry):
        lo, hi = carry
        mid = lo + (hi - lo) * 0.5
        mid = jnp.where(mid < hi, mid, lo)
        above = aff_t > mid
        few = jnp.sum(above.astype(jnp.int32), axis=1, keepdims=True) < cap
        up = jnp.min(jnp.where(above, aff_t, jnp.inf), axis=1, keepdims=True)
        dn = jnp.max(jnp.where(above, -jnp.inf, aff_t), axis=1, keepdims=True)
        return jnp.where(few, lo, up), jnp.where(few, dn, hi)

    bounds = (jnp.min(aff_t, axis=1, keepdims=True), jnp.max(aff_t, axis=1, keepdims=True))
    thr_t, _ = lax.while_loop(lambda c: jnp.max((c[0] < c[1]).astype(jnp.int32)) > 0, narrow, bounds)
    thr_sq = jnp.broadcast_to(jnp.concatenate([thr_t, jnp.zeros((LANES - N_EXPERTS, 1), F32)], axis=0), (LANES, LANES))
    thr = thr_sq.T[0:1, :]
    gt = aff > thr
    eq = aff == thr
    need = cap - jnp.sum(gt.astype(jnp.int32), axis=0, keepdims=True)

    r = lax.broadcasted_iota(jnp.int32, (ROUTE_BLK, ROUTE_BLK), 0)
    c = lax.broadcasted_iota(jnp.int32, (ROUTE_BLK, ROUTE_BLK), 1)
    tri = (c < r).astype(BF16)

    def prefix(mask):
        m = mask.astype(F32)
        carry = jnp.zeros((1, LANES), F32)
        outs = []
        for b in range(nblk):
            mb = m[b * ROUTE_BLK:(b + 1) * ROUTE_BLK]
            outs.append(jnp.dot(tri, mb.astype(BF16), preferred_element_type=F32) + carry)
            carry = carry + jnp.sum(mb, axis=0, keepdims=True)
        return jnp.concatenate(outs, axis=0).astype(jnp.int32)

    sel = (gt | (eq & (prefix(eq) < need))) & (lane < N_EXPERTS)
    pos = prefix(sel)
    nt = n // COMBINE_T
    key_ref[...] = pos
    st_ref[...] = jnp.zeros(st_ref.shape, jnp.int32)
    st_ref[0:nt, :] = key_ref[pl.ds(0, nt, stride=COMBINE_T), :]
    st_ref[nt:nt + 1, :] = jnp.full((1, LANES), cap, jnp.int32)
    slot1 = jnp.where(sel, pos + 1, 0)
    key_ref[...] = slot1
    slot_t_ref[...] = slot1.T[:N_EXPERTS]
    aff_t_ref[...] = aff_t

    lo = st_ref[0:nt, :]
    hi = st_ref[pl.ds(1, nt), :]
    a = lo - (lo & (GROUP - 1))
    ng = jnp.where(hi > lo, lax.shift_right_logical(hi - a + (GROUP - 1), GROUP_BITS), 0)
    er = lax.broadcasted_iota(jnp.int32, (LANES, LANES), 0)
    ec = lax.broadcasted_iota(jnp.int32, (LANES, LANES), 1)
    gs = jnp.dot(ng.astype(BF16), (er < ec).astype(BF16), preferred_element_type=F32).astype(jnp.int32)
    elane = lax.broadcasted_iota(jnp.int32, (nt, LANES), 1)
    plan_ref[...] = jnp.where(elane == N_EXPERTS, gs, a | (ng << PLAN_NG_SHIFT) | (gs << PLAN_GS_SHIFT))
    off_ref[...] = gs * GROUP - a


def _compact_kernel(cap, st_s, slot_ref, aff_ref, idx_ref, gate_ref, acc):
    tt = COMBINE_T
    nt = slot_ref.shape[0]
    acc[...] = jnp.zeros(acc.shape, F32)
    win = lax.broadcasted_iota(jnp.int32, (2 * LANES, tt), 0)
    row = lax.broadcasted_iota(jnp.int32, (8, 2 * LANES), 0)
    tok_in_tile = lax.broadcasted_iota(jnp.int32, (1, tt), 1).astype(BF16)
    one = jnp.ones((1, tt), BF16)
    pad = jnp.zeros((3, tt), BF16)

    def tile_body(t, carry):
        slot, af = slot_ref[t], aff_ref[t]
        scale = jnp.where(row == 1, jnp.asarray(t * tt, F32), 1.0)
        for e in range(N_EXPERTS):
            g0 = jnp.minimum(lax.shift_right_logical(st_s[t * LANES + e], 7), cap // LANES - 1)
            rel = slot[e:e + 1, :] - 1 - g0 * LANES
            put = jnp.where(jnp.broadcast_to(rel, (2 * LANES, tt)) == win, 1.0, 0.0).astype(BF16)
            g = af[e:e + 1, :]
            hi = g.astype(BF16)
            r1 = g - hi.astype(F32)
            mid = r1.astype(BF16)
            lo = (r1 - mid.astype(F32)).astype(BF16)
            feat = jnp.concatenate([tok_in_tile, one, hi, mid, lo, pad], axis=0)
            moved = _qk(feat, put) * scale
            acc[e, g0] += moved[:, :LANES]
            acc[e, g0 + 1] += moved[:, LANES:]
        return carry

    lax.fori_loop(0, nt, tile_body, 0)
    groups = range(cap // LANES)
    for e in range(N_EXPERTS):
        idx_ref[e:e + 1, :] = jnp.concatenate(
            [acc[e, g, 0:1, :] + acc[e, g, 1:2, :] for g in groups], axis=1).astype(jnp.int32)
        gate_ref[e:e + 1, :] = jnp.concatenate(
            [acc[e, g, 2:3, :] + acc[e, g, 3:4, :] + acc[e, g, 4:5, :] for g in groups], axis=1)


def _route(aff, tag):
    n = aff.shape[0]
    cap = CAPACITY_FACTOR * n // N_EXPERTS
    nt = n // COMBINE_T
    full = lambda s: pl.BlockSpec(s, lambda i, *_: (0,) * len(s))
    i32 = lambda s: jax.ShapeDtypeStruct(s, jnp.int32)
    assert cap < (1 << PLAN_NG_SHIFT) and COMBINE_T // GROUP + 1 < (1 << (PLAN_GS_SHIFT - PLAN_NG_SHIFT))
    st, key, plan, off, slot_t, aff_t = pl.pallas_call(
        functools.partial(_route_kernel, cap),
        grid=(1,),
        in_specs=[full((n, LANES))],
        out_specs=[full((nt + 8, LANES)), full((n, LANES)), full((nt, LANES)), full((nt, LANES)),
                   full((N_EXPERTS, n)), full((N_EXPERTS, n))],
        out_shape=[i32((nt + 8, LANES)), i32((n, LANES)), i32((nt, LANES)), i32((nt, LANES)),
                   i32((N_EXPERTS, n)), jax.ShapeDtypeStruct((N_EXPERTS, n), F32)],
        compiler_params=_cparams(("arbitrary",)),
        name="route_" + tag,
    )(aff)
    by_tile = lambda a: jnp.swapaxes(a.reshape(N_EXPERTS, nt, COMBINE_T), 0, 1)
    groups = cap // LANES + 1
    idx, gate = pl.pallas_call(
        functools.partial(_compact_kernel, cap),
        grid_spec=pltpu.PrefetchScalarGridSpec(
            num_scalar_prefetch=1, grid=(1,),
            in_specs=[full((nt, N_EXPERTS, COMBINE_T)), full((nt, N_EXPERTS, COMBINE_T))],
            out_specs=[full((N_EXPERTS, cap)), full((N_EXPERTS, cap))],
            scratch_shapes=[pltpu.VMEM((N_EXPERTS, groups, 8, LANES), F32)]),
        out_shape=[i32((N_EXPERTS, cap)), jax.ShapeDtypeStruct((N_EXPERTS, cap), F32)],
        compiler_params=_cparams(("arbitrary",)),
        name="compact_" + tag,
    )(st.reshape(-1), by_tile(slot_t), by_tile(aff_t))
    return idx, gate, key, plan, off


FF_CHUNK = 256


def _ffn_kernel(cap, nf, per, idx_s, hp_hbm, hs_hbm, gate_ref, w1_ref, w3_ref, w2_ref, yp_hbm, ys_hbm,
                xg, xe, acc, gsem, osem):
    e = pl.program_id(0)
    f = pl.program_id(1)
    ne = pl.num_programs(0)
    srcs = (hp_hbm, hs_hbm)
    dsts = (yp_hbm, ys_hbm)
    slot = lax.rem(e, 2)

    def issue_part(ex, part, sl, live):
        tail = cap - (nf - 1) * per
        live_tail = live & (part < nf - 1)
        j0 = pl.multiple_of(part * per, GROUP) if per % GROUP == 0 else part * per
        for p in range(2):
            for i in range(per):
                j = j0 + i
                jc = j if i < tail else jnp.minimum(j, cap - 1)
                tok = idx_s[(p * N_EXPERTS + ex) * cap + jc]

                @pl.when(live if i < tail else live_tail)
                def _():
                    pltpu.make_async_copy(srcs[p].at[pl.ds(tok, 1), :], xg.at[sl, pl.ds(p * cap + jc, 1), :],
                                          gsem.at[sl]).start()

    def out_copy(p, ex):
        return pltpu.make_async_copy(acc.at[pl.ds(p * cap, cap), :],
                                     dsts[p].at[pl.ds(pl.multiple_of(ex * cap, cap), cap), :], osem)

    @pl.when((e == 0) & (f == 0))
    def _prologue():
        def part_body(part, carry):
            issue_part(0, part, 0, True)
            return carry
        lax.fori_loop(0, nf, part_body, 0)

    @pl.when(f == 0)
    def _start_expert():
        for p in range(2):
            pltpu.make_async_copy(srcs[p].at[pl.ds(0, cap), :], xg.at[slot, pl.ds(p * cap, cap), :],
                                  gsem.at[slot]).wait()
        xe[...] = xg[slot].astype(BF16)

    x = xe[...]
    a = jnp.dot(x, w1_ref[...].astype(BF16), preferred_element_type=F32)
    b = jnp.dot(x, w3_ref[...].astype(BF16), preferred_element_type=F32)
    hid = ((a * jax.nn.sigmoid(a)) * b).astype(BF16)

    issue_part(jnp.minimum(e + 1, ne - 1), f, 1 - slot, e + 1 < ne)

    def down():
        return jnp.dot(hid, w2_ref[...].astype(BF16), preferred_element_type=F32)

    @pl.when(f == 0)
    def _first():
        @pl.when(e > 0)
        def _():
            for p in range(2):
                out_copy(p, e - 1).wait()
        acc[...] = down()

    @pl.when((f > 0) & (f < nf - 1))
    def _middle():
        acc[...] += down()

    @pl.when(f == nf - 1)
    def _last():
        acc[...] = (acc[...] + down()) * gate_ref[...]
        for p in range(2):
            out_copy(p, e).start()

        @pl.when(e == ne - 1)
        def _():
            for p in range(2):
                out_copy(p, e).wait()


def _ffn(idx_flat, h2p, h2s, gate, w1, w3, w2, cap):
    nf = EXPERT_FF // FF_CHUNK
    per = -(-cap // nf)
    grid_spec = pltpu.PrefetchScalarGridSpec(
        num_scalar_prefetch=1,
        grid=(N_EXPERTS, nf),
        in_specs=[pl.BlockSpec(memory_space=pl.ANY), pl.BlockSpec(memory_space=pl.ANY),
                  pl.BlockSpec((None, 2 * cap, 1), lambda e, f, *_: (e, 0, 0)),
                  pl.BlockSpec((None, D_MODEL, FF_CHUNK), lambda e, f, *_: (e, 0, f)),
                  pl.BlockSpec((None, D_MODEL, FF_CHUNK), lambda e, f, *_: (e, 0, f)),
                  pl.BlockSpec((None, FF_CHUNK, D_MODEL), lambda e, f, *_: (e, f, 0))],
        out_specs=[pl.BlockSpec(memory_space=pl.ANY), pl.BlockSpec(memory_space=pl.ANY)],
        scratch_shapes=[pltpu.VMEM((2, 2 * cap, D_MODEL), F32), pltpu.VMEM((2 * cap, D_MODEL), BF16),
                        pltpu.VMEM((2 * cap, D_MODEL), F32), pltpu.SemaphoreType.DMA((2,)),
                        pltpu.SemaphoreType.DMA(())])
    out = jax.ShapeDtypeStruct((N_EXPERTS * cap, D_MODEL), F32)
    return pl.pallas_call(
        functools.partial(_ffn_kernel, cap, nf, per),
        grid_spec=grid_spec, out_shape=[out, out],
        compiler_params=_cparams(("arbitrary", "arbitrary")),
        name="expert_ffn",
    )(idx_flat, h2p, h2s, gate, w1, w3, w2)


WAIT_GROUPS = 8
STAGE_ROWS = -(-(N_EXPERTS * (COMBINE_T // GROUP + 1) * GROUP) // MXU_DIM) * MXU_DIM


def _combine_kernel(cap, plan_s, ye_hbm, x1_ref, g2_ref, key_ref, off_ref, o_ref, stage, acc, sem):
    t = pl.program_id(0)
    nt = pl.num_programs(0)
    tt = COMBINE_T
    slot = lax.rem(t, 2)

    def issue_tile(tile, sl):
        for e in range(N_EXPERTS):
            w = plan_s[tile * LANES + e]
            a = w & ((1 << PLAN_NG_SHIFT) - 1)
            ng = lax.shift_right_logical(w, PLAN_NG_SHIFT) & ((1 << (PLAN_GS_SHIFT - PLAN_NG_SHIFT)) - 1)
            gs = lax.shift_right_logical(w, PLAN_GS_SHIFT)

            def issue(g, carry):
                src = pl.multiple_of(e * cap + a + g * GROUP, GROUP)
                dst = pl.multiple_of((gs + g) * GROUP, GROUP)
                pltpu.make_async_copy(ye_hbm.at[pl.ds(src, GROUP), :], stage.at[sl, pl.ds(dst, GROUP), :],
                                      sem.at[sl]).start()
                return carry
            lax.fori_loop(0, ng, issue, 0)

    def wait_tile(tile, sl):
        total = plan_s[tile * LANES + N_EXPERTS]

        def wait_groups(n):
            def body(i, carry):
                pltpu.make_async_copy(ye_hbm.at[pl.ds(0, n * GROUP), :], stage.at[sl, pl.ds(0, n * GROUP), :],
                                      sem.at[sl]).wait()
                return carry
            return body
        lax.fori_loop(0, total // WAIT_GROUPS, wait_groups(WAIT_GROUPS), 0)
        lax.fori_loop(0, lax.rem(total, WAIT_GROUPS), wait_groups(1), 0)

    @pl.when(t == 0)
    def _init():
        stage[...] = jnp.zeros(stage.shape, F32)
        issue_tile(0, 0)

    @pl.when(t + 1 < nt)
    def _prefetch():
        issue_tile(t + 1, 1 - slot)

    wait_tile(t, slot)

    slot1 = key_ref[...]
    srow = jnp.where(slot1 > 0, slot1 - 1 + off_ref[pl.ds(t, 1), :], -1)
    cio = lax.broadcasted_iota(jnp.int32, (tt, MXU_DIM), 1)
    acc[...] = jnp.zeros(acc.shape, F32)

    def add_chunk(kc, carry):
        base = pl.multiple_of(kc * MXU_DIM, MXU_DIM)
        rel = srow - base
        pick = jnp.zeros((tt, MXU_DIM), F32)
        for e in range(N_EXPERTS):
            pick = jnp.where(jnp.broadcast_to(rel[:, e:e + 1], (tt, MXU_DIM)) == cio, 1.0, pick)
        rows = stage[slot, pl.ds(base, MXU_DIM), :]
        acc[...] += jnp.dot(pick.astype(BF16), rows.astype(BF16), preferred_element_type=F32)
        return carry

    total_rows = plan_s[t * LANES + N_EXPERTS] * GROUP
    lax.fori_loop(0, lax.div(total_rows + MXU_DIM - 1, MXU_DIM), add_chunk, 0)
    o_ref[...] = x1_ref[...] + g2_ref[...] * acc[...]


def _combine(plan, off, key, ye, x1, mod, tiles_per_mod, cap, tag):
    n = x1.shape[0]
    tt = COMBINE_T
    nt = n // tt
    grid_spec = pltpu.PrefetchScalarGridSpec(
        num_scalar_prefetch=1,
        grid=(nt,),
        in_specs=[pl.BlockSpec(memory_space=pl.ANY),
                  pl.BlockSpec((tt, D_MODEL), lambda i, *_: (i, 0)),
                  pl.BlockSpec((None, 1, D_MODEL), lambda i, *_: (i // tiles_per_mod, 0, 5)),
                  pl.BlockSpec((tt, LANES), lambda i, *_: (i, 0)),
                  pl.BlockSpec((nt, LANES), lambda i, *_: (0, 0))],
        out_specs=pl.BlockSpec((tt, D_MODEL), lambda i, *_: (i, 0)),
        scratch_shapes=[pltpu.VMEM((2, STAGE_ROWS, D_MODEL), F32), pltpu.VMEM((tt, D_MODEL), F32),
                        pltpu.SemaphoreType.DMA((2,))])
    return pl.pallas_call(
        functools.partial(_combine_kernel, cap),
        grid_spec=grid_spec, out_shape=jax.ShapeDtypeStruct((n, D_MODEL), F32),
        compiler_params=_cparams(("arbitrary",)),
        name="combine_" + tag,
    )(plan.reshape(-1), ye, x1, mod, key, off)


def _prep_weights(norm1_gain, norm2_gain, w_in, a_q_gain, a_k_gain, a_lambda_q1, a_lambda_k1, a_lambda_q2,
                  a_lambda_k2, a_sub_gain, q_a_gain, w_uq, b_qn_gain, b_qr_gain, kv_a_gain, w_ukv, b_kn_gain,
                  b_kr_gain, w_o, w_router):
    tile = lambda g, width: jnp.tile(g.reshape(1, -1), (1, width // g.shape[-1]))
    wuq = w_uq[0].reshape(Q_RANK, B_HEADS, B_QK)
    wuq = jnp.concatenate([wuq[:, :, :B_NOPE].reshape(Q_RANK, -1), wuq[:, :, B_NOPE:].reshape(Q_RANK, -1)], axis=1)
    return {
        "norm1_gain": norm1_gain.reshape(1, -1), "norm2_gain": norm2_gain.reshape(1, -1),
        "w_in": w_in[0].astype(BF16),
        "a_q_gain": tile(a_q_gain, MXU_DIM), "a_k_gain": tile(a_k_gain, MXU_DIM),
        "a_lambda_q1": a_lambda_q1.reshape(1, -1), "a_lambda_k1": a_lambda_k1.reshape(1, -1),
        "a_lambda_q2": a_lambda_q2.reshape(1, -1), "a_lambda_k2": a_lambda_k2.reshape(1, -1),
        "a_sub_gain": a_sub_gain.reshape(1, -1), "q_a_gain": q_a_gain.reshape(1, -1),
        "w_uq": wuq.astype(BF16), "b_qn_gain": b_qn_gain.reshape(1, -1), "b_qr_gain": tile(b_qr_gain, MXU_DIM),
        "kv_a_gain": kv_a_gain.reshape(1, -1), "w_ukv": w_ukv[0].astype(BF16),
        "b_kn_gain": b_kn_gain.reshape(1, -1), "b_kr_gain": tile(b_kr_gain, LANES),
        "w_o": w_o[0].astype(BF16),
        "w_router": jnp.pad(w_router[0], ((0, 0), (0, LANES - N_EXPERTS))).astype(BF16),
        "seg64": _seg_matrix(A_DH),
    }


def kernel(x_prompt, x_sample, cache_diff_k, cache_diff_v, cache_mla_ckv, cache_mla_krope, c, c_ctx, w_ada, b_ada, norm1_gain, norm2_gain, w_in, a_q_gain, a_k_gain, a_lambda_q1, a_lambda_k1, a_lambda_q2, a_lambda_k2, a_sub_gain, q_a_gain, w_uq, b_qn_gain, b_qr_gain, kv_a_gain, w_ukv, b_kn_gain, b_kr_gain, w_o, w_router, w_exp1, w_exp3, w_exp2):
    nbp, lp, _ = x_prompt.shape
    nbs, ls, _ = x_sample.shape
    past = cache_diff_k.shape[2]
    w = _prep_weights(norm1_gain, norm2_gain, w_in, a_q_gain, a_k_gain, a_lambda_q1, a_lambda_k1, a_lambda_q2,
                      a_lambda_k2, a_sub_gain, q_a_gain, w_uq, b_qn_gain, b_qr_gain, kv_a_gain, w_ukv, b_kn_gain,
                      b_kr_gain, w_o, w_router)

    cvec = jnp.concatenate([c_ctx[None], c, jnp.zeros((8 - 1 - nbs, D_MODEL), F32)], axis=0)
    mod = _ada(cvec, w_ada[0], b_ada)
    mod_p = mod[0:1].reshape(1, 1, -1)
    mod_s = mod[1:1 + nbs].reshape(nbs, 1, -1)

    xp = x_prompt.reshape(nbp * lp, D_MODEL)
    xs = x_sample.reshape(nbs * ls, D_MODEL)
    tm = 256
    assert lp == IN_SUB, "the prompt projection writes one batch of new_diff_k per row tile"
    aq_p, ak_p, av_p, qm_p, km_p, bv_p, ak32, av32, ckv32, kr32 = _in_proj(xp, mod_p, False, nbp * lp, w, None)
    aq_s, ak_s, av_s, qm_s, km_s, bv_s = _in_proj(xs, mod_s, True, ls, w, _rope_tables(ls))
    km_c, bv_c = _cache_kv(cache_mla_ckv.reshape(nbs * past, KV_RANK), cache_mla_krope.reshape(nbs * past, B_ROPE), w)

    cache_a = (cache_diff_k.reshape(nbs * past, A_QK), cache_diff_v.reshape(nbs * past, A_V))

    oa_p = _diff_attn(aq_p, ak_p, av_p, None, nbp, lp, lp, A_HEADS, w)
    ob_p = _mla_attn(qm_p, km_p, bv_p, None, nbp, lp, lp, B_HEADS)
    oa_s = _diff_attn(aq_s, ak_s, av_s, cache_a, nbs, ls, 256, 4, w)
    ob_s = _mla_attn(qm_s, km_s, bv_s, (km_c, bv_c), nbs, ls, 256, 4)

    x1_p, h2_p, aff_p = _out_proj(oa_p, ob_p, xp, mod_p, nbp * lp // tm, w, "prompt")
    x1_s, h2_s, aff_s = _out_proj(oa_s, ob_s, xs, mod_s, ls // tm, w, "sample")

    idx_p, gate_p, key_p, plan_p, off_p = _route(aff_p, "prompt")
    idx_s, gate_s, key_s, plan_s, off_s = _route(aff_s, "sample")
    cap = idx_p.shape[1]
    idx_flat = jnp.stack([idx_p, idx_s]).reshape(-1)
    gate = jnp.concatenate([gate_p, gate_s], axis=1)[..., None]
    ye_p, ye_s = _ffn(idx_flat, h2_p, h2_s, gate, w_exp1[0], w_exp3[0], w_exp2[0], cap)

    y_p = _combine(plan_p, off_p, key_p, ye_p, x1_p, mod_p, nbp * lp // COMBINE_T, cap, "prompt")
    y_s = _combine(plan_s, off_s, key_s, ye_s, x1_s, mod_s, ls // COMBINE_T, cap, "sample")

    return (y_p.reshape(nbp, lp, D_MODEL), y_s.reshape(nbs, ls, D_MODEL),
            ak32, av32.reshape(nbp, 1, lp, A_HEADS, A_DV),
            ckv32.reshape(nbp, 1, lp, KV_RANK), kr32.reshape(nbp, 1, lp, B_ROPE))
```

```python
import functools
import math

import numpy as np
import jax
import jax.numpy as jnp
from jax import lax
from jax.experimental import pallas as pl
from jax.experimental.pallas import tpu as pltpu

F32 = jnp.float32
BF16 = jnp.bfloat16

D_MODEL = 2048
GRID_W = 64
A_HEADS = 8
A_DH = 64
A_DV = 128
B_HEADS = 8
B_NOPE = 128
B_ROPE = 64
B_DV = 128
Q_RANK = 512
KV_RANK = 256
N_EXPERTS = 16
EXPERT_FF = 1536
CAPACITY_FACTOR = 2
ROPE_BASE = 10000.0
EPS = 1e-6
LAM_INIT = 0.8 - 0.6 * math.exp(-0.3 * 0)
LOG2E = math.log2(math.e)

A_QK = A_HEADS * 2 * A_DH
A_V = A_HEADS * A_DV
B_QK = B_NOPE + B_ROPE
IN_COLS = 2 * A_QK + A_V + Q_RANK + KV_RANK + B_ROPE
LANES = 128
MXU_DIM = 256
VMEM_LIMIT = 56 * 1024 * 1024


def _cparams(sem):
    return pltpu.CompilerParams(dimension_semantics=sem, vmem_limit_bytes=VMEM_LIMIT)


def _resident(shape):
    nd = len(shape)
    return pl.BlockSpec(shape, lambda *_: (0,) * nd, pipeline_mode=pl.Buffered(1))


def _rms(x, gain):
    return x * lax.rsqrt(jnp.mean(x * x, axis=-1, keepdims=True) + EPS) * gain


def _seg_rms(y, segmat, gain):
    y2 = (y * y).astype(BF16)
    outs = []
    for c in range(y.shape[1] // MXU_DIM):
        sl = slice(c * MXU_DIM, (c + 1) * MXU_DIM)
        ms = jnp.dot(y2[:, sl], segmat, preferred_element_type=F32)
        outs.append(y[:, sl] * lax.rsqrt(ms + EPS) * gain)
    return jnp.concatenate(outs, axis=1)


def _rope(y, cos, sin):
    outs = []
    for c in range(y.shape[1] // LANES):
        yc = y[:, c * LANES:(c + 1) * LANES]
        lane = lax.broadcasted_iota(jnp.int32, yc.shape, 1)
        partner = jnp.where((lane & 16) == 0, pltpu.roll(yc, LANES - 16, 1), pltpu.roll(yc, 16, 1))
        outs.append(yc * cos + partner * sin)
    return outs[0] if len(outs) == 1 else jnp.concatenate(outs, axis=1)


def _rope_tables(n_tok):
    t = np.arange(n_tok)
    row, col = t // GRID_W, t % GRID_W
    nf = A_DH // 4
    inv = ROPE_BASE ** (-np.arange(nf, dtype=np.float64) * 2.0 / (A_DH // 2))
    lane = np.arange(A_DH)
    pos = np.where(lane[None, :] < A_DH // 2, row[:, None], col[:, None]).astype(np.float64)
    ang = pos * inv[lane % nf][None, :]
    sign = np.where((lane % (2 * nf)) < nf, -1.0, 1.0)[None, :]
    cos = np.tile(np.cos(ang), (1, LANES // A_DH)).astype(np.float32)
    sin = np.tile(np.sin(ang) * sign, (1, LANES // A_DH)).astype(np.float32)
    return jnp.asarray(cos), jnp.asarray(sin)


def _seg_matrix(width):
    i = np.arange(MXU_DIM)
    return jnp.asarray(((i[:, None] // width) == (i[None, :] // width)).astype(np.float32) / width, dtype=BF16)


def _ada_kernel(c_ref, w_ref, b_ref, o_ref):
    c = c_ref[...]
    s = c * jax.nn.sigmoid(c)
    o_ref[...] = jnp.dot(s.astype(BF16), w_ref[...].astype(BF16), preferred_element_type=F32) + b_ref[...]


def _ada(cvec, w_ada, b_ada):
    tn = 1024
    n = w_ada.shape[1]
    return pl.pallas_call(
        _ada_kernel,
        grid=(n // tn,),
        in_specs=[pl.BlockSpec((8, D_MODEL), lambda j: (0, 0)),
                  pl.BlockSpec((D_MODEL, tn), lambda j: (0, j)),
                  pl.BlockSpec((1, tn), lambda j: (0, j))],
        out_specs=pl.BlockSpec((8, tn), lambda j: (0, j)),
        out_shape=jax.ShapeDtypeStruct((8, n), F32),
        compiler_params=_cparams(("arbitrary",)),
        name="ada_mod",
    )(cvec, w_ada, b_ada)


def _expand_kv(ckv_n, kr, wukv_ref, kng, km_ref, bv_ref):
    kv = jnp.dot(ckv_n.astype(BF16), wukv_ref[...], preferred_element_type=F32)
    krb = kr[:, :B_ROPE].astype(BF16)
    for h in range(B_HEADS):
        base = h * (B_NOPE + B_DV)
        kn = _rms(kv[:, base:base + B_NOPE], kng)
        km_ref[h] = jnp.concatenate([kn.astype(BF16), krb], axis=1)
        bv_ref[h] = kv[:, base + B_NOPE:base + B_NOPE + B_DV].astype(BF16)


def _in_kernel(sample, *refs):
    (x_ref, sh_ref, sc_ref, n1g_ref, win_ref, aqg_ref, akg_ref, qag_ref, wuq_ref, qng_ref, qrg_ref,
     kvag_ref, wukv_ref, kng_ref, krg_ref, seg_ref) = refs[:16]
    refs = refs[16:]
    if sample:
        cos_ref, sin_ref = refs[:2]
        refs = refs[2:]
        cos, sin = cos_ref[...], sin_ref[...]
    aq_ref, ak_ref, av_ref, qm_ref, km_ref, bv_ref = refs[:6]
    refs = refs[6:]

    x = x_ref[...]
    h = _rms(x, n1g_ref[...]) * (1.0 + sc_ref[...]) + sh_ref[...]
    hb = h.astype(BF16)

    def proj(c0, c1):
        return jnp.dot(hb, win_ref[:, c0:c1], preferred_element_type=F32)

    seg = seg_ref[...]
    c0 = 2 * A_QK + A_V
    y_q = proj(c0, c0 + Q_RANK)
    y_ckv = proj(c0 + Q_RANK, c0 + Q_RANK + KV_RANK)
    kr64 = proj(c0 + Q_RANK + KV_RANK, c0 + Q_RANK + KV_RANK + B_ROPE)
    q_lat = _rms(y_q, qag_ref[...])
    qb = jnp.dot(q_lat.astype(BF16), wuq_ref[...], preferred_element_type=F32)
    ckv = _rms(y_ckv, kvag_ref[...])
    kr_raw = jnp.concatenate([kr64, jnp.zeros_like(kr64)], axis=1)
    kr = kr_raw * lax.rsqrt(jnp.sum(kr_raw * kr_raw, axis=-1, keepdims=True) * (1.0 / B_ROPE) + EPS) * krg_ref[...]
    if sample:
        kr = _rope(kr, cos, sin)
    _expand_kv(ckv, kr, wukv_ref, kng_ref[...], km_ref, bv_ref)
    y_aq = proj(0, A_QK)
    qr = _seg_rms(qb[:, B_HEADS * B_NOPE:], seg, qrg_ref[...])
    if sample:
        qr = _rope(qr, cos, sin)
    scale = B_QK ** -0.5 * LOG2E
    for hd in range(B_HEADS):
        qn = _rms(qb[:, hd * B_NOPE:(hd + 1) * B_NOPE], qng_ref[...])
        qm_ref[hd] = jnp.concatenate(
            [(qn * scale).astype(BF16), (qr[:, hd * B_ROPE:(hd + 1) * B_ROPE] * scale).astype(BF16)], axis=1)

    y_ak = proj(A_QK, 2 * A_QK)
    aq = _seg_rms(y_aq, seg, aqg_ref[...])
    if sample:
        aq = _rope(aq, cos, sin)
    aq_ref[...] = (aq * (A_DH ** -0.5 * LOG2E)).astype(BF16)
    av = proj(2 * A_QK, 2 * A_QK + A_V)
    ak = _seg_rms(y_ak, seg, akg_ref[...])
    if sample:
        ak = _rope(ak, cos, sin)
    ak_ref[...] = ak.astype(BF16)
    av_ref[...] = av.astype(BF16)
    if not sample:
        ak32_ref, av32_ref, ckv32_ref, kr32_ref = refs
        ak32_ref[...] = ak.reshape(ak32_ref.shape)
        av32_ref[...] = av
        ckv32_ref[...] = ckv
        kr32_ref[...] = kr[:, :B_ROPE]


IN_SUB = 256


def _in_proj(x, mod, sample, rows_per_mod, w, tables):
    m = x.shape[0]
    tm = IN_SUB
    nt = m // tm
    tiles_per_mod = rows_per_mod // tm
    row = lambda i: (i, 0)
    modspec = lambda k: pl.BlockSpec((None, 1, D_MODEL), lambda i: (i // tiles_per_mod, 0, k))
    in_specs = [pl.BlockSpec((tm, D_MODEL), row), modspec(0), modspec(1), _resident((1, D_MODEL)),
                _resident((D_MODEL, IN_COLS)), _resident((1, MXU_DIM)), _resident((1, MXU_DIM)),
                _resident((1, Q_RANK)), _resident((Q_RANK, B_HEADS * B_QK)), _resident((1, B_NOPE)),
                _resident((1, MXU_DIM)), _resident((1, KV_RANK)), _resident((KV_RANK, B_HEADS * (B_NOPE + B_DV))),
                _resident((1, B_NOPE)), _resident((1, LANES)), _resident((MXU_DIM, MXU_DIM))]
    args = [x, mod, mod, w["norm1_gain"], w["w_in"], w["a_q_gain"], w["a_k_gain"], w["q_a_gain"], w["w_uq"],
            w["b_qn_gain"], w["b_qr_gain"], w["kv_a_gain"], w["w_ukv"], w["b_kn_gain"], w["b_kr_gain"], w["seg64"]]
    if sample:
        per = tables[0].shape[0] // tm
        in_specs += [pl.BlockSpec((tm, LANES), lambda i: (i % per, 0))] * 2
        args += list(tables)
    out_shape = [jax.ShapeDtypeStruct((m, A_QK), BF16), jax.ShapeDtypeStruct((m, A_QK), BF16),
                 jax.ShapeDtypeStruct((m, A_V), BF16), jax.ShapeDtypeStruct((B_HEADS, m, B_QK), BF16),
                 jax.ShapeDtypeStruct((B_HEADS, m, B_QK), BF16), jax.ShapeDtypeStruct((B_HEADS, m, B_DV), BF16)]
    hspec = lambda d: pl.BlockSpec((B_HEADS, tm, d), lambda i: (0, i, 0))
    out_specs = [pl.BlockSpec((tm, A_QK), row), pl.BlockSpec((tm, A_QK), row), pl.BlockSpec((tm, A_V), row),
                 hspec(B_QK), hspec(B_QK), hspec(B_DV)]
    if not sample:
        out_shape += [jax.ShapeDtypeStruct((nt, 1, tm, A_HEADS, 2, A_DH), F32), jax.ShapeDtypeStruct((m, A_V), F32),
                      jax.ShapeDtypeStruct((m, KV_RANK), F32), jax.ShapeDtypeStruct((m, B_ROPE), F32)]
        out_specs += [pl.BlockSpec((None, None, tm, A_HEADS, 2, A_DH), lambda i: (i, 0, 0, 0, 0, 0)),
                      pl.BlockSpec((tm, A_V), row),
                      pl.BlockSpec((tm, KV_RANK), row), pl.BlockSpec((tm, B_ROPE), row)]
    return pl.pallas_call(
        functools.partial(_in_kernel, sample),
        grid=(nt,), in_specs=in_specs, out_specs=out_specs, out_shape=out_shape,
        compiler_params=_cparams(("arbitrary",)),
        name="in_proj_sample" if sample else "in_proj_prompt",
    )(*args)


def _cache_kv_kernel(ckv_ref, kr_ref, wukv_ref, kng_ref, km_ref, bv_ref):
    _expand_kv(ckv_ref[...], kr_ref[...], wukv_ref, kng_ref[...], km_ref, bv_ref)


def _cache_kv(ckv, kr, w):
    m = ckv.shape[0]
    return pl.pallas_call(
        _cache_kv_kernel,
        grid=(1,),
        in_specs=[_resident((m, KV_RANK)), _resident((m, B_ROPE)),
                  _resident((KV_RANK, B_HEADS * (B_NOPE + B_DV))), _resident((1, B_NOPE))],
        out_specs=[pl.BlockSpec((B_HEADS, m, B_QK), lambda i: (0, 0, 0)),
                   pl.BlockSpec((B_HEADS, m, B_DV), lambda i: (0, 0, 0))],
        out_shape=[jax.ShapeDtypeStruct((B_HEADS, m, B_QK), BF16), jax.ShapeDtypeStruct((B_HEADS, m, B_DV), BF16)],
        compiler_params=_cparams(("arbitrary",)),
        name="cache_kv",
    )(ckv, kr, w["w_ukv"], w["b_kn_gain"])


def _qk(q, k):
    return lax.dot_general(q, k, (((1,), (1,)), ((), ())), preferred_element_type=F32)


KEY_CHUNK = 512
SCORE_AHEAD = 1


def _scores(q, k, kc):
    s = _qk(q, k)
    return s if kc is None else jnp.concatenate([s, _qk(q, kc)], axis=1)


def _weighted(p, v, vc):
    if vc is None:
        return jnp.dot(p, v, preferred_element_type=F32)
    lk = v.shape[0]
    return (jnp.dot(p[:, :lk], v, preferred_element_type=F32)
            + jnp.dot(p[:, lk:], vc, preferred_element_type=F32))


def _diff_attn_kernel(hb, cached, q_ref, k_ref, v_ref, *refs):
    if cached:
        kc_ref, vc_ref = refs[:2]
        refs = refs[2:]
    lq1, lk1, lq2, lk2, sub_ref, o_ref = refs
    lam = (jnp.exp(jnp.sum(lq1[...] * lk1[...], axis=-1, keepdims=True))
           - jnp.exp(jnp.sum(lq2[...] * lk2[...], axis=-1, keepdims=True)) + LAM_INIT)
    def scores(h, comp):
        sl = slice(h * LANES, (h + 1) * LANES)
        q, k = q_ref[:, sl], k_ref[:, sl]
        kc = kc_ref[:, sl].astype(BF16) if cached else None
        lane = lax.broadcasted_iota(jnp.int32, q.shape, 1)
        keep = (lane < A_DH) if comp == 0 else (lane >= A_DH)
        return _scores(jnp.where(keep, q, jnp.zeros_like(q)), k, kc)

    s1_next, s2_next = scores(0, 0), scores(0, 1)
    for h in range(hb):
        sl = slice(h * LANES, (h + 1) * LANES)
        s1, s2 = s1_next, s2_next
        if h + 1 < hb:
            s1_next = scores(h + 1, 0)
        e1 = jnp.exp2(s1 - jnp.max(s1, axis=-1, keepdims=True))
        if h + 1 < hb:
            s2_next = scores(h + 1, 1)
        e2 = jnp.exp2(s2 - jnp.max(s2, axis=-1, keepdims=True))
        if cached:
            ones = lambda x: jnp.concatenate([x, jnp.ones_like(x)], axis=1)
            v, vc = ones(v_ref[:, sl]), ones(vc_ref[:, sl].astype(BF16))
            n1 = _weighted(e1.astype(BF16), v, vc)
            n2 = _weighted(e2.astype(BF16), v, vc)
            o = n1[:, :LANES] * (1.0 / n1[:, LANES:LANES + 1]) - n2[:, :LANES] * (lam / n2[:, LANES:LANES + 1])
        else:
            l1 = jnp.sum(e1, axis=-1, keepdims=True)
            l2 = jnp.sum(e2, axis=-1, keepdims=True)
            p = e1 - e2 * (lam * l1 / l2)
            o = jnp.dot(p.astype(BF16), v_ref[:, sl], preferred_element_type=F32) * (1.0 / l1)
        o_ref[:, sl] = (_rms(o, sub_ref[...]) * (1.0 - LAM_INIT)).astype(BF16)


def _diff_attn(q, k, v, cache, nb, lq, tq, hb, w):
    nq = lq // tq
    vec = _resident((1, A_DH))
    kv = lambda rows: pl.BlockSpec((rows, hb * LANES), lambda b, g, i: (b, g))
    in_specs = [pl.BlockSpec((tq, hb * LANES), lambda b, g, i: (b * nq + i, g)), kv(lq), kv(lq)]
    args = [q, k, v]
    if cache is not None:
        past = cache[0].shape[0] // nb
        in_specs += [kv(past), kv(past)]
        args += list(cache)
    return pl.pallas_call(
        functools.partial(_diff_attn_kernel, hb, cache is not None),
        grid=(nb, A_HEADS // hb, nq),
        in_specs=in_specs + [vec, vec, vec, vec, _resident((1, A_DV))],
        out_specs=pl.BlockSpec((tq, hb * LANES), lambda b, g, i: (b * nq + i, g)),
        out_shape=jax.ShapeDtypeStruct((nb * lq, A_V), BF16),
        compiler_params=_cparams(("arbitrary", "arbitrary", "arbitrary")),
        name="diff_attn_cached" if cache is not None else "diff_attn",
    )(*args, w["a_lambda_q1"], w["a_lambda_k1"], w["a_lambda_q2"], w["a_lambda_k2"], w["a_sub_gain"])


def _mla_attn_kernel(hb, cached, q_ref, k_ref, v_ref, *refs):
    if cached:
        kc_ref, vc_ref, o_ref = refs
    else:
        (o_ref,) = refs
    lk = k_ref.shape[1]
    ck = min(lk, KEY_CHUNK)
    assert lk % ck == 0, "score chunks must tile the key axis"

    def head_scores(h):
        q = q_ref[h]
        s = [_qk(q, k_ref[h, c * ck:(c + 1) * ck]) for c in range(lk // ck)]
        return s + [_qk(q, kc_ref[h])] if cached else s

    ahead = [head_scores(i) for i in range(min(SCORE_AHEAD, hb))]
    for h in range(hb):
        vs = [v_ref[h, c * ck:(c + 1) * ck] for c in range(lk // ck)] + ([vc_ref[h]] if cached else [])
        s = ahead.pop(0)
        if h + SCORE_AHEAD < hb:
            ahead.append(head_scores(h + SCORE_AHEAD))
        m = functools.reduce(jnp.maximum, [jnp.max(sc, axis=-1, keepdims=True) for sc in s])
        l = 0.0
        o = 0.0
        for sc, v in zip(s, vs):
            e = jnp.exp2(sc - m)
            if cached:
                v = jnp.concatenate([v, jnp.ones_like(v)], axis=1)
            else:
                l = l + jnp.sum(e, axis=-1, keepdims=True)
            o = o + jnp.dot(e.astype(BF16), v, preferred_element_type=F32)
        if cached:
            o, l = o[:, :LANES], o[:, LANES:LANES + 1]
        o_ref[:, h * LANES:(h + 1) * LANES] = (o * (1.0 / l)).astype(BF16)


def _mla_attn(q, k, v, cache, nb, lq, tq, hb):
    nq = lq // tq
    kv = lambda rows, d: pl.BlockSpec((hb, rows, d), lambda b, g, i: (g, b, 0))
    in_specs = [pl.BlockSpec((hb, tq, B_QK), lambda b, g, i: (g, b * nq + i, 0)), kv(lq, B_QK), kv(lq, B_DV)]
    args = [q, k, v]
    if cache is not None:
        past = cache[0].shape[1] // nb
        in_specs += [kv(past, B_QK), kv(past, B_DV)]
        args += list(cache)
    return pl.pallas_call(
        functools.partial(_mla_attn_kernel, hb, cache is not None),
        grid=(nb, B_HEADS // hb, nq),
        in_specs=in_specs,
        out_specs=pl.BlockSpec((tq, hb * LANES), lambda b, g, i: (b * nq + i, g)),
        out_shape=jax.ShapeDtypeStruct((nb * lq, B_HEADS * B_DV), BF16),
        compiler_params=_cparams(("arbitrary", "arbitrary", "arbitrary")),
        name="mla_attn_cached" if cache is not None else "mla_attn",
    )(*args)


def _out_kernel(oa_ref, ob_ref, x_ref, g1_ref, sh2_ref, sc2_ref, n2g_ref, wo_ref, wr_ref,
                x1_ref, h2_ref, aff_ref):
    sub = OUT_SUB
    rows = [pl.ds(r * sub, sub) for r in range(x_ref.shape[0] // sub)]
    os_ = [jnp.dot(oa_ref[r, :], wo_ref[:A_V, :], preferred_element_type=F32)
           + jnp.dot(ob_ref[r, :], wo_ref[A_V:, :], preferred_element_type=F32) for r in rows]
    for r, o in zip(rows, os_):
        x1 = x_ref[r, :] + g1_ref[...] * o
        x1_ref[r, :] = x1
        h2 = _rms(x1, n2g_ref[...]) * (1.0 + sc2_ref[...]) + sh2_ref[...]
        h2_ref[r, :] = h2
        logits = jnp.dot(h2.astype(BF16), wr_ref[...], preferred_element_type=F32)
        lane = lax.broadcasted_iota(jnp.int32, logits.shape, 1)
        logits = jnp.where(lane < N_EXPERTS, logits, -jnp.inf)
        e = jnp.exp(logits - jnp.max(logits, axis=-1, keepdims=True))
        aff_ref[r, :] = e / jnp.sum(e, axis=-1, keepdims=True)


OUT_SUB = 256


def _out_proj(oa, ob, x, mod, tiles_per_mod, w, tag):
    m = x.shape[0]
    tm = 512
    tiles_per_mod = tiles_per_mod * 256 // tm
    row = lambda i: (i, 0)
    modspec = lambda k: pl.BlockSpec((None, 1, D_MODEL), lambda i: (i // tiles_per_mod, 0, k))
    return pl.pallas_call(
        _out_kernel,
        grid=(m // tm,),
        in_specs=[pl.BlockSpec((tm, A_V), row), pl.BlockSpec((tm, A_V), row), pl.BlockSpec((tm, D_MODEL), row),
                  modspec(2), modspec(3), modspec(4), _resident((1, D_MODEL)),
                  _resident((2 * A_V, D_MODEL)), _resident((D_MODEL, LANES))],
        out_specs=[pl.BlockSpec((tm, D_MODEL), row), pl.BlockSpec((tm, D_MODEL), row),
                   pl.BlockSpec((tm, LANES), row)],
        out_shape=[jax.ShapeDtypeStruct((m, D_MODEL), F32), jax.ShapeDtypeStruct((m, D_MODEL), F32),
                   jax.ShapeDtypeStruct((m, LANES), F32)],
        compiler_params=_cparams(("arbitrary",)),
        name="out_proj_" + tag,
    )(oa, ob, x, mod, mod, mod, w["norm2_gain"], w["w_o"], w["w_router"])


ROUTE_BLK = 256
COMBINE_T = 128
GROUP_BITS = 3
GROUP = 1 << GROUP_BITS
PLAN_NG_SHIFT = 10
PLAN_GS_SHIFT = 15


def _route_kernel(cap, aff_ref, st_ref, key_ref, plan_ref, off_ref, slot_t_ref, aff_t_ref):
    n = aff_ref.shape[0]
    nblk = n // ROUTE_BLK
    aff = aff_ref[...]
    lane = lax.broadcasted_iota(jnp.int32, (n, LANES), 1)

    aff_t = aff.T[:N_EXPERTS]

    def narrow(carry):
        lo, hi = carry
        mid = lo + (hi - lo) * 0.5
        mid = jnp.where(mid < hi, mid, lo)
        above = aff_t > mid
        few = jnp.sum(above.astype(jnp.int32), axis=1, keepdims=True) < cap
        up = jnp.min(jnp.where(above, aff_t, jnp.inf), axis=1, keepdims=True)
        dn = jnp.max(jnp.where(above, -jnp.inf, aff_t), axis=1, keepdims=True)
        return jnp.where(few, lo, up), jnp.where(few, dn, hi)

    bounds = (jnp.min(aff_t, axis=1, keepdims=True), jnp.max(aff_t, axis=1, keepdims=True))
    thr_t, _ = lax.while_loop(lambda c: jnp.max((c[0] < c[1]).astype(jnp.int32)) > 0, narrow, bounds)
    thr_sq = jnp.broadcast_to(jnp.concatenate([thr_t, jnp.zeros((LANES - N_EXPERTS, 1), F32)], axis=0), (LANES, LANES))
    thr = thr_sq.T[0:1, :]
    gt = aff > thr
    eq = aff == thr
    need = cap - jnp.sum(gt.astype(jnp.int32), axis=0, keepdims=True)

    r = lax.broadcasted_iota(jnp.int32, (ROUTE_BLK, ROUTE_BLK), 0)
    c = lax.broadcasted_iota(jnp.int32, (ROUTE_BLK, ROUTE_BLK), 1)
    tri = (c < r).astype(BF16)

    def prefix(mask):
        m = mask.astype(F32)
        carry = jnp.zeros((1, LANES), F32)
        outs = []
        for b in range(nblk):
            mb = m[b * ROUTE_BLK:(b + 1) * ROUTE_BLK]
            outs.append(jnp.dot(tri, mb.astype(BF16), preferred_element_type=F32) + carry)
            carry = carry + jnp.sum(mb, axis=0, keepdims=True)
        return jnp.concatenate(outs, axis=0).astype(jnp.int32)

    sel = (gt | (eq & (prefix(eq) < need))) & (lane < N_EXPERTS)
    pos = prefix(sel)
    nt = n // COMBINE_T
    key_ref[...] = pos
    st_ref[...] = jnp.zeros(st_ref.shape, jnp.int32)
    st_ref[0:nt, :] = key_ref[pl.ds(0, nt, stride=COMBINE_T), :]
    st_ref[nt:nt + 1, :] = jnp.full((1, LANES), cap, jnp.int32)
    slot1 = jnp.where(sel, pos + 1, 0)
    key_ref[...] = slot1
    slot_t_ref[...] = slot1.T[:N_EXPERTS]
    aff_t_ref[...] = aff_t

    lo = st_ref[0:nt, :]
    hi = st_ref[pl.ds(1, nt), :]
    a = lo - (lo & (GROUP - 1))
    ng = jnp.where(hi > lo, lax.shift_right_logical(hi - a + (GROUP - 1), GROUP_BITS), 0)
    er = lax.broadcasted_iota(jnp.int32, (LANES, LANES), 0)
    ec = lax.broadcasted_iota(jnp.int32, (LANES, LANES), 1)
    gs = jnp.dot(ng.astype(BF16), (er < ec).astype(BF16), preferred_element_type=F32).astype(jnp.int32)
    elane = lax.broadcasted_iota(jnp.int32, (nt, LANES), 1)
    plan_ref[...] = jnp.where(elane == N_EXPERTS, gs, a | (ng << PLAN_NG_SHIFT) | (gs << PLAN_GS_SHIFT))
    off_ref[...] = gs * GROUP - a


def _compact_kernel(cap, st_s, slot_ref, aff_ref, idx_ref, gate_ref, acc):
    tt = COMBINE_T
    nt = slot_ref.shape[0]
    acc[...] = jnp.zeros(acc.shape, F32)
    win = lax.broadcasted_iota(jnp.int32, (2 * LANES, tt), 0)
    row = lax.broadcasted_iota(jnp.int32, (8, 2 * LANES), 0)
    tok_in_tile = lax.broadcasted_iota(jnp.int32, (1, tt), 1).astype(BF16)
    one = jnp.ones((1, tt), BF16)
    pad = jnp.zeros((3, tt), BF16)

    def tile_body(t, carry):
        slot, af = slot_ref[t], aff_ref[t]
        scale = jnp.where(row == 1, jnp.asarray(t * tt, F32), 1.0)
        for e in range(N_EXPERTS):
            g0 = jnp.minimum(lax.shift_right_logical(st_s[t * LANES + e], 7), cap // LANES - 1)
            rel = slot[e:e + 1, :] - 1 - g0 * LANES
            put = jnp.where(jnp.broadcast_to(rel, (2 * LANES, tt)) == win, 1.0, 0.0).astype(BF16)
            g = af[e:e + 1, :]
            hi = g.astype(BF16)
            r1 = g - hi.astype(F32)
            mid = r1.astype(BF16)
            lo = (r1 - mid.astype(F32)).astype(BF16)
            feat = jnp.concatenate([tok_in_tile, one, hi, mid, lo, pad], axis=0)
            moved = _qk(feat, put) * scale
            acc[e, g0] += moved[:, :LANES]
            acc[e, g0 + 1] += moved[:, LANES:]
        return carry

    lax.fori_loop(0, nt, tile_body, 0)
    groups = range(cap // LANES)
    for e in range(N_EXPERTS):
        idx_ref[e:e + 1, :] = jnp.concatenate(
            [acc[e, g, 0:1, :] + acc[e, g, 1:2, :] for g in groups], axis=1).astype(jnp.int32)
        gate_ref[e:e + 1, :] = jnp.concatenate(
            [acc[e, g, 2:3, :] + acc[e, g, 3:4, :] + acc[e, g, 4:5, :] for g in groups], axis=1)


def _route(aff, tag):
    n = aff.shape[0]
    cap = CAPACITY_FACTOR * n // N_EXPERTS
    nt = n // COMBINE_T
    full = lambda s: pl.BlockSpec(s, lambda i, *_: (0,) * len(s))
    i32 = lambda s: jax.ShapeDtypeStruct(s, jnp.int32)
    assert cap < (1 << PLAN_NG_SHIFT) and COMBINE_T // GROUP + 1 < (1 << (PLAN_GS_SHIFT - PLAN_NG_SHIFT))
    st, key, plan, off, slot_t, aff_t = pl.pallas_call(
        functools.partial(_route_kernel, cap),
        grid=(1,),
        in_specs=[full((n, LANES))],
        out_specs=[full((nt + 8, LANES)), full((n, LANES)), full((nt, LANES)), full((nt, LANES)),
                   full((N_EXPERTS, n)), full((N_EXPERTS, n))],
        out_shape=[i32((nt + 8, LANES)), i32((n, LANES)), i32((nt, LANES)), i32((nt, LANES)),
                   i32((N_EXPERTS, n)), jax.ShapeDtypeStruct((N_EXPERTS, n), F32)],
        compiler_params=_cparams(("arbitrary",)),
        name="route_" + tag,
    )(aff)
    by_tile = lambda a: jnp.swapaxes(a.reshape(N_EXPERTS, nt, COMBINE_T), 0, 1)
    groups = cap // LANES + 1
    idx, gate = pl.pallas_call(
        functools.partial(_compact_kernel, cap),
        grid_spec=pltpu.PrefetchScalarGridSpec(
            num_scalar_prefetch=1, grid=(1,),
            in_specs=[full((nt, N_EXPERTS, COMBINE_T)), full((nt, N_EXPERTS, COMBINE_T))],
            out_specs=[full((N_EXPERTS, cap)), full((N_EXPERTS, cap))],
            scratch_shapes=[pltpu.VMEM((N_EXPERTS, groups, 8, LANES), F32)]),
        out_shape=[i32((N_EXPERTS, cap)), jax.ShapeDtypeStruct((N_EXPERTS, cap), F32)],
        compiler_params=_cparams(("arbitrary",)),
        name="compact_" + tag,
    )(st.reshape(-1), by_tile(slot_t), by_tile(aff_t))
    slot_bt = by_tile(slot_t)
    srow = jnp.where(slot_bt > 0, slot_bt - 1 + off[:, :N_EXPERTS, None], -1)
    return idx, gate, plan, srow


FF_CHUNK = 256


def _ffn_kernel(cap, nf, per, idx_s, hp_hbm, hs_hbm, gate_ref, w1_ref, w3_ref, w2_ref, yp_hbm, ys_hbm,
                xg, xe, acc, gsem, osem):
    e = pl.program_id(0)
    f = pl.program_id(1)
    ne = pl.num_programs(0)
    srcs = (hp_hbm, hs_hbm)
    dsts = (yp_hbm, ys_hbm)
    slot = lax.rem(e, 2)

    def issue_part(ex, part, sl, live):
        tail = cap - (nf - 1) * per
        live_tail = live & (part < nf - 1)
        j0 = pl.multiple_of(part * per, GROUP) if per % GROUP == 0 else part * per
        for p in range(2):
            for i in range(per):
                j = j0 + i
                jc = j if i < tail else jnp.minimum(j, cap - 1)
                tok = idx_s[(p * N_EXPERTS + ex) * cap + jc]

                @pl.when(live if i < tail else live_tail)
                def _():
                    pltpu.make_async_copy(srcs[p].at[pl.ds(tok, 1), :], xg.at[sl, pl.ds(p * cap + jc, 1), :],
                                          gsem.at[sl]).start()

    def out_copy(p, ex):
        return pltpu.make_async_copy(acc.at[pl.ds(p * cap, cap), :],
                                     dsts[p].at[pl.ds(pl.multiple_of(ex * cap, cap), cap), :], osem)

    @pl.when((e == 0) & (f == 0))
    def _prologue():
        def part_body(part, carry):
            issue_part(0, part, 0, True)
            return carry
        lax.fori_loop(0, nf, part_body, 0)

    @pl.when(f == 0)
    def _start_expert():
        for p in range(2):
            pltpu.make_async_copy(srcs[p].at[pl.ds(0, cap), :], xg.at[slot, pl.ds(p * cap, cap), :],
                                  gsem.at[slot]).wait()
        xe[...] = xg[slot].astype(BF16)

    x = xe[...]
    a = jnp.dot(x, w1_ref[...].astype(BF16), preferred_element_type=F32)
    b = jnp.dot(x, w3_ref[...].astype(BF16), preferred_element_type=F32)
    hid = ((a * jax.nn.sigmoid(a)) * b).astype(BF16)

    issue_part(jnp.minimum(e + 1, ne - 1), f, 1 - slot, e + 1 < ne)

    def down():
        return jnp.dot(hid, w2_ref[...].astype(BF16), preferred_element_type=F32)

    @pl.when(f == 0)
    def _first():
        @pl.when(e > 0)
        def _():
            for p in range(2):
                out_copy(p, e - 1).wait()
        acc[...] = down()

    @pl.when((f > 0) & (f < nf - 1))
    def _middle():
        acc[...] += down()

    @pl.when(f == nf - 1)
    def _last():
        acc[...] = (acc[...] + down()) * gate_ref[...]
        for p in range(2):
            out_copy(p, e).start()

        @pl.when(e == ne - 1)
        def _():
            for p in range(2):
                out_copy(p, e).wait()


def _ffn(idx_flat, h2p, h2s, gate, w1, w3, w2, cap):
    nf = EXPERT_FF // FF_CHUNK
    per = -(-cap // nf)
    grid_spec = pltpu.PrefetchScalarGridSpec(
        num_scalar_prefetch=1,
        grid=(N_EXPERTS, nf),
        in_specs=[pl.BlockSpec(memory_space=pl.ANY), pl.BlockSpec(memory_space=pl.ANY),
                  pl.BlockSpec((None, 2 * cap, 1), lambda e, f, *_: (e, 0, 0)),
                  pl.BlockSpec((None, D_MODEL, FF_CHUNK), lambda e, f, *_: (e, 0, f)),
                  pl.BlockSpec((None, D_MODEL, FF_CHUNK), lambda e, f, *_: (e, 0, f)),
                  pl.BlockSpec((None, FF_CHUNK, D_MODEL), lambda e, f, *_: (e, f, 0))],
        out_specs=[pl.BlockSpec(memory_space=pl.ANY), pl.BlockSpec(memory_space=pl.ANY)],
        scratch_shapes=[pltpu.VMEM((2, 2 * cap, D_MODEL), F32), pltpu.VMEM((2 * cap, D_MODEL), BF16),
                        pltpu.VMEM((2 * cap, D_MODEL), F32), pltpu.SemaphoreType.DMA((2,)),
                        pltpu.SemaphoreType.DMA(())])
    out = jax.ShapeDtypeStruct((N_EXPERTS * cap, D_MODEL), F32)
    return pl.pallas_call(
        functools.partial(_ffn_kernel, cap, nf, per),
        grid_spec=grid_spec, out_shape=[out, out],
        compiler_params=_cparams(("arbitrary", "arbitrary")),
        name="expert_ffn",
    )(idx_flat, h2p, h2s, gate, w1, w3, w2)


WAIT_GROUPS = 8
STAGE_ROWS = -(-(N_EXPERTS * (COMBINE_T // GROUP + 1) * GROUP) // MXU_DIM) * MXU_DIM


def _combine_kernel(cap, plan_s, ye_hbm, x1_ref, g2_ref, srow_ref, o_ref, stage, acc, sem):
    t = pl.program_id(0)
    nt = pl.num_programs(0)
    tt = COMBINE_T
    slot = lax.rem(t, 2)

    def issue_tile(tile, sl):
        for e in range(N_EXPERTS):
            w = plan_s[tile * LANES + e]
            a = w & ((1 << PLAN_NG_SHIFT) - 1)
            ng = lax.shift_right_logical(w, PLAN_NG_SHIFT) & ((1 << (PLAN_GS_SHIFT - PLAN_NG_SHIFT)) - 1)
            gs = lax.shift_right_logical(w, PLAN_GS_SHIFT)

            def issue(g, carry):
                src = pl.multiple_of(e * cap + a + g * GROUP, GROUP)
                dst = pl.multiple_of((gs + g) * GROUP, GROUP)
                pltpu.make_async_copy(ye_hbm.at[pl.ds(src, GROUP), :], stage.at[sl, pl.ds(dst, GROUP), :],
                                      sem.at[sl]).start()
                return carry
            lax.fori_loop(0, ng, issue, 0)

    def wait_tile(tile, sl):
        total = plan_s[tile * LANES + N_EXPERTS]

        def wait_groups(n):
            def body(i, carry):
                pltpu.make_async_copy(ye_hbm.at[pl.ds(0, n * GROUP), :], stage.at[sl, pl.ds(0, n * GROUP), :],
                                      sem.at[sl]).wait()
                return carry
            return body
        lax.fori_loop(0, total // WAIT_GROUPS, wait_groups(WAIT_GROUPS), 0)
        lax.fori_loop(0, lax.rem(total, WAIT_GROUPS), wait_groups(1), 0)

    @pl.when(t == 0)
    def _init():
        stage[...] = jnp.zeros(stage.shape, F32)
        issue_tile(0, 0)

    @pl.when(t + 1 < nt)
    def _prefetch():
        issue_tile(t + 1, 1 - slot)

    wait_tile(t, slot)

    srow = srow_ref[...]
    rio = lax.broadcasted_iota(jnp.int32, (MXU_DIM, tt), 0)
    acc[...] = jnp.zeros(acc.shape, F32)

    def add_chunk(kc, carry):
        base = pl.multiple_of(kc * MXU_DIM, MXU_DIM)
        rows = stage[slot, pl.ds(base, MXU_DIM), :].astype(BF16)
        rel = srow - base
        pick_t = jnp.zeros((MXU_DIM, tt), F32)
        for e in range(N_EXPERTS):
            pick_t = jnp.where(jnp.broadcast_to(rel[e:e + 1, :], (MXU_DIM, tt)) == rio, 1.0, pick_t)
        acc[...] += jnp.dot(pick_t.T.astype(BF16), rows, preferred_element_type=F32)
        return carry

    total_rows = plan_s[t * LANES + N_EXPERTS] * GROUP
    lax.fori_loop(0, lax.div(total_rows + MXU_DIM - 1, MXU_DIM), add_chunk, 0)
    o_ref[...] = x1_ref[...] + g2_ref[...] * acc[...]


def _combine(plan, srow, ye, x1, mod, tiles_per_mod, cap, tag):
    n = x1.shape[0]
    tt = COMBINE_T
    nt = n // tt
    grid_spec = pltpu.PrefetchScalarGridSpec(
        num_scalar_prefetch=1,
        grid=(nt,),
        in_specs=[pl.BlockSpec(memory_space=pl.ANY),
                  pl.BlockSpec((tt, D_MODEL), lambda i, *_: (i, 0)),
                  pl.BlockSpec((None, 1, D_MODEL), lambda i, *_: (i // tiles_per_mod, 0, 5)),
                  pl.BlockSpec((None, N_EXPERTS, tt), lambda i, *_: (i, 0, 0))],
        out_specs=pl.BlockSpec((tt, D_MODEL), lambda i, *_: (i, 0)),
        scratch_shapes=[pltpu.VMEM((2, STAGE_ROWS, D_MODEL), F32), pltpu.VMEM((tt, D_MODEL), F32),
                        pltpu.SemaphoreType.DMA((2,))])
    return pl.pallas_call(
        functools.partial(_combine_kernel, cap),
        grid_spec=grid_spec, out_shape=jax.ShapeDtypeStruct((n, D_MODEL), F32),
        compiler_params=_cparams(("arbitrary",)),
        name="combine_" + tag,
    )(plan.reshape(-1), ye, x1, mod, srow)


def _prep_weights(norm1_gain, norm2_gain, w_in, a_q_gain, a_k_gain, a_lambda_q1, a_lambda_k1, a_lambda_q2,
                  a_lambda_k2, a_sub_gain, q_a_gain, w_uq, b_qn_gain, b_qr_gain, kv_a_gain, w_ukv, b_kn_gain,
                  b_kr_gain, w_o, w_router):
    tile = lambda g, width: jnp.tile(g.reshape(1, -1), (1, width // g.shape[-1]))
    wuq = w_uq[0].reshape(Q_RANK, B_HEADS, B_QK)
    wuq = jnp.concatenate([wuq[:, :, :B_NOPE].reshape(Q_RANK, -1), wuq[:, :, B_NOPE:].reshape(Q_RANK, -1)], axis=1)
    return {
        "norm1_gain": norm1_gain.reshape(1, -1), "norm2_gain": norm2_gain.reshape(1, -1),
        "w_in": w_in[0].astype(BF16),
        "a_q_gain": tile(a_q_gain, MXU_DIM), "a_k_gain": tile(a_k_gain, MXU_DIM),
        "a_lambda_q1": a_lambda_q1.reshape(1, -1), "a_lambda_k1": a_lambda_k1.reshape(1, -1),
        "a_lambda_q2": a_lambda_q2.reshape(1, -1), "a_lambda_k2": a_lambda_k2.reshape(1, -1),
        "a_sub_gain": a_sub_gain.reshape(1, -1), "q_a_gain": q_a_gain.reshape(1, -1),
        "w_uq": wuq.astype(BF16), "b_qn_gain": b_qn_gain.reshape(1, -1), "b_qr_gain": tile(b_qr_gain, MXU_DIM),
        "kv_a_gain": kv_a_gain.reshape(1, -1), "w_ukv": w_ukv[0].astype(BF16),
        "b_kn_gain": b_kn_gain.reshape(1, -1), "b_kr_gain": tile(b_kr_gain, LANES),
        "w_o": w_o[0].astype(BF16),
        "w_router": jnp.pad(w_router[0], ((0, 0), (0, LANES - N_EXPERTS))).astype(BF16),
        "seg64": _seg_matrix(A_DH),
    }


def kernel(x_prompt, x_sample, cache_diff_k, cache_diff_v, cache_mla_ckv, cache_mla_krope, c, c_ctx, w_ada, b_ada, norm1_gain, norm2_gain, w_in, a_q_gain, a_k_gain, a_lambda_q1, a_lambda_k1, a_lambda_q2, a_lambda_k2, a_sub_gain, q_a_gain, w_uq, b_qn_gain, b_qr_gain, kv_a_gain, w_ukv, b_kn_gain, b_kr_gain, w_o, w_router, w_exp1, w_exp3, w_exp2):
    nbp, lp, _ = x_prompt.shape
    nbs, ls, _ = x_sample.shape
    past = cache_diff_k.shape[2]
    w = _prep_weights(norm1_gain, norm2_gain, w_in, a_q_gain, a_k_gain, a_lambda_q1, a_lambda_k1, a_lambda_q2,
                      a_lambda_k2, a_sub_gain, q_a_gain, w_uq, b_qn_gain, b_qr_gain, kv_a_gain, w_ukv, b_kn_gain,
                      b_kr_gain, w_o, w_router)

    cvec = jnp.concatenate([c_ctx[None], c, jnp.zeros((8 - 1 - nbs, D_MODEL), F32)], axis=0)
    mod = _ada(cvec, w_ada[0], b_ada)
    mod_p = mod[0:1].reshape(1, 1, -1)
    mod_s = mod[1:1 + nbs].reshape(nbs, 1, -1)

    xp = x_prompt.reshape(nbp * lp, D_MODEL)
    xs = x_sample.reshape(nbs * ls, D_MODEL)
    tm = 256
    assert lp == IN_SUB, "the prompt projection writes one batch of new_diff_k per row tile"
    aq_p, ak_p, av_p, qm_p, km_p, bv_p, ak32, av32, ckv32, kr32 = _in_proj(xp, mod_p, False, nbp * lp, w, None)
    aq_s, ak_s, av_s, qm_s, km_s, bv_s = _in_proj(xs, mod_s, True, ls, w, _rope_tables(ls))
    km_c, bv_c = _cache_kv(cache_mla_ckv.reshape(nbs * past, KV_RANK), cache_mla_krope.reshape(nbs * past, B_ROPE), w)

    cache_a = (cache_diff_k.reshape(nbs * past, A_QK), cache_diff_v.reshape(nbs * past, A_V))

    oa_p = _diff_attn(aq_p, ak_p, av_p, None, nbp, lp, lp, A_HEADS, w)
    ob_p = _mla_attn(qm_p, km_p, bv_p, None, nbp, lp, lp, B_HEADS)
    oa_s = _diff_attn(aq_s, ak_s, av_s, cache_a, nbs, ls, 256, 4, w)
    ob_s = _mla_attn(qm_s, km_s, bv_s, (km_c, bv_c), nbs, ls, 256, 4)

    x1_p, h2_p, aff_p = _out_proj(oa_p, ob_p, xp, mod_p, nbp * lp // tm, w, "prompt")
    x1_s, h2_s, aff_s = _out_proj(oa_s, ob_s, xs, mod_s, ls // tm, w, "sample")

    idx_p, gate_p, plan_p, srow_p = _route(aff_p, "prompt")
    idx_s, gate_s, plan_s, srow_s = _route(aff_s, "sample")
    cap = idx_p.shape[1]
    idx_flat = jnp.stack([idx_p, idx_s]).reshape(-1)
    gate = jnp.concatenate([gate_p, gate_s], axis=1)[..., None]
    ye_p, ye_s = _ffn(idx_flat, h2_p, h2_s, gate, w_exp1[0], w_exp3[0], w_exp2[0], cap)

    y_p = _combine(plan_p, srow_p, ye_p, x1_p, mod_p, nbp * lp // COMBINE_T, cap, "prompt")
    y_s = _combine(plan_s, srow_s, ye_s, x1_s, mod_s, ls // COMBINE_T, cap, "sample")

    return (y_p.reshape(nbp, lp, D_MODEL), y_s.reshape(nbs, ls, D_MODEL),
            ak32, av32.reshape(nbp, 1, lp, A_HEADS, A_DV),
            ckv32.reshape(nbp, 1, lp, KV_RANK), kr32.reshape(nbp, 1, lp, B_ROPE))
```

```python
import functools
import math

import numpy as np
import jax
import jax.numpy as jnp
from jax import lax
from jax.experimental import pallas as pl
from jax.experimental.pallas import tpu as pltpu

F32 = jnp.float32
BF16 = jnp.bfloat16

D_MODEL = 2048
GRID_W = 64
A_HEADS = 8
A_DH = 64
A_DV = 128
B_HEADS = 8
B_NOPE = 128
B_ROPE = 64
B_DV = 128
Q_RANK = 512
KV_RANK = 256
N_EXPERTS = 16
EXPERT_FF = 1536
CAPACITY_FACTOR = 2
ROPE_BASE = 10000.0
EPS = 1e-6
LAM_INIT = 0.8 - 0.6 * math.exp(-0.3 * 0)
LOG2E = math.log2(math.e)

A_QK = A_HEADS * 2 * A_DH
A_V = A_HEADS * A_DV
B_QK = B_NOPE + B_ROPE
IN_COLS = 2 * A_QK + A_V + Q_RANK + KV_RANK + B_ROPE
LANES = 128
MXU_DIM = 256
VMEM_LIMIT = 56 * 1024 * 1024


def _cparams(sem):
    return pltpu.CompilerParams(dimension_semantics=sem, vmem_limit_bytes=VMEM_LIMIT)


def _resident(shape):
    nd = len(shape)
    return pl.BlockSpec(shape, lambda *_: (0,) * nd, pipeline_mode=pl.Buffered(1))


def _rms(x, gain):
    return x * lax.rsqrt(jnp.mean(x * x, axis=-1, keepdims=True) + EPS) * gain


def _seg_rms(y, segmat, gain):
    y2 = (y * y).astype(BF16)
    outs = []
    for c in range(y.shape[1] // MXU_DIM):
        sl = slice(c * MXU_DIM, (c + 1) * MXU_DIM)
        ms = jnp.dot(y2[:, sl], segmat, preferred_element_type=F32)
        outs.append(y[:, sl] * lax.rsqrt(ms + EPS) * gain)
    return jnp.concatenate(outs, axis=1)


def _rope(y, cos, sin):
    outs = []
    for c in range(y.shape[1] // LANES):
        yc = y[:, c * LANES:(c + 1) * LANES]
        lane = lax.broadcasted_iota(jnp.int32, yc.shape, 1)
        partner = jnp.where((lane & 16) == 0, pltpu.roll(yc, LANES - 16, 1), pltpu.roll(yc, 16, 1))
        outs.append(yc * cos + partner * sin)
    return outs[0] if len(outs) == 1 else jnp.concatenate(outs, axis=1)


def _rope_tables(n_tok):
    t = np.arange(n_tok)
    row, col = t // GRID_W, t % GRID_W
    nf = A_DH // 4
    inv = ROPE_BASE ** (-np.arange(nf, dtype=np.float64) * 2.0 / (A_DH // 2))
    lane = np.arange(A_DH)
    pos = np.where(lane[None, :] < A_DH // 2, row[:, None], col[:, None]).astype(np.float64)
    ang = pos * inv[lane % nf][None, :]
    sign = np.where((lane % (2 * nf)) < nf, -1.0, 1.0)[None, :]
    cos = np.tile(np.cos(ang), (1, LANES // A_DH)).astype(np.float32)
    sin = np.tile(np.sin(ang) * sign, (1, LANES // A_DH)).astype(np.float32)
    return jnp.asarray(cos), jnp.asarray(sin)


def _seg_matrix(width):
    i = np.arange(MXU_DIM)
    return jnp.asarray(((i[:, None] // width) == (i[None, :] // width)).astype(np.float32) / width, dtype=BF16)


def _ada_kernel(c_ref, w_ref, b_ref, o_ref):
    c = c_ref[...]
    s = c * jax.nn.sigmoid(c)
    o_ref[...] = jnp.dot(s.astype(BF16), w_ref[...].astype(BF16), preferred_element_type=F32) + b_ref[...]


def _ada(cvec, w_ada, b_ada):
    tn = 1024
    n = w_ada.shape[1]
    return pl.pallas_call(
        _ada_kernel,
        grid=(n // tn,),
        in_specs=[pl.BlockSpec((8, D_MODEL), lambda j: (0, 0)),
                  pl.BlockSpec((D_MODEL, tn), lambda j: (0, j)),
                  pl.BlockSpec((1, tn), lambda j: (0, j))],
        out_specs=pl.BlockSpec((8, tn), lambda j: (0, j)),
        out_shape=jax.ShapeDtypeStruct((8, n), F32),
        compiler_params=_cparams(("arbitrary",)),
        name="ada_mod",
    )(cvec, w_ada, b_ada)


def _expand_kv(ckv_n, kr, wukv_ref, kng, km_ref, bv_ref):
    kv = jnp.dot(ckv_n.astype(BF16), wukv_ref[...], preferred_element_type=F32)
    krb = kr[:, :B_ROPE].astype(BF16)
    for h in range(B_HEADS):
        base = h * (B_NOPE + B_DV)
        kn = _rms(kv[:, base:base + B_NOPE], kng)
        km_ref[h] = jnp.concatenate([kn.astype(BF16), krb], axis=1)
        bv_ref[h] = kv[:, base + B_NOPE:base + B_NOPE + B_DV].astype(BF16)


def _in_kernel(sample, *refs):
    (x_ref, sh_ref, sc_ref, n1g_ref, win_ref, aqg_ref, akg_ref, qag_ref, wuq_ref, qng_ref, qrg_ref,
     kvag_ref, wukv_ref, kng_ref, krg_ref, seg_ref) = refs[:16]
    refs = refs[16:]
    if sample:
        cos_ref, sin_ref = refs[:2]
        refs = refs[2:]
        cos, sin = cos_ref[...], sin_ref[...]
    aq_ref, ak_ref, av_ref, qm_ref, km_ref, bv_ref = refs[:6]
    refs = refs[6:]

    x = x_ref[...]
    h = _rms(x, n1g_ref[...]) * (1.0 + sc_ref[...]) + sh_ref[...]
    hb = h.astype(BF16)

    def proj(c0, c1):
        return jnp.dot(hb, win_ref[:, c0:c1], preferred_element_type=F32)

    seg = seg_ref[...]
    c0 = 2 * A_QK + A_V
    y_q = proj(c0, c0 + Q_RANK)
    y_ckv = proj(c0 + Q_RANK, c0 + Q_RANK + KV_RANK)
    kr64 = proj(c0 + Q_RANK + KV_RANK, c0 + Q_RANK + KV_RANK + B_ROPE)
    q_lat = _rms(y_q, qag_ref[...])
    qb = jnp.dot(q_lat.astype(BF16), wuq_ref[...], preferred_element_type=F32)
    ckv = _rms(y_ckv, kvag_ref[...])
    kr_raw = jnp.concatenate([kr64, jnp.zeros_like(kr64)], axis=1)
    kr = kr_raw * lax.rsqrt(jnp.sum(kr_raw * kr_raw, axis=-1, keepdims=True) * (1.0 / B_ROPE) + EPS) * krg_ref[...]
    if sample:
        kr = _rope(kr, cos, sin)
    _expand_kv(ckv, kr, wukv_ref, kng_ref[...], km_ref, bv_ref)
    y_aq = proj(0, A_QK)
    qr = _seg_rms(qb[:, B_HEADS * B_NOPE:], seg, qrg_ref[...])
    if sample:
        qr = _rope(qr, cos, sin)
    scale = B_QK ** -0.5 * LOG2E
    for hd in range(B_HEADS):
        qn = _rms(qb[:, hd * B_NOPE:(hd + 1) * B_NOPE], qng_ref[...])
        qm_ref[hd] = jnp.concatenate(
            [(qn * scale).astype(BF16), (qr[:, hd * B_ROPE:(hd + 1) * B_ROPE] * scale).astype(BF16)], axis=1)

    y_ak = proj(A_QK, 2 * A_QK)
    aq = _seg_rms(y_aq, seg, aqg_ref[...])
    if sample:
        aq = _rope(aq, cos, sin)
    aq_ref[...] = (aq * (A_DH ** -0.5 * LOG2E)).astype(BF16)
    av = proj(2 * A_QK, 2 * A_QK + A_V)
    ak = _seg_rms(y_ak, seg, akg_ref[...])
    if sample:
        ak = _rope(ak, cos, sin)
    ak_ref[...] = ak.astype(BF16)
    av_ref[...] = av.astype(BF16)
    if not sample:
        ak32_ref, av32_ref, ckv32_ref, kr32_ref = refs
        ak32_ref[...] = ak.reshape(ak32_ref.shape)
        av32_ref[...] = av
        ckv32_ref[...] = ckv
        kr32_ref[...] = kr[:, :B_ROPE]


IN_SUB = 256


def _in_proj(x, mod, sample, rows_per_mod, w, tables):
    m = x.shape[0]
    tm = IN_SUB
    nt = m // tm
    tiles_per_mod = rows_per_mod // tm
    row = lambda i: (i, 0)
    modspec = lambda k: pl.BlockSpec((None, 1, D_MODEL), lambda i: (i // tiles_per_mod, 0, k))
    in_specs = [pl.BlockSpec((tm, D_MODEL), row), modspec(0), modspec(1), _resident((1, D_MODEL)),
                _resident((D_MODEL, IN_COLS)), _resident((1, MXU_DIM)), _resident((1, MXU_DIM)),
                _resident((1, Q_RANK)), _resident((Q_RANK, B_HEADS * B_QK)), _resident((1, B_NOPE)),
                _resident((1, MXU_DIM)), _resident((1, KV_RANK)), _resident((KV_RANK, B_HEADS * (B_NOPE + B_DV))),
                _resident((1, B_NOPE)), _resident((1, LANES)), _resident((MXU_DIM, MXU_DIM))]
    args = [x, mod, mod, w["norm1_gain"], w["w_in"], w["a_q_gain"], w["a_k_gain"], w["q_a_gain"], w["w_uq"],
            w["b_qn_gain"], w["b_qr_gain"], w["kv_a_gain"], w["w_ukv"], w["b_kn_gain"], w["b_kr_gain"], w["seg64"]]
    if sample:
        per = tables[0].shape[0] // tm
        in_specs += [pl.BlockSpec((tm, LANES), lambda i: (i % per, 0))] * 2
        args += list(tables)
    out_shape = [jax.ShapeDtypeStruct((m, A_QK), BF16), jax.ShapeDtypeStruct((m, A_QK), BF16),
                 jax.ShapeDtypeStruct((m, A_V), BF16), jax.ShapeDtypeStruct((B_HEADS, m, B_QK), BF16),
                 jax.ShapeDtypeStruct((B_HEADS, m, B_QK), BF16), jax.ShapeDtypeStruct((B_HEADS, m, B_DV), BF16)]
    hspec = lambda d: pl.BlockSpec((B_HEADS, tm, d), lambda i: (0, i, 0))
    out_specs = [pl.BlockSpec((tm, A_QK), row), pl.BlockSpec((tm, A_QK), row), pl.BlockSpec((tm, A_V), row),
                 hspec(B_QK), hspec(B_QK), hspec(B_DV)]
    if not sample:
        out_shape += [jax.ShapeDtypeStruct((nt, 1, tm, A_HEADS, 2, A_DH), F32), jax.ShapeDtypeStruct((m, A_V), F32),
                      jax.ShapeDtypeStruct((m, KV_RANK), F32), jax.ShapeDtypeStruct((m, B_ROPE), F32)]
        out_specs += [pl.BlockSpec((None, None, tm, A_HEADS, 2, A_DH), lambda i: (i, 0, 0, 0, 0, 0)),
                      pl.BlockSpec((tm, A_V), row),
                      pl.BlockSpec((tm, KV_RANK), row), pl.BlockSpec((tm, B_ROPE), row)]
    return pl.pallas_call(
        functools.partial(_in_kernel, sample),
        grid=(nt,), in_specs=in_specs, out_specs=out_specs, out_shape=out_shape,
        compiler_params=_cparams(("arbitrary",)),
        name="in_proj_sample" if sample else "in_proj_prompt",
    )(*args)


def _cache_kv_kernel(ckv_ref, kr_ref, wukv_ref, kng_ref, km_ref, bv_ref):
    _expand_kv(ckv_ref[...], kr_ref[...], wukv_ref, kng_ref[...], km_ref, bv_ref)


def _cache_kv(ckv, kr, w):
    m = ckv.shape[0]
    return pl.pallas_call(
        _cache_kv_kernel,
        grid=(1,),
        in_specs=[_resident((m, KV_RANK)), _resident((m, B_ROPE)),
                  _resident((KV_RANK, B_HEADS * (B_NOPE + B_DV))), _resident((1, B_NOPE))],
        out_specs=[pl.BlockSpec((B_HEADS, m, B_QK), lambda i: (0, 0, 0)),
                   pl.BlockSpec((B_HEADS, m, B_DV), lambda i: (0, 0, 0))],
        out_shape=[jax.ShapeDtypeStruct((B_HEADS, m, B_QK), BF16), jax.ShapeDtypeStruct((B_HEADS, m, B_DV), BF16)],
        compiler_params=_cparams(("arbitrary",)),
        name="cache_kv",
    )(ckv, kr, w["w_ukv"], w["b_kn_gain"])


def _qk(q, k):
    return lax.dot_general(q, k, (((1,), (1,)), ((), ())), preferred_element_type=F32)


KEY_CHUNK = 512
SCORE_AHEAD = 1


def _scores(q, k, kc):
    s = _qk(q, k)
    return s if kc is None else jnp.concatenate([s, _qk(q, kc)], axis=1)


def _weighted(p, v, vc):
    if vc is None:
        return jnp.dot(p, v, preferred_element_type=F32)
    lk = v.shape[0]
    return (jnp.dot(p[:, :lk], v, preferred_element_type=F32)
            + jnp.dot(p[:, lk:], vc, preferred_element_type=F32))


def _diff_attn_kernel(hb, cached, q_ref, k_ref, v_ref, *refs):
    if cached:
        kc_ref, vc_ref = refs[:2]
        refs = refs[2:]
    lq1, lk1, lq2, lk2, sub_ref, o_ref = refs
    lam = (jnp.exp(jnp.sum(lq1[...] * lk1[...], axis=-1, keepdims=True))
           - jnp.exp(jnp.sum(lq2[...] * lk2[...], axis=-1, keepdims=True)) + LAM_INIT)
    def scores(h, comp):
        sl = slice(h * LANES, (h + 1) * LANES)
        q, k = q_ref[:, sl], k_ref[:, sl]
        kc = kc_ref[:, sl].astype(BF16) if cached else None
        lane = lax.broadcasted_iota(jnp.int32, q.shape, 1)
        keep = (lane < A_DH) if comp == 0 else (lane >= A_DH)
        return _scores(jnp.where(keep, q, jnp.zeros_like(q)), k, kc)

    s1_next, s2_next = scores(0, 0), scores(0, 1)
    for h in range(hb):
        sl = slice(h * LANES, (h + 1) * LANES)
        s1, s2 = s1_next, s2_next
        if h + 1 < hb:
            s1_next = scores(h + 1, 0)
        e1 = jnp.exp2(s1 - jnp.max(s1, axis=-1, keepdims=True))
        if h + 1 < hb:
            s2_next = scores(h + 1, 1)
        e2 = jnp.exp2(s2 - jnp.max(s2, axis=-1, keepdims=True))
        if cached:
            ones = lambda x: jnp.concatenate([x, jnp.ones_like(x)], axis=1)
            v, vc = ones(v_ref[:, sl]), ones(vc_ref[:, sl].astype(BF16))
            n1 = _weighted(e1.astype(BF16), v, vc)
            n2 = _weighted(e2.astype(BF16), v, vc)
            o = n1[:, :LANES] * (1.0 / n1[:, LANES:LANES + 1]) - n2[:, :LANES] * (lam / n2[:, LANES:LANES + 1])
        else:
            l1 = jnp.sum(e1, axis=-1, keepdims=True)
            l2 = jnp.sum(e2, axis=-1, keepdims=True)
            p = e1 - e2 * (lam * l1 / l2)
            o = jnp.dot(p.astype(BF16), v_ref[:, sl], preferred_element_type=F32) * (1.0 / l1)
        o_ref[:, sl] = (_rms(o, sub_ref[...]) * (1.0 - LAM_INIT)).astype(BF16)


def _diff_attn(q, k, v, cache, nb, lq, tq, hb, w):
    nq = lq // tq
    vec = _resident((1, A_DH))
    kv = lambda rows: pl.BlockSpec((rows, hb * LANES), lambda b, g, i: (b, g))
    in_specs = [pl.BlockSpec((tq, hb * LANES), lambda b, g, i: (b * nq + i, g)), kv(lq), kv(lq)]
    args = [q, k, v]
    if cache is not None:
        past = cache[0].shape[0] // nb
        in_specs += [kv(past), kv(past)]
        args += list(cache)
    return pl.pallas_call(
        functools.partial(_diff_attn_kernel, hb, cache is not None),
        grid=(nb, A_HEADS // hb, nq),
        in_specs=in_specs + [vec, vec, vec, vec, _resident((1, A_DV))],
        out_specs=pl.BlockSpec((tq, hb * LANES), lambda b, g, i: (b * nq + i, g)),
        out_shape=jax.ShapeDtypeStruct((nb * lq, A_V), BF16),
        compiler_params=_cparams(("arbitrary", "arbitrary", "arbitrary")),
        name="diff_attn_cached" if cache is not None else "diff_attn",
    )(*args, w["a_lambda_q1"], w["a_lambda_k1"], w["a_lambda_q2"], w["a_lambda_k2"], w["a_sub_gain"])


def _mla_attn_kernel(hb, cached, q_ref, k_ref, v_ref, *refs):
    if cached:
        kc_ref, vc_ref, o_ref = refs
    else:
        (o_ref,) = refs
    lk = k_ref.shape[1]
    ck = min(lk, KEY_CHUNK)
    def head_scores(h):
        q = q_ref[h]
        s = [_qk(q, k_ref[h, c * ck:(c + 1) * ck]) for c in range(lk // ck)]
        return s + [_qk(q, kc_ref[h])] if cached else s

    ahead = [head_scores(i) for i in range(min(SCORE_AHEAD, hb))]
    for h in range(hb):
        vs = [v_ref[h, c * ck:(c + 1) * ck] for c in range(lk // ck)] + ([vc_ref[h]] if cached else [])
        s = ahead.pop(0)
        if h + SCORE_AHEAD < hb:
            ahead.append(head_scores(h + SCORE_AHEAD))
        m = functools.reduce(jnp.maximum, [jnp.max(sc, axis=-1, keepdims=True) for sc in s])
        l = 0.0
        o = 0.0
        for sc, v in zip(s, vs):
            e = jnp.exp2(sc - m)
            if cached:
                v = jnp.concatenate([v, jnp.ones_like(v)], axis=1)
            else:
                l = l + jnp.sum(e, axis=-1, keepdims=True)
            o = o + jnp.dot(e.astype(BF16), v, preferred_element_type=F32)
        if cached:
            o, l = o[:, :LANES], o[:, LANES:LANES + 1]
        o_ref[:, h * LANES:(h + 1) * LANES] = (o * (1.0 / l)).astype(BF16)


def _mla_attn(q, k, v, cache, nb, lq, tq, hb):
    nq = lq // tq
    kv = lambda rows, d: pl.BlockSpec((hb, rows, d), lambda b, g, i: (g, b, 0))
    in_specs = [pl.BlockSpec((hb, tq, B_QK), lambda b, g, i: (g, b * nq + i, 0)), kv(lq, B_QK), kv(lq, B_DV)]
    args = [q, k, v]
    if cache is not None:
        past = cache[0].shape[1] // nb
        in_specs += [kv(past, B_QK), kv(past, B_DV)]
        args += list(cache)
    return pl.pallas_call(
        functools.partial(_mla_attn_kernel, hb, cache is not None),
        grid=(nb, B_HEADS // hb, nq),
        in_specs=in_specs,
        out_specs=pl.BlockSpec((tq, hb * LANES), lambda b, g, i: (b * nq + i, g)),
        out_shape=jax.ShapeDtypeStruct((nb * lq, B_HEADS * B_DV), BF16),
        compiler_params=_cparams(("arbitrary", "arbitrary", "arbitrary")),
        name="mla_attn_cached" if cache is not None else "mla_attn",
    )(*args)


def _out_kernel(oa_ref, ob_ref, x_ref, g1_ref, sh2_ref, sc2_ref, n2g_ref, wo_ref, wr_ref,
                x1_ref, h2_ref, aff_ref):
    sub = OUT_SUB
    rows = [pl.ds(r * sub, sub) for r in range(x_ref.shape[0] // sub)]
    os_ = [jnp.dot(oa_ref[r, :], wo_ref[:A_V, :], preferred_element_type=F32)
           + jnp.dot(ob_ref[r, :], wo_ref[A_V:, :], preferred_element_type=F32) for r in rows]
    for r, o in zip(rows, os_):
        x1 = x_ref[r, :] + g1_ref[...] * o
        x1_ref[r, :] = x1
        h2 = _rms(x1, n2g_ref[...]) * (1.0 + sc2_ref[...]) + sh2_ref[...]
        h2_ref[r, :] = h2
        logits = jnp.dot(h2.astype(BF16), wr_ref[...], preferred_element_type=F32)
        lane = lax.broadcasted_iota(jnp.int32, logits.shape, 1)
        logits = jnp.where(lane < N_EXPERTS, logits, -jnp.inf)
        e = jnp.exp(logits - jnp.max(logits, axis=-1, keepdims=True))
        aff_ref[r, :] = e / jnp.sum(e, axis=-1, keepdims=True)


OUT_SUB = 256


def _out_proj(oa, ob, x, mod, tiles_per_mod, w, tag):
    m = x.shape[0]
    tm = 512
    tiles_per_mod = tiles_per_mod * 256 // tm
    row = lambda i: (i, 0)
    modspec = lambda k: pl.BlockSpec((None, 1, D_MODEL), lambda i: (i // tiles_per_mod, 0, k))
    return pl.pallas_call(
        _out_kernel,
        grid=(m // tm,),
        in_specs=[pl.BlockSpec((tm, A_V), row), pl.BlockSpec((tm, A_V), row), pl.BlockSpec((tm, D_MODEL), row),
                  modspec(2), modspec(3), modspec(4), _resident((1, D_MODEL)),
                  _resident((2 * A_V, D_MODEL)), _resident((D_MODEL, LANES))],
        out_specs=[pl.BlockSpec((tm, D_MODEL), row), pl.BlockSpec((tm, D_MODEL), row),
                   pl.BlockSpec((tm, LANES), row)],
        out_shape=[jax.ShapeDtypeStruct((m, D_MODEL), F32), jax.ShapeDtypeStruct((m, D_MODEL), F32),
                   jax.ShapeDtypeStruct((m, LANES), F32)],
        compiler_params=_cparams(("arbitrary",)),
        name="out_proj_" + tag,
    )(oa, ob, x, mod, mod, mod, w["norm2_gain"], w["w_o"], w["w_router"])


ROUTE_BLK = 256
COMBINE_T = 128
GROUP_BITS = 3
GROUP = 1 << GROUP_BITS
PLAN_NG_SHIFT = 10
PLAN_GS_SHIFT = 15


def _route_kernel(cap, aff_ref, st_ref, key_ref, plan_ref, off_ref, slot_t_ref, aff_t_ref):
    n = aff_ref.shape[0]
    nblk = n // ROUTE_BLK
    aff = aff_ref[...]
    lane = lax.broadcasted_iota(jnp.int32, (n, LANES), 1)

    aff_t = aff.T[:N_EXPERTS]

    def narrow(carry):
        lo, hi = carry
        mid = lo + (hi - lo) * 0.5
        mid = jnp.where(mid < hi, mid, lo)
        above = aff_t > mid
        few = jnp.sum(above.astype(jnp.int32), axis=1, keepdims=True) < cap
        up = jnp.min(jnp.where(above, aff_t, jnp.inf), axis=1, keepdims=True)
        dn = jnp.max(jnp.where(above, -jnp.inf, aff_t), axis=1, keepdims=True)
        return jnp.where(few, lo, up), jnp.where(few, dn, hi)

    bounds = (jnp.min(aff_t, axis=1, keepdims=True), jnp.max(aff_t, axis=1, keepdims=True))
    thr_t, _ = lax.while_loop(lambda c: jnp.max((c[0] < c[1]).astype(jnp.int32)) > 0, narrow, bounds)
    thr_sq = jnp.broadcast_to(jnp.concatenate([thr_t, jnp.zeros((LANES - N_EXPERTS, 1), F32)], axis=0), (LANES, LANES))
    thr = thr_sq.T[0:1, :]
    gt = aff > thr
    eq = aff == thr
    need = cap - jnp.sum(gt.astype(jnp.int32), axis=0, keepdims=True)

    r = lax.broadcasted_iota(jnp.int32, (ROUTE_BLK, ROUTE_BLK), 0)
    c = lax.broadcasted_iota(jnp.int32, (ROUTE_BLK, ROUTE_BLK), 1)
    tri = (c < r).astype(BF16)

    def prefix(mask):
        m = mask.astype(F32)
        carry = jnp.zeros((1, LANES), F32)
        outs = []
        for b in range(nblk):
            mb = m[b * ROUTE_BLK:(b + 1) * ROUTE_BLK]
            outs.append(jnp.dot(tri, mb.astype(BF16), preferred_element_type=F32) + carry)
            carry = carry + jnp.sum(mb, axis=0, keepdims=True)
        return jnp.concatenate(outs, axis=0).astype(jnp.int32)

    sel = (gt | (eq & (prefix(eq) < need))) & (lane < N_EXPERTS)
    pos = prefix(sel)
    nt = n // COMBINE_T
    key_ref[...] = pos
    st_ref[...] = jnp.zeros(st_ref.shape, jnp.int32)
    st_ref[0:nt, :] = key_ref[pl.ds(0, nt, stride=COMBINE_T), :]
    st_ref[nt:nt + 1, :] = jnp.full((1, LANES), cap, jnp.int32)
    slot1 = jnp.where(sel, pos + 1, 0)
    key_ref[...] = slot1
    slot_t_ref[...] = slot1.T[:N_EXPERTS]
    aff_t_ref[...] = aff_t

    lo = st_ref[0:nt, :]
    hi = st_ref[pl.ds(1, nt), :]
    a = lo - (lo & (GROUP - 1))
    ng = jnp.where(hi > lo, lax.shift_right_logical(hi - a + (GROUP - 1), GROUP_BITS), 0)
    er = lax.broadcasted_iota(jnp.int32, (LANES, LANES), 0)
    ec = lax.broadcasted_iota(jnp.int32, (LANES, LANES), 1)
    gs = jnp.dot(ng.astype(BF16), (er < ec).astype(BF16), preferred_element_type=F32).astype(jnp.int32)
    elane = lax.broadcasted_iota(jnp.int32, (nt, LANES), 1)
    plan_ref[...] = jnp.where(elane == N_EXPERTS, gs, a | (ng << PLAN_NG_SHIFT) | (gs << PLAN_GS_SHIFT))
    off_ref[...] = gs * GROUP - a


def _compact_kernel(cap, st_s, slot_ref, aff_ref, idx_ref, gate_ref, acc):
    tt = COMBINE_T
    nt = slot_ref.shape[0]
    acc[...] = jnp.zeros(acc.shape, F32)
    win = lax.broadcasted_iota(jnp.int32, (2 * LANES, tt), 0)
    row = lax.broadcasted_iota(jnp.int32, (8, 2 * LANES), 0)
    tok_in_tile = lax.broadcasted_iota(jnp.int32, (1, tt), 1).astype(BF16)
    one = jnp.ones((1, tt), BF16)
    pad = jnp.zeros((3, tt), BF16)

    def tile_body(t, carry):
        slot, af = slot_ref[t], aff_ref[t]
        scale = jnp.where(row == 1, jnp.asarray(t * tt, F32), 1.0)
        for e in range(N_EXPERTS):
            g0 = jnp.minimum(lax.shift_right_logical(st_s[t * LANES + e], 7), cap // LANES - 1)
            rel = slot[e:e + 1, :] - 1 - g0 * LANES
            put = jnp.where(jnp.broadcast_to(rel, (2 * LANES, tt)) == win, 1.0, 0.0).astype(BF16)
            g = af[e:e + 1, :]
            hi = g.astype(BF16)
            r1 = g - hi.astype(F32)
            mid = r1.astype(BF16)
            lo = (r1 - mid.astype(F32)).astype(BF16)
            feat = jnp.concatenate([tok_in_tile, one, hi, mid, lo, pad], axis=0)
            moved = _qk(feat, put) * scale
            acc[e, g0] += moved[:, :LANES]
            acc[e, g0 + 1] += moved[:, LANES:]
        return carry

    lax.fori_loop(0, nt, tile_body, 0)
    groups = range(cap // LANES)
    for e in range(N_EXPERTS):
        idx_ref[e:e + 1, :] = jnp.concatenate(
            [acc[e, g, 0:1, :] + acc[e, g, 1:2, :] for g in groups], axis=1).astype(jnp.int32)
        gate_ref[e:e + 1, :] = jnp.concatenate(
            [acc[e, g, 2:3, :] + acc[e, g, 3:4, :] + acc[e, g, 4:5, :] for g in groups], axis=1)


def _route(aff, tag):
    n = aff.shape[0]
    cap = CAPACITY_FACTOR * n // N_EXPERTS
    nt = n // COMBINE_T
    full = lambda s: pl.BlockSpec(s, lambda i, *_: (0,) * len(s))
    i32 = lambda s: jax.ShapeDtypeStruct(s, jnp.int32)
    assert cap < (1 << PLAN_NG_SHIFT) and COMBINE_T // GROUP + 1 < (1 << (PLAN_GS_SHIFT - PLAN_NG_SHIFT))
    st, key, plan, off, slot_t, aff_t = pl.pallas_call(
        functools.partial(_route_kernel, cap),
        grid=(1,),
        in_specs=[full((n, LANES))],
        out_specs=[full((nt + 8, LANES)), full((n, LANES)), full((nt, LANES)), full((nt, LANES)),
                   full((N_EXPERTS, n)), full((N_EXPERTS, n))],
        out_shape=[i32((nt + 8, LANES)), i32((n, LANES)), i32((nt, LANES)), i32((nt, LANES)),
                   i32((N_EXPERTS, n)), jax.ShapeDtypeStruct((N_EXPERTS, n), F32)],
        compiler_params=_cparams(("arbitrary",)),
        name="route_" + tag,
    )(aff)
    by_tile = lambda a: jnp.swapaxes(a.reshape(N_EXPERTS, nt, COMBINE_T), 0, 1)
    groups = cap // LANES + 1
    idx, gate = pl.pallas_call(
        functools.partial(_compact_kernel, cap),
        grid_spec=pltpu.PrefetchScalarGridSpec(
            num_scalar_prefetch=1, grid=(1,),
            in_specs=[full((nt, N_EXPERTS, COMBINE_T)), full((nt, N_EXPERTS, COMBINE_T))],
            out_specs=[full((N_EXPERTS, cap)), full((N_EXPERTS, cap))],
            scratch_shapes=[pltpu.VMEM((N_EXPERTS, groups, 8, LANES), F32)]),
        out_shape=[i32((N_EXPERTS, cap)), jax.ShapeDtypeStruct((N_EXPERTS, cap), F32)],
        compiler_params=_cparams(("arbitrary",)),
        name="compact_" + tag,
    )(st.reshape(-1), by_tile(slot_t), by_tile(aff_t))
    slot_bt = by_tile(slot_t)
    srow = jnp.where(slot_bt > 0, slot_bt - 1 + off[:, :N_EXPERTS, None], -1)
    return idx, gate, plan, srow


FF_CHUNK = 256


def _ffn_kernel(cap, nf, per, idx_s, hp_hbm, hs_hbm, gate_ref, w1_ref, w3_ref, w2_ref, yp_hbm, ys_hbm,
                xg, xe, acc, gsem, osem):
    e = pl.program_id(0)
    f = pl.program_id(1)
    ne = pl.num_programs(0)
    srcs = (hp_hbm, hs_hbm)
    dsts = (yp_hbm, ys_hbm)
    slot = lax.rem(e, 2)

    def issue_part(ex, part, sl, live):
        tail = cap - (nf - 1) * per
        live_tail = live & (part < nf - 1)
        j0 = pl.multiple_of(part * per, GROUP) if per % GROUP == 0 else part * per
        for p in range(2):
            for i in range(per):
                j = j0 + i
                jc = j if i < tail else jnp.minimum(j, cap - 1)
                tok = idx_s[(p * N_EXPERTS + ex) * cap + jc]

                @pl.when(live if i < tail else live_tail)
                def _():
                    pltpu.make_async_copy(srcs[p].at[pl.ds(tok, 1), :], xg.at[sl, pl.ds(p * cap + jc, 1), :],
                                          gsem.at[sl]).start(priority=i % 2)

    def out_copy(p, ex):
        return pltpu.make_async_copy(acc.at[pl.ds(p * cap, cap), :],
                                     dsts[p].at[pl.ds(pl.multiple_of(ex * cap, cap), cap), :], osem)

    @pl.when((e == 0) & (f == 0))
    def _prologue():
        def part_body(part, carry):
            issue_part(0, part, 0, True)
            return carry
        lax.fori_loop(0, nf, part_body, 0)

    @pl.when(f == 0)
    def _start_expert():
        for p in range(2):
            pltpu.make_async_copy(srcs[p].at[pl.ds(0, cap), :], xg.at[slot, pl.ds(p * cap, cap), :],
                                  gsem.at[slot]).wait()
        xe[...] = xg[slot].astype(BF16)

    x = xe[...]
    a = jnp.dot(x, w1_ref[...].astype(BF16), preferred_element_type=F32)
    b = jnp.dot(x, w3_ref[...].astype(BF16), preferred_element_type=F32)
    hid = ((a * jax.nn.sigmoid(a)) * b).astype(BF16)

    issue_part(jnp.minimum(e + 1, ne - 1), f, 1 - slot, e + 1 < ne)

    def down():
        return jnp.dot(hid, w2_ref[...].astype(BF16), preferred_element_type=F32)

    @pl.when(f == 0)
    def _first():
        @pl.when(e > 0)
        def _():
            for p in range(2):
                out_copy(p, e - 1).wait()
        acc[...] = down()

    @pl.when((f > 0) & (f < nf - 1))
    def _middle():
        acc[...] += down()

    @pl.when(f == nf - 1)
    def _last():
        acc[...] = (acc[...] + down()) * gate_ref[...]
        for p in range(2):
            out_copy(p, e).start()

        @pl.when(e == ne - 1)
        def _():
            for p in range(2):
                out_copy(p, e).wait()


def _ffn(idx_flat, h2p, h2s, gate, w1, w3, w2, cap):
    nf = EXPERT_FF // FF_CHUNK
    per = -(-cap // nf)
    grid_spec = pltpu.PrefetchScalarGridSpec(
        num_scalar_prefetch=1,
        grid=(N_EXPERTS, nf),
        in_specs=[pl.BlockSpec(memory_space=pl.ANY), pl.BlockSpec(memory_space=pl.ANY),
                  pl.BlockSpec((None, 2 * cap, 1), lambda e, f, *_: (e, 0, 0)),
                  pl.BlockSpec((None, D_MODEL, FF_CHUNK), lambda e, f, *_: (e, 0, f)),
                  pl.BlockSpec((None, D_MODEL, FF_CHUNK), lambda e, f, *_: (e, 0, f)),
                  pl.BlockSpec((None, FF_CHUNK, D_MODEL), lambda e, f, *_: (e, f, 0))],
        out_specs=[pl.BlockSpec(memory_space=pl.ANY), pl.BlockSpec(memory_space=pl.ANY)],
        scratch_shapes=[pltpu.VMEM((2, 2 * cap, D_MODEL), F32), pltpu.VMEM((2 * cap, D_MODEL), BF16),
                        pltpu.VMEM((2 * cap, D_MODEL), F32), pltpu.SemaphoreType.DMA((2,)),
                        pltpu.SemaphoreType.DMA(())])
    out = jax.ShapeDtypeStruct((N_EXPERTS * cap, D_MODEL), F32)
    return pl.pallas_call(
        functools.partial(_ffn_kernel, cap, nf, per),
        grid_spec=grid_spec, out_shape=[out, out],
        compiler_params=_cparams(("arbitrary", "arbitrary")),
        name="expert_ffn",
    )(idx_flat, h2p, h2s, gate, w1, w3, w2)


WAIT_GROUPS = 8
STAGE_ROWS = -(-(N_EXPERTS * (COMBINE_T // GROUP + 1) * GROUP) // MXU_DIM) * MXU_DIM


def _combine_kernel(cap, plan_s, ye_hbm, x1_ref, g2_ref, srow_ref, o_ref, stage, acc, sem):
    t = pl.program_id(0)
    nt = pl.num_programs(0)
    tt = COMBINE_T
    slot = lax.rem(t, 2)

    def issue_tile(tile, sl):
        for e in range(N_EXPERTS):
            w = plan_s[tile * LANES + e]
            a = w & ((1 << PLAN_NG_SHIFT) - 1)
            ng = lax.shift_right_logical(w, PLAN_NG_SHIFT) & ((1 << (PLAN_GS_SHIFT - PLAN_NG_SHIFT)) - 1)
            gs = lax.shift_right_logical(w, PLAN_GS_SHIFT)

            def issue(g, carry):
                src = pl.multiple_of(e * cap + a + g * GROUP, GROUP)
                dst = pl.multiple_of((gs + g) * GROUP, GROUP)
                pltpu.make_async_copy(ye_hbm.at[pl.ds(src, GROUP), :], stage.at[sl, pl.ds(dst, GROUP), :],
                                      sem.at[sl]).start()
                return carry
            lax.fori_loop(0, ng, issue, 0)

    def wait_tile(tile, sl):
        total = plan_s[tile * LANES + N_EXPERTS]

        def wait_groups(n):
            def body(i, carry):
                pltpu.make_async_copy(ye_hbm.at[pl.ds(0, n * GROUP), :], stage.at[sl, pl.ds(0, n * GROUP), :],
                                      sem.at[sl]).wait()
                return carry
            return body
        lax.fori_loop(0, total // WAIT_GROUPS, wait_groups(WAIT_GROUPS), 0)
        lax.fori_loop(0, lax.rem(total, WAIT_GROUPS), wait_groups(1), 0)

    @pl.when(t == 0)
    def _init():
        stage[...] = jnp.zeros(stage.shape, F32)
        issue_tile(0, 0)

    @pl.when(t + 1 < nt)
    def _prefetch():
        issue_tile(t + 1, 1 - slot)

    wait_tile(t, slot)

    srow = srow_ref[...]
    rio = lax.broadcasted_iota(jnp.int32, (MXU_DIM, tt), 0)
    acc[...] = jnp.zeros(acc.shape, F32)

    def add_chunk(kc, carry):
        base = pl.multiple_of(kc * MXU_DIM, MXU_DIM)
        rows = stage[slot, pl.ds(base, MXU_DIM), :].astype(BF16)
        rel = srow - base
        pick_t = jnp.zeros((MXU_DIM, tt), F32)
        for e in range(N_EXPERTS):
            pick_t = jnp.where(jnp.broadcast_to(rel[e:e + 1, :], (MXU_DIM, tt)) == rio, 1.0, pick_t)
        acc[...] += jnp.dot(pick_t.T.astype(BF16), rows, preferred_element_type=F32)
        return carry

    total_rows = plan_s[t * LANES + N_EXPERTS] * GROUP
    lax.fori_loop(0, lax.div(total_rows + MXU_DIM - 1, MXU_DIM), add_chunk, 0)
    o_ref[...] = x1_ref[...] + g2_ref[...] * acc[...]


def _combine(plan, srow, ye, x1, mod, tiles_per_mod, cap, tag):
    n = x1.shape[0]
    tt = COMBINE_T
    nt = n // tt
    grid_spec = pltpu.PrefetchScalarGridSpec(
        num_scalar_prefetch=1,
        grid=(nt,),
        in_specs=[pl.BlockSpec(memory_space=pl.ANY),
                  pl.BlockSpec((tt, D_MODEL), lambda i, *_: (i, 0)),
                  pl.BlockSpec((None, 1, D_MODEL), lambda i, *_: (i // tiles_per_mod, 0, 5)),
                  pl.BlockSpec((None, N_EXPERTS, tt), lambda i, *_: (i, 0, 0))],
        out_specs=pl.BlockSpec((tt, D_MODEL), lambda i, *_: (i, 0)),
        scratch_shapes=[pltpu.VMEM((2, STAGE_ROWS, D_MODEL), F32), pltpu.VMEM((tt, D_MODEL), F32),
                        pltpu.SemaphoreType.DMA((2,))])
    return pl.pallas_call(
        functools.partial(_combine_kernel, cap),
        grid_spec=grid_spec, out_shape=jax.ShapeDtypeStruct((n, D_MODEL), F32),
        compiler_params=_cparams(("arbitrary",)),
        name="combine_" + tag,
    )(plan.reshape(-1), ye, x1, mod, srow)


def _prep_weights(norm1_gain, norm2_gain, w_in, a_q_gain, a_k_gain, a_lambda_q1, a_lambda_k1, a_lambda_q2,
                  a_lambda_k2, a_sub_gain, q_a_gain, w_uq, b_qn_gain, b_qr_gain, kv_a_gain, w_ukv, b_kn_gain,
                  b_kr_gain, w_o, w_router):
    tile = lambda g, width: jnp.tile(g.reshape(1, -1), (1, width // g.shape[-1]))
    wuq = w_uq[0].reshape(Q_RANK, B_HEADS, B_QK)
    wuq = jnp.concatenate([wuq[:, :, :B_NOPE].reshape(Q_RANK, -1), wuq[:, :, B_NOPE:].reshape(Q_RANK, -1)], axis=1)
    return {
        "norm1_gain": norm1_gain.reshape(1, -1), "norm2_gain": norm2_gain.reshape(1, -1),
        "w_in": w_in[0].astype(BF16),
        "a_q_gain": tile(a_q_gain, MXU_DIM), "a_k_gain": tile(a_k_gain, MXU_DIM),
        "a_lambda_q1": a_lambda_q1.reshape(1, -1), "a_lambda_k1": a_lambda_k1.reshape(1, -1),
        "a_lambda_q2": a_lambda_q2.reshape(1, -1), "a_lambda_k2": a_lambda_k2.reshape(1, -1),
        "a_sub_gain": a_sub_gain.reshape(1, -1), "q_a_gain": q_a_gain.reshape(1, -1),
        "w_uq": wuq.astype(BF16), "b_qn_gain": b_qn_gain.reshape(1, -1), "b_qr_gain": tile(b_qr_gain, MXU_DIM),
        "kv_a_gain": kv_a_gain.reshape(1, -1), "w_ukv": w_ukv[0].astype(BF16),
        "b_kn_gain": b_kn_gain.reshape(1, -1), "b_kr_gain": tile(b_kr_gain, LANES),
        "w_o": w_o[0].astype(BF16),
        "w_router": jnp.pad(w_router[0], ((0, 0), (0, LANES - N_EXPERTS))).astype(BF16),
        "seg64": _seg_matrix(A_DH),
    }


def kernel(x_prompt, x_sample, cache_diff_k, cache_diff_v, cache_mla_ckv, cache_mla_krope, c, c_ctx, w_ada, b_ada, norm1_gain, norm2_gain, w_in, a_q_gain, a_k_gain, a_lambda_q1, a_lambda_k1, a_lambda_q2, a_lambda_k2, a_sub_gain, q_a_gain, w_uq, b_qn_gain, b_qr_gain, kv_a_gain, w_ukv, b_kn_gain, b_kr_gain, w_o, w_router, w_exp1, w_exp3, w_exp2):
    nbp, lp, _ = x_prompt.shape
    nbs, ls, _ = x_sample.shape
    past = cache_diff_k.shape[2]
    w = _prep_weights(norm1_gain, norm2_gain, w_in, a_q_gain, a_k_gain, a_lambda_q1, a_lambda_k1, a_lambda_q2,
                      a_lambda_k2, a_sub_gain, q_a_gain, w_uq, b_qn_gain, b_qr_gain, kv_a_gain, w_ukv, b_kn_gain,
                      b_kr_gain, w_o, w_router)

    cvec = jnp.concatenate([c_ctx[None], c, jnp.zeros((8 - 1 - nbs, D_MODEL), F32)], axis=0)
    mod = _ada(cvec, w_ada[0], b_ada)
    mod_p = mod[0:1].reshape(1, 1, -1)
    mod_s = mod[1:1 + nbs].reshape(nbs, 1, -1)

    xp = x_prompt.reshape(nbp * lp, D_MODEL)
    xs = x_sample.reshape(nbs * ls, D_MODEL)
    tm = 256
    assert lp == IN_SUB, "the prompt projection writes one batch of new_diff_k per row tile"
    aq_p, ak_p, av_p, qm_p, km_p, bv_p, ak32, av32, ckv32, kr32 = _in_proj(xp, mod_p, False, nbp * lp, w, None)
    aq_s, ak_s, av_s, qm_s, km_s, bv_s = _in_proj(xs, mod_s, True, ls, w, _rope_tables(ls))
    km_c, bv_c = _cache_kv(cache_mla_ckv.reshape(nbs * past, KV_RANK), cache_mla_krope.reshape(nbs * past, B_ROPE), w)

    cache_a = (cache_diff_k.reshape(nbs * past, A_QK), cache_diff_v.reshape(nbs * past, A_V))

    oa_p = _diff_attn(aq_p, ak_p, av_p, None, nbp, lp, lp, A_HEADS, w)
    ob_p = _mla_attn(qm_p, km_p, bv_p, None, nbp, lp, lp, B_HEADS)
    oa_s = _diff_attn(aq_s, ak_s, av_s, cache_a, nbs, ls, 256, 4, w)
    ob_s = _mla_attn(qm_s, km_s, bv_s, (km_c, bv_c), nbs, ls, 256, 4)

    x1_p, h2_p, aff_p = _out_proj(oa_p, ob_p, xp, mod_p, nbp * lp // tm, w, "prompt")
    x1_s, h2_s, aff_s = _out_proj(oa_s, ob_s, xs, mod_s, ls // tm, w, "sample")

    idx_p, gate_p, plan_p, srow_p = _route(aff_p, "prompt")
    idx_s, gate_s, plan_s, srow_s = _route(aff_s, "sample")
    cap = idx_p.shape[1]
    idx_flat = jnp.stack([idx_p, idx_s]).reshape(-1)
    gate = jnp.concatenate([gate_p, gate_s], axis=1)[..., None]
    ye_p, ye_s = _ffn(idx_flat, h2_p, h2_s, gate, w_exp1[0], w_exp3[0], w_exp2[0], cap)

    y_p = _combine(plan_p, srow_p, ye_p, x1_p, mod_p, nbp * lp // COMBINE_T, cap, "prompt")
    y_s = _combine(plan_s, srow_s, ye_s, x1_s, mod_s, ls // COMBINE_T, cap, "sample")

    return (y_p.reshape(nbp, lp, D_MODEL), y_s.reshape(nbs, ls, D_MODEL),
            ak32, av32.reshape(nbp, 1, lp, A_HEADS, A_DV),
            ckv32.reshape(nbp, 1, lp, KV_RANK), kr32.reshape(nbp, 1, lp, B_ROPE))
```
